```python
import functools
import jax, jax.numpy as jnp
from jax import lax
import numpy as np

D_MODEL = 1024
BATCH = 2
SEQ = 8192
DEPTH = 1
DEC_BATCH = 128
DEC_SEQ = 8
PAST_LEN = 8192
PAGE_SIZE = 128

MIX_WIDTH = D_MODEL
LRU_WIDTH = MIX_WIDTH // 2
LRU_BLOCKS = 8
LRU_BLOCK = LRU_WIDTH // LRU_BLOCKS
CONV_WIDTH = 4
LRU_C = 8.0
ATTN_HEADS = 8
HEAD_DIM = (MIX_WIDTH - LRU_WIDTH) // ATTN_HEADS
KV_HEADS = 2
GROUP = ATTN_HEADS // KV_HEADS
WINDOW = 128
ROPE_THETA = 10000.0
N_MEM = 256
X_HEADS = 4
X_HEAD_DIM = D_MODEL // X_HEADS
D_FF = 2816
EPS = 1e-6
Q_WIDTH = ATTN_HEADS * HEAD_DIM
KV_WIDTH = KV_HEADS * HEAD_DIM
IN_COLS = 2 * LRU_WIDTH + Q_WIDTH + 2 * KV_WIDTH

kernel_name = 'hymba_rglru_swa_macaron_memxattn_step'


def rmsnorm(x, g):
    xf = x.astype(jnp.float32)
    y = xf * lax.rsqrt(jnp.mean(xf * xf, axis=-1, keepdims=True) + EPS)
    return (y * g.astype(jnp.float32)).astype(x.dtype)


def swiglu(x, w_gate, w_up, w_down):
    return (jax.nn.silu(x @ w_gate) * (x @ w_up)) @ w_down


def rope(z, pos):
    half = HEAD_DIM // 2
    inv = ROPE_THETA ** (-jnp.arange(half, dtype=jnp.float32) / half)
    ang = pos.astype(jnp.float32)[:, None] * inv[None, :]
    cos = jnp.cos(ang)[:, None, :]
    sin = jnp.sin(ang)[:, None, :]
    z1 = z[..., :half].astype(jnp.float32)
    z2 = z[..., half:].astype(jnp.float32)
    return jnp.concatenate([z1 * cos - z2 * sin, z2 * cos + z1 * sin], axis=-1).astype(z.dtype)


def sink_softmax(s, mask, sink):
    s = jnp.where(mask, s, -jnp.inf)
    sk = jnp.broadcast_to(sink.astype(jnp.float32), s.shape[:-1] + (1,))
    p = jax.nn.softmax(jnp.concatenate([s, sk], axis=-1), axis=-1)
    return p[..., :-1]


def lru_combine(left, right):
    a1, b1 = left
    a2, b2 = right
    return a1 * a2, a2 * b1 + b2


def rg_lru(u, conv_buf, h0, conv_w, conv_b, w_a, b_a, w_i, b_i, lam):
    n, t = u.shape[:2]
    xc = jnp.concatenate([conv_buf.astype(u.dtype), u], axis=1)
    conv = conv_b
    for j in range(CONV_WIDTH):
        conv = conv + xc[:, j:j + t] * conv_w[j]
    new_buf = xc[:, -(CONV_WIDTH - 1):]
    xb = conv.reshape(n, t, LRU_BLOCKS, LRU_BLOCK)
    r = jax.nn.sigmoid(jnp.einsum('ntgi,gij->ntgj', xb, w_a).reshape(n, t, LRU_WIDTH) + b_a)
    gi = jax.nn.sigmoid(jnp.einsum('ntgi,gij->ntgj', xb, w_i).reshape(n, t, LRU_WIDTH) + b_i)
    log_a = -LRU_C * r.astype(jnp.float32) * jax.nn.softplus(-lam.astype(jnp.float32))
    a = jnp.exp(log_a)
    b = jnp.sqrt(-jnp.expm1(2.0 * log_a)) * (gi * conv).astype(jnp.float32)
    b = b.at[:, 0].add(a[:, 0] * h0.astype(jnp.float32))
    _, h = lax.associative_scan(lru_combine, (a, b), axis=1)
    return h, h[:, -1], new_buf


def swa_prompt(q, k, v, sink):
    n, t = q.shape[:2]
    nb = t // WINDOW
    qb = q.reshape(n, nb, WINDOW, KV_HEADS, GROUP, HEAD_DIM)

    def band(z):
        zb = z.reshape(n, nb, WINDOW, KV_HEADS, HEAD_DIM)
        prev = jnp.concatenate([jnp.zeros_like(zb[:, :1]), zb[:, :-1]], axis=1)
        return jnp.concatenate([prev, zb], axis=2)

    kk, vv = band(k), band(v)
    s = jnp.einsum('nbqkgd,nbskd->nbkgqs', qb, kk, preferred_element_type=jnp.float32) * (HEAD_DIM ** -0.5)
    i = jnp.arange(WINDOW)[:, None]
    j = jnp.arange(2 * WINDOW)[None, :]
    dist = i + WINDOW - j
    within = (dist >= 0) & (dist < WINDOW)
    blk = jnp.arange(nb)[:, None, None]
    mask = within[None] & ((blk > 0) | (j[None] >= WINDOW))
    p = sink_softmax(s, mask[None, :, None, None], sink.reshape(KV_HEADS, GROUP)[None, None, :, :, None, None])
    o = jnp.einsum('nbkgqs,nbskd->nbqkgd', p.astype(vv.dtype), vv).reshape(n, t, Q_WIDTH)
    return o, k[:, -WINDOW:], v[:, -WINDOW:]


def swa_sample(q, k, v, sink, buf_k, buf_v):
    n, s_len = q.shape[:2]
    kk = jnp.concatenate([buf_k.astype(k.dtype), k], axis=1)
    vv = jnp.concatenate([buf_v.astype(v.dtype), v], axis=1)
    qp = PAST_LEN + jnp.arange(s_len, dtype=jnp.int32)
    kp = jnp.concatenate([PAST_LEN - WINDOW + jnp.arange(WINDOW, dtype=jnp.int32), qp])
    dist = qp[:, None] - kp[None, :]
    mask = (dist >= 0) & (dist < WINDOW) & (kp[None, :] >= 0)
    qg = q.reshape(n, s_len, KV_HEADS, GROUP, HEAD_DIM)
    s = jnp.einsum('nqkgd,nskd->nkgqs', qg, kk, preferred_element_type=jnp.float32) * (HEAD_DIM ** -0.5)
    p = sink_softmax(s, mask, sink.reshape(KV_HEADS, GROUP)[None, :, :, None, None])
    o = jnp.einsum('nkgqs,nskd->nqkgd', p.astype(vv.dtype), vv).reshape(n, s_len, Q_WIDTH)
    return o, kk[:, -WINDOW:], vv[:, -WINDOW:]


def memory_kv(mem, g_mem, w_ck, w_cv):
    n, m = mem.shape[:2]
    mm = rmsnorm(mem, g_mem)
    mk = (mm @ w_ck).reshape(n, m, X_HEADS, X_HEAD_DIM)
    mv = (mm @ w_cv).reshape(n, m, X_HEADS, X_HEAD_DIM)
    return mk, mv


def cross_attend(h, mk, mv, w_cq, w_co):
    n, t = h.shape[:2]
    q = (h @ w_cq).reshape(n, t, X_HEADS, X_HEAD_DIM)
    s = jnp.einsum('nthd,nmhd->nhtm', q, mk.astype(q.dtype), preferred_element_type=jnp.float32) * (X_HEAD_DIM ** -0.5)
    p = jax.nn.softmax(s, axis=-1).astype(mv.dtype)
    o = jnp.einsum('nhtm,nmhd->nthd', p, mv).reshape(n, t, D_MODEL).astype(h.dtype)
    return o @ w_co


def trunk_layer(x, pos, mk, mv, conv_buf, h0, attend, p):
    x = x + 0.5 * swiglu(rmsnorm(x, p['g_ffn1']), p['w1_gate'], p['w1_up'], p['w1_down'])
    h = rmsnorm(x, p['g_mix'])
    n, t = h.shape[:2]
    proj = h @ p['w_in']
    o1 = LRU_WIDTH
    o2 = 2 * LRU_WIDTH
    o3 = o2 + Q_WIDTH
    o4 = o3 + KV_WIDTH
    u, gate = proj[..., :o1], proj[..., o1:o2]
    q = rope(proj[..., o2:o3].reshape(n, t, ATTN_HEADS, HEAD_DIM), pos)
    k = rope(proj[..., o3:o4].reshape(n, t, KV_HEADS, HEAD_DIM), pos)
    v = proj[..., o4:].reshape(n, t, KV_HEADS, HEAD_DIM)
    hl, h_last, new_conv = rg_lru(u, conv_buf, h0, p['conv_w'], p['conv_b'], p['w_a'], p['b_a'], p['w_i'], p['b_i'], p['lam'])
    lru_out = rmsnorm(hl.astype(x.dtype) * jax.nn.gelu(gate), p['g_lru_out'])
    attn_o, new_k, new_v = attend(q, k, v, p['sink'])
    attn_out = rmsnorm(attn_o.astype(x.dtype), p['g_attn_out'])
    x = x + jnp.concatenate([lru_out, attn_out], axis=-1) @ p['w_out']
    x = x + cross_attend(rmsnorm(x, p['g_xattn']), mk, mv, p['w_cq'], p['w_co'])
    x = x + 0.5 * swiglu(rmsnorm(x, p['g_ffn2']), p['w2_gate'], p['w2_up'], p['w2_down'])
    return x, new_k, new_v, new_conv, h_last.astype(x.dtype)


def setup_inputs(seed: int = 0) -> dict:
    key = jax.random.key(seed)
    ks = iter(list(jax.random.split(key, 48)))

    def nrm(shape, scale):
        return jax.random.normal(next(ks), shape, jnp.float32) * scale

    def gain(shape):
        return 1.0 + nrm(shape, 0.05)

    L = DEPTH
    D = D_MODEL
    u = jax.random.uniform(next(ks), (L, LRU_WIDTH), jnp.float32, minval=0.9, maxval=0.999)
    a0 = u ** (1.0 / LRU_C)
    lam = jnp.log(a0) - jnp.log1p(-a0)
    return {
        'x_prompt': nrm((BATCH, SEQ, D), 1.0),
        'x_sample': nrm((DEC_BATCH, DEC_SEQ, D), 1.0),
        'mem_prompt': nrm((BATCH, N_MEM, D), 1.0),
        'cache_mem_k': nrm((L, DEC_BATCH, N_MEM, X_HEADS, X_HEAD_DIM), 1.0),
        'cache_mem_v': nrm((L, DEC_BATCH, N_MEM, X_HEADS, X_HEAD_DIM), 1.0),
        'cache_swa_k': nrm((L, DEC_BATCH, WINDOW, KV_HEADS, HEAD_DIM), 1.0),
        'cache_swa_v': nrm((L, DEC_BATCH, WINDOW, KV_HEADS, HEAD_DIM), 1.0),
        'state_conv': nrm((L, DEC_BATCH, CONV_WIDTH - 1, LRU_WIDTH), 1.0),
        'state_lru_h': nrm((L, DEC_BATCH, LRU_WIDTH), 0.5),
        'g_ffn1': gain((L, D)),
        'w1_gate': nrm((L, D, D_FF), D ** -0.5),
        'w1_up': nrm((L, D, D_FF), D ** -0.5),
        'w1_down': nrm((L, D_FF, D), D_FF ** -0.5),
        'g_mix': gain((L, D)),
        'w_in': nrm((L, D, IN_COLS), D ** -0.5),
        'conv_w': nrm((L, CONV_WIDTH, LRU_WIDTH), CONV_WIDTH ** -0.5),
        'conv_b': nrm((L, LRU_WIDTH), 0.02),
        'w_a': nrm((L, LRU_BLOCKS, LRU_BLOCK, LRU_BLOCK), LRU_BLOCK ** -0.5),
        'b_a': nrm((L, LRU_WIDTH), 0.02),
        'w_i': nrm((L, LRU_BLOCKS, LRU_BLOCK, LRU_BLOCK), LRU_BLOCK ** -0.5),
        'b_i': nrm((L, LRU_WIDTH), 0.02),
        'lam': lam,
        'sink': nrm((L, ATTN_HEADS), 0.5),
        'g_lru_out': gain((L, LRU_WIDTH)),
        'g_attn_out': gain((L, Q_WIDTH)),
        'w_out': nrm((L, MIX_WIDTH, D), MIX_WIDTH ** -0.5),
        'g_xattn': gain((L, D)),
        'g_mem': gain((L, D)),
        'w_cq': nrm((L, D, D), D ** -0.5),
        'w_ck': nrm((L, D, D), D ** -0.5),
        'w_cv': nrm((L, D, D), D ** -0.5),
        'w_co': nrm((L, D, D), D ** -0.5),
        'g_ffn2': gain((L, D)),
        'w2_gate': nrm((L, D, D_FF), D ** -0.5),
        'w2_up': nrm((L, D, D_FF), D ** -0.5),
        'w2_down': nrm((L, D_FF, D), D_FF ** -0.5),
        'g_final': gain((D,)),
    }


def reference(x_prompt, x_sample, mem_prompt, cache_mem_k, cache_mem_v, cache_swa_k, cache_swa_v, state_conv, state_lru_h,
              g_ffn1, w1_gate, w1_up, w1_down, g_mix, w_in, conv_w, conv_b, w_a, b_a, w_i, b_i, lam, sink,
              g_lru_out, g_attn_out, w_out, g_xattn, g_mem, w_cq, w_ck, w_cv, w_co, g_ffn2, w2_gate, w2_up, w2_down, g_final):
    pos_p = jnp.arange(x_prompt.shape[1], dtype=jnp.int32)
    pos_s = PAST_LEN + jnp.arange(x_sample.shape[1], dtype=jnp.int32)
    xp, xs = x_prompt, x_sample
    nbp = x_prompt.shape[0]
    mkp_l, mvp_l, kp_l, vp_l, cp_l, hp_l = [], [], [], [], [], []
    ks_l, vs_l, cs_l, hs_l = [], [], [], []
    for l in range(DEPTH):
        p = {
            'g_ffn1': g_ffn1[l], 'w1_gate': w1_gate[l], 'w1_up': w1_up[l], 'w1_down': w1_down[l],
            'g_mix': g_mix[l], 'w_in': w_in[l], 'conv_w': conv_w[l], 'conv_b': conv_b[l],
            'w_a': w_a[l], 'b_a': b_a[l], 'w_i': w_i[l], 'b_i': b_i[l], 'lam': lam[l], 'sink': sink[l],
            'g_lru_out': g_lru_out[l], 'g_attn_out': g_attn_out[l], 'w_out': w_out[l],
            'g_xattn': g_xattn[l], 'w_cq': w_cq[l], 'w_co': w_co[l],
            'g_ffn2': g_ffn2[l], 'w2_gate': w2_gate[l], 'w2_up': w2_up[l], 'w2_down': w2_down[l],
        }
        mk_p, mv_p = memory_kv(mem_prompt, g_mem[l], w_ck[l], w_cv[l])
        conv0 = jnp.zeros((nbp, CONV_WIDTH - 1, LRU_WIDTH), xp.dtype)
        h0 = jnp.zeros((nbp, LRU_WIDTH), jnp.float32)
        xp, kp_, vp_, cp_, hp_ = trunk_layer(xp, pos_p, mk_p, mv_p, conv0, h0, swa_prompt, p)
        attend_s = functools.partial(swa_sample, buf_k=cache_swa_k[l], buf_v=cache_swa_v[l])
        xs, ks_, vs_, cs_, hs_ = trunk_layer(xs, pos_s, cache_mem_k[l], cache_mem_v[l], state_conv[l], state_lru_h[l], attend_s, p)
        mkp_l.append(mk_p)
        mvp_l.append(mv_p)
        kp_l.append(kp_)
        vp_l.append(vp_)
        cp_l.append(cp_)
        hp_l.append(hp_)
        ks_l.append(ks_)
        vs_l.append(vs_)
        cs_l.append(cs_)
        hs_l.append(hs_)
    y_prompt = rmsnorm(xp, g_final)
    y_sample = rmsnorm(xs, g_final)
    return (y_prompt, y_sample,
            jnp.stack(mkp_l), jnp.stack(mvp_l),
            jnp.stack(kp_l), jnp.stack(vp_l), jnp.stack(cp_l), jnp.stack(hp_l),
            jnp.stack(ks_l), jnp.stack(vs_l), jnp.stack(cs_l), jnp.stack(hs_l))
```

```python
import functools

import jax
import jax.numpy as jnp
from jax import lax
from jax.experimental import pallas as pl
from jax.experimental.pallas import tpu as pltpu

F32 = jnp.float32
BF16 = jnp.bfloat16

D_MODEL = 1024
LRU_WIDTH = 512
LRU_BLOCKS = 8
CONV_WIDTH = 4
LRU_C = 8.0
ATTN_HEADS = 8
HEAD_DIM = 64
KV_HEADS = 2
WINDOW = 128
PAST_LEN = 8192
ROPE_THETA = 10000.0
N_MEM = 256
X_HEADS = 4
X_HEAD_DIM = 256
D_FF = 2816
EPS = 1e-6
Q_WIDTH = ATTN_HEADS * HEAD_DIM
KV_WIDTH = KV_HEADS * HEAD_DIM
IN_COLS = 2 * LRU_WIDTH + Q_WIDTH + 2 * KV_WIDTH

LANES = 128
SUBLANES = 8
VMEM_LIMIT = 56 * 1024 * 1024

ROW_TILE = 512
LRU_TILE = 256
SAMPLE_SEQ_TILE = 8
XATTN_SEQ_TILE = 4


def _params(n_axes):
    return pltpu.CompilerParams(dimension_semantics=("arbitrary",) * n_axes, vmem_limit_bytes=VMEM_LIMIT)


def _const_spec(shape):
    return pl.BlockSpec(shape, lambda *_: (0,) * len(shape), pipeline_mode=pl.Buffered(1))


def _rms(x, g):
    return x * lax.rsqrt(jnp.mean(x * x, axis=-1, keepdims=True) + EPS) * g


def _dot(a, b):
    return jnp.dot(a, b, preferred_element_type=F32)


def _dot_nt(a, b):
    return lax.dot_general(a, b, (((1,), (1,)), ((), ())), preferred_element_type=F32)


def _ffn_body(x_ref, g_ref, wg_ref, wu_ref, wd_ref, gf_ref, o_ref, *, final_norm):
    x = x_ref[...]
    xn = _rms(x, g_ref[...]).astype(BF16)
    gate = _dot(xn, wg_ref[...])
    up = _dot(xn, wu_ref[...])
    h = (gate * jax.nn.sigmoid(gate) * up).astype(BF16)
    y = x + 0.5 * _dot(h, wd_ref[...])
    if final_norm:
        y = _rms(y, gf_ref[...])
    o_ref[...] = y


def _ffn(x, g, wg, wu, wd, g_final, final_norm):
    rows = x.shape[0]
    tm = min(ROW_TILE, rows)
    row_spec = pl.BlockSpec((tm, D_MODEL), lambda i: (i, 0))
    return pl.pallas_call(
        functools.partial(_ffn_body, final_norm=final_norm),
        grid=(rows // tm,),
        in_specs=[row_spec, _const_spec((1, D_MODEL)), _const_spec((D_MODEL, D_FF)), _const_spec((D_MODEL, D_FF)),
                  _const_spec((D_FF, D_MODEL)), _const_spec((1, D_MODEL))],
        out_specs=row_spec,
        out_shape=jax.ShapeDtypeStruct((rows, D_MODEL), F32),
        compiler_params=_params(1),
    )(x, g, wg, wu, wd, g_final)


def _rope(z, cos, sin_signed):
    half = HEAD_DIM // 2
    lane = lax.broadcasted_iota(jnp.int32, z.shape, 1)
    first_half = (lane % HEAD_DIM) < half
    partner = jnp.where(first_half, pltpu.roll(z, LANES - half, axis=1), pltpu.roll(z, half, axis=1))
    return z * cos + partner * sin_signed


def _proj_body(x_ref, g_ref, w_ref, cos_ref, sin_ref, u_ref, gate_ref, q_ref, k_ref, v_ref):
    xn = _rms(x_ref[...], g_ref[...]).astype(BF16)
    cos = cos_ref[...]
    sin = sin_ref[...]
    o_gate, o_q, o_k, o_v = LRU_WIDTH, 2 * LRU_WIDTH, 2 * LRU_WIDTH + Q_WIDTH, 2 * LRU_WIDTH + Q_WIDTH + KV_WIDTH
    u_ref[...] = _dot(xn, w_ref[:, :o_gate])
    gate_ref[...] = _dot(xn, w_ref[:, o_gate:o_q])
    q = _dot(xn, w_ref[:, o_q:o_k])
    for j in range(Q_WIDTH // LANES):
        q_ref[:, j * LANES:(j + 1) * LANES] = _rope(q[:, j * LANES:(j + 1) * LANES], cos, sin)
    k_ref[...] = _rope(_dot(xn, w_ref[:, o_k:o_v]), cos, sin)
    v_ref[...] = _dot(xn, w_ref[:, o_v:])


def _proj(x, g, w_in, cos, sin):
    rows = x.shape[0]
    tm = min(ROW_TILE, rows, cos.shape[0])
    pos_blocks = cos.shape[0] // tm

    def row_spec(width):
        return pl.BlockSpec((tm, width), lambda i: (i, 0))

    pos_spec = pl.BlockSpec((tm, LANES), lambda i: (i % pos_blocks, 0))
    widths = (LRU_WIDTH, LRU_WIDTH, Q_WIDTH, KV_WIDTH, KV_WIDTH)
    return pl.pallas_call(
        _proj_body,
        grid=(rows // tm,),
        in_specs=[row_spec(D_MODEL), _const_spec((1, D_MODEL)), _const_spec((D_MODEL, IN_COLS)), pos_spec, pos_spec],
        out_specs=[row_spec(w) for w in widths],
        out_shape=[jax.ShapeDtypeStruct((rows, w), F32) for w in widths],
        compiler_params=_params(1),
    )(x, g, w_in, cos, sin)


def _rope_tables(pos):
    half = HEAD_DIM // 2
    inv = ROPE_THETA ** (-jnp.arange(half, dtype=F32) / half)
    ang = pos.astype(F32)[:, None] * inv[None, :]
    cos = jnp.cos(ang)
    sin = jnp.sin(ang)
    reps = LANES // HEAD_DIM
    return (jnp.tile(jnp.concatenate([cos, cos], axis=-1), (1, reps)),
            jnp.tile(jnp.concatenate([-sin, sin], axis=-1), (1, reps)))


def _softplus(x):
    return jnp.maximum(x, 0.0) + jnp.log1p(jnp.exp(-jnp.abs(x)))


def _lru_coeffs(conv, wab_ref, bab_ref, lam_ref):
    gates = _dot(conv.astype(BF16), wab_ref[...]) + bab_ref[...]
    r = jax.nn.sigmoid(gates[:, :LRU_WIDTH])
    gi = jax.nn.sigmoid(gates[:, LRU_WIDTH:])
    log_a = -LRU_C * r * _softplus(-lam_ref[...])
    a = jnp.exp(log_a)
    b = jnp.sqrt(-jnp.tanh(log_a) * (a * a + 1.0)) * (gi * conv)
    return a, b


def _segment_scan(a, b, seg):
    row = lax.broadcasted_iota(jnp.int32, a.shape, 0)
    pos = row % seg
    step = 1
    while step < seg:
        a_prev = pltpu.roll(a, step, axis=0)
        b_prev = pltpu.roll(b, step, axis=0)
        live = pos >= step
        b = jnp.where(live, a * b_prev + b, b)
        a = jnp.where(live, a * a_prev, a)
        step *= 2
    return a, b


def _lru_finish(h, gate, g_out):
    return _rms(h * jax.nn.gelu(gate), g_out)


def _lru_prompt_body(u_ref, gate_ref, cw_ref, cb_ref, wab_ref, bab_ref, lam_ref, gout_ref,
                     o_ref, hlast_ref, ext_ref, h_ref, hs_ref, *, tt):
    pad = SUBLANES

    @pl.when(pl.program_id(1) == 0)
    def _():
        ext_ref[0:pad, :] = jnp.zeros((pad, LRU_WIDTH), F32)
        h_ref[...] = jnp.zeros_like(h_ref)

    ext_ref[pad:pad + tt, :] = u_ref[0]
    conv = cb_ref[...]
    for j in range(CONV_WIDTH):
        start = pad - (CONV_WIDTH - 1) + j
        conv = conv + ext_ref[start:start + tt, :] * cw_ref[j:j + 1, :]
    ext_ref[0:pad, :] = ext_ref[tt:tt + pad, :]

    a, b = _lru_coeffs(conv, wab_ref, bab_ref, lam_ref)
    a_cum, h_local = _segment_scan(a, b, tt)
    hs_ref[...] = a_cum * h_ref[...] + h_local
    h_ref[...] = hs_ref[tt - 1:tt, :]
    hlast_ref[0] = h_ref[...]
    o_ref[0] = _lru_finish(hs_ref[...], gate_ref[0], gout_ref[...])


def _lru_prompt(u, gate, conv_w, conv_b, wab, bab, lam, g_out):
    n, t, _ = u.shape
    tt = min(LRU_TILE, t)
    seq_spec = pl.BlockSpec((1, tt, LRU_WIDTH), lambda i, j: (i, j, 0))
    return pl.pallas_call(
        functools.partial(_lru_prompt_body, tt=tt),
        grid=(n, t // tt),
        in_specs=[seq_spec, seq_spec, _const_spec((CONV_WIDTH, LRU_WIDTH)), _const_spec((1, LRU_WIDTH)),
                  _const_spec((LRU_WIDTH, 2 * LRU_WIDTH)), _const_spec((1, 2 * LRU_WIDTH)),
                  _const_spec((1, LRU_WIDTH)), _const_spec((1, LRU_WIDTH))],
        out_specs=[seq_spec, pl.BlockSpec((1, 1, LRU_WIDTH), lambda i, j: (i, 0, 0))],
        out_shape=[jax.ShapeDtypeStruct((n, t, LRU_WIDTH), F32), jax.ShapeDtypeStruct((n, 1, LRU_WIDTH), F32)],
        scratch_shapes=[pltpu.VMEM((tt + SUBLANES, LRU_WIDTH), F32), pltpu.VMEM((1, LRU_WIDTH), F32),
                        pltpu.VMEM((tt, LRU_WIDTH), F32)],
        compiler_params=_params(2),
    )(u, gate, conv_w, conv_b, wab, bab, lam, g_out)


def _lru_sample_body(u_ref, gate_ref, cpad_ref, h0_ref, cw_ref, cb_ref, wab_ref, bab_ref, lam_ref, gout_ref,
                     o_ref, hs_ref, *, seg):
    u = u_ref[...]
    cpad = cpad_ref[...]
    rows = u.shape[0]
    pos = lax.broadcasted_iota(jnp.int32, u.shape, 0) % seg
    conv = cb_ref[...] + u * cw_ref[CONV_WIDTH - 1:CONV_WIDTH, :]
    for back in range(1, CONV_WIDTH):
        shifted = jnp.where(pos >= back, pltpu.roll(u, back, axis=0),
                            pltpu.roll(cpad, (back - seg) % rows, axis=0))
        conv = conv + shifted * cw_ref[CONV_WIDTH - 1 - back:CONV_WIDTH - back, :]
    a, b = _lru_coeffs(conv, wab_ref, bab_ref, lam_ref)
    a_cum, h_local = _segment_scan(a, b, seg)
    h = a_cum * h0_ref[...] + h_local
    hs_ref[...] = h
    o_ref[...] = _lru_finish(h, gate_ref[...], gout_ref[...])


def _lru_sample(u, gate, conv_pad, h0_rep, seg, conv_w, conv_b, wab, bab, lam, g_out):
    rows = u.shape[0]
    tm = min(ROW_TILE, rows)
    row_spec = pl.BlockSpec((tm, LRU_WIDTH), lambda i: (i, 0))
    return pl.pallas_call(
        functools.partial(_lru_sample_body, seg=seg),
        grid=(rows // tm,),
        in_specs=[row_spec, row_spec, row_spec, row_spec, _const_spec((CONV_WIDTH, LRU_WIDTH)),
                  _const_spec((1, LRU_WIDTH)), _const_spec((LRU_WIDTH, 2 * LRU_WIDTH)),
                  _const_spec((1, 2 * LRU_WIDTH)), _const_spec((1, LRU_WIDTH)), _const_spec((1, LRU_WIDTH))],
        out_specs=[row_spec, row_spec],
        out_shape=[jax.ShapeDtypeStruct((rows, LRU_WIDTH), F32)] * 2,
        compiler_params=_params(1),
    )(u, gate, conv_pad, h0_rep, conv_w, conv_b, wab, bab, lam, g_out)


def _block_diag(w):
    nb, bi, bj = w.shape
    eye = jnp.eye(nb, dtype=w.dtype)
    return jnp.einsum('gij,gh->gihj', w, eye).reshape(nb * bi, nb * bj)


def _swa_core(q, keys, vals, sink_ref, mask, g_out):
    r = q.shape[0]
    lane = lax.broadcasted_iota(jnp.int32, (r, LANES), 1)
    low = lane < HEAD_DIM
    zero = jnp.zeros((r, LANES), F32)
    slabs = [q[:, j * LANES:(j + 1) * LANES] for j in range(Q_WIDTH // LANES)]
    group = ATTN_HEADS // KV_HEADS
    straight, swapped = [], []
    for h in range(ATTN_HEADS):
        (straight if (h % 2) == (h // group) else swapped).append(h)
    scale = HEAD_DIM ** -0.5
    out_half = {}
    for heads, kk, vv in ((straight, keys, vals),
                          (swapped, pltpu.roll(keys, HEAD_DIM, axis=1), pltpu.roll(vals, HEAD_DIM, axis=1))):
        qs = jnp.concatenate([jnp.where(low if h % 2 == 0 else ~low, slabs[h // 2], zero) for h in heads], axis=0)
        s = _dot_nt(qs.astype(BF16), kk.astype(BF16)) * scale
        probs = []
        for i, h in enumerate(heads):
            sh = jnp.where(mask, s[i * r:(i + 1) * r], -jnp.inf)
            sink = sink_ref[h]
            m = jnp.maximum(jnp.max(sh, axis=-1, keepdims=True), sink)
            e = jnp.exp(sh - m)
            denom = jnp.sum(e, axis=-1, keepdims=True) + jnp.exp(sink - m)
            probs.append(e / denom)
        o = _dot(jnp.concatenate(probs, axis=0).astype(BF16), vv.astype(BF16))
        for i, h in enumerate(heads):
            out_half[h] = o[i * r:(i + 1) * r]
    out = jnp.concatenate([jnp.where(low, out_half[2 * j], out_half[2 * j + 1]) for j in range(Q_WIDTH // LANES)],
                          axis=1)
    return _rms(out, g_out)


def _swa_prompt_body(sink_ref, q_ref, kp_ref, kc_ref, vp_ref, vc_ref, gout_ref, o_ref):
    blk = pl.program_id(1)
    keys = jnp.concatenate([kp_ref[0], kc_ref[0]], axis=0)
    vals = jnp.concatenate([vp_ref[0], vc_ref[0]], axis=0)
    i = lax.broadcasted_iota(jnp.int32, (WINDOW, 2 * WINDOW), 0)
    j = lax.broadcasted_iota(jnp.int32, (WINDOW, 2 * WINDOW), 1)
    dist = i + WINDOW - j
    mask = (dist >= 0) & (dist < WINDOW) & ((blk > 0) | (j >= WINDOW))
    o_ref[0] = _swa_core(q_ref[0], keys, vals, sink_ref, mask, gout_ref[...])


def _swa_prompt(q, k, v, sink, g_out):
    n, t, _ = q.shape
    cur = lambda i, j: (i, j, 0)
    prev = lambda i, j: (i, jnp.maximum(j - 1, 0), 0)
    kv_block = (1, WINDOW, KV_WIDTH)
    return pl.pallas_call(
        _swa_prompt_body,
        grid=(n, t // WINDOW),
        in_specs=[pl.BlockSpec(memory_space=pltpu.SMEM), pl.BlockSpec((1, WINDOW, Q_WIDTH), cur),
                  pl.BlockSpec(kv_block, prev), pl.BlockSpec(kv_block, cur),
                  pl.BlockSpec(kv_block, prev), pl.BlockSpec(kv_block, cur), _const_spec((1, Q_WIDTH))],
        out_specs=pl.BlockSpec((1, WINDOW, Q_WIDTH), cur),
        out_shape=jax.ShapeDtypeStruct((n, t, Q_WIDTH), F32),
        compiler_params=_params(2),
    )(sink, q, k, k, v, v, g_out)


def _swa_sample_body(sink_ref, q_ref, k_ref, v_ref, bk_ref, bv_ref, gout_ref, o_ref, nk_ref, nv_ref, *, s_len, past):
    qp = past + lax.broadcasted_iota(jnp.int32, (s_len, WINDOW + s_len), 0)
    col = lax.broadcasted_iota(jnp.int32, (s_len, WINDOW + s_len), 1)
    kp = jnp.where(col < WINDOW, past - WINDOW + col, past + col - WINDOW)
    dist = qp - kp
    mask = (dist >= 0) & (dist < WINDOW) & (kp >= 0)
    for b in range(bk_ref.shape[0]):
        rows = slice(b * s_len, (b + 1) * s_len)
        keys = jnp.concatenate([bk_ref[b], k_ref[rows, :]], axis=0)
        vals = jnp.concatenate([bv_ref[b], v_ref[rows, :]], axis=0)
        o_ref[rows, :] = _swa_core(q_ref[rows, :], keys, vals, sink_ref, mask, gout_ref[...])
        nk_ref[b] = keys[s_len:, :]
        nv_ref[b] = vals[s_len:, :]


def _swa_sample(q, k, v, buf_k, buf_v, sink, g_out, s_len, past):
    n = buf_k.shape[0]
    sb = min(SAMPLE_SEQ_TILE, n)

    def row_spec(width):
        return pl.BlockSpec((sb * s_len, width), lambda i: (i, 0))

    buf_spec = pl.BlockSpec((sb, WINDOW, KV_WIDTH), lambda i: (i, 0, 0))
    return pl.pallas_call(
        functools.partial(_swa_sample_body, s_len=s_len, past=past),
        grid=(n // sb,),
        in_specs=[pl.BlockSpec(memory_space=pltpu.SMEM), row_spec(Q_WIDTH), row_spec(KV_WIDTH), row_spec(KV_WIDTH),
                  buf_spec, buf_spec, _const_spec((1, Q_WIDTH))],
        out_specs=[row_spec(Q_WIDTH), buf_spec, buf_spec],
        out_shape=[jax.ShapeDtypeStruct((n * s_len, Q_WIDTH), F32),
                   jax.ShapeDtypeStruct((n, WINDOW, KV_WIDTH), F32), jax.ShapeDtypeStruct((n, WINDOW, KV_WIDTH), F32)],
        compiler_params=_params(1),
    )(sink, q, k, v, buf_k, buf_v, g_out)


def _mix_out_body(x_ref, lru_ref, attn_ref, wout_ref, gx_ref, wcq_ref, mk_ref, mv_ref, wco_ref, o_ref, *, seq_rows):
    x = (x_ref[...] + _dot(lru_ref[...].astype(BF16), wout_ref[:LRU_WIDTH, :])
         + _dot(attn_ref[...].astype(BF16), wout_ref[LRU_WIDTH:, :]))
    q = _dot(_rms(x, gx_ref[...]).astype(BF16), wcq_ref[...])
    scale = X_HEAD_DIM ** -0.5
    outs = []
    for b in range(mk_ref.shape[0]):
        heads = []
        for h in range(X_HEADS):
            cols = slice(h * X_HEAD_DIM, (h + 1) * X_HEAD_DIM)
            qh = q[b * seq_rows:(b + 1) * seq_rows, cols].astype(BF16)
            s = _dot_nt(qh, mk_ref[b, :, cols].astype(BF16)) * scale
            e = jnp.exp(s - jnp.max(s, axis=-1, keepdims=True))
            p = e / jnp.sum(e, axis=-1, keepdims=True)
            heads.append(_dot(p.astype(BF16), mv_ref[b, :, cols].astype(BF16)))
        outs.append(jnp.concatenate(heads, axis=1))
    o = outs[0] if len(outs) == 1 else jnp.concatenate(outs, axis=0)
    o_ref[...] = x + _dot(o.astype(BF16), wco_ref[...])


def _mix_out(x, lru_out, attn_out, w_out, g_x, w_cq, mk, mv, w_co, seq_rows, seq_tile):
    rows = x.shape[0]
    if seq_tile == 1:
        tm = min(ROW_TILE, seq_rows)
        per_seq = seq_rows // tm
        mem_map = lambda i: (i // per_seq, 0, 0)
        body_rows = tm
    else:
        tm = seq_tile * seq_rows
        mem_map = lambda i: (i, 0, 0)
        body_rows = seq_rows

    def row_spec(width):
        return pl.BlockSpec((tm, width), lambda i: (i, 0))

    mem_spec = pl.BlockSpec((seq_tile, N_MEM, D_MODEL), mem_map)
    w_spec = _const_spec((D_MODEL, D_MODEL))
    return pl.pallas_call(
        functools.partial(_mix_out_body, seq_rows=body_rows),
        grid=(rows // tm,),
        in_specs=[row_spec(D_MODEL), row_spec(LRU_WIDTH), row_spec(Q_WIDTH), w_spec, _const_spec((1, D_MODEL)),
                  w_spec, mem_spec, mem_spec, w_spec],
        out_specs=row_spec(D_MODEL),
        out_shape=jax.ShapeDtypeStruct((rows, D_MODEL), F32),
        compiler_params=_params(1),
    )(x, lru_out, attn_out, w_out, g_x, w_cq, mk, mv, w_co)


def _mem_kv_body(mem_ref, g_ref, wk_ref, wv_ref, k_ref, v_ref):
    mm = _rms(mem_ref[...], g_ref[...]).astype(BF16)
    k_ref[...] = _dot(mm, wk_ref[...])
    v_ref[...] = _dot(mm, wv_ref[...])


def _mem_kv(mem, g, w_ck, w_cv):
    rows = mem.shape[0]
    tm = min(ROW_TILE, rows)
    row_spec = pl.BlockSpec((tm, D_MODEL), lambda i: (i, 0))
    w_spec = _const_spec((D_MODEL, D_MODEL))
    return pl.pallas_call(
        _mem_kv_body,
        grid=(rows // tm,),
        in_specs=[row_spec, _const_spec((1, D_MODEL)), w_spec, w_spec],
        out_specs=[row_spec, row_spec],
        out_shape=[jax.ShapeDtypeStruct((rows, D_MODEL), F32)] * 2,
        compiler_params=_params(1),
    )(mem, g, w_ck, w_cv)


def kernel(x_prompt, x_sample, mem_prompt, cache_mem_k, cache_mem_v, cache_swa_k, cache_swa_v, state_conv, state_lru_h,
           g_ffn1, w1_gate, w1_up, w1_down, g_mix, w_in, conv_w, conv_b, w_a, b_a, w_i, b_i, lam, sink,
           g_lru_out, g_attn_out, w_out, g_xattn, g_mem, w_cq, w_ck, w_cv, w_co, g_ffn2, w2_gate, w2_up, w2_down,
           g_final):
    nbp, seq, _ = x_prompt.shape
    nbs, dec_seq, _ = x_sample.shape
    depth = g_ffn1.shape[0]
    past = PAST_LEN
    cos_p, sin_p = _rope_tables(jnp.arange(seq, dtype=jnp.int32))
    cos_s, sin_s = _rope_tables(past + jnp.arange(dec_seq, dtype=jnp.int32))
    sample_pos_rows = min(ROW_TILE, nbs * dec_seq)
    cos_s = jnp.tile(cos_s, (sample_pos_rows // dec_seq, 1))
    sin_s = jnp.tile(sin_s, (sample_pos_rows // dec_seq, 1))

    xp = x_prompt.reshape(nbp * seq, D_MODEL)
    xs = x_sample.reshape(nbs * dec_seq, D_MODEL)
    g_fin = g_final.reshape(1, D_MODEL)
    row = lambda a: a.reshape(1, -1)
    outs = [[] for _ in range(10)]
    for l in range(depth):
        last = l == depth - 1
        bf = lambda a: a[l].astype(BF16)
        w1g, w1u, w1d, w2g, w2u, w2d = bf(w1_gate), bf(w1_up), bf(w1_down), bf(w2_gate), bf(w2_up), bf(w2_down)
        win, wout, wcq, wck, wcv, wco = bf(w_in), bf(w_out), bf(w_cq), bf(w_ck), bf(w_cv), bf(w_co)
        wab = jnp.concatenate([_block_diag(w_a[l]), _block_diag(w_i[l])], axis=1).astype(BF16)
        bab = jnp.concatenate([b_a[l], b_i[l]]).reshape(1, -1)
        lru_w = (conv_w[l], row(conv_b[l]), wab, bab, row(lam[l]), row(g_lru_out[l]))

        mk_p, mv_p = _mem_kv(mem_prompt.reshape(nbp * N_MEM, D_MODEL), row(g_mem[l]), wck, wcv)
        mk_p = mk_p.reshape(nbp, N_MEM, D_MODEL)
        mv_p = mv_p.reshape(nbp, N_MEM, D_MODEL)
        xp = _ffn(xp, row(g_ffn1[l]), w1g, w1u, w1d, g_fin, False)
        u, gate, q, k, v = _proj(xp, row(g_mix[l]), win, cos_p, sin_p)
        u3 = u.reshape(nbp, seq, LRU_WIDTH)
        k3 = k.reshape(nbp, seq, KV_WIDTH)
        v3 = v.reshape(nbp, seq, KV_WIDTH)
        lru_out, h_last = _lru_prompt(u3, gate.reshape(nbp, seq, LRU_WIDTH), *lru_w)
        attn_out = _swa_prompt(q.reshape(nbp, seq, Q_WIDTH), k3, v3, sink[l], row(g_attn_out[l]))
        xp = _mix_out(xp, lru_out.reshape(nbp * seq, LRU_WIDTH), attn_out.reshape(nbp * seq, Q_WIDTH), wout,
                      row(g_xattn[l]), wcq, mk_p, mv_p, wco, seq, 1)
        xp = _ffn(xp, row(g_ffn2[l]), w2g, w2u, w2d, g_fin, last)
        outs[0].append(mk_p.reshape(nbp, N_MEM, X_HEADS, X_HEAD_DIM))
        outs[1].append(mv_p.reshape(nbp, N_MEM, X_HEADS, X_HEAD_DIM))
        outs[2].append(k3[:, -WINDOW:].reshape(nbp, WINDOW, KV_HEADS, HEAD_DIM))
        outs[3].append(v3[:, -WINDOW:].reshape(nbp, WINDOW, KV_HEADS, HEAD_DIM))
        outs[4].append(u3[:, -(CONV_WIDTH - 1):])
        outs[5].append(h_last.reshape(nbp, LRU_WIDTH))

        xs = _ffn(xs, row(g_ffn1[l]), w1g, w1u, w1d, g_fin, False)
        u, gate, q, k, v = _proj(xs, row(g_mix[l]), win, cos_s, sin_s)
        conv_pad = jnp.pad(state_conv[l], ((0, 0), (dec_seq - (CONV_WIDTH - 1), 0), (0, 0)))
        h0_rep = jnp.repeat(state_lru_h[l], dec_seq, axis=0)
        lru_out, hs = _lru_sample(u, gate, conv_pad.reshape(nbs * dec_seq, LRU_WIDTH), h0_rep, dec_seq, *lru_w)
        attn_out, new_k, new_v = _swa_sample(q, k, v, cache_swa_k[l].reshape(nbs, WINDOW, KV_WIDTH),
                                             cache_swa_v[l].reshape(nbs, WINDOW, KV_WIDTH), sink[l],
                                             row(g_attn_out[l]), dec_seq, past)
        xs = _mix_out(xs, lru_out, attn_out, wout, row(g_xattn[l]), wcq,
                      cache_mem_k[l].reshape(nbs, N_MEM, D_MODEL), cache_mem_v[l].reshape(nbs, N_MEM, D_MODEL), wco,
                      dec_seq, min(XATTN_SEQ_TILE, nbs))
        xs = _ffn(xs, row(g_ffn2[l]), w2g, w2u, w2d, g_fin, last)
        outs[6].append(new_k.reshape(nbs, WINDOW, KV_HEADS, HEAD_DIM))
        outs[7].append(new_v.reshape(nbs, WINDOW, KV_HEADS, HEAD_DIM))
        outs[8].append(u.reshape(nbs, dec_seq, LRU_WIDTH)[:, -(CONV_WIDTH - 1):])
        outs[9].append(hs.reshape(nbs, dec_seq, LRU_WIDTH)[:, -1])

    return (xp.reshape(nbp, seq, D_MODEL), xs.reshape(nbs, dec_seq, D_MODEL)) + tuple(jnp.stack(o) for o in outs)
```

```python
import functools

import jax
import jax.numpy as jnp
from jax import lax
from jax.experimental import pallas as pl
from jax.experimental.pallas import tpu as pltpu

F32 = jnp.float32
BF16 = jnp.bfloat16

D_MODEL = 1024
LRU_WIDTH = 512
LRU_BLOCKS = 8
CONV_WIDTH = 4
LRU_C = 8.0
ATTN_HEADS = 8
HEAD_DIM = 64
KV_HEADS = 2
WINDOW = 128
PAST_LEN = 8192
ROPE_THETA = 10000.0
N_MEM = 256
X_HEADS = 4
X_HEAD_DIM = 256
D_FF = 2816
EPS = 1e-6
Q_WIDTH = ATTN_HEADS * HEAD_DIM
KV_WIDTH = KV_HEADS * HEAD_DIM
IN_COLS = 2 * LRU_WIDTH + Q_WIDTH + 2 * KV_WIDTH

LANES = 128
SUBLANES = 8
VMEM_LIMIT = 56 * 1024 * 1024

ROW_TILE = 512
LRU_TILE = 256
SAMPLE_SEQ_TILE = 8
XATTN_SEQ_TILE = 4


def _params(n_axes):
    return pltpu.CompilerParams(dimension_semantics=("arbitrary",) * n_axes, vmem_limit_bytes=VMEM_LIMIT)


def _const_spec(shape):
    return pl.BlockSpec(shape, lambda *_: (0,) * len(shape), pipeline_mode=pl.Buffered(1))


def _rms(x, g):
    return x * lax.rsqrt(jnp.mean(x * x, axis=-1, keepdims=True) + EPS) * g


def _dot(a, b):
    return jnp.dot(a, b, preferred_element_type=F32)


def _dot_nt(a, b):
    return lax.dot_general(a, b, (((1,), (1,)), ((), ())), preferred_element_type=F32)


def _ffn_tail(x, g_ref, wg_ref, wu_ref, wd_ref, gf_ref, o_ref, final_norm):
    xn = _rms(x, g_ref[...]).astype(BF16)
    gate = _dot(xn, wg_ref[...])
    up = _dot(xn, wu_ref[...])
    h = (gate * jax.nn.sigmoid(gate) * up).astype(BF16)
    y = x + 0.5 * _dot(h, wd_ref[...])
    if final_norm:
        y = _rms(y, gf_ref[...])
    o_ref[...] = y


def _ffn_body(x_ref, g_ref, wg_ref, wu_ref, wd_ref, gf_ref, o_ref, *, final_norm):
    _ffn_tail(x_ref[...], g_ref, wg_ref, wu_ref, wd_ref, gf_ref, o_ref, final_norm)


def _proj_ffn_body(x_ref, a_ref, wa_ref, g_ref, wg_ref, wu_ref, wd_ref, gf_ref, o_ref, *, final_norm):
    x = x_ref[...] + _dot(a_ref[...].astype(BF16), wa_ref[...])
    _ffn_tail(x, g_ref, wg_ref, wu_ref, wd_ref, gf_ref, o_ref, final_norm)


def _ffn(x, g, wg, wu, wd, g_final, final_norm, attn=None, w_attn=None):
    rows = x.shape[0]
    tm = min(ROW_TILE, rows)
    row_spec = pl.BlockSpec((tm, D_MODEL), lambda i: (i, 0))
    ffn_specs = [_const_spec((1, D_MODEL)), _const_spec((D_MODEL, D_FF)), _const_spec((D_MODEL, D_FF)),
                 _const_spec((D_FF, D_MODEL)), _const_spec((1, D_MODEL))]
    if attn is None:
        body, lead_specs, lead = _ffn_body, [row_spec], (x,)
    else:
        body, lead_specs, lead = _proj_ffn_body, [row_spec, row_spec, _const_spec((D_MODEL, D_MODEL))], (x, attn, w_attn)
    return pl.pallas_call(
        functools.partial(body, final_norm=final_norm),
        grid=(rows // tm,),
        in_specs=lead_specs + ffn_specs,
        out_specs=row_spec,
        out_shape=jax.ShapeDtypeStruct((rows, D_MODEL), F32),
        compiler_params=_params(1),
    )(*lead, g, wg, wu, wd, g_final)


def _rope(z, cos, sin_signed):
    half = HEAD_DIM // 2
    lane = lax.broadcasted_iota(jnp.int32, z.shape, 1)
    first_half = (lane % HEAD_DIM) < half
    partner = jnp.where(first_half, pltpu.roll(z, LANES - half, axis=1), pltpu.roll(z, half, axis=1))
    return z * cos + partner * sin_signed


def _proj_body(x_ref, g_ref, w_ref, cos_ref, sin_ref, u_ref, gate_ref, q_ref, k_ref, v_ref):
    xn = _rms(x_ref[...], g_ref[...]).astype(BF16)
    cos = cos_ref[...]
    sin = sin_ref[...]
    o_gate, o_q, o_k, o_v = LRU_WIDTH, 2 * LRU_WIDTH, 2 * LRU_WIDTH + Q_WIDTH, 2 * LRU_WIDTH + Q_WIDTH + KV_WIDTH
    u_ref[...] = _dot(xn, w_ref[:, :o_gate])
    gate_ref[...] = _dot(xn, w_ref[:, o_gate:o_q])
    q = _dot(xn, w_ref[:, o_q:o_k])
    for j in range(Q_WIDTH // LANES):
        q_ref[:, j * LANES:(j + 1) * LANES] = _rope(q[:, j * LANES:(j + 1) * LANES], cos, sin)
    k_ref[...] = _rope(_dot(xn, w_ref[:, o_k:o_v]), cos, sin)
    v_ref[...] = _dot(xn, w_ref[:, o_v:])


def _proj(x, g, w_in, cos, sin):
    rows = x.shape[0]
    tm = min(ROW_TILE, rows, cos.shape[0])
    pos_blocks = cos.shape[0] // tm

    def row_spec(width):
        return pl.BlockSpec((tm, width), lambda i: (i, 0))

    pos_spec = pl.BlockSpec((tm, LANES), lambda i: (i % pos_blocks, 0))
    widths = (LRU_WIDTH, LRU_WIDTH, Q_WIDTH, KV_WIDTH, KV_WIDTH)
    return pl.pallas_call(
        _proj_body,
        grid=(rows // tm,),
        in_specs=[row_spec(D_MODEL), _const_spec((1, D_MODEL)), _const_spec((D_MODEL, IN_COLS)), pos_spec, pos_spec],
        out_specs=[row_spec(w) for w in widths],
        out_shape=[jax.ShapeDtypeStruct((rows, w), F32) for w in widths],
        compiler_params=_params(1),
    )(x, g, w_in, cos, sin)


def _rope_tables(pos):
    half = HEAD_DIM // 2
    inv = ROPE_THETA ** (-jnp.arange(half, dtype=F32) / half)
    ang = pos.astype(F32)[:, None] * inv[None, :]
    cos = jnp.cos(ang)
    sin = jnp.sin(ang)
    reps = LANES // HEAD_DIM
    return (jnp.tile(jnp.concatenate([cos, cos], axis=-1), (1, reps)),
            jnp.tile(jnp.concatenate([-sin, sin], axis=-1), (1, reps)))


def _softplus(x):
    return jnp.maximum(x, 0.0) + jnp.log1p(jnp.exp(-jnp.abs(x)))


def _lru_coeffs(conv, wab_ref, bab_ref, lam_ref):
    gates = _dot(conv.astype(BF16), wab_ref[...]) + bab_ref[...]
    r = jax.nn.sigmoid(gates[:, :LRU_WIDTH])
    gi = jax.nn.sigmoid(gates[:, LRU_WIDTH:])
    log_a = -LRU_C * r * _softplus(-lam_ref[...])
    a = jnp.exp(log_a)
    b = jnp.sqrt(-jnp.tanh(log_a) * (a * a + 1.0)) * (gi * conv)
    return a, b


def _segment_scan(a, b, seg):
    row = lax.broadcasted_iota(jnp.int32, a.shape, 0)
    pos = row % seg
    step = 1
    while step < seg:
        a_prev = pltpu.roll(a, step, axis=0)
        b_prev = pltpu.roll(b, step, axis=0)
        live = pos >= step
        b = jnp.where(live, a * b_prev + b, b)
        a = jnp.where(live, a * a_prev, a)
        step *= 2
    return a, b


def _lru_finish(h, gate, g_out):
    return _rms(h * jax.nn.gelu(gate), g_out)


def _lru_prompt_body(u_ref, gate_ref, cw_ref, cb_ref, wab_ref, bab_ref, lam_ref, gout_ref,
                     o_ref, hlast_ref, ext_ref, h_ref, hs_ref, *, tt):
    pad = SUBLANES

    @pl.when(pl.program_id(1) == 0)
    def _():
        ext_ref[0:pad, :] = jnp.zeros((pad, LRU_WIDTH), F32)
        h_ref[...] = jnp.zeros_like(h_ref)

    ext_ref[pad:pad + tt, :] = u_ref[0]
    conv = cb_ref[...]
    for j in range(CONV_WIDTH):
        start = pad - (CONV_WIDTH - 1) + j
        conv = conv + ext_ref[start:start + tt, :] * cw_ref[j:j + 1, :]
    ext_ref[0:pad, :] = ext_ref[tt:tt + pad, :]

    a, b = _lru_coeffs(conv, wab_ref, bab_ref, lam_ref)
    a_cum, h_local = _segment_scan(a, b, tt)
    hs_ref[...] = a_cum * h_ref[...] + h_local
    h_ref[...] = hs_ref[tt - 1:tt, :]
    hlast_ref[0] = h_ref[...]
    o_ref[0] = _lru_finish(hs_ref[...], gate_ref[0], gout_ref[...])


def _lru_prompt(u, gate, conv_w, conv_b, wab, bab, lam, g_out):
    n, t, _ = u.shape
    tt = min(LRU_TILE, t)
    seq_spec = pl.BlockSpec((1, tt, LRU_WIDTH), lambda i, j: (i, j, 0))
    return pl.pallas_call(
        functools.partial(_lru_prompt_body, tt=tt),
        grid=(n, t // tt),
        in_specs=[seq_spec, seq_spec, _const_spec((CONV_WIDTH, LRU_WIDTH)), _const_spec((1, LRU_WIDTH)),
                  _const_spec((LRU_WIDTH, 2 * LRU_WIDTH)), _const_spec((1, 2 * LRU_WIDTH)),
                  _const_spec((1, LRU_WIDTH)), _const_spec((1, LRU_WIDTH))],
        out_specs=[seq_spec, pl.BlockSpec((1, 1, LRU_WIDTH), lambda i, j: (i, 0, 0))],
        out_shape=[jax.ShapeDtypeStruct((n, t, LRU_WIDTH), F32), jax.ShapeDtypeStruct((n, 1, LRU_WIDTH), F32)],
        scratch_shapes=[pltpu.VMEM((tt + SUBLANES, LRU_WIDTH), F32), pltpu.VMEM((1, LRU_WIDTH), F32),
                        pltpu.VMEM((tt, LRU_WIDTH), F32)],
        compiler_params=_params(2),
    )(u, gate, conv_w, conv_b, wab, bab, lam, g_out)


def _lru_sample_body(u_ref, gate_ref, cpad_ref, h0_ref, cw_ref, cb_ref, wab_ref, bab_ref, lam_ref, gout_ref,
                     o_ref, hs_ref, *, seg):
    u = u_ref[...]
    cpad = cpad_ref[...]
    rows = u.shape[0]
    pos = lax.broadcasted_iota(jnp.int32, u.shape, 0) % seg
    conv = cb_ref[...] + u * cw_ref[CONV_WIDTH - 1:CONV_WIDTH, :]
    for back in range(1, CONV_WIDTH):
        shifted = jnp.where(pos >= back, pltpu.roll(u, back, axis=0),
                            pltpu.roll(cpad, (back - seg) % rows, axis=0))
        conv = conv + shifted * cw_ref[CONV_WIDTH - 1 - back:CONV_WIDTH - back, :]
    a, b = _lru_coeffs(conv, wab_ref, bab_ref, lam_ref)
    a_cum, h_local = _segment_scan(a, b, seg)
    h = a_cum * h0_ref[...] + h_local
    hs_ref[...] = h
    o_ref[...] = _lru_finish(h, gate_ref[...], gout_ref[...])


def _lru_sample(u, gate, conv_pad, h0_rep, seg, conv_w, conv_b, wab, bab, lam, g_out):
    rows = u.shape[0]
    tm = min(ROW_TILE, rows)
    row_spec = pl.BlockSpec((tm, LRU_WIDTH), lambda i: (i, 0))
    return pl.pallas_call(
        functools.partial(_lru_sample_body, seg=seg),
        grid=(rows // tm,),
        in_specs=[row_spec, row_spec, row_spec, row_spec, _const_spec((CONV_WIDTH, LRU_WIDTH)),
                  _const_spec((1, LRU_WIDTH)), _const_spec((LRU_WIDTH, 2 * LRU_WIDTH)),
                  _const_spec((1, 2 * LRU_WIDTH)), _const_spec((1, LRU_WIDTH)), _const_spec((1, LRU_WIDTH))],
        out_specs=[row_spec, row_spec],
        out_shape=[jax.ShapeDtypeStruct((rows, LRU_WIDTH), F32)] * 2,
        compiler_params=_params(1),
    )(u, gate, conv_pad, h0_rep, conv_w, conv_b, wab, bab, lam, g_out)


def _block_diag(w):
    nb, bi, bj = w.shape
    eye = jnp.eye(nb, dtype=w.dtype)
    return jnp.einsum('gij,gh->gihj', w, eye).reshape(nb * bi, nb * bj)


def _bdot_nt(a, b):
    return lax.dot_general(a, b, (((2,), (2,)), ((0,), (0,))), preferred_element_type=F32)


def _bdot(a, b):
    return lax.dot_general(a, b, (((2,), (1,)), ((0,), (0,))), preferred_element_type=F32)


def _swa_core(q, keys, vals, sink_ref, mask, g_out):
    n, r, _ = q.shape
    lane = lax.broadcasted_iota(jnp.int32, (n, r, LANES), 2)
    low = lane < HEAD_DIM
    zero = jnp.zeros((n, r, LANES), F32)
    slabs = [q[:, :, j * LANES:(j + 1) * LANES] for j in range(Q_WIDTH // LANES)]
    group = ATTN_HEADS // KV_HEADS
    straight, swapped = [], []
    for h in range(ATTN_HEADS):
        (straight if (h % 2) == (h // group) else swapped).append(h)
    scale = HEAD_DIM ** -0.5
    out_half = {}
    for heads, kk, vv in ((straight, keys, vals),
                          (swapped, pltpu.roll(keys, HEAD_DIM, axis=2), pltpu.roll(vals, HEAD_DIM, axis=2))):
        qs = jnp.concatenate([jnp.where(low if h % 2 == 0 else ~low, slabs[h // 2], zero) for h in heads], axis=1)
        s = _bdot_nt(qs.astype(BF16), kk.astype(BF16)) * scale
        probs = []
        for i, h in enumerate(heads):
            sh = jnp.where(mask[None], s[:, i * r:(i + 1) * r, :], -jnp.inf)
            sink = sink_ref[h]
            m = jnp.maximum(jnp.max(sh, axis=-1, keepdims=True), sink)
            e = jnp.exp(sh - m)
            denom = jnp.sum(e, axis=-1, keepdims=True) + jnp.exp(sink - m)
            probs.append(e / denom)
        o = _bdot(jnp.concatenate(probs, axis=1).astype(BF16), vv.astype(BF16))
        for i, h in enumerate(heads):
            out_half[h] = o[:, i * r:(i + 1) * r, :]
    out = jnp.concatenate([jnp.where(low, out_half[2 * j], out_half[2 * j + 1]) for j in range(Q_WIDTH // LANES)],
                          axis=2)
    return _rms(out, g_out)


def _swa_prompt_body(sink_ref, q_ref, kp_ref, kc_ref, vp_ref, vc_ref, gout_ref, o_ref):
    blk = pl.program_id(1)
    keys = jnp.concatenate([kp_ref[...], kc_ref[...]], axis=1)
    vals = jnp.concatenate([vp_ref[...], vc_ref[...]], axis=1)
    i = lax.broadcasted_iota(jnp.int32, (WINDOW, 2 * WINDOW), 0)
    j = lax.broadcasted_iota(jnp.int32, (WINDOW, 2 * WINDOW), 1)
    dist = i + WINDOW - j
    mask = (dist >= 0) & (dist < WINDOW) & ((blk > 0) | (j >= WINDOW))
    o_ref[...] = _swa_core(q_ref[...], keys, vals, sink_ref, mask, gout_ref[...])


def _swa_prompt(q, k, v, sink, g_out):
    n, t, _ = q.shape
    cur = lambda i, j: (i, j, 0)
    prev = lambda i, j: (i, jnp.maximum(j - 1, 0), 0)
    kv_block = (1, WINDOW, KV_WIDTH)
    return pl.pallas_call(
        _swa_prompt_body,
        grid=(n, t // WINDOW),
        in_specs=[pl.BlockSpec(memory_space=pltpu.SMEM), pl.BlockSpec((1, WINDOW, Q_WIDTH), cur),
                  pl.BlockSpec(kv_block, prev), pl.BlockSpec(kv_block, cur),
                  pl.BlockSpec(kv_block, prev), pl.BlockSpec(kv_block, cur), _const_spec((1, Q_WIDTH))],
        out_specs=pl.BlockSpec((1, WINDOW, Q_WIDTH), cur),
        out_shape=jax.ShapeDtypeStruct((n, t, Q_WIDTH), F32),
        compiler_params=_params(2),
    )(sink, q, k, k, v, v, g_out)


def _swa_sample_body(sink_ref, q_ref, k_ref, v_ref, bk_ref, bv_ref, gout_ref, o_ref, nk_ref, nv_ref, *, s_len, past):
    qp = past + lax.broadcasted_iota(jnp.int32, (s_len, WINDOW + s_len), 0)
    col = lax.broadcasted_iota(jnp.int32, (s_len, WINDOW + s_len), 1)
    kp = jnp.where(col < WINDOW, past - WINDOW + col, past + col - WINDOW)
    dist = qp - kp
    mask = (dist >= 0) & (dist < WINDOW) & (kp >= 0)
    keys = jnp.concatenate([bk_ref[...], k_ref[...]], axis=1)
    vals = jnp.concatenate([bv_ref[...], v_ref[...]], axis=1)
    o_ref[...] = _swa_core(q_ref[...], keys, vals, sink_ref, mask, gout_ref[...])
    nk_ref[...] = keys[:, s_len:, :]
    nv_ref[...] = vals[:, s_len:, :]


def _swa_sample(q, k, v, buf_k, buf_v, sink, g_out, past):
    n, s_len, _ = q.shape
    sb = min(SAMPLE_SEQ_TILE, n)

    def seq_spec(rows, width):
        return pl.BlockSpec((sb, rows, width), lambda i: (i, 0, 0))

    buf_spec = seq_spec(WINDOW, KV_WIDTH)
    return pl.pallas_call(
        functools.partial(_swa_sample_body, s_len=s_len, past=past),
        grid=(n // sb,),
        in_specs=[pl.BlockSpec(memory_space=pltpu.SMEM), seq_spec(s_len, Q_WIDTH), seq_spec(s_len, KV_WIDTH),
                  seq_spec(s_len, KV_WIDTH), buf_spec, buf_spec, _const_spec((1, Q_WIDTH))],
        out_specs=[seq_spec(s_len, Q_WIDTH), buf_spec, buf_spec],
        out_shape=[jax.ShapeDtypeStruct((n, s_len, Q_WIDTH), F32),
                   jax.ShapeDtypeStruct((n, WINDOW, KV_WIDTH), F32), jax.ShapeDtypeStruct((n, WINDOW, KV_WIDTH), F32)],
        compiler_params=_params(1),
    )(sink, q, k, v, buf_k, buf_v, g_out)


def _softmax(s):
    e = jnp.exp(s - jnp.max(s, axis=-1, keepdims=True))
    return e / jnp.sum(e, axis=-1, keepdims=True)


def _mix_and_query(x_ref, lru_ref, attn_ref, wout_ref, gx_ref, wcq_ref):
    x = (x_ref[...] + _dot(lru_ref[...].astype(BF16), wout_ref[:LRU_WIDTH, :])
         + _dot(attn_ref[...].astype(BF16), wout_ref[LRU_WIDTH:, :]))
    return x, _dot(_rms(x, gx_ref[...]).astype(BF16), wcq_ref[...])


def _mix_out_body(x_ref, lru_ref, attn_ref, wout_ref, gx_ref, wcq_ref, mk_ref, mv_ref, wco_ref, o_ref):
    x, q = _mix_and_query(x_ref, lru_ref, attn_ref, wout_ref, gx_ref, wcq_ref)
    scale = X_HEAD_DIM ** -0.5
    heads = []
    for h in range(X_HEADS):
        cols = slice(h * X_HEAD_DIM, (h + 1) * X_HEAD_DIM)
        s = _dot_nt(q[:, cols].astype(BF16), mk_ref[0, :, cols].astype(BF16)) * scale
        heads.append(_dot(_softmax(s).astype(BF16), mv_ref[0, :, cols].astype(BF16)))
    o_ref[...] = x + _dot(jnp.concatenate(heads, axis=1).astype(BF16), wco_ref[...])


def _mix_out(x, lru_out, attn_out, w_out, g_x, w_cq, mk, mv, w_co, seq_rows):
    rows = x.shape[0]
    tm = min(ROW_TILE, seq_rows)
    per_seq = seq_rows // tm

    def row_spec(width):
        return pl.BlockSpec((tm, width), lambda i: (i, 0))

    mem_spec = pl.BlockSpec((1, N_MEM, D_MODEL), lambda i: (i // per_seq, 0, 0))
    w_spec = _const_spec((D_MODEL, D_MODEL))
    return pl.pallas_call(
        _mix_out_body,
        grid=(rows // tm,),
        in_specs=[row_spec(D_MODEL), row_spec(LRU_WIDTH), row_spec(Q_WIDTH), w_spec, _const_spec((1, D_MODEL)),
                  w_spec, mem_spec, mem_spec, w_spec],
        out_specs=row_spec(D_MODEL),
        out_shape=jax.ShapeDtypeStruct((rows, D_MODEL), F32),
        compiler_params=_params(1),
    )(x, lru_out, attn_out, w_out, g_x, w_cq, mk, mv, w_co)


def _mix_query_body(x_ref, lru_ref, attn_ref, wout_ref, gx_ref, wcq_ref, x_out_ref, q_ref):
    x_out_ref[...], q_ref[...] = _mix_and_query(x_ref, lru_ref, attn_ref, wout_ref, gx_ref, wcq_ref)


def _mix_query(x, lru_out, attn_out, w_out, g_x, w_cq):
    rows = x.shape[0]
    tm = min(ROW_TILE, rows)

    def row_spec(width):
        return pl.BlockSpec((tm, width), lambda i: (i, 0))

    w_spec = _const_spec((D_MODEL, D_MODEL))
    return pl.pallas_call(
        _mix_query_body,
        grid=(rows // tm,),
        in_specs=[row_spec(D_MODEL), row_spec(LRU_WIDTH), row_spec(Q_WIDTH), w_spec, _const_spec((1, D_MODEL)), w_spec],
        out_specs=[row_spec(D_MODEL), row_spec(D_MODEL)],
        out_shape=[jax.ShapeDtypeStruct((rows, D_MODEL), F32)] * 2,
        compiler_params=_params(1),
    )(x, lru_out, attn_out, w_out, g_x, w_cq)


def _xattn_cache_body(q_ref, mk_ref, mv_ref, o_ref):
    sb, steps, _ = q_ref.shape
    blocks = D_MODEL // LANES
    chunks = X_HEAD_DIM // LANES
    width = mk_ref.shape[1]
    cls = lax.broadcasted_iota(jnp.int32, (sb, steps, width), 2) % blocks
    block_cls = [(j % chunks) * X_HEADS + j // chunks for j in range(blocks)]
    scale = X_HEAD_DIM ** -0.5

    qs = jnp.concatenate([q_ref[:, :, j * LANES:(j + 1) * LANES] for j in range(blocks)], axis=1)
    s = _bdot_nt(qs.astype(BF16), mk_ref[...].astype(BF16)) * scale
    part = jnp.zeros((sb, steps, width), F32)
    for j in range(blocks):
        part = part + jnp.where(cls == block_cls[j], s[:, j * steps:(j + 1) * steps, :], 0.0)
    score = part + pltpu.roll(part, width - X_HEADS, axis=2)
    top = jnp.zeros((sb, steps, width), F32)
    for h in range(X_HEADS):
        mine = cls == h
        top = jnp.where(mine, jnp.max(jnp.where(mine, score, -jnp.inf), axis=-1, keepdims=True), top)
    e = jnp.where(cls < X_HEADS, jnp.exp(score - top), 0.0)
    denom = jnp.ones((sb, steps, width), F32)
    for h in range(X_HEADS):
        mine = cls == h
        denom = jnp.where(mine, jnp.sum(jnp.where(mine, e, 0.0), axis=-1, keepdims=True), denom)
    p = e / denom
    p = p + pltpu.roll(p, X_HEADS, axis=2)
    ps = jnp.concatenate([jnp.where(cls == block_cls[j], p, 0.0) for j in range(blocks)], axis=1)
    o = _bdot(ps.astype(BF16), mv_ref[...].astype(BF16))
    for j in range(blocks):
        o_ref[:, :, j * LANES:(j + 1) * LANES] = o[:, j * steps:(j + 1) * steps, :]


def _interleave_chunks(cache):
    n = cache.shape[0]
    chunks = X_HEAD_DIM // LANES
    c = cache.reshape(n, N_MEM, X_HEADS, chunks, LANES)
    return jnp.transpose(c, (0, 1, 3, 2, 4)).reshape(n, N_MEM * chunks * X_HEADS, LANES)


def _xattn_cache(q, cache_k, cache_v):
    n, s_len, _ = q.shape
    sb = min(XATTN_SEQ_TILE, n)
    q_spec = pl.BlockSpec((sb, s_len, D_MODEL), lambda i: (i, 0, 0))
    rows = cache_k.shape[1]
    mem_spec = pl.BlockSpec((sb, rows, LANES), lambda i: (i, 0, 0))
    return pl.pallas_call(
        _xattn_cache_body,
        grid=(n // sb,),
        in_specs=[q_spec, mem_spec, mem_spec],
        out_specs=q_spec,
        out_shape=jax.ShapeDtypeStruct((n, s_len, D_MODEL), F32),
        compiler_params=_params(1),
    )(q, cache_k, cache_v)


def _mem_kv_body(mem_ref, g_ref, wk_ref, wv_ref, k_ref, v_ref):
    mm = _rms(mem_ref[...], g_ref[...]).astype(BF16)
    k_ref[...] = _dot(mm, wk_ref[...])
    v_ref[...] = _dot(mm, wv_ref[...])


def _mem_kv(mem, g, w_ck, w_cv):
    rows = mem.shape[0]
    tm = min(ROW_TILE, rows)
    row_spec = pl.BlockSpec((tm, D_MODEL), lambda i: (i, 0))
    w_spec = _const_spec((D_MODEL, D_MODEL))
    return pl.pallas_call(
        _mem_kv_body,
        grid=(rows // tm,),
        in_specs=[row_spec, _const_spec((1, D_MODEL)), w_spec, w_spec],
        out_specs=[row_spec, row_spec],
        out_shape=[jax.ShapeDtypeStruct((rows, D_MODEL), F32)] * 2,
        compiler_params=_params(1),
    )(mem, g, w_ck, w_cv)


def kernel(x_prompt, x_sample, mem_prompt, cache_mem_k, cache_mem_v, cache_swa_k, cache_swa_v, state_conv, state_lru_h,
           g_ffn1, w1_gate, w1_up, w1_down, g_mix, w_in, conv_w, conv_b, w_a, b_a, w_i, b_i, lam, sink,
           g_lru_out, g_attn_out, w_out, g_xattn, g_mem, w_cq, w_ck, w_cv, w_co, g_ffn2, w2_gate, w2_up, w2_down,
           g_final):
    nbp, seq, _ = x_prompt.shape
    nbs, dec_seq, _ = x_sample.shape
    depth = g_ffn1.shape[0]
    past = PAST_LEN
    cos_p, sin_p = _rope_tables(jnp.arange(seq, dtype=jnp.int32))
    cos_s, sin_s = _rope_tables(past + jnp.arange(dec_seq, dtype=jnp.int32))
    sample_pos_rows = min(ROW_TILE, nbs * dec_seq)
    cos_s = jnp.tile(cos_s, (sample_pos_rows // dec_seq, 1))
    sin_s = jnp.tile(sin_s, (sample_pos_rows // dec_seq, 1))

    xp = x_prompt.reshape(nbp * seq, D_MODEL)
    xs = x_sample.reshape(nbs * dec_seq, D_MODEL)
    g_fin = g_final.reshape(1, D_MODEL)
    row = lambda a: a.reshape(1, -1)
    outs = [[] for _ in range(10)]
    for l in range(depth):
        last = l == depth - 1
        bf = lambda a: a[l].astype(BF16)
        w1g, w1u, w1d, w2g, w2u, w2d = bf(w1_gate), bf(w1_up), bf(w1_down), bf(w2_gate), bf(w2_up), bf(w2_down)
        win, wout, wcq, wck, wcv, wco = bf(w_in), bf(w_out), bf(w_cq), bf(w_ck), bf(w_cv), bf(w_co)
        wab = jnp.concatenate([_block_diag(w_a[l]), _block_diag(w_i[l])], axis=1).astype(BF16)
        bab = jnp.concatenate([b_a[l], b_i[l]]).reshape(1, -1)
        lru_w = (conv_w[l], row(conv_b[l]), wab, bab, row(lam[l]), row(g_lru_out[l]))

        mk_p, mv_p = _mem_kv(mem_prompt.reshape(nbp * N_MEM, D_MODEL), row(g_mem[l]), wck, wcv)
        mk_p = mk_p.reshape(nbp, N_MEM, D_MODEL)
        mv_p = mv_p.reshape(nbp, N_MEM, D_MODEL)
        xp = _ffn(xp, row(g_ffn1[l]), w1g, w1u, w1d, g_fin, False)
        u, gate, q, k, v = _proj(xp, row(g_mix[l]), win, cos_p, sin_p)
        u3 = u.reshape(nbp, seq, LRU_WIDTH)
        k3 = k.reshape(nbp, seq, KV_WIDTH)
        v3 = v.reshape(nbp, seq, KV_WIDTH)
        lru_out, h_last = _lru_prompt(u3, gate.reshape(nbp, seq, LRU_WIDTH), *lru_w)
        attn_out = _swa_prompt(q.reshape(nbp, seq, Q_WIDTH), k3, v3, sink[l], row(g_attn_out[l]))
        xp = _mix_out(xp, lru_out.reshape(nbp * seq, LRU_WIDTH), attn_out.reshape(nbp * seq, Q_WIDTH), wout,
                      row(g_xattn[l]), wcq, mk_p, mv_p, wco, seq)
        xp = _ffn(xp, row(g_ffn2[l]), w2g, w2u, w2d, g_fin, last)
        outs[0].append(mk_p.reshape(nbp, N_MEM, X_HEADS, X_HEAD_DIM))
        outs[1].append(mv_p.reshape(nbp, N_MEM, X_HEADS, X_HEAD_DIM))
        outs[2].append(k3[:, -WINDOW:].reshape(nbp, WINDOW, KV_HEADS, HEAD_DIM))
        outs[3].append(v3[:, -WINDOW:].reshape(nbp, WINDOW, KV_HEADS, HEAD_DIM))
        outs[4].append(u3[:, -(CONV_WIDTH - 1):])
        outs[5].append(h_last.reshape(nbp, LRU_WIDTH))

        xs = _ffn(xs, row(g_ffn1[l]), w1g, w1u, w1d, g_fin, False)
        u, gate, q, k, v = _proj(xs, row(g_mix[l]), win, cos_s, sin_s)
        conv_pad = jnp.pad(state_conv[l], ((0, 0), (dec_seq - (CONV_WIDTH - 1), 0), (0, 0)))
        h0_rep = jnp.repeat(state_lru_h[l], dec_seq, axis=0)
        lru_out, hs = _lru_sample(u, gate, conv_pad.reshape(nbs * dec_seq, LRU_WIDTH), h0_rep, dec_seq, *lru_w)
        per_seq = lambda a: a.reshape(nbs, dec_seq, a.shape[-1])
        attn_out, new_k, new_v = _swa_sample(per_seq(q), per_seq(k), per_seq(v),
                                             cache_swa_k[l].reshape(nbs, WINDOW, KV_WIDTH),
                                             cache_swa_v[l].reshape(nbs, WINDOW, KV_WIDTH), sink[l],
                                             row(g_attn_out[l]), past)
        xs, xq = _mix_query(xs, lru_out, attn_out.reshape(nbs * dec_seq, Q_WIDTH), wout, row(g_xattn[l]), wcq)
        xo = _xattn_cache(per_seq(xq), _interleave_chunks(cache_mem_k[l]), _interleave_chunks(cache_mem_v[l]))
        xs = _ffn(xs, row(g_ffn2[l]), w2g, w2u, w2d, g_fin, last, xo.reshape(nbs * dec_seq, D_MODEL), wco)
        outs[6].append(new_k.reshape(nbs, WINDOW, KV_HEADS, HEAD_DIM))
        outs[7].append(new_v.reshape(nbs, WINDOW, KV_HEADS, HEAD_DIM))
        outs[8].append(u.reshape(nbs, dec_seq, LRU_WIDTH)[:, -(CONV_WIDTH - 1):])
        outs[9].append(hs.reshape(nbs, dec_seq, LRU_WIDTH)[:, -1])

    return (xp.reshape(nbp, seq, D_MODEL), xs.reshape(nbs, dec_seq, D_MODEL)) + tuple(jnp.stack(o) for o in outs)
```

```python
import functools

import jax
import jax.numpy as jnp
from jax import lax
from jax.experimental import pallas as pl
from jax.experimental.pallas import tpu as pltpu

F32 = jnp.float32
BF16 = jnp.bfloat16

D_MODEL = 1024
LRU_WIDTH = 512
LRU_BLOCKS = 8
CONV_WIDTH = 4
LRU_C = 8.0
ATTN_HEADS = 8
HEAD_DIM = 64
KV_HEADS = 2
WINDOW = 128
PAST_LEN = 8192
ROPE_THETA = 10000.0
N_MEM = 256
X_HEADS = 4
X_HEAD_DIM = 256
D_FF = 2816
EPS = 1e-6
Q_WIDTH = ATTN_HEADS * HEAD_DIM
KV_WIDTH = KV_HEADS * HEAD_DIM
IN_COLS = 2 * LRU_WIDTH + Q_WIDTH + 2 * KV_WIDTH

LANES = 128
SUBLANES = 8
VMEM_LIMIT = 56 * 1024 * 1024

ROW_TILE = 512
LRU_TILE = 256
FFN_CHUNK = 256
SAMPLE_SEQ_TILE = 8
XATTN_SEQ_TILE = 4


def _params(n_axes):
    return pltpu.CompilerParams(dimension_semantics=("arbitrary",) * n_axes, vmem_limit_bytes=VMEM_LIMIT)


def _const_spec(shape):
    return pl.BlockSpec(shape, lambda *_: (0,) * len(shape), pipeline_mode=pl.Buffered(1))


def _rms(x, g):
    return x * lax.rsqrt(jnp.mean(x * x, axis=-1, keepdims=True) + EPS) * g


def _dot(a, b):
    return jnp.dot(a, b, preferred_element_type=F32)


def _dot_nt(a, b):
    return lax.dot_general(a, b, (((1,), (1,)), ((), ())), preferred_element_type=F32)


def _ffn_step(x, g_ref, wg_ref, wu_ref, wd_ref):
    xn = _rms(x, g_ref[...]).astype(BF16)
    gate = _dot(xn, wg_ref[...])
    up = _dot(xn, wu_ref[...])
    h = (gate * jax.nn.sigmoid(gate) * up).astype(BF16)
    return x + 0.5 * _dot(h, wd_ref[...])


def _ffn_tail(x, g_ref, wg_ref, wu_ref, wd_ref, gf_ref, o_ref, final_norm):
    y = _ffn_step(x, g_ref, wg_ref, wu_ref, wd_ref)
    if final_norm:
        y = _rms(y, gf_ref[...])
    o_ref[...] = y


def _ffn_body(x_ref, g_ref, wg_ref, wu_ref, wd_ref, gf_ref, o_ref, *, final_norm):
    _ffn_tail(x_ref[...], g_ref, wg_ref, wu_ref, wd_ref, gf_ref, o_ref, final_norm)


def _proj_ffn_body(x_ref, a_ref, wa_ref, g_ref, wg_ref, wu_ref, wd_ref, gf_ref, o_ref, *, final_norm):
    x = x_ref[...] + _dot(a_ref[...].astype(BF16), wa_ref[...])
    _ffn_tail(x, g_ref, wg_ref, wu_ref, wd_ref, gf_ref, o_ref, final_norm)


def _ffn(x, g, wg, wu, wd, g_final, final_norm, attn=None, w_attn=None):
    rows = x.shape[0]
    tm = min(ROW_TILE, rows)
    row_spec = pl.BlockSpec((tm, D_MODEL), lambda i: (i, 0))
    ffn_specs = [_const_spec((1, D_MODEL)), _const_spec((D_MODEL, D_FF)), _const_spec((D_MODEL, D_FF)),
                 _const_spec((D_FF, D_MODEL)), _const_spec((1, D_MODEL))]
    if attn is None:
        body, lead_specs, lead = _ffn_body, [row_spec], (x,)
    else:
        body, lead_specs, lead = _proj_ffn_body, [row_spec, row_spec, _const_spec((D_MODEL, D_MODEL))], (x, attn, w_attn)
    return pl.pallas_call(
        functools.partial(body, final_norm=final_norm),
        grid=(rows // tm,),
        in_specs=lead_specs + ffn_specs,
        out_specs=row_spec,
        out_shape=jax.ShapeDtypeStruct((rows, D_MODEL), F32),
        compiler_params=_params(1),
    )(*lead, g, wg, wu, wd, g_final)


def _rope(z, cos, sin_signed):
    half = HEAD_DIM // 2
    lane = lax.broadcasted_iota(jnp.int32, z.shape, 1)
    first_half = (lane % HEAD_DIM) < half
    partner = jnp.where(first_half, pltpu.roll(z, LANES - half, axis=1), pltpu.roll(z, half, axis=1))
    return z * cos + partner * sin_signed


def _project_pieces(x, g_ref, w_ref, cos_ref, sin_ref, store_u, store_gate, q_ref, k_ref, v_ref):
    o_gate, o_q, o_k, o_v = LRU_WIDTH, 2 * LRU_WIDTH, 2 * LRU_WIDTH + Q_WIDTH, 2 * LRU_WIDTH + Q_WIDTH + KV_WIDTH
    xn = []

    def normed():
        if not xn:
            xn.append(_rms(x(), g_ref[...]).astype(BF16))
        return xn[0]

    def rope_into(ref, z):
        for j in range(z.shape[1] // LANES):
            cols = slice(j * LANES, (j + 1) * LANES)
            ref[:, cols] = _rope(z[:, cols], cos_ref[...], sin_ref[...])

    def store_v(z):
        v_ref[...] = z

    return [lambda: rope_into(q_ref, _dot(normed(), w_ref[:, o_q:o_k])),
            lambda: rope_into(k_ref, _dot(normed(), w_ref[:, o_k:o_v])),
            lambda: store_v(_dot(normed(), w_ref[:, o_v:])),
            lambda: store_u(_dot(normed(), w_ref[:, :o_gate])),
            lambda: store_gate(_dot(normed(), w_ref[:, o_gate:o_q]))]


def _proj_body(x_ref, g_ref, w_ref, cos_ref, sin_ref, u_ref, gate_ref, q_ref, k_ref, v_ref):
    def store_u(z):
        u_ref[...] = z

    def store_gate(z):
        gate_ref[...] = z

    for piece in _project_pieces(lambda: x_ref[...], g_ref, w_ref, cos_ref, sin_ref, store_u, store_gate,
                                 q_ref, k_ref, v_ref):
        piece()


def _proj(x, g, w_in, cos, sin):
    rows = x.shape[0]
    tm = min(ROW_TILE, rows, cos.shape[0])
    pos_blocks = cos.shape[0] // tm

    def row_spec(width):
        return pl.BlockSpec((tm, width), lambda i: (i, 0))

    pos_spec = pl.BlockSpec((tm, LANES), lambda i: (i % pos_blocks, 0))
    widths = (LRU_WIDTH, LRU_WIDTH, Q_WIDTH, KV_WIDTH, KV_WIDTH)
    return pl.pallas_call(
        _proj_body,
        grid=(rows // tm,),
        in_specs=[row_spec(D_MODEL), _const_spec((1, D_MODEL)), _const_spec((D_MODEL, IN_COLS)), pos_spec, pos_spec],
        out_specs=[row_spec(w) for w in widths],
        out_shape=[jax.ShapeDtypeStruct((rows, w), F32) for w in widths],
        compiler_params=_params(1),
    )(x, g, w_in, cos, sin)


def _rope_tables(pos):
    half = HEAD_DIM // 2
    inv = ROPE_THETA ** (-jnp.arange(half, dtype=F32) / half)
    ang = pos.astype(F32)[:, None] * inv[None, :]
    cos = jnp.cos(ang)
    sin = jnp.sin(ang)
    reps = LANES // HEAD_DIM
    return (jnp.tile(jnp.concatenate([cos, cos], axis=-1), (1, reps)),
            jnp.tile(jnp.concatenate([-sin, sin], axis=-1), (1, reps)))


def _softplus(x):
    return jnp.maximum(x, 0.0) + jnp.log1p(jnp.exp(-jnp.abs(x)))


def _lru_coeffs(conv, wab, bab, lam):
    w = conv.shape[1]
    gates = _dot(conv.astype(BF16), wab) + bab
    r = jax.nn.sigmoid(gates[:, :w])
    gi = jax.nn.sigmoid(gates[:, w:])
    log_a = -LRU_C * r * _softplus(-lam)
    a = jnp.exp(log_a)
    b = jnp.sqrt(-jnp.tanh(log_a) * (a * a + 1.0)) * (gi * conv)
    return a, b


def _segment_scan(a, b, seg):
    step = 1
    while step < seg:
        a, b = _scan_step(a, b, seg, step)
        step *= 2
    return a, b


def _scan_step(a, b, seg, step):
    pos = lax.broadcasted_iota(jnp.int32, a.shape, 0) % seg
    live = pos >= step
    a_prev = pltpu.roll(a, step, axis=0)
    b_prev = pltpu.roll(b, step, axis=0)
    return jnp.where(live, a * a_prev, a), jnp.where(live, a * b_prev + b, b)


def _lru_hidden(conv, h_in, seg, wab, bab, lam):
    a, b = _lru_coeffs(conv, wab, bab, lam)
    a_cum, h_local = _segment_scan(a, b, seg)
    return a_cum * h_in + h_local


def _lane_chunks(width):
    return [slice(c * LANES, (c + 1) * LANES) for c in range(width // LANES)]


def _ffn_chunk(xn, wg_ref, wu_ref, wd_ref, cols):
    gate = _dot(xn, wg_ref[:, cols])
    up = _dot(xn, wu_ref[:, cols])
    return _dot((gate * jax.nn.sigmoid(gate) * up).astype(BF16), wd_ref[cols, :])


def _ffn_proj_lru_body(x_ref, g1_ref, wg_ref, wu_ref, wd_ref, gmix_ref, win_ref, cos_ref, sin_ref,
                       cw_ref, cb_ref, wab_ref, bab_ref, lam_ref, gout_ref,
                       x_out_ref, q_ref, k_ref, v_ref, lru_ref, hlast_ref, utail_ref,
                       ug_ref, ext_ref, h_ref, hs_ref, *, tt, tiles_per_seq):
    pad = SUBLANES
    s = pl.program_id(0)
    lead_slot = s % 2
    lag_slot = 1 - lead_slot

    @pl.when(s == 0)
    def _():
        ug_ref[1] = jnp.zeros((2, tt, LRU_WIDTH), F32)

    @pl.when((s == 0) | (lax.rem(s - 1, tiles_per_seq) == 0))
    def _():
        ext_ref[0:pad, :] = jnp.zeros((pad, LRU_WIDTH), F32)
        h_ref[...] = jnp.zeros_like(h_ref)

    state = {"sumsq": jnp.zeros((tt, 1), F32)}
    scan_steps = [1 << i for i in range(tt.bit_length() - 1)]

    def lru_pieces(c, cols):
        def coeffs():
            ext_ref[pad:pad + tt, cols] = ug_ref[lag_slot, 0, :, cols]
            conv = cb_ref[:, cols]
            for j in range(CONV_WIDTH):
                start = pad - (CONV_WIDTH - 1) + j
                conv = conv + ext_ref[start:start + tt, cols] * cw_ref[j:j + 1, cols]
            ext_ref[0:pad, cols] = ext_ref[tt:tt + pad, cols]
            utail_ref[0, :, cols] = ext_ref[0:pad, cols]
            state[c] = _lru_coeffs(conv, wab_ref[c], bab_ref[c], lam_ref[:, cols])

        def scan(steps):
            def run():
                for step in steps:
                    state[c] = _scan_step(*state[c], tt, step)
            return run

        def finish():
            a_cum, h_local = state.pop(c)
            hs_ref[:, cols] = a_cum * h_ref[:, cols] + h_local
            h_ref[:, cols] = hs_ref[tt - 1:tt, cols]
            hlast_ref[0, :, cols] = h_ref[:, cols]
            y = hs_ref[:, cols] * jax.nn.gelu(ug_ref[lag_slot, 1, :, cols])
            hs_ref[:, cols] = y
            state["sumsq"] = state["sumsq"] + jnp.sum(y * y, axis=-1, keepdims=True)

        half = len(scan_steps) // 2
        return [coeffs, scan(scan_steps[:half]), scan(scan_steps[half:]), finish]

    vector_pieces = [p for c, cols in enumerate(_lane_chunks(LRU_WIDTH)) for p in lru_pieces(c, cols)]

    x_in = x_ref[...]
    xn = _rms(x_in, g1_ref[...]).astype(BF16)

    ffn_cols = [slice(f, f + FFN_CHUNK) for f in range(0, D_FF, FFN_CHUNK)]

    def gate_up(f):
        return _dot(xn, wg_ref[:, ffn_cols[f]]), _dot(xn, wu_ref[:, ffn_cols[f]])

    def ffn_piece(f):
        def run():
            gate, up = state.pop("gate_up") if "gate_up" in state else gate_up(f)
            if f + 1 < len(ffn_cols):
                state["gate_up"] = gate_up(f + 1)
            part = _dot((gate * jax.nn.sigmoid(gate) * up).astype(BF16), wd_ref[ffn_cols[f], :])
            state["acc"] = part if "acc" not in state else state["acc"] + part
        return run

    def ffn_out():
        if "x" not in state:
            state["x"] = x_in + 0.5 * state.pop("acc")
            x_out_ref[...] = state["x"]
        return state["x"]

    def store_u(z):
        ug_ref[lead_slot, 0] = z

    def store_gate(z):
        ug_ref[lead_slot, 1] = z

    matmul_pieces = [ffn_piece(f) for f in range(len(ffn_cols))]
    matmul_pieces += _project_pieces(ffn_out, gmix_ref, win_ref, cos_ref, sin_ref, store_u, store_gate,
                                     q_ref, k_ref, v_ref)

    assert len(vector_pieces) == len(matmul_pieces)
    for vector_piece, matmul_piece in zip(vector_pieces, matmul_pieces):
        vector_piece()
        matmul_piece()
    lru_ref[...] = hs_ref[...] * lax.rsqrt(state["sumsq"] * (1.0 / LRU_WIDTH) + EPS) * gout_ref[...]


def _ffn_proj_lru(x, seq, g1, wg, wu, wd, g_mix, w_in, cos, sin, conv_w, conv_b, wab, bab, lam, g_out):
    rows = x.shape[0]
    n = rows // seq
    tt = min(LRU_TILE, seq)
    tiles_per_seq = seq // tt
    tiles = rows // tt
    pos_blocks = cos.shape[0] // tt
    lead = lambda s: jnp.minimum(s, tiles - 1)
    lag = lambda s: jnp.maximum(s - 1, 0)

    def lead_spec(width):
        return pl.BlockSpec((tt, width), lambda s: (lead(s), 0))

    pos_spec = pl.BlockSpec((tt, LANES), lambda s: (lead(s) % pos_blocks, 0))
    seq_spec = lambda r: pl.BlockSpec((1, r, LRU_WIDTH), lambda s: (lag(s) // tiles_per_seq, 0, 0))
    widths = (D_MODEL, Q_WIDTH, KV_WIDTH, KV_WIDTH)
    return pl.pallas_call(
        functools.partial(_ffn_proj_lru_body, tt=tt, tiles_per_seq=tiles_per_seq),
        grid=(tiles + 1,),
        in_specs=[lead_spec(D_MODEL), _const_spec((1, D_MODEL)), _const_spec((D_MODEL, D_FF)), _const_spec((D_MODEL, D_FF)),
                  _const_spec((D_FF, D_MODEL)), _const_spec((1, D_MODEL)), _const_spec((D_MODEL, IN_COLS)),
                  pos_spec, pos_spec] + _lru_weight_specs(),
        out_specs=[lead_spec(w) for w in widths]
        + [pl.BlockSpec((tt, LRU_WIDTH), lambda s: (lag(s), 0)), seq_spec(1), seq_spec(SUBLANES)],
        out_shape=[jax.ShapeDtypeStruct((rows, w), F32) for w in widths]
        + [jax.ShapeDtypeStruct((rows, LRU_WIDTH), F32), jax.ShapeDtypeStruct((n, 1, LRU_WIDTH), F32),
           jax.ShapeDtypeStruct((n, SUBLANES, LRU_WIDTH), F32)],
        scratch_shapes=[pltpu.VMEM((2, 2, tt, LRU_WIDTH), F32), pltpu.VMEM((tt + SUBLANES, LRU_WIDTH), F32),
                        pltpu.VMEM((1, LRU_WIDTH), F32), pltpu.VMEM((tt, LRU_WIDTH), F32)],
        compiler_params=_params(1),
    )(x, g1, wg, wu, wd, g_mix, w_in, cos, sin, conv_w, conv_b, wab, bab, lam, g_out)


def _lru_sample_body(u_ref, gate_ref, cpad_ref, h0_ref, cw_ref, cb_ref, wab_ref, bab_ref, lam_ref, gout_ref,
                     o_ref, hs_ref, *, seg):
    rows = u_ref.shape[0]
    pos = lax.broadcasted_iota(jnp.int32, (rows, LANES), 0) % seg
    sumsq = jnp.zeros((rows, 1), F32)
    for c, cols in enumerate(_lane_chunks(LRU_WIDTH)):
        u = u_ref[:, cols]
        cpad = cpad_ref[:, cols]
        conv = cb_ref[:, cols] + u * cw_ref[CONV_WIDTH - 1:CONV_WIDTH, cols]
        for back in range(1, CONV_WIDTH):
            shifted = jnp.where(pos >= back, pltpu.roll(u, back, axis=0),
                                pltpu.roll(cpad, (back - seg) % rows, axis=0))
            conv = conv + shifted * cw_ref[CONV_WIDTH - 1 - back:CONV_WIDTH - back, cols]
        h = _lru_hidden(conv, h0_ref[:, cols], seg, wab_ref[c], bab_ref[c], lam_ref[:, cols])
        hs_ref[:, cols] = h
        y = h * jax.nn.gelu(gate_ref[:, cols])
        o_ref[:, cols] = y
        sumsq = sumsq + jnp.sum(y * y, axis=-1, keepdims=True)
    o_ref[...] = o_ref[...] * lax.rsqrt(sumsq * (1.0 / LRU_WIDTH) + EPS) * gout_ref[...]


def _lru_weight_specs():
    chunks = LRU_WIDTH // LANES
    return [_const_spec((CONV_WIDTH, LRU_WIDTH)), _const_spec((1, LRU_WIDTH)),
            _const_spec((chunks, LANES, 2 * LANES)), _const_spec((chunks, 1, 2 * LANES)),
            _const_spec((1, LRU_WIDTH)), _const_spec((1, LRU_WIDTH))]


def _lru_gate_chunks(w_a, b_a, w_i, b_i):
    chunks = LRU_WIDTH // LANES
    per = LRU_BLOCKS // chunks
    wa = w_a.reshape(chunks, per, *w_a.shape[1:])
    wi = w_i.reshape(chunks, per, *w_i.shape[1:])
    wab = jnp.stack([jnp.concatenate([_block_diag(wa[c]), _block_diag(wi[c])], axis=1) for c in range(chunks)])
    bab = jnp.concatenate([b_a.reshape(chunks, 1, LANES), b_i.reshape(chunks, 1, LANES)], axis=2)
    return wab.astype(BF16), bab


def _lru_sample(u, gate, conv_pad, h0_rep, seg, conv_w, conv_b, wab, bab, lam, g_out):
    rows = u.shape[0]
    tm = min(ROW_TILE, rows)
    row_spec = pl.BlockSpec((tm, LRU_WIDTH), lambda i: (i, 0))
    return pl.pallas_call(
        functools.partial(_lru_sample_body, seg=seg),
        grid=(rows // tm,),
        in_specs=[row_spec, row_spec, row_spec, row_spec] + _lru_weight_specs(),
        out_specs=[row_spec, row_spec],
        out_shape=[jax.ShapeDtypeStruct((rows, LRU_WIDTH), F32)] * 2,
        compiler_params=_params(1),
    )(u, gate, conv_pad, h0_rep, conv_w, conv_b, wab, bab, lam, g_out)


def _block_diag(w):
    nb, bi, bj = w.shape
    eye = jnp.eye(nb, dtype=w.dtype)
    return jnp.einsum('gij,gh->gihj', w, eye).reshape(nb * bi, nb * bj)


def _bdot_nt(a, b):
    return lax.dot_general(a, b, (((2,), (2,)), ((0,), (0,))), preferred_element_type=F32)


def _bdot(a, b):
    return lax.dot_general(a, b, (((2,), (1,)), ((0,), (0,))), preferred_element_type=F32)


def _swa_core(q, keys, vals, sink_ref, mask, g_out):
    n, r, _ = q.shape
    lane = lax.broadcasted_iota(jnp.int32, (n, r, LANES), 2)
    low = lane < HEAD_DIM
    zero = jnp.zeros((n, r, LANES), F32)
    slabs = [q[:, :, j * LANES:(j + 1) * LANES] for j in range(Q_WIDTH // LANES)]
    group = ATTN_HEADS // KV_HEADS
    straight, swapped = [], []
    for h in range(ATTN_HEADS):
        (straight if (h % 2) == (h // group) else swapped).append(h)
    scale = HEAD_DIM ** -0.5
    out_half = {}
    for heads, kk, vv in ((straight, keys, vals),
                          (swapped, pltpu.roll(keys, HEAD_DIM, axis=2), pltpu.roll(vals, HEAD_DIM, axis=2))):
        qs = jnp.concatenate([jnp.where(low if h % 2 == 0 else ~low, slabs[h // 2], zero) for h in heads], axis=1)
        s = _bdot_nt(qs.astype(BF16), kk.astype(BF16)) * scale
        probs = []
        for i, h in enumerate(heads):
            sh = jnp.where(mask[None], s[:, i * r:(i + 1) * r, :], -jnp.inf)
            sink = sink_ref[h]
            m = jnp.maximum(jnp.max(sh, axis=-1, keepdims=True), sink)
            e = jnp.exp(sh - m)
            denom = jnp.sum(e, axis=-1, keepdims=True) + jnp.exp(sink - m)
            probs.append(e / denom)
        o = _bdot(jnp.concatenate(probs, axis=1).astype(BF16), vv.astype(BF16))
        for i, h in enumerate(heads):
            out_half[h] = o[:, i * r:(i + 1) * r, :]
    out = jnp.concatenate([jnp.where(low, out_half[2 * j], out_half[2 * j + 1]) for j in range(Q_WIDTH // LANES)],
                          axis=2)
    return _rms(out, g_out)


def _swa_prompt_body(sink_ref, q_ref, kp_ref, kc_ref, vp_ref, vc_ref, gout_ref, o_ref):
    blk = pl.program_id(1)
    keys = jnp.concatenate([kp_ref[...], kc_ref[...]], axis=1)
    vals = jnp.concatenate([vp_ref[...], vc_ref[...]], axis=1)
    i = lax.broadcasted_iota(jnp.int32, (WINDOW, 2 * WINDOW), 0)
    j = lax.broadcasted_iota(jnp.int32, (WINDOW, 2 * WINDOW), 1)
    dist = i + WINDOW - j
    mask = (dist >= 0) & (dist < WINDOW) & ((blk > 0) | (j >= WINDOW))
    o_ref[...] = _swa_core(q_ref[...], keys, vals, sink_ref, mask, gout_ref[...])


def _swa_prompt(q, k, v, sink, g_out):
    n, t, _ = q.shape
    cur = lambda i, j: (i, j, 0)
    prev = lambda i, j: (i, jnp.maximum(j - 1, 0), 0)
    kv_block = (1, WINDOW, KV_WIDTH)
    return pl.pallas_call(
        _swa_prompt_body,
        grid=(n, t // WINDOW),
        in_specs=[pl.BlockSpec(memory_space=pltpu.SMEM), pl.BlockSpec((1, WINDOW, Q_WIDTH), cur),
                  pl.BlockSpec(kv_block, prev), pl.BlockSpec(kv_block, cur),
                  pl.BlockSpec(kv_block, prev), pl.BlockSpec(kv_block, cur), _const_spec((1, Q_WIDTH))],
        out_specs=pl.BlockSpec((1, WINDOW, Q_WIDTH), cur),
        out_shape=jax.ShapeDtypeStruct((n, t, Q_WIDTH), F32),
        compiler_params=_params(2),
    )(sink, q, k, k, v, v, g_out)


def _swa_sample_body(sink_ref, q_ref, k_ref, v_ref, bk_ref, bv_ref, gout_ref, o_ref, nk_ref, nv_ref, *, s_len, past):
    qp = past + lax.broadcasted_iota(jnp.int32, (s_len, WINDOW + s_len), 0)
    col = lax.broadcasted_iota(jnp.int32, (s_len, WINDOW + s_len), 1)
    kp = jnp.where(col < WINDOW, past - WINDOW + col, past + col - WINDOW)
    dist = qp - kp
    mask = (dist >= 0) & (dist < WINDOW) & (kp >= 0)
    keys = jnp.concatenate([bk_ref[...], k_ref[...]], axis=1)
    vals = jnp.concatenate([bv_ref[...], v_ref[...]], axis=1)
    o_ref[...] = _swa_core(q_ref[...], keys, vals, sink_ref, mask, gout_ref[...])
    nk_ref[...] = keys[:, s_len:, :]
    nv_ref[...] = vals[:, s_len:, :]


def _swa_sample(q, k, v, buf_k, buf_v, sink, g_out, past):
    n, s_len, _ = q.shape
    sb = min(SAMPLE_SEQ_TILE, n)

    def seq_spec(rows, width):
        return pl.BlockSpec((sb, rows, width), lambda i: (i, 0, 0))

    buf_spec = seq_spec(WINDOW, KV_WIDTH)
    return pl.pallas_call(
        functools.partial(_swa_sample_body, s_len=s_len, past=past),
        grid=(n // sb,),
        in_specs=[pl.BlockSpec(memory_space=pltpu.SMEM), seq_spec(s_len, Q_WIDTH), seq_spec(s_len, KV_WIDTH),
                  seq_spec(s_len, KV_WIDTH), buf_spec, buf_spec, _const_spec((1, Q_WIDTH))],
        out_specs=[seq_spec(s_len, Q_WIDTH), buf_spec, buf_spec],
        out_shape=[jax.ShapeDtypeStruct((n, s_len, Q_WIDTH), F32),
                   jax.ShapeDtypeStruct((n, WINDOW, KV_WIDTH), F32), jax.ShapeDtypeStruct((n, WINDOW, KV_WIDTH), F32)],
        compiler_params=_params(1),
    )(sink, q, k, v, buf_k, buf_v, g_out)


def _softmax(s):
    e = jnp.exp(s - jnp.max(s, axis=-1, keepdims=True))
    return e / jnp.sum(e, axis=-1, keepdims=True)


def _mix_and_query(x_ref, lru_ref, attn_ref, wout_ref, gx_ref, wcq_ref):
    x = (x_ref[...] + _dot(lru_ref[...].astype(BF16), wout_ref[:LRU_WIDTH, :])
         + _dot(attn_ref[...].astype(BF16), wout_ref[LRU_WIDTH:, :]))
    return x, _dot(_rms(x, gx_ref[...]).astype(BF16), wcq_ref[...])


def _mix_out_body(x_ref, lru_ref, attn_ref, wout_ref, gx_ref, wcq_ref, mk_ref, mv_ref, wco_ref, o_ref):
    x, q = _mix_and_query(x_ref, lru_ref, attn_ref, wout_ref, gx_ref, wcq_ref)
    scale = X_HEAD_DIM ** -0.5
    heads = []
    for h in range(X_HEADS):
        cols = slice(h * X_HEAD_DIM, (h + 1) * X_HEAD_DIM)
        s = _dot_nt(q[:, cols].astype(BF16), mk_ref[0, :, cols].astype(BF16)) * scale
        heads.append(_dot(_softmax(s).astype(BF16), mv_ref[0, :, cols].astype(BF16)))
    o_ref[...] = x + _dot(jnp.concatenate(heads, axis=1).astype(BF16), wco_ref[...])


def _mix_out(x, lru_out, attn_out, w_out, g_x, w_cq, mk, mv, w_co, seq_rows):
    rows = x.shape[0]
    tm = min(ROW_TILE, seq_rows)
    per_seq = seq_rows // tm

    def row_spec(width):
        return pl.BlockSpec((tm, width), lambda i: (i, 0))

    mem_spec = pl.BlockSpec((1, N_MEM, D_MODEL), lambda i: (i // per_seq, 0, 0))
    w_spec = _const_spec((D_MODEL, D_MODEL))
    return pl.pallas_call(
        _mix_out_body,
        grid=(rows // tm,),
        in_specs=[row_spec(D_MODEL), row_spec(LRU_WIDTH), row_spec(Q_WIDTH), w_spec, _const_spec((1, D_MODEL)),
                  w_spec, mem_spec, mem_spec, w_spec],
        out_specs=row_spec(D_MODEL),
        out_shape=jax.ShapeDtypeStruct((rows, D_MODEL), F32),
        compiler_params=_params(1),
    )(x, lru_out, attn_out, w_out, g_x, w_cq, mk, mv, w_co)


def _mix_query_body(x_ref, lru_ref, attn_ref, wout_ref, gx_ref, wcq_ref, x_out_ref, q_ref):
    x_out_ref[...], q_ref[...] = _mix_and_query(x_ref, lru_ref, attn_ref, wout_ref, gx_ref, wcq_ref)


def _mix_query(x, lru_out, attn_out, w_out, g_x, w_cq):
    rows = x.shape[0]
    tm = min(ROW_TILE, rows)

    def row_spec(width):
        return pl.BlockSpec((tm, width), lambda i: (i, 0))

    w_spec = _const_spec((D_MODEL, D_MODEL))
    return pl.pallas_call(
        _mix_query_body,
        grid=(rows // tm,),
        in_specs=[row_spec(D_MODEL), row_spec(LRU_WIDTH), row_spec(Q_WIDTH), w_spec, _const_spec((1, D_MODEL)), w_spec],
        out_specs=[row_spec(D_MODEL), row_spec(D_MODEL)],
        out_shape=[jax.ShapeDtypeStruct((rows, D_MODEL), F32)] * 2,
        compiler_params=_params(1),
    )(x, lru_out, attn_out, w_out, g_x, w_cq)


def _xattn_cache_body(q_ref, mk_ref, mv_ref, o_ref):
    sb, steps, _ = q_ref.shape
    blocks = D_MODEL // LANES
    chunks = X_HEAD_DIM // LANES
    width = mk_ref.shape[1]
    cls = lax.broadcasted_iota(jnp.int32, (sb, steps, width), 2) % blocks
    block_cls = [(j % chunks) * X_HEADS + j // chunks for j in range(blocks)]
    scale = X_HEAD_DIM ** -0.5

    qs = jnp.concatenate([q_ref[:, :, j * LANES:(j + 1) * LANES] for j in range(blocks)], axis=1)
    s = _bdot_nt(qs.astype(BF16), mk_ref[...].astype(BF16)) * scale
    part = jnp.zeros((sb, steps, width), F32)
    for j in range(blocks):
        part = part + jnp.where(cls == block_cls[j], s[:, j * steps:(j + 1) * steps, :], 0.0)
    score = part + pltpu.roll(part, width - X_HEADS, axis=2)
    top = jnp.zeros((sb, steps, width), F32)
    for h in range(X_HEADS):
        mine = cls == h
        top = jnp.where(mine, jnp.max(jnp.where(mine, score, -jnp.inf), axis=-1, keepdims=True), top)
    e = jnp.where(cls < X_HEADS, jnp.exp(score - top), 0.0)
    denom = jnp.ones((sb, steps, width), F32)
    for h in range(X_HEADS):
        mine = cls == h
        denom = jnp.where(mine, jnp.sum(jnp.where(mine, e, 0.0), axis=-1, keepdims=True), denom)
    p = e / denom
    p = p + pltpu.roll(p, X_HEADS, axis=2)
    ps = jnp.concatenate([jnp.where(cls == block_cls[j], p, 0.0) for j in range(blocks)], axis=1)
    o = _bdot(ps.astype(BF16), mv_ref[...].astype(BF16))
    for j in range(blocks):
        o_ref[:, :, j * LANES:(j + 1) * LANES] = o[:, j * steps:(j + 1) * steps, :]


def _interleave_chunks(cache):
    n = cache.shape[0]
    chunks = X_HEAD_DIM // LANES
    c = cache.reshape(n, N_MEM, X_HEADS, chunks, LANES)
    return jnp.transpose(c, (0, 1, 3, 2, 4)).reshape(n, N_MEM * chunks * X_HEADS, LANES)


def _xattn_cache(q, cache_k, cache_v):
    n, s_len, _ = q.shape
    sb = min(XATTN_SEQ_TILE, n)
    q_spec = pl.BlockSpec((sb, s_len, D_MODEL), lambda i: (i, 0, 0))
    rows = cache_k.shape[1]
    mem_spec = pl.BlockSpec((sb, rows, LANES), lambda i: (i, 0, 0))
    return pl.pallas_call(
        _xattn_cache_body,
        grid=(n // sb,),
        in_specs=[q_spec, mem_spec, mem_spec],
        out_specs=q_spec,
        out_shape=jax.ShapeDtypeStruct((n, s_len, D_MODEL), F32),
        compiler_params=_params(1),
    )(q, cache_k, cache_v)


def _mem_kv_body(mem_ref, g_ref, wk_ref, wv_ref, k_ref, v_ref):
    mm = _rms(mem_ref[...], g_ref[...]).astype(BF16)
    k_ref[...] = _dot(mm, wk_ref[...])
    v_ref[...] = _dot(mm, wv_ref[...])


def _mem_kv(mem, g, w_ck, w_cv):
    rows = mem.shape[0]
    tm = min(ROW_TILE, rows)
    row_spec = pl.BlockSpec((tm, D_MODEL), lambda i: (i, 0))
    w_spec = _const_spec((D_MODEL, D_MODEL))
    return pl.pallas_call(
        _mem_kv_body,
        grid=(rows // tm,),
        in_specs=[row_spec, _const_spec((1, D_MODEL)), w_spec, w_spec],
        out_specs=[row_spec, row_spec],
        out_shape=[jax.ShapeDtypeStruct((rows, D_MODEL), F32)] * 2,
        compiler_params=_params(1),
    )(mem, g, w_ck, w_cv)


def kernel(x_prompt, x_sample, mem_prompt, cache_mem_k, cache_mem_v, cache_swa_k, cache_swa_v, state_conv, state_lru_h,
           g_ffn1, w1_gate, w1_up, w1_down, g_mix, w_in, conv_w, conv_b, w_a, b_a, w_i, b_i, lam, sink,
           g_lru_out, g_attn_out, w_out, g_xattn, g_mem, w_cq, w_ck, w_cv, w_co, g_ffn2, w2_gate, w2_up, w2_down,
           g_final):
    nbp, seq, _ = x_prompt.shape
    nbs, dec_seq, _ = x_sample.shape
    depth = g_ffn1.shape[0]
    past = PAST_LEN
    cos_p, sin_p = _rope_tables(jnp.arange(seq, dtype=jnp.int32))
    cos_s, sin_s = _rope_tables(past + jnp.arange(dec_seq, dtype=jnp.int32))
    sample_pos_rows = min(ROW_TILE, nbs * dec_seq)
    cos_s = jnp.tile(cos_s, (sample_pos_rows // dec_seq, 1))
    sin_s = jnp.tile(sin_s, (sample_pos_rows // dec_seq, 1))

    xp = x_prompt.reshape(nbp * seq, D_MODEL)
    xs = x_sample.reshape(nbs * dec_seq, D_MODEL)
    g_fin = g_final.reshape(1, D_MODEL)
    row = lambda a: a.reshape(1, -1)
    outs = [[] for _ in range(10)]
    for l in range(depth):
        last = l == depth - 1
        bf = lambda a: a[l].astype(BF16)
        w1g, w1u, w1d, w2g, w2u, w2d = bf(w1_gate), bf(w1_up), bf(w1_down), bf(w2_gate), bf(w2_up), bf(w2_down)
        win, wout, wcq, wck, wcv, wco = bf(w_in), bf(w_out), bf(w_cq), bf(w_ck), bf(w_cv), bf(w_co)
        wab, bab = _lru_gate_chunks(w_a[l], b_a[l], w_i[l], b_i[l])
        lru_w = (conv_w[l], row(conv_b[l]), wab, bab, row(lam[l]), row(g_lru_out[l]))

        mk_p, mv_p = _mem_kv(mem_prompt.reshape(nbp * N_MEM, D_MODEL), row(g_mem[l]), wck, wcv)
        mk_p = mk_p.reshape(nbp, N_MEM, D_MODEL)
        mv_p = mv_p.reshape(nbp, N_MEM, D_MODEL)
        xp, q, k, v, lru_out, h_last, u_tail = _ffn_proj_lru(xp, seq, row(g_ffn1[l]), w1g, w1u, w1d, row(g_mix[l]), win,
                                                             cos_p, sin_p, *lru_w)
        k3 = k.reshape(nbp, seq, KV_WIDTH)
        v3 = v.reshape(nbp, seq, KV_WIDTH)
        attn_out = _swa_prompt(q.reshape(nbp, seq, Q_WIDTH), k3, v3, sink[l], row(g_attn_out[l]))
        xp = _mix_out(xp, lru_out, attn_out.reshape(nbp * seq, Q_WIDTH), wout,
                      row(g_xattn[l]), wcq, mk_p, mv_p, wco, seq)
        xp = _ffn(xp, row(g_ffn2[l]), w2g, w2u, w2d, g_fin, last)
        outs[0].append(mk_p.reshape(nbp, N_MEM, X_HEADS, X_HEAD_DIM))
        outs[1].append(mv_p.reshape(nbp, N_MEM, X_HEADS, X_HEAD_DIM))
        outs[2].append(k3[:, -WINDOW:].reshape(nbp, WINDOW, KV_HEADS, HEAD_DIM))
        outs[3].append(v3[:, -WINDOW:].reshape(nbp, WINDOW, KV_HEADS, HEAD_DIM))
        outs[4].append(u_tail[:, -(CONV_WIDTH - 1):])
        outs[5].append(h_last.reshape(nbp, LRU_WIDTH))

        xs = _ffn(xs, row(g_ffn1[l]), w1g, w1u, w1d, g_fin, False)
        u, gate, q, k, v = _proj(xs, row(g_mix[l]), win, cos_s, sin_s)
        conv_pad = jnp.pad(state_conv[l], ((0, 0), (dec_seq - (CONV_WIDTH - 1), 0), (0, 0)))
        h0_rep = jnp.repeat(state_lru_h[l], dec_seq, axis=0)
        lru_out, hs = _lru_sample(u, gate, conv_pad.reshape(nbs * dec_seq, LRU_WIDTH), h0_rep, dec_seq, *lru_w)
        per_seq = lambda a: a.reshape(nbs, dec_seq, a.shape[-1])
        attn_out, new_k, new_v = _swa_sample(per_seq(q), per_seq(k), per_seq(v),
                                             cache_swa_k[l].reshape(nbs, WINDOW, KV_WIDTH),
                                             cache_swa_v[l].reshape(nbs, WINDOW, KV_WIDTH), sink[l],
                                             row(g_attn_out[l]), past)
        xs, xq = _mix_query(xs, lru_out, attn_out.reshape(nbs * dec_seq, Q_WIDTH), wout, row(g_xattn[l]), wcq)
        xo = _xattn_cache(per_seq(xq), _interleave_chunks(cache_mem_k[l]), _interleave_chunks(cache_mem_v[l]))
        xs = _ffn(xs, row(g_ffn2[l]), w2g, w2u, w2d, g_fin, last, xo.reshape(nbs * dec_seq, D_MODEL), wco)
        outs[6].append(new_k.reshape(nbs, WINDOW, KV_HEADS, HEAD_DIM))
        outs[7].append(new_v.reshape(nbs, WINDOW, KV_HEADS, HEAD_DIM))
        outs[8].append(u.reshape(nbs, dec_seq, LRU_WIDTH)[:, -(CONV_WIDTH - 1):])
        outs[9].append(hs.reshape(nbs, dec_seq, LRU_WIDTH)[:, -1])

    return (xp.reshape(nbp, seq, D_MODEL), xs.reshape(nbs, dec_seq, D_MODEL)) + tuple(jnp.stack(o) for o in outs)
```

```python
import functools

import jax
import jax.numpy as jnp
from jax import lax
from jax.experimental import pallas as pl
from jax.experimental.pallas import tpu as pltpu

F32 = jnp.float32
BF16 = jnp.bfloat16

D_MODEL = 1024
LRU_WIDTH = 512
LRU_BLOCKS = 8
CONV_WIDTH = 4
LRU_C = 8.0
ATTN_HEADS = 8
HEAD_DIM = 64
KV_HEADS = 2
WINDOW = 128
PAST_LEN = 8192
ROPE_THETA = 10000.0
N_MEM = 256
X_HEADS = 4
X_HEAD_DIM = 256
D_FF = 2816
EPS = 1e-6
Q_WIDTH = ATTN_HEADS * HEAD_DIM
KV_WIDTH = KV_HEADS * HEAD_DIM
IN_COLS = 2 * LRU_WIDTH + Q_WIDTH + 2 * KV_WIDTH

LANES = 128
SUBLANES = 8
VMEM_LIMIT = 56 * 1024 * 1024

ROW_TILE = 512
LRU_TILE = 256
FFN_CHUNK = 256
SAMPLE_SEQ_TILE = 8
XATTN_SEQ_TILE = 4


def _params(n_axes):
    return pltpu.CompilerParams(dimension_semantics=("arbitrary",) * n_axes, vmem_limit_bytes=VMEM_LIMIT)


def _const_spec(shape):
    return pl.BlockSpec(shape, lambda *_: (0,) * len(shape), pipeline_mode=pl.Buffered(1))


def _rms(x, g):
    return x * lax.rsqrt(jnp.mean(x * x, axis=-1, keepdims=True) + EPS) * g


def _dot(a, b):
    return jnp.dot(a, b, preferred_element_type=F32)


def _dot_nt(a, b):
    return lax.dot_general(a, b, (((1,), (1,)), ((), ())), preferred_element_type=F32)


def _ffn_step(x, g_ref, wg_ref, wu_ref, wd_ref):
    xn = _rms(x, g_ref[...]).astype(BF16)
    gate = _dot(xn, wg_ref[...])
    up = _dot(xn, wu_ref[...])
    h = (gate * jax.nn.sigmoid(gate) * up).astype(BF16)
    return x + 0.5 * _dot(h, wd_ref[...])


def _ffn_tail(x, g_ref, wg_ref, wu_ref, wd_ref, gf_ref, o_ref, final_norm):
    y = _ffn_step(x, g_ref, wg_ref, wu_ref, wd_ref)
    if final_norm:
        y = _rms(y, gf_ref[...])
    o_ref[...] = y


def _ffn_body(x_ref, g_ref, wg_ref, wu_ref, wd_ref, gf_ref, o_ref, *, final_norm):
    _ffn_tail(x_ref[...], g_ref, wg_ref, wu_ref, wd_ref, gf_ref, o_ref, final_norm)


def _proj_ffn_body(x_ref, a_ref, wa_ref, g_ref, wg_ref, wu_ref, wd_ref, gf_ref, o_ref, *, final_norm):
    x = x_ref[...] + _dot(a_ref[...].astype(BF16), wa_ref[...])
    _ffn_tail(x, g_ref, wg_ref, wu_ref, wd_ref, gf_ref, o_ref, final_norm)


def _ffn(x, g, wg, wu, wd, g_final, final_norm, attn=None, w_attn=None):
    rows = x.shape[0]
    tm = min(ROW_TILE, rows)
    row_spec = pl.BlockSpec((tm, D_MODEL), lambda i: (i, 0))
    ffn_specs = [_const_spec((1, D_MODEL)), _const_spec((D_MODEL, D_FF)), _const_spec((D_MODEL, D_FF)),
                 _const_spec((D_FF, D_MODEL)), _const_spec((1, D_MODEL))]
    if attn is None:
        body, lead_specs, lead = _ffn_body, [row_spec], (x,)
    else:
        body, lead_specs, lead = _proj_ffn_body, [row_spec, row_spec, _const_spec((D_MODEL, D_MODEL))], (x, attn, w_attn)
    return pl.pallas_call(
        functools.partial(body, final_norm=final_norm),
        grid=(rows // tm,),
        in_specs=lead_specs + ffn_specs,
        out_specs=row_spec,
        out_shape=jax.ShapeDtypeStruct((rows, D_MODEL), F32),
        compiler_params=_params(1),
    )(*lead, g, wg, wu, wd, g_final)


def _rope(z, cos, sin_signed):
    half = HEAD_DIM // 2
    lane = lax.broadcasted_iota(jnp.int32, z.shape, 1)
    first_half = (lane % HEAD_DIM) < half
    partner = jnp.where(first_half, pltpu.roll(z, LANES - half, axis=1), pltpu.roll(z, half, axis=1))
    return z * cos + partner * sin_signed


def _project_pieces(x, g_ref, w_ref, cos_ref, sin_ref, store_u, store_gate, q_ref, k_ref, v_ref):
    o_gate, o_q, o_k, o_v = LRU_WIDTH, 2 * LRU_WIDTH, 2 * LRU_WIDTH + Q_WIDTH, 2 * LRU_WIDTH + Q_WIDTH + KV_WIDTH
    xn = []

    def normed():
        if not xn:
            xn.append(_rms(x(), g_ref[...]).astype(BF16))
        return xn[0]

    def rope_into(ref, z):
        for j in range(z.shape[1] // LANES):
            cols = slice(j * LANES, (j + 1) * LANES)
            ref[:, cols] = _rope(z[:, cols], cos_ref[...], sin_ref[...])

    def store_v(z):
        v_ref[...] = z

    return [lambda: rope_into(q_ref, _dot(normed(), w_ref[:, o_q:o_k])),
            lambda: rope_into(k_ref, _dot(normed(), w_ref[:, o_k:o_v])),
            lambda: store_v(_dot(normed(), w_ref[:, o_v:])),
            lambda: store_u(_dot(normed(), w_ref[:, :o_gate])),
            lambda: store_gate(_dot(normed(), w_ref[:, o_gate:o_q]))]


def _proj_body(x_ref, g_ref, w_ref, cos_ref, sin_ref, u_ref, gate_ref, q_ref, k_ref, v_ref):
    def store_u(z):
        u_ref[...] = z

    def store_gate(z):
        gate_ref[...] = z

    for piece in _project_pieces(lambda: x_ref[...], g_ref, w_ref, cos_ref, sin_ref, store_u, store_gate,
                                 q_ref, k_ref, v_ref):
        piece()


def _proj(x, g, w_in, cos, sin):
    rows = x.shape[0]
    tm = min(ROW_TILE, rows, cos.shape[0])
    pos_blocks = cos.shape[0] // tm

    def row_spec(width):
        return pl.BlockSpec((tm, width), lambda i: (i, 0))

    pos_spec = pl.BlockSpec((tm, LANES), lambda i: (i % pos_blocks, 0))
    widths = (LRU_WIDTH, LRU_WIDTH, Q_WIDTH, KV_WIDTH, KV_WIDTH)
    return pl.pallas_call(
        _proj_body,
        grid=(rows // tm,),
        in_specs=[row_spec(D_MODEL), _const_spec((1, D_MODEL)), _const_spec((D_MODEL, IN_COLS)), pos_spec, pos_spec],
        out_specs=[row_spec(w) for w in widths],
        out_shape=[jax.ShapeDtypeStruct((rows, w), F32) for w in widths],
        compiler_params=_params(1),
    )(x, g, w_in, cos, sin)


def _rope_tables(pos):
    half = HEAD_DIM // 2
    inv = ROPE_THETA ** (-jnp.arange(half, dtype=F32) / half)
    ang = pos.astype(F32)[:, None] * inv[None, :]
    cos = jnp.cos(ang)
    sin = jnp.sin(ang)
    reps = LANES // HEAD_DIM
    return (jnp.tile(jnp.concatenate([cos, cos], axis=-1), (1, reps)),
            jnp.tile(jnp.concatenate([-sin, sin], axis=-1), (1, reps)))


def _softplus(x):
    return jnp.maximum(x, 0.0) + jnp.log1p(jnp.exp(-jnp.abs(x)))


def _lru_coeffs(conv, wab, bab, lam):
    w = conv.shape[1]
    gates = _dot(conv.astype(BF16), wab) + bab
    r = jax.nn.sigmoid(gates[:, :w])
    gi = jax.nn.sigmoid(gates[:, w:])
    log_a = -LRU_C * r * _softplus(-lam)
    a = jnp.exp(log_a)
    b = jnp.sqrt(-jnp.tanh(log_a) * (a * a + 1.0)) * (gi * conv)
    return a, b


def _segment_scan(a, b, seg):
    step = 1
    while step < seg:
        a, b = _scan_step(a, b, seg, step)
        step *= 2
    return a, b


def _scan_step(a, b, seg, step):
    pos = lax.broadcasted_iota(jnp.int32, a.shape, 0) % seg
    live = pos >= step
    a_prev = pltpu.roll(a, step, axis=0)
    b_prev = pltpu.roll(b, step, axis=0)
    return jnp.where(live, a * a_prev, a), jnp.where(live, a * b_prev + b, b)


def _lru_hidden(conv, h_in, seg, wab, bab, lam):
    a, b = _lru_coeffs(conv, wab, bab, lam)
    a_cum, h_local = _segment_scan(a, b, seg)
    return a_cum * h_in + h_local


def _lane_chunks(width):
    return [slice(c * LANES, (c + 1) * LANES) for c in range(width // LANES)]


def _ffn_chunk(xn, wg_ref, wu_ref, wd_ref, cols):
    gate = _dot(xn, wg_ref[:, cols])
    up = _dot(xn, wu_ref[:, cols])
    return _dot((gate * jax.nn.sigmoid(gate) * up).astype(BF16), wd_ref[cols, :])


def _ffn_proj_lru_body(x_ref, g1_ref, wg_ref, wu_ref, wd_ref, gmix_ref, win_ref, cos_ref, sin_ref,
                       cw_ref, cb_ref, wab_ref, bab_ref, lam_ref, gout_ref,
                       x_out_ref, q_ref, k_ref, v_ref, lru_ref, hlast_ref, utail_ref,
                       ug_ref, ext_ref, h_ref, hs_ref, *, tt, tiles_per_seq):
    pad = SUBLANES
    s = pl.program_id(0)
    lead_slot = s % 2
    lag_slot = 1 - lead_slot

    @pl.when(s == 0)
    def _():
        ug_ref[1] = jnp.zeros((2, tt, LRU_WIDTH), F32)

    @pl.when((s == 0) | (lax.rem(s - 1, tiles_per_seq) == 0))
    def _():
        ext_ref[0:pad, :] = jnp.zeros((pad, LRU_WIDTH), F32)
        h_ref[...] = jnp.zeros_like(h_ref)

    state = {"sumsq": jnp.zeros((tt, 1), F32)}
    scan_steps = [1 << i for i in range(tt.bit_length() - 1)]

    def lru_pieces(c, cols):
        def coeffs():
            ext_ref[pad:pad + tt, cols] = ug_ref[lag_slot, 0, :, cols]
            conv = cb_ref[:, cols]
            for j in range(CONV_WIDTH):
                start = pad - (CONV_WIDTH - 1) + j
                conv = conv + ext_ref[start:start + tt, cols] * cw_ref[j:j + 1, cols]
            ext_ref[0:pad, cols] = ext_ref[tt:tt + pad, cols]
            utail_ref[0, :, cols] = ext_ref[0:pad, cols]
            state[c] = _lru_coeffs(conv, wab_ref[c], bab_ref[c], lam_ref[:, cols])

        def scan(steps):
            def run():
                for step in steps:
                    state[c] = _scan_step(*state[c], tt, step)
            return run

        def finish():
            a_cum, h_local = state.pop(c)
            hs_ref[:, cols] = a_cum * h_ref[:, cols] + h_local
            h_ref[:, cols] = hs_ref[tt - 1:tt, cols]
            hlast_ref[0, :, cols] = h_ref[:, cols]
            y = hs_ref[:, cols] * jax.nn.gelu(ug_ref[lag_slot, 1, :, cols])
            hs_ref[:, cols] = y
            state["sumsq"] = state["sumsq"] + jnp.sum(y * y, axis=-1, keepdims=True)

        half = len(scan_steps) // 2
        return [coeffs, scan(scan_steps[:half]), scan(scan_steps[half:]), finish]

    vector_pieces = [p for c, cols in enumerate(_lane_chunks(LRU_WIDTH)) for p in lru_pieces(c, cols)]

    x_in = x_ref[...]
    xn = _rms(x_in, g1_ref[...]).astype(BF16)

    ffn_cols = [slice(f, f + FFN_CHUNK) for f in range(0, D_FF, FFN_CHUNK)]

    def gate_up(f):
        return _dot(xn, wg_ref[:, ffn_cols[f]]), _dot(xn, wu_ref[:, ffn_cols[f]])

    def ffn_piece(f):
        def run():
            gate, up = state.pop("gate_up") if "gate_up" in state else gate_up(f)
            if f + 1 < len(ffn_cols):
                state["gate_up"] = gate_up(f + 1)
            part = _dot((gate * jax.nn.sigmoid(gate) * up).astype(BF16), wd_ref[ffn_cols[f], :])
            state["acc"] = part if "acc" not in state else state["acc"] + part
        return run

    def ffn_out():
        if "x" not in state:
            state["x"] = x_in + 0.5 * state.pop("acc")
            x_out_ref[...] = state["x"]
        return state["x"]

    def store_u(z):
        ug_ref[lead_slot, 0] = z

    def store_gate(z):
        ug_ref[lead_slot, 1] = z

    matmul_pieces = [ffn_piece(f) for f in range(len(ffn_cols))]
    matmul_pieces += _project_pieces(ffn_out, gmix_ref, win_ref, cos_ref, sin_ref, store_u, store_gate,
                                     q_ref, k_ref, v_ref)

    assert len(vector_pieces) == len(matmul_pieces)
    for vector_piece, matmul_piece in zip(vector_pieces, matmul_pieces):
        vector_piece()
        matmul_piece()
    lru_ref[...] = hs_ref[...] * lax.rsqrt(state["sumsq"] * (1.0 / LRU_WIDTH) + EPS) * gout_ref[...]


def _ffn_proj_lru(x, seq, g1, wg, wu, wd, g_mix, w_in, cos, sin, conv_w, conv_b, wab, bab, lam, g_out):
    rows = x.shape[0]
    n = rows // seq
    tt = min(LRU_TILE, seq)
    tiles_per_seq = seq // tt
    tiles = rows // tt
    pos_blocks = cos.shape[0] // tt
    lead = lambda s: jnp.minimum(s, tiles - 1)
    lag = lambda s: jnp.maximum(s - 1, 0)

    def lead_spec(width):
        return pl.BlockSpec((tt, width), lambda s: (lead(s), 0))

    pos_spec = pl.BlockSpec((tt, LANES), lambda s: (lead(s) % pos_blocks, 0))
    seq_spec = lambda r: pl.BlockSpec((1, r, LRU_WIDTH), lambda s: (lag(s) // tiles_per_seq, 0, 0))
    widths = (D_MODEL, Q_WIDTH, KV_WIDTH, KV_WIDTH)
    return pl.pallas_call(
        functools.partial(_ffn_proj_lru_body, tt=tt, tiles_per_seq=tiles_per_seq),
        grid=(tiles + 1,),
        in_specs=[lead_spec(D_MODEL), _const_spec((1, D_MODEL)), _const_spec((D_MODEL, D_FF)), _const_spec((D_MODEL, D_FF)),
                  _const_spec((D_FF, D_MODEL)), _const_spec((1, D_MODEL)), _const_spec((D_MODEL, IN_COLS)),
                  pos_spec, pos_spec] + _lru_weight_specs(),
        out_specs=[lead_spec(w) for w in widths]
        + [pl.BlockSpec((tt, LRU_WIDTH), lambda s: (lag(s), 0)), seq_spec(1), seq_spec(SUBLANES)],
        out_shape=[jax.ShapeDtypeStruct((rows, w), F32) for w in widths]
        + [jax.ShapeDtypeStruct((rows, LRU_WIDTH), F32), jax.ShapeDtypeStruct((n, 1, LRU_WIDTH), F32),
           jax.ShapeDtypeStruct((n, SUBLANES, LRU_WIDTH), F32)],
        scratch_shapes=[pltpu.VMEM((2, 2, tt, LRU_WIDTH), F32), pltpu.VMEM((tt + SUBLANES, LRU_WIDTH), F32),
                        pltpu.VMEM((1, LRU_WIDTH), F32), pltpu.VMEM((tt, LRU_WIDTH), F32)],
        compiler_params=_params(1),
    )(x, g1, wg, wu, wd, g_mix, w_in, cos, sin, conv_w, conv_b, wab, bab, lam, g_out)


def _lru_sample_body(u_ref, gate_ref, cpad_ref, h0_ref, cw_ref, cb_ref, wab_ref, bab_ref, lam_ref, gout_ref,
                     o_ref, hs_ref, *, seg):
    rows = u_ref.shape[0]
    pos = lax.broadcasted_iota(jnp.int32, (rows, LANES), 0) % seg
    sumsq = jnp.zeros((rows, 1), F32)
    for c, cols in enumerate(_lane_chunks(LRU_WIDTH)):
        u = u_ref[:, cols]
        cpad = cpad_ref[:, cols]
        conv = cb_ref[:, cols] + u * cw_ref[CONV_WIDTH - 1:CONV_WIDTH, cols]
        for back in range(1, CONV_WIDTH):
            shifted = jnp.where(pos >= back, pltpu.roll(u, back, axis=0),
                                pltpu.roll(cpad, (back - seg) % rows, axis=0))
            conv = conv + shifted * cw_ref[CONV_WIDTH - 1 - back:CONV_WIDTH - back, cols]
        h = _lru_hidden(conv, h0_ref[:, cols], seg, wab_ref[c], bab_ref[c], lam_ref[:, cols])
        hs_ref[:, cols] = h
        y = h * jax.nn.gelu(gate_ref[:, cols])
        o_ref[:, cols] = y
        sumsq = sumsq + jnp.sum(y * y, axis=-1, keepdims=True)
    o_ref[...] = o_ref[...] * lax.rsqrt(sumsq * (1.0 / LRU_WIDTH) + EPS) * gout_ref[...]


def _lru_weight_specs():
    chunks = LRU_WIDTH // LANES
    return [_const_spec((CONV_WIDTH, LRU_WIDTH)), _const_spec((1, LRU_WIDTH)),
            _const_spec((chunks, LANES, 2 * LANES)), _const_spec((chunks, 1, 2 * LANES)),
            _const_spec((1, LRU_WIDTH)), _const_spec((1, LRU_WIDTH))]


def _lru_gate_chunks(w_a, b_a, w_i, b_i):
    chunks = LRU_WIDTH // LANES
    per = LRU_BLOCKS // chunks
    wa = w_a.reshape(chunks, per, *w_a.shape[1:])
    wi = w_i.reshape(chunks, per, *w_i.shape[1:])
    wab = jnp.stack([jnp.concatenate([_block_diag(wa[c]), _block_diag(wi[c])], axis=1) for c in range(chunks)])
    bab = jnp.concatenate([b_a.reshape(chunks, 1, LANES), b_i.reshape(chunks, 1, LANES)], axis=2)
    return wab.astype(BF16), bab


def _lru_sample(u, gate, conv_pad, h0_rep, seg, conv_w, conv_b, wab, bab, lam, g_out):
    rows = u.shape[0]
    tm = min(ROW_TILE, rows)
    row_spec = pl.BlockSpec((tm, LRU_WIDTH), lambda i: (i, 0))
    return pl.pallas_call(
        functools.partial(_lru_sample_body, seg=seg),
        grid=(rows // tm,),
        in_specs=[row_spec, row_spec, row_spec, row_spec] + _lru_weight_specs(),
        out_specs=[row_spec, row_spec],
        out_shape=[jax.ShapeDtypeStruct((rows, LRU_WIDTH), F32)] * 2,
        compiler_params=_params(1),
    )(u, gate, conv_pad, h0_rep, conv_w, conv_b, wab, bab, lam, g_out)


def _block_diag(w):
    nb, bi, bj = w.shape
    eye = jnp.eye(nb, dtype=w.dtype)
    return jnp.einsum('gij,gh->gihj', w, eye).reshape(nb * bi, nb * bj)


def _bdot_nt(a, b):
    return lax.dot_general(a, b, (((2,), (2,)), ((0,), (0,))), preferred_element_type=F32)


def _bdot(a, b):
    return lax.dot_general(a, b, (((2,), (1,)), ((0,), (0,))), preferred_element_type=F32)


def _swa_pieces(q, keys, vals, sink_ref, mask, g_out, store):
    group = ATTN_HEADS // KV_HEADS
    straight = [h for h in range(ATTN_HEADS) if (h % 2) == (h // group)]
    swapped = [h for h in range(ATTN_HEADS) if (h % 2) != (h // group)]
    scale = HEAD_DIM ** -0.5
    out_half = {}

    def low_lanes(shape):
        return lax.broadcasted_iota(jnp.int32, shape, 2) < HEAD_DIM

    def head_group(heads, swap):
        held = {}

        def arrange(z):
            return pltpu.roll(z, HEAD_DIM, axis=2) if swap else z

        def scores():
            qv = q()
            n, r, _ = qv.shape
            low = low_lanes((n, r, LANES))
            zero = jnp.zeros((n, r, LANES), F32)
            qs = jnp.concatenate(
                [jnp.where(low if h % 2 == 0 else ~low, qv[:, :, (h // 2) * LANES:(h // 2 + 1) * LANES], zero)
                 for h in heads], axis=1)
            held["s"] = _bdot_nt(qs.astype(BF16), arrange(keys()).astype(BF16)) * scale

        def probabilities():
            s = held.pop("s")
            r = s.shape[1] // len(heads)
            visible = mask()[None]
            probs = []
            for i, h in enumerate(heads):
                sh = jnp.where(visible, s[:, i * r:(i + 1) * r, :], -jnp.inf)
                sink = sink_ref[h]
                m = jnp.maximum(jnp.max(sh, axis=-1, keepdims=True), sink)
                e = jnp.exp(sh - m)
                denom = jnp.sum(e, axis=-1, keepdims=True) + jnp.exp(sink - m)
                probs.append(e / denom)
            held["p"] = jnp.concatenate(probs, axis=1).astype(BF16)

        def weighted_values():
            p = held.pop("p")
            r = p.shape[1] // len(heads)
            o = _bdot(p, arrange(vals()).astype(BF16))
            for i, h in enumerate(heads):
                out_half[h] = o[:, i * r:(i + 1) * r, :]

        return [scores, probabilities, weighted_values]

    def finish():
        low = low_lanes(out_half[0].shape)
        out = jnp.concatenate([jnp.where(low, out_half[2 * j], out_half[2 * j + 1])
                               for j in range(Q_WIDTH // LANES)], axis=2)
        store(_rms(out, g_out))

    return head_group(straight, False) + head_group(swapped, True) + [finish]


def _band_mask(first_block):
    i = lax.broadcasted_iota(jnp.int32, (WINDOW, 2 * WINDOW), 0)
    j = lax.broadcasted_iota(jnp.int32, (WINDOW, 2 * WINDOW), 1)
    dist = i + WINDOW - j
    return (dist >= 0) & (dist < WINDOW) & (jnp.logical_not(first_block) | (j >= WINDOW))


def _swa_sample_body(sink_ref, q_ref, k_ref, v_ref, bk_ref, bv_ref, gout_ref, o_ref, nk_ref, nv_ref, *, s_len, past):
    def mask():
        qp = past + lax.broadcasted_iota(jnp.int32, (s_len, WINDOW + s_len), 0)
        col = lax.broadcasted_iota(jnp.int32, (s_len, WINDOW + s_len), 1)
        kp = jnp.where(col < WINDOW, past - WINDOW + col, past + col - WINDOW)
        dist = qp - kp
        return (dist >= 0) & (dist < WINDOW) & (kp >= 0)

    keys = jnp.concatenate([bk_ref[...], k_ref[...]], axis=1)
    vals = jnp.concatenate([bv_ref[...], v_ref[...]], axis=1)

    def store(o):
        o_ref[...] = o

    for piece in _swa_pieces(lambda: q_ref[...], lambda: keys, lambda: vals, sink_ref, mask, gout_ref[...], store):
        piece()
    nk_ref[...] = keys[:, s_len:, :]
    nv_ref[...] = vals[:, s_len:, :]


def _swa_sample(q, k, v, buf_k, buf_v, sink, g_out, past):
    n, s_len, _ = q.shape
    sb = min(SAMPLE_SEQ_TILE, n)

    def seq_spec(rows, width):
        return pl.BlockSpec((sb, rows, width), lambda i: (i, 0, 0))

    buf_spec = seq_spec(WINDOW, KV_WIDTH)
    return pl.pallas_call(
        functools.partial(_swa_sample_body, s_len=s_len, past=past),
        grid=(n // sb,),
        in_specs=[pl.BlockSpec(memory_space=pltpu.SMEM), seq_spec(s_len, Q_WIDTH), seq_spec(s_len, KV_WIDTH),
                  seq_spec(s_len, KV_WIDTH), buf_spec, buf_spec, _const_spec((1, Q_WIDTH))],
        out_specs=[seq_spec(s_len, Q_WIDTH), buf_spec, buf_spec],
        out_shape=[jax.ShapeDtypeStruct((n, s_len, Q_WIDTH), F32),
                   jax.ShapeDtypeStruct((n, WINDOW, KV_WIDTH), F32), jax.ShapeDtypeStruct((n, WINDOW, KV_WIDTH), F32)],
        compiler_params=_params(1),
    )(sink, q, k, v, buf_k, buf_v, g_out)


def _softmax(s):
    e = jnp.exp(s - jnp.max(s, axis=-1, keepdims=True))
    return e / jnp.sum(e, axis=-1, keepdims=True)


def _mix_and_query(x_ref, lru_ref, attn_ref, wout_ref, gx_ref, wcq_ref):
    x = (x_ref[...] + _dot(lru_ref[...].astype(BF16), wout_ref[:LRU_WIDTH, :])
         + _dot(attn_ref[...].astype(BF16), wout_ref[LRU_WIDTH:, :]))
    return x, _dot(_rms(x, gx_ref[...]).astype(BF16), wcq_ref[...])


def _interleave(primary, secondary):
    due = [((i + 1) * len(primary)) // (len(secondary) + 1) for i in range(len(secondary))]
    pending = list(zip(due, secondary))
    for i, piece in enumerate(primary):
        while pending and pending[0][0] <= i:
            pending.pop(0)[1]()
        piece()
    for _, piece in pending:
        piece()


def _swa_mix_ffn_body(sink_ref, q_ref, k_ref, v_ref, kp_ref, vp_ref, gattn_ref,
                      x_ref, lru_ref, wout_ref, gx_ref, wcq_ref, mk_ref, mv_ref, wco_ref,
                      g2_ref, wg_ref, wu_ref, wd_ref, gf_ref, o_ref, attn_ref,
                      *, tt, tiles, tiles_per_seq, final_norm):
    s = pl.program_id(0)
    lead_slot = s % 2
    lag_slot = 1 - lead_slot

    @pl.when(s == 0)
    def _():
        attn_ref[1] = jnp.zeros((tt, Q_WIDTH), F32)

    blocks = tt // WINDOW
    first_pos_block = lax.rem(jnp.minimum(s, tiles - 1), tiles_per_seq) * blocks
    attention_pieces = []
    for j in range(blocks):
        rows = slice(j * WINDOW, (j + 1) * WINDOW)

        def band(ref, prev_ref, j=j, rows=rows):
            prev = prev_ref[...] if j == 0 else ref[(j - 1) * WINDOW:j * WINDOW, :]
            return jnp.concatenate([prev, ref[rows, :]], axis=0)[None]

        def store(o, rows=rows):
            attn_ref[lead_slot, rows, :] = o[0]

        attention_pieces += _swa_pieces(
            lambda rows=rows: q_ref[rows, :][None], functools.partial(band, k_ref, kp_ref),
            functools.partial(band, v_ref, vp_ref), sink_ref,
            lambda j=j: _band_mask(first_pos_block + j == 0), gattn_ref[...], store)

    state = {}
    scale = X_HEAD_DIM ** -0.5

    def mix():
        state["x"] = (x_ref[...] + _dot(lru_ref[...].astype(BF16), wout_ref[:LRU_WIDTH, :])
                      + _dot(attn_ref[lag_slot].astype(BF16), wout_ref[LRU_WIDTH:, :]))

    def query():
        state["q"] = _dot(_rms(state["x"], gx_ref[...]).astype(BF16), wcq_ref[...])

    def memory_head(h):
        def run():
            cols = slice(h * X_HEAD_DIM, (h + 1) * X_HEAD_DIM)
            sc = _dot_nt(state["q"][:, cols].astype(BF16), mk_ref[0, :, cols].astype(BF16)) * scale
            state["o", h] = _dot(_softmax(sc).astype(BF16), mv_ref[0, :, cols].astype(BF16))
        return run

    def memory_out():
        o = jnp.concatenate([state.pop(("o", h)) for h in range(X_HEADS)], axis=1)
        state["x"] = state["x"] + _dot(o.astype(BF16), wco_ref[...])
        state["xn"] = _rms(state["x"], g2_ref[...]).astype(BF16)

    ffn_cols = [slice(f, f + FFN_CHUNK) for f in range(0, D_FF, FFN_CHUNK)]

    def gate_up(f):
        return _dot(state["xn"], wg_ref[:, ffn_cols[f]]), _dot(state["xn"], wu_ref[:, ffn_cols[f]])

    def ffn_piece(f):
        def run():
            gate, up = state.pop("gate_up") if "gate_up" in state else gate_up(f)
            if f + 1 < len(ffn_cols):
                state["gate_up"] = gate_up(f + 1)
            part = _dot((gate * jax.nn.sigmoid(gate) * up).astype(BF16), wd_ref[ffn_cols[f], :])
            state["acc"] = part if "acc" not in state else state["acc"] + part
        return run

    def finish():
        y = state["x"] + 0.5 * state["acc"]
        o_ref[...] = _rms(y, gf_ref[...]) if final_norm else y

    layer_pieces = ([mix, query] + [memory_head(h) for h in range(X_HEADS)] + [memory_out]
                    + [ffn_piece(f) for f in range(len(ffn_cols))] + [finish])
    _interleave(layer_pieces, attention_pieces)


def _swa_mix_ffn(q, k, v, sink, g_attn, x, lru_out, seq, w_out, g_x, w_cq, mk, mv, w_co, g2, wg, wu, wd, g_final,
                 final_norm):
    rows = x.shape[0]
    tt = min(LRU_TILE, seq)
    tiles_per_seq = seq // tt
    tiles = rows // tt
    blocks = tt // WINDOW
    lead = lambda s: jnp.minimum(s, tiles - 1)
    lag = lambda s: jnp.maximum(s - 1, 0)

    def lead_spec(width):
        return pl.BlockSpec((tt, width), lambda s: (lead(s), 0))

    def lag_spec(width):
        return pl.BlockSpec((tt, width), lambda s: (lag(s), 0))

    prev_spec = pl.BlockSpec((WINDOW, KV_WIDTH), lambda s: (jnp.maximum(lead(s) * blocks - 1, 0), 0))
    mem_spec = pl.BlockSpec((1, N_MEM, D_MODEL), lambda s: (lag(s) // tiles_per_seq, 0, 0))
    w_spec = _const_spec((D_MODEL, D_MODEL))
    return pl.pallas_call(
        functools.partial(_swa_mix_ffn_body, tt=tt, tiles=tiles, tiles_per_seq=tiles_per_seq, final_norm=final_norm),
        grid=(tiles + 1,),
        in_specs=[pl.BlockSpec(memory_space=pltpu.SMEM), lead_spec(Q_WIDTH), lead_spec(KV_WIDTH), lead_spec(KV_WIDTH),
                  prev_spec, prev_spec, _const_spec((1, Q_WIDTH)),
                  lag_spec(D_MODEL), lag_spec(LRU_WIDTH), w_spec, _const_spec((1, D_MODEL)), w_spec,
                  mem_spec, mem_spec, w_spec,
                  _const_spec((1, D_MODEL)), _const_spec((D_MODEL, D_FF)), _const_spec((D_MODEL, D_FF)),
                  _const_spec((D_FF, D_MODEL)), _const_spec((1, D_MODEL))],
        out_specs=lag_spec(D_MODEL),
        out_shape=jax.ShapeDtypeStruct((rows, D_MODEL), F32),
        scratch_shapes=[pltpu.VMEM((2, tt, Q_WIDTH), F32)],
        compiler_params=_params(1),
    )(sink, q, k, v, k, v, g_attn, x, lru_out, w_out, g_x, w_cq, mk, mv, w_co, g2, wg, wu, wd, g_final)


def _mix_query_body(x_ref, lru_ref, attn_ref, wout_ref, gx_ref, wcq_ref, x_out_ref, q_ref):
    x_out_ref[...], q_ref[...] = _mix_and_query(x_ref, lru_ref, attn_ref, wout_ref, gx_ref, wcq_ref)


def _mix_query(x, lru_out, attn_out, w_out, g_x, w_cq):
    rows = x.shape[0]
    tm = min(ROW_TILE, rows)

    def row_spec(width):
        return pl.BlockSpec((tm, width), lambda i: (i, 0))

    w_spec = _const_spec((D_MODEL, D_MODEL))
    return pl.pallas_call(
        _mix_query_body,
        grid=(rows // tm,),
        in_specs=[row_spec(D_MODEL), row_spec(LRU_WIDTH), row_spec(Q_WIDTH), w_spec, _const_spec((1, D_MODEL)), w_spec],
        out_specs=[row_spec(D_MODEL), row_spec(D_MODEL)],
        out_shape=[jax.ShapeDtypeStruct((rows, D_MODEL), F32)] * 2,
        compiler_params=_params(1),
    )(x, lru_out, attn_out, w_out, g_x, w_cq)


def _xattn_cache_body(q_ref, mk_ref, mv_ref, o_ref):
    sb, steps, _ = q_ref.shape
    blocks = D_MODEL // LANES
    chunks = X_HEAD_DIM // LANES
    width = mk_ref.shape[1]
    cls = lax.broadcasted_iota(jnp.int32, (sb, steps, width), 2) % blocks
    block_cls = [(j % chunks) * X_HEADS + j // chunks for j in range(blocks)]
    scale = X_HEAD_DIM ** -0.5

    qs = jnp.concatenate([q_ref[:, :, j * LANES:(j + 1) * LANES] for j in range(blocks)], axis=1)
    s = _bdot_nt(qs.astype(BF16), mk_ref[...].astype(BF16)) * scale
    part = jnp.zeros((sb, steps, width), F32)
    for j in range(blocks):
        part = part + jnp.where(cls == block_cls[j], s[:, j * steps:(j + 1) * steps, :], 0.0)
    score = part + pltpu.roll(part, width - X_HEADS, axis=2)
    top = jnp.zeros((sb, steps, width), F32)
    for h in range(X_HEADS):
        mine = cls == h
        top = jnp.where(mine, jnp.max(jnp.where(mine, score, -jnp.inf), axis=-1, keepdims=True), top)
    e = jnp.where(cls < X_HEADS, jnp.exp(score - top), 0.0)
    denom = jnp.ones((sb, steps, width), F32)
    for h in range(X_HEADS):
        mine = cls == h
        denom = jnp.where(mine, jnp.sum(jnp.where(mine, e, 0.0), axis=-1, keepdims=True), denom)
    p = e / denom
    p = p + pltpu.roll(p, X_HEADS, axis=2)
    ps = jnp.concatenate([jnp.where(cls == block_cls[j], p, 0.0) for j in range(blocks)], axis=1)
    o = _bdot(ps.astype(BF16), mv_ref[...].astype(BF16))
    for j in range(blocks):
        o_ref[:, :, j * LANES:(j + 1) * LANES] = o[:, j * steps:(j + 1) * steps, :]


def _interleave_chunks(cache):
    n = cache.shape[0]
    chunks = X_HEAD_DIM // LANES
    c = cache.reshape(n, N_MEM, X_HEADS, chunks, LANES)
    return jnp.transpose(c, (0, 1, 3, 2, 4)).reshape(n, N_MEM * chunks * X_HEADS, LANES)


def _xattn_cache(q, cache_k, cache_v):
    n, s_len, _ = q.shape
    sb = min(XATTN_SEQ_TILE, n)
    q_spec = pl.BlockSpec((sb, s_len, D_MODEL), lambda i: (i, 0, 0))
    rows = cache_k.shape[1]
    mem_spec = pl.BlockSpec((sb, rows, LANES), lambda i: (i, 0, 0))
    return pl.pallas_call(
        _xattn_cache_body,
        grid=(n // sb,),
        in_specs=[q_spec, mem_spec, mem_spec],
        out_specs=q_spec,
        out_shape=jax.ShapeDtypeStruct((n, s_len, D_MODEL), F32),
        compiler_params=_params(1),
    )(q, cache_k, cache_v)


def _mem_kv_body(mem_ref, g_ref, wk_ref, wv_ref, k_ref, v_ref):
    mm = _rms(mem_ref[...], g_ref[...]).astype(BF16)
    k_ref[...] = _dot(mm, wk_ref[...])
    v_ref[...] = _dot(mm, wv_ref[...])


def _mem_kv(mem, g, w_ck, w_cv):
    rows = mem.shape[0]
    tm = min(ROW_TILE, rows)
    row_spec = pl.BlockSpec((tm, D_MODEL), lambda i: (i, 0))
    w_spec = _const_spec((D_MODEL, D_MODEL))
    return pl.pallas_call(
        _mem_kv_body,
        grid=(rows // tm,),
        in_specs=[row_spec, _const_spec((1, D_MODEL)), w_spec, w_spec],
        out_specs=[row_spec, row_spec],
        out_shape=[jax.ShapeDtypeStruct((rows, D_MODEL), F32)] * 2,
        compiler_params=_params(1),
    )(mem, g, w_ck, w_cv)


def kernel(x_prompt, x_sample, mem_prompt, cache_mem_k, cache_mem_v, cache_swa_k, cache_swa_v, state_conv, state_lru_h,
           g_ffn1, w1_gate, w1_up, w1_down, g_mix, w_in, conv_w, conv_b, w_a, b_a, w_i, b_i, lam, sink,
           g_lru_out, g_attn_out, w_out, g_xattn, g_mem, w_cq, w_ck, w_cv, w_co, g_ffn2, w2_gate, w2_up, w2_down,
           g_final):
    nbp, seq, _ = x_prompt.shape
    nbs, dec_seq, _ = x_sample.shape
    depth = g_ffn1.shape[0]
    past = PAST_LEN
    cos_p, sin_p = _rope_tables(jnp.arange(seq, dtype=jnp.int32))
    cos_s, sin_s = _rope_tables(past + jnp.arange(dec_seq, dtype=jnp.int32))
    sample_pos_rows = min(ROW_TILE, nbs * dec_seq)
    cos_s = jnp.tile(cos_s, (sample_pos_rows // dec_seq, 1))
    sin_s = jnp.tile(sin_s, (sample_pos_rows // dec_seq, 1))

    xp = x_prompt.reshape(nbp * seq, D_MODEL)
    xs = x_sample.reshape(nbs * dec_seq, D_MODEL)
    g_fin = g_final.reshape(1, D_MODEL)
    row = lambda a: a.reshape(1, -1)
    outs = [[] for _ in range(10)]
    for l in range(depth):
        last = l == depth - 1
        bf = lambda a: a[l].astype(BF16)
        w1g, w1u, w1d, w2g, w2u, w2d = bf(w1_gate), bf(w1_up), bf(w1_down), bf(w2_gate), bf(w2_up), bf(w2_down)
        win, wout, wcq, wck, wcv, wco = bf(w_in), bf(w_out), bf(w_cq), bf(w_ck), bf(w_cv), bf(w_co)
        wab, bab = _lru_gate_chunks(w_a[l], b_a[l], w_i[l], b_i[l])
        lru_w = (conv_w[l], row(conv_b[l]), wab, bab, row(lam[l]), row(g_lru_out[l]))

        mk_p, mv_p = _mem_kv(mem_prompt.reshape(nbp * N_MEM, D_MODEL), row(g_mem[l]), wck, wcv)
        mk_p = mk_p.reshape(nbp, N_MEM, D_MODEL)
        mv_p = mv_p.reshape(nbp, N_MEM, D_MODEL)
        xp, q, k, v, lru_out, h_last, u_tail = _ffn_proj_lru(xp, seq, row(g_ffn1[l]), w1g, w1u, w1d, row(g_mix[l]), win,
                                                             cos_p, sin_p, *lru_w)
        k3 = k.reshape(nbp, seq, KV_WIDTH)
        v3 = v.reshape(nbp, seq, KV_WIDTH)
        xp = _swa_mix_ffn(q, k, v, sink[l], row(g_attn_out[l]), xp, lru_out, seq, wout, row(g_xattn[l]), wcq,
                          mk_p, mv_p, wco, row(g_ffn2[l]), w2g, w2u, w2d, g_fin, last)
        outs[0].append(mk_p.reshape(nbp, N_MEM, X_HEADS, X_HEAD_DIM))
        outs[1].append(mv_p.reshape(nbp, N_MEM, X_HEADS, X_HEAD_DIM))
        outs[2].append(k3[:, -WINDOW:].reshape(nbp, WINDOW, KV_HEADS, HEAD_DIM))
        outs[3].append(v3[:, -WINDOW:].reshape(nbp, WINDOW, KV_HEADS, HEAD_DIM))
        outs[4].append(u_tail[:, -(CONV_WIDTH - 1):])
        outs[5].append(h_last.reshape(nbp, LRU_WIDTH))

        xs = _ffn(xs, row(g_ffn1[l]), w1g, w1u, w1d, g_fin, False)
        u, gate, q, k, v = _proj(xs, row(g_mix[l]), win, cos_s, sin_s)
        conv_pad = jnp.pad(state_conv[l], ((0, 0), (dec_seq - (CONV_WIDTH - 1), 0), (0, 0)))
        h0_rep = jnp.repeat(state_lru_h[l], dec_seq, axis=0)
        lru_out, hs = _lru_sample(u, gate, conv_pad.reshape(nbs * dec_seq, LRU_WIDTH), h0_rep, dec_seq, *lru_w)
        per_seq = lambda a: a.reshape(nbs, dec_seq, a.shape[-1])
        attn_out, new_k, new_v = _swa_sample(per_seq(q), per_seq(k), per_seq(v),
                                             cache_swa_k[l].reshape(nbs, WINDOW, KV_WIDTH),
                                             cache_swa_v[l].reshape(nbs, WINDOW, KV_WIDTH), sink[l],
                                             row(g_attn_out[l]), past)
        xs, xq = _mix_query(xs, lru_out, attn_out.reshape(nbs * dec_seq, Q_WIDTH), wout, row(g_xattn[l]), wcq)
        xo = _xattn_cache(per_seq(xq), _interleave_chunks(cache_mem_k[l]), _interleave_chunks(cache_mem_v[l]))
        xs = _ffn(xs, row(g_ffn2[l]), w2g, w2u, w2d, g_fin, last, xo.reshape(nbs * dec_seq, D_MODEL), wco)
        outs[6].append(new_k.reshape(nbs, WINDOW, KV_HEADS, HEAD_DIM))
        outs[7].append(new_v.reshape(nbs, WINDOW, KV_HEADS, HEAD_DIM))
        outs[8].append(u.reshape(nbs, dec_seq, LRU_WIDTH)[:, -(CONV_WIDTH - 1):])
        outs[9].append(hs.reshape(nbs, dec_seq, LRU_WIDTH)[:, -1])

    return (xp.reshape(nbp, seq, D_MODEL), xs.reshape(nbs, dec_seq, D_MODEL)) + tuple(jnp.stack(o) for o in outs)
```

```python
import functools

import jax
import jax.numpy as jnp
from jax import lax
from jax.experimental import pallas as pl
from jax.experimental.pallas import tpu as pltpu

F32 = jnp.float32
BF16 = jnp.bfloat16

D_MODEL = 1024
LRU_WIDTH = 512
LRU_BLOCKS = 8
CONV_WIDTH = 4
LRU_C = 8.0
ATTN_HEADS = 8
HEAD_DIM = 64
KV_HEADS = 2
WINDOW = 128
PAST_LEN = 8192
ROPE_THETA = 10000.0
N_MEM = 256
X_HEADS = 4
X_HEAD_DIM = 256
D_FF = 2816
EPS = 1e-6
Q_WIDTH = ATTN_HEADS * HEAD_DIM
KV_WIDTH = KV_HEADS * HEAD_DIM
IN_COLS = 2 * LRU_WIDTH + Q_WIDTH + 2 * KV_WIDTH

LANES = 128
SUBLANES = 8
VMEM_LIMIT = 56 * 1024 * 1024

ROW_TILE = 512
LRU_TILE = 512
FFN_CHUNK = 256
SAMPLE_SEQ_TILE = 8
XATTN_SEQ_TILE = 4


def _params(n_axes):
    return pltpu.CompilerParams(dimension_semantics=("arbitrary",) * n_axes, vmem_limit_bytes=VMEM_LIMIT)


def _const_spec(shape):
    return pl.BlockSpec(shape, lambda *_: (0,) * len(shape), pipeline_mode=pl.Buffered(1))


def _rms(x, g):
    return x * lax.rsqrt(jnp.mean(x * x, axis=-1, keepdims=True) + EPS) * g


def _dot(a, b):
    return jnp.dot(a, b, preferred_element_type=F32)


def _dot_nt(a, b):
    return lax.dot_general(a, b, (((1,), (1,)), ((), ())), preferred_element_type=F32)


def _ffn_step(x, g_ref, wg_ref, wu_ref, wd_ref):
    xn = _rms(x, g_ref[...]).astype(BF16)
    gate = _dot(xn, wg_ref[...])
    up = _dot(xn, wu_ref[...])
    h = (gate * jax.nn.sigmoid(gate) * up).astype(BF16)
    return x + 0.5 * _dot(h, wd_ref[...])


def _ffn_tail(x, g_ref, wg_ref, wu_ref, wd_ref, gf_ref, o_ref, final_norm):
    y = _ffn_step(x, g_ref, wg_ref, wu_ref, wd_ref)
    if final_norm:
        y = _rms(y, gf_ref[...])
    o_ref[...] = y


def _ffn_body(x_ref, g_ref, wg_ref, wu_ref, wd_ref, gf_ref, o_ref, *, final_norm):
    _ffn_tail(x_ref[...], g_ref, wg_ref, wu_ref, wd_ref, gf_ref, o_ref, final_norm)


def _proj_ffn_body(x_ref, a_ref, wa_ref, g_ref, wg_ref, wu_ref, wd_ref, gf_ref, o_ref, *, final_norm):
    x = x_ref[...] + _dot(a_ref[...].astype(BF16), wa_ref[...])
    _ffn_tail(x, g_ref, wg_ref, wu_ref, wd_ref, gf_ref, o_ref, final_norm)


def _ffn(x, g, wg, wu, wd, g_final, final_norm, attn=None, w_attn=None):
    rows = x.shape[0]
    tm = min(ROW_TILE, rows)
    row_spec = pl.BlockSpec((tm, D_MODEL), lambda i: (i, 0))
    ffn_specs = [_const_spec((1, D_MODEL)), _const_spec((D_MODEL, D_FF)), _const_spec((D_MODEL, D_FF)),
                 _const_spec((D_FF, D_MODEL)), _const_spec((1, D_MODEL))]
    if attn is None:
        body, lead_specs, lead = _ffn_body, [row_spec], (x,)
    else:
        body, lead_specs, lead = _proj_ffn_body, [row_spec, row_spec, _const_spec((D_MODEL, D_MODEL))], (x, attn, w_attn)
    return pl.pallas_call(
        functools.partial(body, final_norm=final_norm),
        grid=(rows // tm,),
        in_specs=lead_specs + ffn_specs,
        out_specs=row_spec,
        out_shape=jax.ShapeDtypeStruct((rows, D_MODEL), F32),
        compiler_params=_params(1),
    )(*lead, g, wg, wu, wd, g_final)


def _rope(z, cos, sin_signed):
    half = HEAD_DIM // 2
    lane = lax.broadcasted_iota(jnp.int32, z.shape, 1)
    first_half = (lane % HEAD_DIM) < half
    partner = jnp.where(first_half, pltpu.roll(z, LANES - half, axis=1), pltpu.roll(z, half, axis=1))
    return z * cos + partner * sin_signed


def _project_pieces(x, g_ref, w_ref, cos_ref, sin_ref, store_u, store_gate, q_ref, k_ref, v_ref):
    o_gate, o_q, o_k, o_v = LRU_WIDTH, 2 * LRU_WIDTH, 2 * LRU_WIDTH + Q_WIDTH, 2 * LRU_WIDTH + Q_WIDTH + KV_WIDTH
    xn = []

    def normed():
        if not xn:
            xn.append(_rms(x(), g_ref[...]).astype(BF16))
        return xn[0]

    def rope_into(ref, z):
        for j in range(z.shape[1] // LANES):
            cols = slice(j * LANES, (j + 1) * LANES)
            ref[:, cols] = _rope(z[:, cols], cos_ref[...], sin_ref[...])

    def store_v(z):
        v_ref[...] = z

    return [lambda: rope_into(q_ref, _dot(normed(), w_ref[:, o_q:o_k])),
            lambda: rope_into(k_ref, _dot(normed(), w_ref[:, o_k:o_v])),
            lambda: store_v(_dot(normed(), w_ref[:, o_v:])),
            lambda: store_u(_dot(normed(), w_ref[:, :o_gate])),
            lambda: store_gate(_dot(normed(), w_ref[:, o_gate:o_q]))]


def _proj_body(x_ref, g_ref, w_ref, cos_ref, sin_ref, u_ref, gate_ref, q_ref, k_ref, v_ref):
    def store_u(z):
        u_ref[...] = z

    def store_gate(z):
        gate_ref[...] = z

    for piece in _project_pieces(lambda: x_ref[...], g_ref, w_ref, cos_ref, sin_ref, store_u, store_gate,
                                 q_ref, k_ref, v_ref):
        piece()


def _proj(x, g, w_in, cos, sin):
    rows = x.shape[0]
    tm = min(ROW_TILE, rows, cos.shape[0])
    pos_blocks = cos.shape[0] // tm

    def row_spec(width):
        return pl.BlockSpec((tm, width), lambda i: (i, 0))

    pos_spec = pl.BlockSpec((tm, LANES), lambda i: (i % pos_blocks, 0))
    widths = (LRU_WIDTH, LRU_WIDTH, Q_WIDTH, KV_WIDTH, KV_WIDTH)
    return pl.pallas_call(
        _proj_body,
        grid=(rows // tm,),
        in_specs=[row_spec(D_MODEL), _const_spec((1, D_MODEL)), _const_spec((D_MODEL, IN_COLS)), pos_spec, pos_spec],
        out_specs=[row_spec(w) for w in widths],
        out_shape=[jax.ShapeDtypeStruct((rows, w), F32) for w in widths],
        compiler_params=_params(1),
    )(x, g, w_in, cos, sin)


def _rope_tables(pos):
    half = HEAD_DIM // 2
    inv = ROPE_THETA ** (-jnp.arange(half, dtype=F32) / half)
    ang = pos.astype(F32)[:, None] * inv[None, :]
    cos = jnp.cos(ang)
    sin = jnp.sin(ang)
    reps = LANES // HEAD_DIM
    return (jnp.tile(jnp.concatenate([cos, cos], axis=-1), (1, reps)),
            jnp.tile(jnp.concatenate([-sin, sin], axis=-1), (1, reps)))


def _softplus(x):
    return jnp.maximum(x, 0.0) + jnp.log1p(jnp.exp(-jnp.abs(x)))


def _lru_coeffs(conv, wab, bab, lam):
    w = conv.shape[1]
    gates = _dot(conv.astype(BF16), wab) + bab
    r = jax.nn.sigmoid(gates[:, :w])
    gi = jax.nn.sigmoid(gates[:, w:])
    log_a = -LRU_C * r * _softplus(-lam)
    a = jnp.exp(log_a)
    b = jnp.sqrt(-jnp.tanh(log_a) * (a * a + 1.0)) * (gi * conv)
    return a, b


def _segment_scan(a, b, seg):
    step = 1
    while step < seg:
        a, b = _scan_step(a, b, seg, step)
        step *= 2
    return a, b


def _scan_step(a, b, seg, step):
    pos = lax.broadcasted_iota(jnp.int32, a.shape, 0) % seg
    live = pos >= step
    a_prev = pltpu.roll(a, step, axis=0)
    b_prev = pltpu.roll(b, step, axis=0)
    return jnp.where(live, a * a_prev, a), jnp.where(live, a * b_prev + b, b)


def _lru_hidden(conv, h_in, seg, wab, bab, lam):
    a, b = _lru_coeffs(conv, wab, bab, lam)
    a_cum, h_local = _segment_scan(a, b, seg)
    return a_cum * h_in + h_local


def _lane_chunks(width):
    return [slice(c * LANES, (c + 1) * LANES) for c in range(width // LANES)]


def _ffn_chunk(xn, wg_ref, wu_ref, wd_ref, cols):
    gate = _dot(xn, wg_ref[:, cols])
    up = _dot(xn, wu_ref[:, cols])
    return _dot((gate * jax.nn.sigmoid(gate) * up).astype(BF16), wd_ref[cols, :])


def _ffn_proj_lru_body(x_ref, g1_ref, wg_ref, wu_ref, wd_ref, gmix_ref, win_ref, cos_ref, sin_ref,
                       cw_ref, cb_ref, wab_ref, bab_ref, lam_ref, gout_ref,
                       x_out_ref, q_ref, k_ref, v_ref, lru_ref, hlast_ref, utail_ref,
                       ug_ref, ext_ref, h_ref, hs_ref, *, tt, tiles_per_seq):
    pad = SUBLANES
    s = pl.program_id(0)
    lead_slot = s % 2
    lag_slot = 1 - lead_slot

    @pl.when(s == 0)
    def _():
        ug_ref[1] = jnp.zeros((2, tt, LRU_WIDTH), F32)

    @pl.when((s == 0) | (lax.rem(s - 1, tiles_per_seq) == 0))
    def _():
        ext_ref[0:pad, :] = jnp.zeros((pad, LRU_WIDTH), F32)
        h_ref[...] = jnp.zeros_like(h_ref)

    state = {"sumsq": jnp.zeros((tt, 1), F32)}
    scan_steps = [1 << i for i in range(tt.bit_length() - 1)]

    def lru_pieces(c, cols):
        def coeffs():
            ext_ref[pad:pad + tt, cols] = ug_ref[lag_slot, 0, :, cols]
            conv = cb_ref[:, cols]
            for j in range(CONV_WIDTH):
                start = pad - (CONV_WIDTH - 1) + j
                conv = conv + ext_ref[start:start + tt, cols] * cw_ref[j:j + 1, cols]
            ext_ref[0:pad, cols] = ext_ref[tt:tt + pad, cols]
            utail_ref[0, :, cols] = ext_ref[0:pad, cols]
            state[c] = _lru_coeffs(conv, wab_ref[c], bab_ref[c], lam_ref[:, cols])

        def scan(steps):
            def run():
                for step in steps:
                    state[c] = _scan_step(*state[c], tt, step)
            return run

        def finish():
            a_cum, h_local = state.pop(c)
            hs_ref[:, cols] = a_cum * h_ref[:, cols] + h_local
            h_ref[:, cols] = hs_ref[tt - 1:tt, cols]
            hlast_ref[0, :, cols] = h_ref[:, cols]
            y = hs_ref[:, cols] * jax.nn.gelu(ug_ref[lag_slot, 1, :, cols])
            hs_ref[:, cols] = y
            state["sumsq"] = state["sumsq"] + jnp.sum(y * y, axis=-1, keepdims=True)

        half = len(scan_steps) // 2
        return [coeffs, scan(scan_steps[:half]), scan(scan_steps[half:]), finish]

    vector_pieces = [p for c, cols in enumerate(_lane_chunks(LRU_WIDTH)) for p in lru_pieces(c, cols)]

    x_in = x_ref[...]
    xn = _rms(x_in, g1_ref[...]).astype(BF16)

    ffn_cols = [slice(f, f + FFN_CHUNK) for f in range(0, D_FF, FFN_CHUNK)]

    def gate_up(f):
        return _dot(xn, wg_ref[:, ffn_cols[f]]), _dot(xn, wu_ref[:, ffn_cols[f]])

    def ffn_piece(f):
        def run():
            gate, up = state.pop("gate_up") if "gate_up" in state else gate_up(f)
            if f + 1 < len(ffn_cols):
                state["gate_up"] = gate_up(f + 1)
            part = _dot((gate * jax.nn.sigmoid(gate) * up).astype(BF16), wd_ref[ffn_cols[f], :])
            state["acc"] = part if "acc" not in state else state["acc"] + part
        return run

    def ffn_out():
        if "x" not in state:
            state["x"] = x_in + 0.5 * state.pop("acc")
            x_out_ref[...] = state["x"]
        return state["x"]

    def store_u(z):
        ug_ref[lead_slot, 0] = z

    def store_gate(z):
        ug_ref[lead_slot, 1] = z

    matmul_pieces = [ffn_piece(f) for f in range(len(ffn_cols))]
    matmul_pieces += _project_pieces(ffn_out, gmix_ref, win_ref, cos_ref, sin_ref, store_u, store_gate,
                                     q_ref, k_ref, v_ref)

    assert len(vector_pieces) == len(matmul_pieces)
    for vector_piece, matmul_piece in zip(vector_pieces, matmul_pieces):
        vector_piece()
        matmul_piece()
    lru_ref[...] = hs_ref[...] * lax.rsqrt(state["sumsq"] * (1.0 / LRU_WIDTH) + EPS) * gout_ref[...]


def _ffn_proj_lru(x, seq, g1, wg, wu, wd, g_mix, w_in, cos, sin, conv_w, conv_b, wab, bab, lam, g_out):
    rows = x.shape[0]
    n = rows // seq
    tt = min(LRU_TILE, seq)
    tiles_per_seq = seq // tt
    tiles = rows // tt
    pos_blocks = cos.shape[0] // tt
    lead = lambda s: jnp.minimum(s, tiles - 1)
    lag = lambda s: jnp.maximum(s - 1, 0)

    def lead_spec(width):
        return pl.BlockSpec((tt, width), lambda s: (lead(s), 0))

    pos_spec = pl.BlockSpec((tt, LANES), lambda s: (lead(s) % pos_blocks, 0))
    seq_spec = lambda r: pl.BlockSpec((1, r, LRU_WIDTH), lambda s: (lag(s) // tiles_per_seq, 0, 0))
    widths = (D_MODEL, Q_WIDTH, KV_WIDTH, KV_WIDTH)
    return pl.pallas_call(
        functools.partial(_ffn_proj_lru_body, tt=tt, tiles_per_seq=tiles_per_seq),
        grid=(tiles + 1,),
        in_specs=[lead_spec(D_MODEL), _const_spec((1, D_MODEL)), _const_spec((D_MODEL, D_FF)), _const_spec((D_MODEL, D_FF)),
                  _const_spec((D_FF, D_MODEL)), _const_spec((1, D_MODEL)), _const_spec((D_MODEL, IN_COLS)),
                  pos_spec, pos_spec] + _lru_weight_specs(),
        out_specs=[lead_spec(w) for w in widths]
        + [pl.BlockSpec((tt, LRU_WIDTH), lambda s: (lag(s), 0)), seq_spec(1), seq_spec(SUBLANES)],
        out_shape=[jax.ShapeDtypeStruct((rows, w), F32) for w in widths]
        + [jax.ShapeDtypeStruct((rows, LRU_WIDTH), F32), jax.ShapeDtypeStruct((n, 1, LRU_WIDTH), F32),
           jax.ShapeDtypeStruct((n, SUBLANES, LRU_WIDTH), F32)],
        scratch_shapes=[pltpu.VMEM((2, 2, tt, LRU_WIDTH), F32), pltpu.VMEM((tt + SUBLANES, LRU_WIDTH), F32),
                        pltpu.VMEM((1, LRU_WIDTH), F32), pltpu.VMEM((tt, LRU_WIDTH), F32)],
        compiler_params=_params(1),
    )(x, g1, wg, wu, wd, g_mix, w_in, cos, sin, conv_w, conv_b, wab, bab, lam, g_out)


def _lru_sample_body(u_ref, gate_ref, cpad_ref, h0_ref, cw_ref, cb_ref, wab_ref, bab_ref, lam_ref, gout_ref,
                     o_ref, hs_ref, *, seg):
    rows = u_ref.shape[0]
    pos = lax.broadcasted_iota(jnp.int32, (rows, LANES), 0) % seg
    sumsq = jnp.zeros((rows, 1), F32)
    for c, cols in enumerate(_lane_chunks(LRU_WIDTH)):
        u = u_ref[:, cols]
        cpad = cpad_ref[:, cols]
        conv = cb_ref[:, cols] + u * cw_ref[CONV_WIDTH - 1:CONV_WIDTH, cols]
        for back in range(1, CONV_WIDTH):
            shifted = jnp.where(pos >= back, pltpu.roll(u, back, axis=0),
                                pltpu.roll(cpad, (back - seg) % rows, axis=0))
            conv = conv + shifted * cw_ref[CONV_WIDTH - 1 - back:CONV_WIDTH - back, cols]
        h = _lru_hidden(conv, h0_ref[:, cols], seg, wab_ref[c], bab_ref[c], lam_ref[:, cols])
        hs_ref[:, cols] = h
        y = h * jax.nn.gelu(gate_ref[:, cols])
        o_ref[:, cols] = y
        sumsq = sumsq + jnp.sum(y * y, axis=-1, keepdims=True)
    o_ref[...] = o_ref[...] * lax.rsqrt(sumsq * (1.0 / LRU_WIDTH) + EPS) * gout_ref[...]


def _lru_weight_specs():
    chunks = LRU_WIDTH // LANES
    return [_const_spec((CONV_WIDTH, LRU_WIDTH)), _const_spec((1, LRU_WIDTH)),
            _const_spec((chunks, LANES, 2 * LANES)), _const_spec((chunks, 1, 2 * LANES)),
            _const_spec((1, LRU_WIDTH)), _const_spec((1, LRU_WIDTH))]


def _lru_gate_chunks(w_a, b_a, w_i, b_i):
    chunks = LRU_WIDTH // LANES
    per = LRU_BLOCKS // chunks
    wa = w_a.reshape(chunks, per, *w_a.shape[1:])
    wi = w_i.reshape(chunks, per, *w_i.shape[1:])
    wab = jnp.stack([jnp.concatenate([_block_diag(wa[c]), _block_diag(wi[c])], axis=1) for c in range(chunks)])
    bab = jnp.concatenate([b_a.reshape(chunks, 1, LANES), b_i.reshape(chunks, 1, LANES)], axis=2)
    return wab.astype(BF16), bab


def _lru_sample(u, gate, conv_pad, h0_rep, seg, conv_w, conv_b, wab, bab, lam, g_out):
    rows = u.shape[0]
    tm = min(ROW_TILE, rows)
    row_spec = pl.BlockSpec((tm, LRU_WIDTH), lambda i: (i, 0))
    return pl.pallas_call(
        functools.partial(_lru_sample_body, seg=seg),
        grid=(rows // tm,),
        in_specs=[row_spec, row_spec, row_spec, row_spec] + _lru_weight_specs(),
        out_specs=[row_spec, row_spec],
        out_shape=[jax.ShapeDtypeStruct((rows, LRU_WIDTH), F32)] * 2,
        compiler_params=_params(1),
    )(u, gate, conv_pad, h0_rep, conv_w, conv_b, wab, bab, lam, g_out)


def _block_diag(w):
    nb, bi, bj = w.shape
    eye = jnp.eye(nb, dtype=w.dtype)
    return jnp.einsum('gij,gh->gihj', w, eye).reshape(nb * bi, nb * bj)


def _bdot_nt(a, b):
    return lax.dot_general(a, b, (((2,), (2,)), ((0,), (0,))), preferred_element_type=F32)


def _bdot(a, b):
    return lax.dot_general(a, b, (((2,), (1,)), ((0,), (0,))), preferred_element_type=F32)


def _swa_pieces(q, keys, vals, sink_ref, mask, g_out, store):
    group = ATTN_HEADS // KV_HEADS
    straight = [h for h in range(ATTN_HEADS) if (h % 2) == (h // group)]
    swapped = [h for h in range(ATTN_HEADS) if (h % 2) != (h // group)]
    scale = HEAD_DIM ** -0.5
    out_half = {}

    def low_lanes(shape):
        return lax.broadcasted_iota(jnp.int32, shape, 2) < HEAD_DIM

    def head_group(heads, swap):
        held = {}

        def arrange(z):
            return pltpu.roll(z, HEAD_DIM, axis=2) if swap else z

        def scores():
            qv = q()
            n, r, _ = qv.shape
            low = low_lanes((n, r, LANES))
            zero = jnp.zeros((n, r, LANES), F32)
            qs = jnp.concatenate(
                [jnp.where(low if h % 2 == 0 else ~low, qv[:, :, (h // 2) * LANES:(h // 2 + 1) * LANES], zero)
                 for h in heads], axis=1)
            held["s"] = _bdot_nt(qs.astype(BF16), arrange(keys()).astype(BF16)) * scale

        def probabilities():
            s = held.pop("s")
            r = s.shape[1] // len(heads)
            visible = mask()[None]
            probs = []
            for i, h in enumerate(heads):
                sh = jnp.where(visible, s[:, i * r:(i + 1) * r, :], -jnp.inf)
                sink = sink_ref[h]
                m = jnp.maximum(jnp.max(sh, axis=-1, keepdims=True), sink)
                e = jnp.exp(sh - m)
                denom = jnp.sum(e, axis=-1, keepdims=True) + jnp.exp(sink - m)
                probs.append(e / denom)
            held["p"] = jnp.concatenate(probs, axis=1).astype(BF16)

        def weighted_values():
            p = held.pop("p")
            r = p.shape[1] // len(heads)
            o = _bdot(p, arrange(vals()).astype(BF16))
            for i, h in enumerate(heads):
                out_half[h] = o[:, i * r:(i + 1) * r, :]

        return [scores, probabilities, weighted_values]

    def finish():
        low = low_lanes(out_half[0].shape)
        out = jnp.concatenate([jnp.where(low, out_half[2 * j], out_half[2 * j + 1])
                               for j in range(Q_WIDTH // LANES)], axis=2)
        store(_rms(out, g_out))

    return head_group(straight, False) + head_group(swapped, True) + [finish]


def _band_mask(first_block):
    i = lax.broadcasted_iota(jnp.int32, (WINDOW, 2 * WINDOW), 0)
    j = lax.broadcasted_iota(jnp.int32, (WINDOW, 2 * WINDOW), 1)
    dist = i + WINDOW - j
    return (dist >= 0) & (dist < WINDOW) & (jnp.logical_not(first_block) | (j >= WINDOW))


def _swa_sample_body(sink_ref, q_ref, k_ref, v_ref, bk_ref, bv_ref, gout_ref, o_ref, nk_ref, nv_ref, *, s_len, past):
    def mask():
        qp = past + lax.broadcasted_iota(jnp.int32, (s_len, WINDOW + s_len), 0)
        col = lax.broadcasted_iota(jnp.int32, (s_len, WINDOW + s_len), 1)
        kp = jnp.where(col < WINDOW, past - WINDOW + col, past + col - WINDOW)
        dist = qp - kp
        return (dist >= 0) & (dist < WINDOW) & (kp >= 0)

    keys = jnp.concatenate([bk_ref[...], k_ref[...]], axis=1)
    vals = jnp.concatenate([bv_ref[...], v_ref[...]], axis=1)

    def store(o):
        o_ref[...] = o

    for piece in _swa_pieces(lambda: q_ref[...], lambda: keys, lambda: vals, sink_ref, mask, gout_ref[...], store):
        piece()
    nk_ref[...] = keys[:, s_len:, :]
    nv_ref[...] = vals[:, s_len:, :]


def _swa_sample(q, k, v, buf_k, buf_v, sink, g_out, past):
    n, s_len, _ = q.shape
    sb = min(SAMPLE_SEQ_TILE, n)

    def seq_spec(rows, width):
        return pl.BlockSpec((sb, rows, width), lambda i: (i, 0, 0))

    buf_spec = seq_spec(WINDOW, KV_WIDTH)
    return pl.pallas_call(
        functools.partial(_swa_sample_body, s_len=s_len, past=past),
        grid=(n // sb,),
        in_specs=[pl.BlockSpec(memory_space=pltpu.SMEM), seq_spec(s_len, Q_WIDTH), seq_spec(s_len, KV_WIDTH),
                  seq_spec(s_len, KV_WIDTH), buf_spec, buf_spec, _const_spec((1, Q_WIDTH))],
        out_specs=[seq_spec(s_len, Q_WIDTH), buf_spec, buf_spec],
        out_shape=[jax.ShapeDtypeStruct((n, s_len, Q_WIDTH), F32),
                   jax.ShapeDtypeStruct((n, WINDOW, KV_WIDTH), F32), jax.ShapeDtypeStruct((n, WINDOW, KV_WIDTH), F32)],
        compiler_params=_params(1),
    )(sink, q, k, v, buf_k, buf_v, g_out)


def _softmax(s):
    e = jnp.exp(s - jnp.max(s, axis=-1, keepdims=True))
    return e / jnp.sum(e, axis=-1, keepdims=True)


def _mix_and_query(x_ref, lru_ref, attn_ref, wout_ref, gx_ref, wcq_ref):
    x = (x_ref[...] + _dot(lru_ref[...].astype(BF16), wout_ref[:LRU_WIDTH, :])
         + _dot(attn_ref[...].astype(BF16), wout_ref[LRU_WIDTH:, :]))
    return x, _dot(_rms(x, gx_ref[...]).astype(BF16), wcq_ref[...])


def _interleave(primary, secondary):
    due = [((i + 1) * len(primary)) // (len(secondary) + 1) for i in range(len(secondary))]
    pending = list(zip(due, secondary))
    for i, piece in enumerate(primary):
        while pending and pending[0][0] <= i:
            pending.pop(0)[1]()
        piece()
    for _, piece in pending:
        piece()


def _swa_mix_ffn_body(sink_ref, q_ref, k_ref, v_ref, kp_ref, vp_ref, gattn_ref,
                      x_ref, lru_ref, wout_ref, gx_ref, wcq_ref, mk_ref, mv_ref, wco_ref,
                      g2_ref, wg_ref, wu_ref, wd_ref, gf_ref, o_ref, attn_ref,
                      *, tt, tiles, tiles_per_seq, final_norm):
    s = pl.program_id(0)
    lead_slot = s % 2
    lag_slot = 1 - lead_slot

    @pl.when(s == 0)
    def _():
        attn_ref[1] = jnp.zeros((tt, Q_WIDTH), F32)

    blocks = tt // WINDOW
    first_pos_block = lax.rem(jnp.minimum(s, tiles - 1), tiles_per_seq) * blocks
    attention_pieces = []
    for j in range(blocks):
        rows = slice(j * WINDOW, (j + 1) * WINDOW)

        def band(ref, prev_ref, j=j, rows=rows):
            prev = prev_ref[...] if j == 0 else ref[(j - 1) * WINDOW:j * WINDOW, :]
            return jnp.concatenate([prev, ref[rows, :]], axis=0)[None]

        def store(o, rows=rows):
            attn_ref[lead_slot, rows, :] = o[0]

        attention_pieces += _swa_pieces(
            lambda rows=rows: q_ref[rows, :][None], functools.partial(band, k_ref, kp_ref),
            functools.partial(band, v_ref, vp_ref), sink_ref,
            lambda j=j: _band_mask(first_pos_block + j == 0), gattn_ref[...], store)

    state = {}
    scale = X_HEAD_DIM ** -0.5

    def mix():
        state["x"] = (x_ref[...] + _dot(lru_ref[...].astype(BF16), wout_ref[:LRU_WIDTH, :])
                      + _dot(attn_ref[lag_slot].astype(BF16), wout_ref[LRU_WIDTH:, :]))

    def query():
        state["q"] = _dot(_rms(state["x"], gx_ref[...]).astype(BF16), wcq_ref[...])

    def memory_head(h):
        def run():
            cols = slice(h * X_HEAD_DIM, (h + 1) * X_HEAD_DIM)
            sc = _dot_nt(state["q"][:, cols].astype(BF16), mk_ref[0, :, cols].astype(BF16)) * scale
            state["o", h] = _dot(_softmax(sc).astype(BF16), mv_ref[0, :, cols].astype(BF16))
        return run

    def memory_out():
        o = jnp.concatenate([state.pop(("o", h)) for h in range(X_HEADS)], axis=1)
        state["x"] = state["x"] + _dot(o.astype(BF16), wco_ref[...])
        state["xn"] = _rms(state["x"], g2_ref[...]).astype(BF16)

    ffn_cols = [slice(f, f + FFN_CHUNK) for f in range(0, D_FF, FFN_CHUNK)]

    def gate_up(f):
        return _dot(state["xn"], wg_ref[:, ffn_cols[f]]), _dot(state["xn"], wu_ref[:, ffn_cols[f]])

    def ffn_piece(f):
        def run():
            gate, up = state.pop("gate_up") if "gate_up" in state else gate_up(f)
            if f + 1 < len(ffn_cols):
                state["gate_up"] = gate_up(f + 1)
            part = _dot((gate * jax.nn.sigmoid(gate) * up).astype(BF16), wd_ref[ffn_cols[f], :])
            state["acc"] = part if "acc" not in state else state["acc"] + part
        return run

    def finish():
        y = state["x"] + 0.5 * state["acc"]
        o_ref[...] = _rms(y, gf_ref[...]) if final_norm else y

    layer_pieces = ([mix, query] + [memory_head(h) for h in range(X_HEADS)] + [memory_out]
                    + [ffn_piece(f) for f in range(len(ffn_cols))] + [finish])
    _interleave(layer_pieces, attention_pieces)


def _swa_mix_ffn(q, k, v, sink, g_attn, x, lru_out, seq, w_out, g_x, w_cq, mk, mv, w_co, g2, wg, wu, wd, g_final,
                 final_norm):
    rows = x.shape[0]
    tt = min(LRU_TILE, seq)
    tiles_per_seq = seq // tt
    tiles = rows // tt
    blocks = tt // WINDOW
    lead = lambda s: jnp.minimum(s, tiles - 1)
    lag = lambda s: jnp.maximum(s - 1, 0)

    def lead_spec(width):
        return pl.BlockSpec((tt, width), lambda s: (lead(s), 0))

    def lag_spec(width):
        return pl.BlockSpec((tt, width), lambda s: (lag(s), 0))

    prev_spec = pl.BlockSpec((WINDOW, KV_WIDTH), lambda s: (jnp.maximum(lead(s) * blocks - 1, 0), 0))
    mem_spec = pl.BlockSpec((1, N_MEM, D_MODEL), lambda s: (lag(s) // tiles_per_seq, 0, 0))
    w_spec = _const_spec((D_MODEL, D_MODEL))
    return pl.pallas_call(
        functools.partial(_swa_mix_ffn_body, tt=tt, tiles=tiles, tiles_per_seq=tiles_per_seq, final_norm=final_norm),
        grid=(tiles + 1,),
        in_specs=[pl.BlockSpec(memory_space=pltpu.SMEM), lead_spec(Q_WIDTH), lead_spec(KV_WIDTH), lead_spec(KV_WIDTH),
                  prev_spec, prev_spec, _const_spec((1, Q_WIDTH)),
                  lag_spec(D_MODEL), lag_spec(LRU_WIDTH), w_spec, _const_spec((1, D_MODEL)), w_spec,
                  mem_spec, mem_spec, w_spec,
                  _const_spec((1, D_MODEL)), _const_spec((D_MODEL, D_FF)), _const_spec((D_MODEL, D_FF)),
                  _const_spec((D_FF, D_MODEL)), _const_spec((1, D_MODEL))],
        out_specs=lag_spec(D_MODEL),
        out_shape=jax.ShapeDtypeStruct((rows, D_MODEL), F32),
        scratch_shapes=[pltpu.VMEM((2, tt, Q_WIDTH), F32)],
        compiler_params=_params(1),
    )(sink, q, k, v, k, v, g_attn, x, lru_out, w_out, g_x, w_cq, mk, mv, w_co, g2, wg, wu, wd, g_final)


def _mix_query_body(x_ref, lru_ref, attn_ref, wout_ref, gx_ref, wcq_ref, x_out_ref, q_ref):
    x_out_ref[...], q_ref[...] = _mix_and_query(x_ref, lru_ref, attn_ref, wout_ref, gx_ref, wcq_ref)


def _mix_query(x, lru_out, attn_out, w_out, g_x, w_cq):
    rows = x.shape[0]
    tm = min(ROW_TILE, rows)

    def row_spec(width):
        return pl.BlockSpec((tm, width), lambda i: (i, 0))

    w_spec = _const_spec((D_MODEL, D_MODEL))
    return pl.pallas_call(
        _mix_query_body,
        grid=(rows // tm,),
        in_specs=[row_spec(D_MODEL), row_spec(LRU_WIDTH), row_spec(Q_WIDTH), w_spec, _const_spec((1, D_MODEL)), w_spec],
        out_specs=[row_spec(D_MODEL), row_spec(D_MODEL)],
        out_shape=[jax.ShapeDtypeStruct((rows, D_MODEL), F32)] * 2,
        compiler_params=_params(1),
    )(x, lru_out, attn_out, w_out, g_x, w_cq)


def _xattn_cache_body(q_ref, mk_ref, mv_ref, o_ref):
    sb, steps, _ = q_ref.shape
    blocks = D_MODEL // LANES
    chunks = X_HEAD_DIM // LANES
    width = mk_ref.shape[1]
    cls = lax.broadcasted_iota(jnp.int32, (sb, steps, width), 2) % blocks
    block_cls = [(j % chunks) * X_HEADS + j // chunks for j in range(blocks)]
    scale = X_HEAD_DIM ** -0.5

    qs = jnp.concatenate([q_ref[:, :, j * LANES:(j + 1) * LANES] for j in range(blocks)], axis=1)
    s = _bdot_nt(qs.astype(BF16), mk_ref[...].astype(BF16)) * scale
    part = jnp.zeros((sb, steps, width), F32)
    for j in range(blocks):
        part = part + jnp.where(cls == block_cls[j], s[:, j * steps:(j + 1) * steps, :], 0.0)
    score = part + pltpu.roll(part, width - X_HEADS, axis=2)
    top = jnp.zeros((sb, steps, width), F32)
    for h in range(X_HEADS):
        mine = cls == h
        top = jnp.where(mine, jnp.max(jnp.where(mine, score, -jnp.inf), axis=-1, keepdims=True), top)
    e = jnp.where(cls < X_HEADS, jnp.exp(score - top), 0.0)
    denom = jnp.ones((sb, steps, width), F32)
    for h in range(X_HEADS):
        mine = cls == h
        denom = jnp.where(mine, jnp.sum(jnp.where(mine, e, 0.0), axis=-1, keepdims=True), denom)
    p = e / denom
    p = p + pltpu.roll(p, X_HEADS, axis=2)
    ps = jnp.concatenate([jnp.where(cls == block_cls[j], p, 0.0) for j in range(blocks)], axis=1)
    o = _bdot(ps.astype(BF16), mv_ref[...].astype(BF16))
    for j in range(blocks):
        o_ref[:, :, j * LANES:(j + 1) * LANES] = o[:, j * steps:(j + 1) * steps, :]


def _interleave_chunks(cache):
    n = cache.shape[0]
    chunks = X_HEAD_DIM // LANES
    c = cache.reshape(n, N_MEM, X_HEADS, chunks, LANES)
    return jnp.transpose(c, (0, 1, 3, 2, 4)).reshape(n, N_MEM * chunks * X_HEADS, LANES)


def _xattn_cache(q, cache_k, cache_v):
    n, s_len, _ = q.shape
    sb = min(XATTN_SEQ_TILE, n)
    q_spec = pl.BlockSpec((sb, s_len, D_MODEL), lambda i: (i, 0, 0))
    rows = cache_k.shape[1]
    mem_spec = pl.BlockSpec((sb, rows, LANES), lambda i: (i, 0, 0))
    return pl.pallas_call(
        _xattn_cache_body,
        grid=(n // sb,),
        in_specs=[q_spec, mem_spec, mem_spec],
        out_specs=q_spec,
        out_shape=jax.ShapeDtypeStruct((n, s_len, D_MODEL), F32),
        compiler_params=_params(1),
    )(q, cache_k, cache_v)


def _mem_kv_body(mem_ref, g_ref, wk_ref, wv_ref, k_ref, v_ref):
    mm = _rms(mem_ref[...], g_ref[...]).astype(BF16)
    k_ref[...] = _dot(mm, wk_ref[...])
    v_ref[...] = _dot(mm, wv_ref[...])


def _mem_kv(mem, g, w_ck, w_cv):
    rows = mem.shape[0]
    tm = min(ROW_TILE, rows)
    row_spec = pl.BlockSpec((tm, D_MODEL), lambda i: (i, 0))
    w_spec = _const_spec((D_MODEL, D_MODEL))
    return pl.pallas_call(
        _mem_kv_body,
        grid=(rows // tm,),
        in_specs=[row_spec, _const_spec((1, D_MODEL)), w_spec, w_spec],
        out_specs=[row_spec, row_spec],
        out_shape=[jax.ShapeDtypeStruct((rows, D_MODEL), F32)] * 2,
        compiler_params=_params(1),
    )(mem, g, w_ck, w_cv)


def kernel(x_prompt, x_sample, mem_prompt, cache_mem_k, cache_mem_v, cache_swa_k, cache_swa_v, state_conv, state_lru_h,
           g_ffn1, w1_gate, w1_up, w1_down, g_mix, w_in, conv_w, conv_b, w_a, b_a, w_i, b_i, lam, sink,
           g_lru_out, g_attn_out, w_out, g_xattn, g_mem, w_cq, w_ck, w_cv, w_co, g_ffn2, w2_gate, w2_up, w2_down,
           g_final):
    nbp, seq, _ = x_prompt.shape
    nbs, dec_seq, _ = x_sample.shape
    depth = g_ffn1.shape[0]
    past = PAST_LEN
    cos_p, sin_p = _rope_tables(jnp.arange(seq, dtype=jnp.int32))
    cos_s, sin_s = _rope_tables(past + jnp.arange(dec_seq, dtype=jnp.int32))
    sample_pos_rows = min(ROW_TILE, nbs * dec_seq)
    cos_s = jnp.tile(cos_s, (sample_pos_rows // dec_seq, 1))
    sin_s = jnp.tile(sin_s, (sample_pos_rows // dec_seq, 1))

    xp = x_prompt.reshape(nbp * seq, D_MODEL)
    xs = x_sample.reshape(nbs * dec_seq, D_MODEL)
    g_fin = g_final.reshape(1, D_MODEL)
    row = lambda a: a.reshape(1, -1)
    outs = [[] for _ in range(10)]
    for l in range(depth):
        last = l == depth - 1
        bf = lambda a: a[l].astype(BF16)
        w1g, w1u, w1d, w2g, w2u, w2d = bf(w1_gate), bf(w1_up), bf(w1_down), bf(w2_gate), bf(w2_up), bf(w2_down)
        win, wout, wcq, wck, wcv, wco = bf(w_in), bf(w_out), bf(w_cq), bf(w_ck), bf(w_cv), bf(w_co)
        wab, bab = _lru_gate_chunks(w_a[l], b_a[l], w_i[l], b_i[l])
        lru_w = (conv_w[l], row(conv_b[l]), wab, bab, row(lam[l]), row(g_lru_out[l]))

        mk_p, mv_p = _mem_kv(mem_prompt.reshape(nbp * N_MEM, D_MODEL), row(g_mem[l]), wck, wcv)
        mk_p = mk_p.reshape(nbp, N_MEM, D_MODEL)
        mv_p = mv_p.reshape(nbp, N_MEM, D_MODEL)
        xp, q, k, v, lru_out, h_last, u_tail = _ffn_proj_lru(xp, seq, row(g_ffn1[l]), w1g, w1u, w1d, row(g_mix[l]), win,
                                                             cos_p, sin_p, *lru_w)
        k3 = k.reshape(nbp, seq, KV_WIDTH)
        v3 = v.reshape(nbp, seq, KV_WIDTH)
        xp = _swa_mix_ffn(q, k, v, sink[l], row(g_attn_out[l]), xp, lru_out, seq, wout, row(g_xattn[l]), wcq,
                          mk_p, mv_p, wco, row(g_ffn2[l]), w2g, w2u, w2d, g_fin, last)
        outs[0].append(mk_p.reshape(nbp, N_MEM, X_HEADS, X_HEAD_DIM))
        outs[1].append(mv_p.reshape(nbp, N_MEM, X_HEADS, X_HEAD_DIM))
        outs[2].append(k3[:, -WINDOW:].reshape(nbp, WINDOW, KV_HEADS, HEAD_DIM))
        outs[3].append(v3[:, -WINDOW:].reshape(nbp, WINDOW, KV_HEADS, HEAD_DIM))
        outs[4].append(u_tail[:, -(CONV_WIDTH - 1):])
        outs[5].append(h_last.reshape(nbp, LRU_WIDTH))

        xs = _ffn(xs, row(g_ffn1[l]), w1g, w1u, w1d, g_fin, False)
        u, gate, q, k, v = _proj(xs, row(g_mix[l]), win, cos_s, sin_s)
        conv_pad = jnp.pad(state_conv[l], ((0, 0), (dec_seq - (CONV_WIDTH - 1), 0), (0, 0)))
        h0_rep = jnp.repeat(state_lru_h[l], dec_seq, axis=0)
        lru_out, hs = _lru_sample(u, gate, conv_pad.reshape(nbs * dec_seq, LRU_WIDTH), h0_rep, dec_seq, *lru_w)
        per_seq = lambda a: a.reshape(nbs, dec_seq, a.shape[-1])
        attn_out, new_k, new_v = _swa_sample(per_seq(q), per_seq(k), per_seq(v),
                                             cache_swa_k[l].reshape(nbs, WINDOW, KV_WIDTH),
                                             cache_swa_v[l].reshape(nbs, WINDOW, KV_WIDTH), sink[l],
                                             row(g_attn_out[l]), past)
        xs, xq = _mix_query(xs, lru_out, attn_out.reshape(nbs * dec_seq, Q_WIDTH), wout, row(g_xattn[l]), wcq)
        xo = _xattn_cache(per_seq(xq), _interleave_chunks(cache_mem_k[l]), _interleave_chunks(cache_mem_v[l]))
        xs = _ffn(xs, row(g_ffn2[l]), w2g, w2u, w2d, g_fin, last, xo.reshape(nbs * dec_seq, D_MODEL), wco)
        outs[6].append(new_k.reshape(nbs, WINDOW, KV_HEADS, HEAD_DIM))
        outs[7].append(new_v.reshape(nbs, WINDOW, KV_HEADS, HEAD_DIM))
        outs[8].append(u.reshape(nbs, dec_seq, LRU_WIDTH)[:, -(CONV_WIDTH - 1):])
        outs[9].append(hs.reshape(nbs, dec_seq, LRU_WIDTH)[:, -1])

    return (xp.reshape(nbp, seq, D_MODEL), xs.reshape(nbs, dec_seq, D_MODEL)) + tuple(jnp.stack(o) for o in outs)
```

```python
import functools

import jax
import jax.numpy as jnp
import numpy as np
from jax import lax
from jax.experimental import pallas as pl
from jax.experimental.pallas import tpu as pltpu

F32 = jnp.float32
BF16 = jnp.bfloat16

D_MODEL = 1024
LRU_WIDTH = 512
LRU_BLOCKS = 8
CONV_WIDTH = 4
LRU_C = 8.0
ATTN_HEADS = 8
HEAD_DIM = 64
KV_HEADS = 2
WINDOW = 128
PAST_LEN = 8192
ROPE_THETA = 10000.0
N_MEM = 256
X_HEADS = 4
X_HEAD_DIM = 256
D_FF = 2816
EPS = 1e-6
Q_WIDTH = ATTN_HEADS * HEAD_DIM
KV_WIDTH = KV_HEADS * HEAD_DIM
IN_COLS = 2 * LRU_WIDTH + Q_WIDTH + 2 * KV_WIDTH

LANES = 128
SUBLANES = 8
VMEM_LIMIT = 56 * 1024 * 1024

ROW_TILE = 512
LRU_TILE = 512
FFN_CHUNK = 256
SAMPLE_SEQ_TILE = 8
XATTN_SEQ_TILE = 4


def _params(n_axes):
    return pltpu.CompilerParams(dimension_semantics=("arbitrary",) * n_axes, vmem_limit_bytes=VMEM_LIMIT)


def _const_spec(shape):
    return pl.BlockSpec(shape, lambda *_: (0,) * len(shape), pipeline_mode=pl.Buffered(1))


def _rms(x, g):
    return x * lax.rsqrt(jnp.mean(x * x, axis=-1, keepdims=True) + EPS) * g


def _dot(a, b):
    return jnp.dot(a, b, preferred_element_type=F32)


def _dot_nt(a, b):
    return lax.dot_general(a, b, (((1,), (1,)), ((), ())), preferred_element_type=F32)


def _ffn_step(x, g_ref, wg_ref, wu_ref, wd_ref):
    xn = _rms(x, g_ref[...]).astype(BF16)
    gate = _dot(xn, wg_ref[...])
    up = _dot(xn, wu_ref[...])
    h = (gate * jax.nn.sigmoid(gate) * up).astype(BF16)
    return x + 0.5 * _dot(h, wd_ref[...])


def _ffn_tail(x, g_ref, wg_ref, wu_ref, wd_ref, gf_ref, o_ref, final_norm):
    y = _ffn_step(x, g_ref, wg_ref, wu_ref, wd_ref)
    if final_norm:
        y = _rms(y, gf_ref[...])
    o_ref[...] = y


def _ffn_body(x_ref, g_ref, wg_ref, wu_ref, wd_ref, gf_ref, o_ref, *, final_norm):
    _ffn_tail(x_ref[...], g_ref, wg_ref, wu_ref, wd_ref, gf_ref, o_ref, final_norm)


def _proj_ffn_body(x_ref, a_ref, wa_ref, g_ref, wg_ref, wu_ref, wd_ref, gf_ref, o_ref, *, final_norm):
    x = x_ref[...] + _dot(a_ref[...].astype(BF16), wa_ref[...])
    _ffn_tail(x, g_ref, wg_ref, wu_ref, wd_ref, gf_ref, o_ref, final_norm)


def _ffn(x, g, wg, wu, wd, g_final, final_norm, attn=None, w_attn=None):
    rows = x.shape[0]
    tm = min(ROW_TILE, rows)
    row_spec = pl.BlockSpec((tm, D_MODEL), lambda i: (i, 0))
    ffn_specs = [_const_spec((1, D_MODEL)), _const_spec((D_MODEL, D_FF)), _const_spec((D_MODEL, D_FF)),
                 _const_spec((D_FF, D_MODEL)), _const_spec((1, D_MODEL))]
    if attn is None:
        body, lead_specs, lead = _ffn_body, [row_spec], (x,)
    else:
        body, lead_specs, lead = _proj_ffn_body, [row_spec, row_spec, _const_spec((D_MODEL, D_MODEL))], (x, attn, w_attn)
    return pl.pallas_call(
        functools.partial(body, final_norm=final_norm),
        grid=(rows // tm,),
        in_specs=lead_specs + ffn_specs,
        out_specs=row_spec,
        out_shape=jax.ShapeDtypeStruct((rows, D_MODEL), F32),
        compiler_params=_params(1),
    )(*lead, g, wg, wu, wd, g_final)


def _rope(z, cos, sin_signed):
    half = HEAD_DIM // 2
    lane = lax.broadcasted_iota(jnp.int32, z.shape, 1)
    first_half = (lane % HEAD_DIM) < half
    partner = jnp.where(first_half, pltpu.roll(z, LANES - half, axis=1), pltpu.roll(z, half, axis=1))
    return z * cos + partner * sin_signed


def _project_pieces(x, g_ref, w_ref, cos_ref, sin_ref, store_u, store_gate, q_ref, k_ref, v_ref):
    o_gate, o_q, o_k, o_v = LRU_WIDTH, 2 * LRU_WIDTH, 2 * LRU_WIDTH + Q_WIDTH, 2 * LRU_WIDTH + Q_WIDTH + KV_WIDTH
    xn = []

    def normed():
        if not xn:
            xn.append(_rms(x(), g_ref[...]).astype(BF16))
        return xn[0]

    def rope_into(ref, z):
        for j in range(z.shape[1] // LANES):
            cols = slice(j * LANES, (j + 1) * LANES)
            ref[:, cols] = _rope(z[:, cols], cos_ref[...], sin_ref[...])

    def store_v(z):
        v_ref[...] = z

    return [lambda: rope_into(q_ref, _dot(normed(), w_ref[:, o_q:o_k])),
            lambda: rope_into(k_ref, _dot(normed(), w_ref[:, o_k:o_v])),
            lambda: store_v(_dot(normed(), w_ref[:, o_v:])),
            lambda: store_u(_dot(normed(), w_ref[:, :o_gate])),
            lambda: store_gate(_dot(normed(), w_ref[:, o_gate:o_q]))]


def _proj_body(x_ref, g_ref, w_ref, cos_ref, sin_ref, u_ref, gate_ref, q_ref, k_ref, v_ref):
    def store_u(z):
        u_ref[...] = z

    def store_gate(z):
        gate_ref[...] = z

    for piece in _project_pieces(lambda: x_ref[...], g_ref, w_ref, cos_ref, sin_ref, store_u, store_gate,
                                 q_ref, k_ref, v_ref):
        piece()


def _proj(x, g, w_in, cos, sin):
    rows = x.shape[0]
    tm = min(ROW_TILE, rows, cos.shape[0])
    pos_blocks = cos.shape[0] // tm

    def row_spec(width):
        return pl.BlockSpec((tm, width), lambda i: (i, 0))

    pos_spec = pl.BlockSpec((tm, LANES), lambda i: (i % pos_blocks, 0))
    widths = (LRU_WIDTH, LRU_WIDTH, Q_WIDTH, KV_WIDTH, KV_WIDTH)
    return pl.pallas_call(
        _proj_body,
        grid=(rows // tm,),
        in_specs=[row_spec(D_MODEL), _const_spec((1, D_MODEL)), _const_spec((D_MODEL, IN_COLS)), pos_spec, pos_spec],
        out_specs=[row_spec(w) for w in widths],
        out_shape=[jax.ShapeDtypeStruct((rows, w), F32) for w in widths],
        compiler_params=_params(1),
    )(x, g, w_in, cos, sin)


def _rope_tables(pos):
    half = HEAD_DIM // 2
    inv = ROPE_THETA ** (-np.arange(half, dtype=np.float64) / half)
    ang = pos.astype(np.float64)[:, None] * inv[None, :]
    cos = np.cos(ang)
    sin = np.sin(ang)
    reps = LANES // HEAD_DIM
    return (jnp.asarray(np.tile(np.concatenate([cos, cos], axis=-1), (1, reps)), dtype=F32),
            jnp.asarray(np.tile(np.concatenate([-sin, sin], axis=-1), (1, reps)), dtype=F32))


def _softplus(x):
    return jnp.maximum(x, 0.0) + jnp.log1p(jnp.exp(-jnp.abs(x)))


def _lru_coeffs(conv, wab, bab, lam):
    w = conv.shape[1]
    gates = _dot(conv.astype(BF16), wab) + bab
    r = jax.nn.sigmoid(gates[:, :w])
    gi = jax.nn.sigmoid(gates[:, w:])
    log_a = -LRU_C * r * _softplus(-lam)
    a = jnp.exp(log_a)
    b = jnp.sqrt(-jnp.tanh(log_a) * (a * a + 1.0)) * (gi * conv)
    return a, b


def _segment_scan(a, b, seg):
    step = 1
    while step < seg:
        a, b = _scan_step(a, b, seg, step)
        step *= 2
    return a, b


def _scan_step(a, b, seg, step):
    pos = lax.broadcasted_iota(jnp.int32, a.shape, 0) % seg
    live = pos >= step
    a_prev = pltpu.roll(a, step, axis=0)
    b_prev = pltpu.roll(b, step, axis=0)
    return jnp.where(live, a * a_prev, a), jnp.where(live, a * b_prev + b, b)


def _lru_hidden(conv, h_in, seg, wab, bab, lam):
    a, b = _lru_coeffs(conv, wab, bab, lam)
    a_cum, h_local = _segment_scan(a, b, seg)
    return a_cum * h_in + h_local


def _lane_chunks(width):
    return [slice(c * LANES, (c + 1) * LANES) for c in range(width // LANES)]


def _ffn_pieces(normed, wg_ref, wu_ref, wd_ref, state):
    cols = [slice(f, min(f + FFN_CHUNK, D_FF)) for f in range(0, D_FF, FFN_CHUNK)]

    def gate_up(f):
        return _dot(normed(), wg_ref[:, cols[f]]), _dot(normed(), wu_ref[:, cols[f]])

    def piece(f):
        def run():
            gate, up = state.pop("gate_up") if "gate_up" in state else gate_up(f)
            if f + 1 < len(cols):
                state["gate_up"] = gate_up(f + 1)
            part = _dot((gate * jax.nn.sigmoid(gate) * up).astype(BF16), wd_ref[cols[f], :])
            state["acc"] = part if "acc" not in state else state["acc"] + part
        return run

    return [piece(f) for f in range(len(cols))]


def _interleave(primary, secondary):
    due = [((i + 1) * len(primary)) // (len(secondary) + 1) for i in range(len(secondary))]
    pending = list(zip(due, secondary))
    for i, piece in enumerate(primary):
        while pending and pending[0][0] <= i:
            pending.pop(0)[1]()
        piece()
    for _, piece in pending:
        piece()


def _ffn_proj_lru_body(x_ref, g1_ref, wg_ref, wu_ref, wd_ref, gmix_ref, win_ref, cos_ref, sin_ref,
                       cw_ref, cb_ref, wab_ref, bab_ref, lam_ref, gout_ref,
                       x_out_ref, q_ref, k_ref, v_ref, lru_ref, hlast_ref, utail_ref,
                       ug_ref, ext_ref, h_ref, hs_ref, *, tt, tiles_per_seq):
    pad = SUBLANES
    s = pl.program_id(0)
    lead_slot = s % 2
    lag_slot = 1 - lead_slot

    @pl.when(s == 0)
    def _():
        ug_ref[1] = jnp.zeros((2, tt, LRU_WIDTH), F32)

    @pl.when((s == 0) | (lax.rem(s - 1, tiles_per_seq) == 0))
    def _():
        ext_ref[0:pad, :] = jnp.zeros((pad, LRU_WIDTH), F32)
        h_ref[...] = jnp.zeros_like(h_ref)

    state = {"sumsq": jnp.zeros((tt, 1), F32)}
    scan_steps = [1 << i for i in range(tt.bit_length() - 1)]

    def lru_pieces(c, cols):
        def coeffs():
            ext_ref[pad:pad + tt, cols] = ug_ref[lag_slot, 0, :, cols]
            conv = cb_ref[:, cols]
            for j in range(CONV_WIDTH):
                start = pad - (CONV_WIDTH - 1) + j
                conv = conv + ext_ref[start:start + tt, cols] * cw_ref[j:j + 1, cols]
            ext_ref[0:pad, cols] = ext_ref[tt:tt + pad, cols]
            utail_ref[0, :, cols] = ext_ref[0:pad, cols]
            state[c] = _lru_coeffs(conv, wab_ref[c], bab_ref[c], lam_ref[:, cols])

        def scan(steps):
            def run():
                for step in steps:
                    state[c] = _scan_step(*state[c], tt, step)
            return run

        def finish():
            a_cum, h_local = state.pop(c)
            hs_ref[:, cols] = a_cum * h_ref[:, cols] + h_local
            h_ref[:, cols] = hs_ref[tt - 1:tt, cols]
            hlast_ref[0, :, cols] = h_ref[:, cols]
            y = hs_ref[:, cols] * jax.nn.gelu(ug_ref[lag_slot, 1, :, cols])
            hs_ref[:, cols] = y
            state["sumsq"] = state["sumsq"] + jnp.sum(y * y, axis=-1, keepdims=True)

        half = len(scan_steps) // 2
        return [coeffs, scan(scan_steps[:half]), scan(scan_steps[half:]), finish]

    vector_pieces = [p for c, cols in enumerate(_lane_chunks(LRU_WIDTH)) for p in lru_pieces(c, cols)]

    def ffn_in():
        if "xn" not in state:
            state["xn"] = _rms(x_ref[...], g1_ref[...]).astype(BF16)
        return state["xn"]

    def ffn_out():
        if "x" not in state:
            state["x"] = x_ref[...] + 0.5 * state.pop("acc")
            x_out_ref[...] = state["x"]
        return state["x"]

    def store_u(z):
        ug_ref[lead_slot, 0] = z

    def store_gate(z):
        ug_ref[lead_slot, 1] = z

    matmul_pieces = _ffn_pieces(ffn_in, wg_ref, wu_ref, wd_ref, state)
    matmul_pieces += _project_pieces(ffn_out, gmix_ref, win_ref, cos_ref, sin_ref, store_u, store_gate,
                                     q_ref, k_ref, v_ref)

    assert len(vector_pieces) == len(matmul_pieces)
    for vector_piece, matmul_piece in zip(vector_pieces, matmul_pieces):
        vector_piece()
        matmul_piece()
    lru_ref[...] = hs_ref[...] * lax.rsqrt(state["sumsq"] * (1.0 / LRU_WIDTH) + EPS) * gout_ref[...]


def _ffn_proj_lru(x, seq, g1, wg, wu, wd, g_mix, w_in, cos, sin, conv_w, conv_b, wab, bab, lam, g_out):
    rows = x.shape[0]
    n = rows // seq
    tt = min(LRU_TILE, seq)
    tiles_per_seq = seq // tt
    tiles = rows // tt
    pos_blocks = cos.shape[0] // tt
    lead = lambda s: jnp.minimum(s, tiles - 1)
    lag = lambda s: jnp.maximum(s - 1, 0)

    def lead_spec(width):
        return pl.BlockSpec((tt, width), lambda s: (lead(s), 0))

    pos_spec = pl.BlockSpec((tt, LANES), lambda s: (lead(s) % pos_blocks, 0))
    seq_spec = lambda r: pl.BlockSpec((1, r, LRU_WIDTH), lambda s: (lag(s) // tiles_per_seq, 0, 0))
    widths = (D_MODEL, Q_WIDTH, KV_WIDTH, KV_WIDTH)
    return pl.pallas_call(
        functools.partial(_ffn_proj_lru_body, tt=tt, tiles_per_seq=tiles_per_seq),
        grid=(tiles + 1,),
        in_specs=[lead_spec(D_MODEL), _const_spec((1, D_MODEL)), _const_spec((D_MODEL, D_FF)), _const_spec((D_MODEL, D_FF)),
                  _const_spec((D_FF, D_MODEL)), _const_spec((1, D_MODEL)), _const_spec((D_MODEL, IN_COLS)),
                  pos_spec, pos_spec] + _lru_weight_specs(),
        out_specs=[lead_spec(w) for w in widths]
        + [pl.BlockSpec((tt, LRU_WIDTH), lambda s: (lag(s), 0)), seq_spec(1), seq_spec(SUBLANES)],
        out_shape=[jax.ShapeDtypeStruct((rows, w), F32) for w in widths]
        + [jax.ShapeDtypeStruct((rows, LRU_WIDTH), F32), jax.ShapeDtypeStruct((n, 1, LRU_WIDTH), F32),
           jax.ShapeDtypeStruct((n, SUBLANES, LRU_WIDTH), F32)],
        scratch_shapes=[pltpu.VMEM((2, 2, tt, LRU_WIDTH), F32), pltpu.VMEM((tt + SUBLANES, LRU_WIDTH), F32),
                        pltpu.VMEM((1, LRU_WIDTH), F32), pltpu.VMEM((tt, LRU_WIDTH), F32)],
        compiler_params=_params(1),
    )(x, g1, wg, wu, wd, g_mix, w_in, cos, sin, conv_w, conv_b, wab, bab, lam, g_out)


def _lru_sample_body(u_ref, gate_ref, cpad_ref, h0_ref, cw_ref, cb_ref, wab_ref, bab_ref, lam_ref, gout_ref,
                     o_ref, hs_ref, *, seg):
    rows = u_ref.shape[0]
    pos = lax.broadcasted_iota(jnp.int32, (rows, LANES), 0) % seg
    sumsq = jnp.zeros((rows, 1), F32)
    for c, cols in enumerate(_lane_chunks(LRU_WIDTH)):
        u = u_ref[:, cols]
        cpad = cpad_ref[:, cols]
        conv = cb_ref[:, cols] + u * cw_ref[CONV_WIDTH - 1:CONV_WIDTH, cols]
        for back in range(1, CONV_WIDTH):
            shifted = jnp.where(pos >= back, pltpu.roll(u, back, axis=0),
                                pltpu.roll(cpad, (back - seg) % rows, axis=0))
            conv = conv + shifted * cw_ref[CONV_WIDTH - 1 - back:CONV_WIDTH - back, cols]
        h = _lru_hidden(conv, h0_ref[:, cols], seg, wab_ref[c], bab_ref[c], lam_ref[:, cols])
        hs_ref[:, cols] = h
        y = h * jax.nn.gelu(gate_ref[:, cols])
        o_ref[:, cols] = y
        sumsq = sumsq + jnp.sum(y * y, axis=-1, keepdims=True)
    o_ref[...] = o_ref[...] * lax.rsqrt(sumsq * (1.0 / LRU_WIDTH) + EPS) * gout_ref[...]


def _lru_weight_specs():
    chunks = LRU_WIDTH // LANES
    return [_const_spec((CONV_WIDTH, LRU_WIDTH)), _const_spec((1, LRU_WIDTH)),
            _const_spec((chunks, LANES, 2 * LANES)), _const_spec((chunks, 1, 2 * LANES)),
            _const_spec((1, LRU_WIDTH)), _const_spec((1, LRU_WIDTH))]


def _lru_gate_chunks(w_a, b_a, w_i, b_i):
    chunks = LRU_WIDTH // LANES
    per = LRU_BLOCKS // chunks
    wa = w_a.reshape(chunks, per, *w_a.shape[1:])
    wi = w_i.reshape(chunks, per, *w_i.shape[1:])
    wab = jnp.stack([jnp.concatenate([_block_diag(wa[c]), _block_diag(wi[c])], axis=1) for c in range(chunks)])
    bab = jnp.concatenate([b_a.reshape(chunks, 1, LANES), b_i.reshape(chunks, 1, LANES)], axis=2)
    return wab.astype(BF16), bab


def _lru_sample(u, gate, conv_pad, h0_rep, seg, conv_w, conv_b, wab, bab, lam, g_out):
    rows = u.shape[0]
    tm = min(ROW_TILE, rows)
    row_spec = pl.BlockSpec((tm, LRU_WIDTH), lambda i: (i, 0))
    return pl.pallas_call(
        functools.partial(_lru_sample_body, seg=seg),
        grid=(rows // tm,),
        in_specs=[row_spec, row_spec, row_spec, row_spec] + _lru_weight_specs(),
        out_specs=[row_spec, row_spec],
        out_shape=[jax.ShapeDtypeStruct((rows, LRU_WIDTH), F32)] * 2,
        compiler_params=_params(1),
    )(u, gate, conv_pad, h0_rep, conv_w, conv_b, wab, bab, lam, g_out)


def _block_diag(w):
    nb, bi, bj = w.shape
    eye = jnp.eye(nb, dtype=w.dtype)
    return jnp.einsum('gij,gh->gihj', w, eye).reshape(nb * bi, nb * bj)


def _bdot_nt(a, b):
    return lax.dot_general(a, b, (((2,), (2,)), ((0,), (0,))), preferred_element_type=F32)


def _bdot(a, b):
    return lax.dot_general(a, b, (((2,), (1,)), ((0,), (0,))), preferred_element_type=F32)


def _swa_pieces(q, keys, vals, sink_ref, mask, g_out, store):
    group = ATTN_HEADS // KV_HEADS
    straight = [h for h in range(ATTN_HEADS) if (h % 2) == (h // group)]
    swapped = [h for h in range(ATTN_HEADS) if (h % 2) != (h // group)]
    scale = HEAD_DIM ** -0.5
    out_half = {}

    def low_lanes(shape):
        return lax.broadcasted_iota(jnp.int32, shape, 2) < HEAD_DIM

    def head_group(heads, swap):
        held = {}

        def arrange(z):
            return pltpu.roll(z, HEAD_DIM, axis=2) if swap else z

        def scores():
            qv = q()
            n, r, _ = qv.shape
            low = low_lanes((n, r, LANES))
            zero = jnp.zeros((n, r, LANES), F32)
            qs = jnp.concatenate(
                [jnp.where(low if h % 2 == 0 else ~low, qv[:, :, (h // 2) * LANES:(h // 2 + 1) * LANES], zero)
                 for h in heads], axis=1)
            held["s"] = _bdot_nt(qs.astype(BF16), arrange(keys()).astype(BF16)) * scale

        def probabilities():
            s = held.pop("s")
            r = s.shape[1] // len(heads)
            visible = mask()[None]
            probs = []
            for i, h in enumerate(heads):
                sh = jnp.where(visible, s[:, i * r:(i + 1) * r, :], -jnp.inf)
                sink = sink_ref[h]
                m = jnp.maximum(jnp.max(sh, axis=-1, keepdims=True), sink)
                e = jnp.exp(sh - m)
                denom = jnp.sum(e, axis=-1, keepdims=True) + jnp.exp(sink - m)
                probs.append(e / denom)
            held["p"] = jnp.concatenate(probs, axis=1).astype(BF16)

        def weighted_values():
            p = held.pop("p")
            r = p.shape[1] // len(heads)
            o = _bdot(p, arrange(vals()).astype(BF16))
            for i, h in enumerate(heads):
                out_half[h] = o[:, i * r:(i + 1) * r, :]

        return [scores, probabilities, weighted_values]

    def finish():
        low = low_lanes(out_half[0].shape)
        out = jnp.concatenate([jnp.where(low, out_half[2 * j], out_half[2 * j + 1])
                               for j in range(Q_WIDTH // LANES)], axis=2)
        store(_rms(out, g_out))

    return head_group(straight, False) + head_group(swapped, True) + [finish]


def _band_mask(first_block):
    i = lax.broadcasted_iota(jnp.int32, (WINDOW, 2 * WINDOW), 0)
    j = lax.broadcasted_iota(jnp.int32, (WINDOW, 2 * WINDOW), 1)
    dist = i + WINDOW - j
    return (dist >= 0) & (dist < WINDOW) & (jnp.logical_not(first_block) | (j >= WINDOW))


def _swa_sample_body(sink_ref, q_ref, k_ref, v_ref, bk_ref, bv_ref, gout_ref, o_ref, nk_ref, nv_ref, *, s_len, past):
    def mask():
        qp = past + lax.broadcasted_iota(jnp.int32, (s_len, WINDOW + s_len), 0)
        col = lax.broadcasted_iota(jnp.int32, (s_len, WINDOW + s_len), 1)
        kp = jnp.where(col < WINDOW, past - WINDOW + col, past + col - WINDOW)
        dist = qp - kp
        return (dist >= 0) & (dist < WINDOW) & (kp >= 0)

    keys = jnp.concatenate([bk_ref[...], k_ref[...]], axis=1)
    vals = jnp.concatenate([bv_ref[...], v_ref[...]], axis=1)

    def store(o):
        o_ref[...] = o

    for piece in _swa_pieces(lambda: q_ref[...], lambda: keys, lambda: vals, sink_ref, mask, gout_ref[...], store):
        piece()
    nk_ref[...] = keys[:, s_len:, :]
    nv_ref[...] = vals[:, s_len:, :]


def _swa_sample(q, k, v, buf_k, buf_v, sink, g_out, past):
    n, s_len, _ = q.shape
    sb = min(SAMPLE_SEQ_TILE, n)

    def seq_spec(rows, width):
        return pl.BlockSpec((sb, rows, width), lambda i: (i, 0, 0))

    buf_spec = seq_spec(WINDOW, KV_WIDTH)
    return pl.pallas_call(
        functools.partial(_swa_sample_body, s_len=s_len, past=past),
        grid=(n // sb,),
        in_specs=[pl.BlockSpec(memory_space=pltpu.SMEM), seq_spec(s_len, Q_WIDTH), seq_spec(s_len, KV_WIDTH),
                  seq_spec(s_len, KV_WIDTH), buf_spec, buf_spec, _const_spec((1, Q_WIDTH))],
        out_specs=[seq_spec(s_len, Q_WIDTH), buf_spec, buf_spec],
        out_shape=[jax.ShapeDtypeStruct((n, s_len, Q_WIDTH), F32),
                   jax.ShapeDtypeStruct((n, WINDOW, KV_WIDTH), F32), jax.ShapeDtypeStruct((n, WINDOW, KV_WIDTH), F32)],
        compiler_params=_params(1),
    )(sink, q, k, v, buf_k, buf_v, g_out)


def _softmax(s):
    e = jnp.exp(s - jnp.max(s, axis=-1, keepdims=True))
    return e / jnp.sum(e, axis=-1, keepdims=True)


def _mix_and_query(x_ref, lru_ref, attn_ref, wout_ref, gx_ref, wcq_ref):
    x = (x_ref[...] + _dot(lru_ref[...].astype(BF16), wout_ref[:LRU_WIDTH, :])
         + _dot(attn_ref[...].astype(BF16), wout_ref[LRU_WIDTH:, :]))
    return x, _dot(_rms(x, gx_ref[...]).astype(BF16), wcq_ref[...])


def _swa_mix_ffn_body(sink_ref, q_ref, k_ref, v_ref, kp_ref, vp_ref, gattn_ref,
                      x_ref, lru_ref, wout_ref, gx_ref, wcq_ref, mk_ref, mv_ref, wco_ref,
                      g2_ref, wg_ref, wu_ref, wd_ref, gf_ref, o_ref, attn_ref,
                      *, tt, tiles, tiles_per_seq, final_norm):
    s = pl.program_id(0)
    refs = (sink_ref, q_ref, k_ref, v_ref, kp_ref, vp_ref, gattn_ref, x_ref, lru_ref, wout_ref, gx_ref, wcq_ref,
            mk_ref, mv_ref, wco_ref, g2_ref, wg_ref, wu_ref, wd_ref, gf_ref, o_ref, attn_ref)
    step = functools.partial(_swa_mix_ffn_step, *refs, tt=tt, tiles_per_seq=tiles_per_seq, final_norm=final_norm)

    @pl.when(s == 0)
    def _():
        step(attend=True, layer=False)

    @pl.when((s > 0) & (s < tiles))
    def _():
        step(attend=True, layer=True)

    @pl.when(s == tiles)
    def _():
        step(attend=False, layer=True)


def _swa_mix_ffn_step(sink_ref, q_ref, k_ref, v_ref, kp_ref, vp_ref, gattn_ref,
                      x_ref, lru_ref, wout_ref, gx_ref, wcq_ref, mk_ref, mv_ref, wco_ref,
                      g2_ref, wg_ref, wu_ref, wd_ref, gf_ref, o_ref, attn_ref,
                      *, tt, tiles_per_seq, final_norm, attend, layer):
    s = pl.program_id(0)
    lead_slot = s % 2
    lag_slot = 1 - lead_slot

    blocks = tt // WINDOW
    first_pos_block = lax.rem(s, tiles_per_seq) * blocks
    attention_pieces = []
    for j in range(blocks):
        rows = slice(j * WINDOW, (j + 1) * WINDOW)

        def band(ref, prev_ref, j=j, rows=rows):
            prev = prev_ref[...] if j == 0 else ref[(j - 1) * WINDOW:j * WINDOW, :]
            return jnp.concatenate([prev, ref[rows, :]], axis=0)[None]

        def store(o, rows=rows):
            attn_ref[lead_slot, rows, :] = o[0]

        attention_pieces += _swa_pieces(
            lambda rows=rows: q_ref[rows, :][None], functools.partial(band, k_ref, kp_ref),
            functools.partial(band, v_ref, vp_ref), sink_ref,
            lambda j=j: _band_mask(first_pos_block + j == 0), gattn_ref[...], store)

    state = {}
    scale = X_HEAD_DIM ** -0.5

    def mix():
        state["x"] = (x_ref[...] + _dot(lru_ref[...].astype(BF16), wout_ref[:LRU_WIDTH, :])
                      + _dot(attn_ref[lag_slot].astype(BF16), wout_ref[LRU_WIDTH:, :]))

    def query():
        state["q"] = _dot(_rms(state["x"], gx_ref[...]).astype(BF16), wcq_ref[...])

    def memory_head(h):
        def run():
            cols = slice(h * X_HEAD_DIM, (h + 1) * X_HEAD_DIM)
            sc = _dot_nt(state["q"][:, cols].astype(BF16), mk_ref[0, :, cols].astype(BF16)) * scale
            state["o", h] = _dot(_softmax(sc).astype(BF16), mv_ref[0, :, cols].astype(BF16))
        return run

    def memory_out():
        o = jnp.concatenate([state.pop(("o", h)) for h in range(X_HEADS)], axis=1)
        state["x"] = state["x"] + _dot(o.astype(BF16), wco_ref[...])
        state["xn"] = _rms(state["x"], g2_ref[...]).astype(BF16)

    def finish():
        y = state["x"] + 0.5 * state["acc"]
        o_ref[...] = _rms(y, gf_ref[...]) if final_norm else y

    layer_pieces = ([mix, query] + [memory_head(h) for h in range(X_HEADS)] + [memory_out]
                    + _ffn_pieces(lambda: state["xn"], wg_ref, wu_ref, wd_ref, state) + [finish])
    _interleave(layer_pieces if layer else [], attention_pieces if attend else [])


def _swa_mix_ffn(q, k, v, sink, g_attn, x, lru_out, seq, w_out, g_x, w_cq, mk, mv, w_co, g2, wg, wu, wd, g_final,
                 final_norm):
    rows = x.shape[0]
    tt = min(LRU_TILE, seq)
    tiles_per_seq = seq // tt
    tiles = rows // tt
    blocks = tt // WINDOW
    lead = lambda s: jnp.minimum(s, tiles - 1)
    lag = lambda s: jnp.maximum(s - 1, 0)

    def lead_spec(width):
        return pl.BlockSpec((tt, width), lambda s: (lead(s), 0))

    def lag_spec(width):
        return pl.BlockSpec((tt, width), lambda s: (lag(s), 0))

    prev_spec = pl.BlockSpec((WINDOW, KV_WIDTH), lambda s: (jnp.maximum(lead(s) * blocks - 1, 0), 0))
    mem_spec = pl.BlockSpec((1, N_MEM, D_MODEL), lambda s: (lag(s) // tiles_per_seq, 0, 0))
    w_spec = _const_spec((D_MODEL, D_MODEL))
    return pl.pallas_call(
        functools.partial(_swa_mix_ffn_body, tt=tt, tiles=tiles, tiles_per_seq=tiles_per_seq, final_norm=final_norm),
        grid=(tiles + 1,),
        in_specs=[pl.BlockSpec(memory_space=pltpu.SMEM), lead_spec(Q_WIDTH), lead_spec(KV_WIDTH), lead_spec(KV_WIDTH),
                  prev_spec, prev_spec, _const_spec((1, Q_WIDTH)),
                  lag_spec(D_MODEL), lag_spec(LRU_WIDTH), w_spec, _const_spec((1, D_MODEL)), w_spec,
                  mem_spec, mem_spec, w_spec,
                  _const_spec((1, D_MODEL)), _const_spec((D_MODEL, D_FF)), _const_spec((D_MODEL, D_FF)),
                  _const_spec((D_FF, D_MODEL)), _const_spec((1, D_MODEL))],
        out_specs=lag_spec(D_MODEL),
        out_shape=jax.ShapeDtypeStruct((rows, D_MODEL), F32),
        scratch_shapes=[pltpu.VMEM((2, tt, Q_WIDTH), F32)],
        compiler_params=_params(1),
    )(sink, q, k, v, k, v, g_attn, x, lru_out, w_out, g_x, w_cq, mk, mv, w_co, g2, wg, wu, wd, g_final)


def _mix_query_body(x_ref, lru_ref, attn_ref, wout_ref, gx_ref, wcq_ref, x_out_ref, q_ref):
    x_out_ref[...], q_ref[...] = _mix_and_query(x_ref, lru_ref, attn_ref, wout_ref, gx_ref, wcq_ref)


def _mix_query(x, lru_out, attn_out, w_out, g_x, w_cq):
    rows = x.shape[0]
    tm = min(ROW_TILE, rows)

    def row_spec(width):
        return pl.BlockSpec((tm, width), lambda i: (i, 0))

    w_spec = _const_spec((D_MODEL, D_MODEL))
    return pl.pallas_call(
        _mix_query_body,
        grid=(rows // tm,),
        in_specs=[row_spec(D_MODEL), row_spec(LRU_WIDTH), row_spec(Q_WIDTH), w_spec, _const_spec((1, D_MODEL)), w_spec],
        out_specs=[row_spec(D_MODEL), row_spec(D_MODEL)],
        out_shape=[jax.ShapeDtypeStruct((rows, D_MODEL), F32)] * 2,
        compiler_params=_params(1),
    )(x, lru_out, attn_out, w_out, g_x, w_cq)


def _xattn_cache_body(q_ref, mk_ref, mv_ref, o_ref):
    sb, steps, _ = q_ref.shape
    blocks = D_MODEL // LANES
    chunks = X_HEAD_DIM // LANES
    width = mk_ref.shape[1]
    cls = lax.broadcasted_iota(jnp.int32, (sb, steps, width), 2) % blocks
    block_cls = [(j % chunks) * X_HEADS + j // chunks for j in range(blocks)]
    scale = X_HEAD_DIM ** -0.5

    qs = jnp.concatenate([q_ref[:, :, j * LANES:(j + 1) * LANES] for j in range(blocks)], axis=1)
    s = _bdot_nt(qs.astype(BF16), mk_ref[...].astype(BF16)) * scale
    part = jnp.zeros((sb, steps, width), F32)
    for j in range(blocks):
        part = part + jnp.where(cls == block_cls[j], s[:, j * steps:(j + 1) * steps, :], 0.0)
    score = part + pltpu.roll(part, width - X_HEADS, axis=2)
    top = jnp.zeros((sb, steps, width), F32)
    for h in range(X_HEADS):
        mine = cls == h
        top = jnp.where(mine, jnp.max(jnp.where(mine, score, -jnp.inf), axis=-1, keepdims=True), top)
    e = jnp.where(cls < X_HEADS, jnp.exp(score - top), 0.0)
    denom = jnp.ones((sb, steps, width), F32)
    for h in range(X_HEADS):
        mine = cls == h
        denom = jnp.where(mine, jnp.sum(jnp.where(mine, e, 0.0), axis=-1, keepdims=True), denom)
    p = e / denom
    p = p + pltpu.roll(p, X_HEADS, axis=2)
    ps = jnp.concatenate([jnp.where(cls == block_cls[j], p, 0.0) for j in range(blocks)], axis=1)
    o = _bdot(ps.astype(BF16), mv_ref[...].astype(BF16))
    for j in range(blocks):
        o_ref[:, :, j * LANES:(j + 1) * LANES] = o[:, j * steps:(j + 1) * steps, :]


def _interleave_chunks(cache):
    n = cache.shape[0]
    chunks = X_HEAD_DIM // LANES
    c = cache.reshape(n, N_MEM, X_HEADS, chunks, LANES)
    return jnp.transpose(c, (0, 1, 3, 2, 4)).reshape(n, N_MEM * chunks * X_HEADS, LANES)


def _xattn_cache(q, cache_k, cache_v):
    n, s_len, _ = q.shape
    sb = min(XATTN_SEQ_TILE, n)
    q_spec = pl.BlockSpec((sb, s_len, D_MODEL), lambda i: (i, 0, 0))
    rows = cache_k.shape[1]
    mem_spec = pl.BlockSpec((sb, rows, LANES), lambda i: (i, 0, 0))
    return pl.pallas_call(
        _xattn_cache_body,
        grid=(n // sb,),
        in_specs=[q_spec, mem_spec, mem_spec],
        out_specs=q_spec,
        out_shape=jax.ShapeDtypeStruct((n, s_len, D_MODEL), F32),
        compiler_params=_params(1),
    )(q, cache_k, cache_v)


def _mem_kv_body(mem_ref, g_ref, wk_ref, wv_ref, k_ref, v_ref):
    mm = _rms(mem_ref[...], g_ref[...]).astype(BF16)
    k_ref[...] = _dot(mm, wk_ref[...])
    v_ref[...] = _dot(mm, wv_ref[...])


def _mem_kv(mem, g, w_ck, w_cv):
    rows = mem.shape[0]
    tm = min(ROW_TILE, rows)
    row_spec = pl.BlockSpec((tm, D_MODEL), lambda i: (i, 0))
    w_spec = _const_spec((D_MODEL, D_MODEL))
    return pl.pallas_call(
        _mem_kv_body,
        grid=(rows // tm,),
        in_specs=[row_spec, _const_spec((1, D_MODEL)), w_spec, w_spec],
        out_specs=[row_spec, row_spec],
        out_shape=[jax.ShapeDtypeStruct((rows, D_MODEL), F32)] * 2,
        compiler_params=_params(1),
    )(mem, g, w_ck, w_cv)


def kernel(x_prompt, x_sample, mem_prompt, cache_mem_k, cache_mem_v, cache_swa_k, cache_swa_v, state_conv, state_lru_h,
           g_ffn1, w1_gate, w1_up, w1_down, g_mix, w_in, conv_w, conv_b, w_a, b_a, w_i, b_i, lam, sink,
           g_lru_out, g_attn_out, w_out, g_xattn, g_mem, w_cq, w_ck, w_cv, w_co, g_ffn2, w2_gate, w2_up, w2_down,
           g_final):
    nbp, seq, _ = x_prompt.shape
    nbs, dec_seq, _ = x_sample.shape
    depth = g_ffn1.shape[0]
    past = PAST_LEN
    cos_p, sin_p = _rope_tables(np.arange(seq, dtype=np.int32))
    sample_pos_rows = min(ROW_TILE, nbs * dec_seq)
    cos_s, sin_s = _rope_tables(np.tile(past + np.arange(dec_seq, dtype=np.int32), sample_pos_rows // dec_seq))

    xp = x_prompt.reshape(nbp * seq, D_MODEL)
    xs = x_sample.reshape(nbs * dec_seq, D_MODEL)
    g_fin = g_final.reshape(1, D_MODEL)
    row = lambda a: a.reshape(1, -1)
    outs = [[] for _ in range(10)]
    for l in range(depth):
        last = l == depth - 1
        bf = lambda a: a[l].astype(BF16)
        w1g, w1u, w1d, w2g, w2u, w2d = bf(w1_gate), bf(w1_up), bf(w1_down), bf(w2_gate), bf(w2_up), bf(w2_down)
        win, wout, wcq, wck, wcv, wco = bf(w_in), bf(w_out), bf(w_cq), bf(w_ck), bf(w_cv), bf(w_co)
        wab, bab = _lru_gate_chunks(w_a[l], b_a[l], w_i[l], b_i[l])
        lru_w = (conv_w[l], row(conv_b[l]), wab, bab, row(lam[l]), row(g_lru_out[l]))

        mk_p, mv_p = _mem_kv(mem_prompt.reshape(nbp * N_MEM, D_MODEL), row(g_mem[l]), wck, wcv)
        mk_p = mk_p.reshape(nbp, N_MEM, D_MODEL)
        mv_p = mv_p.reshape(nbp, N_MEM, D_MODEL)
        xp, q, k, v, lru_out, h_last, u_tail = _ffn_proj_lru(xp, seq, row(g_ffn1[l]), w1g, w1u, w1d, row(g_mix[l]), win,
                                                             cos_p, sin_p, *lru_w)
        k3 = k.reshape(nbp, seq, KV_WIDTH)
        v3 = v.reshape(nbp, seq, KV_WIDTH)
        xp = _swa_mix_ffn(q, k, v, sink[l], row(g_attn_out[l]), xp, lru_out, seq, wout, row(g_xattn[l]), wcq,
                          mk_p, mv_p, wco, row(g_ffn2[l]), w2g, w2u, w2d, g_fin, last)
        outs[0].append(mk_p.reshape(nbp, N_MEM, X_HEADS, X_HEAD_DIM))
        outs[1].append(mv_p.reshape(nbp, N_MEM, X_HEADS, X_HEAD_DIM))
        outs[2].append(k3[:, -WINDOW:].reshape(nbp, WINDOW, KV_HEADS, HEAD_DIM))
        outs[3].append(v3[:, -WINDOW:].reshape(nbp, WINDOW, KV_HEADS, HEAD_DIM))
        outs[4].append(u_tail[:, -(CONV_WIDTH - 1):])
        outs[5].append(h_last.reshape(nbp, LRU_WIDTH))

        xs = _ffn(xs, row(g_ffn1[l]), w1g, w1u, w1d, g_fin, False)
        u, gate, q, k, v = _proj(xs, row(g_mix[l]), win, cos_s, sin_s)
        conv_pad = jnp.pad(state_conv[l], ((0, 0), (dec_seq - (CONV_WIDTH - 1), 0), (0, 0)))
        h0_rep = jnp.repeat(state_lru_h[l], dec_seq, axis=0)
        lru_out, hs = _lru_sample(u, gate, conv_pad.reshape(nbs * dec_seq, LRU_WIDTH), h0_rep, dec_seq, *lru_w)
        per_seq = lambda a: a.reshape(nbs, dec_seq, a.shape[-1])
        attn_out, new_k, new_v = _swa_sample(per_seq(q), per_seq(k), per_seq(v),
                                             cache_swa_k[l].reshape(nbs, WINDOW, KV_WIDTH),
                                             cache_swa_v[l].reshape(nbs, WINDOW, KV_WIDTH), sink[l],
                                             row(g_attn_out[l]), past)
        xs, xq = _mix_query(xs, lru_out, attn_out.reshape(nbs * dec_seq, Q_WIDTH), wout, row(g_xattn[l]), wcq)
        xo = _xattn_cache(per_seq(xq), _interleave_chunks(cache_mem_k[l]), _interleave_chunks(cache_mem_v[l]))
        xs = _ffn(xs, row(g_ffn2[l]), w2g, w2u, w2d, g_fin, last, xo.reshape(nbs * dec_seq, D_MODEL), wco)
        outs[6].append(new_k.reshape(nbs, WINDOW, KV_HEADS, HEAD_DIM))
        outs[7].append(new_v.reshape(nbs, WINDOW, KV_HEADS, HEAD_DIM))
        outs[8].append(u.reshape(nbs, dec_seq, LRU_WIDTH)[:, -(CONV_WIDTH - 1):])
        outs[9].append(hs.reshape(nbs, dec_seq, LRU_WIDTH)[:, -1])

    return (xp.reshape(nbp, seq, D_MODEL), xs.reshape(nbs, dec_seq, D_MODEL)) + tuple(jnp.stack(o) for o in outs)
```

```python
import functools

import jax
import jax.numpy as jnp
import numpy as np
from jax import lax
from jax.experimental import pallas as pl
from jax.experimental.pallas import tpu as pltpu

F32 = jnp.float32
BF16 = jnp.bfloat16

D_MODEL = 1024
LRU_WIDTH = 512
LRU_BLOCKS = 8
CONV_WIDTH = 4
LRU_C = 8.0
ATTN_HEADS = 8
HEAD_DIM = 64
KV_HEADS = 2
WINDOW = 128
PAST_LEN = 8192
ROPE_THETA = 10000.0
N_MEM = 256
X_HEADS = 4
X_HEAD_DIM = 256
D_FF = 2816
EPS = 1e-6
Q_WIDTH = ATTN_HEADS * HEAD_DIM
KV_WIDTH = KV_HEADS * HEAD_DIM
IN_COLS = 2 * LRU_WIDTH + Q_WIDTH + 2 * KV_WIDTH

LANES = 128
SUBLANES = 8
VMEM_LIMIT = 56 * 1024 * 1024

ROW_TILE = 512
LRU_TILE = 512
FFN_CHUNK = 256
SAMPLE_SEQ_TILE = 32
XATTN_SEQ_TILE = 8


def _params(n_axes):
    return pltpu.CompilerParams(dimension_semantics=("arbitrary",) * n_axes, vmem_limit_bytes=VMEM_LIMIT)


def _const_spec(shape):
    return pl.BlockSpec(shape, lambda *_: (0,) * len(shape), pipeline_mode=pl.Buffered(1))


def _rms(x, g):
    return x * lax.rsqrt(jnp.mean(x * x, axis=-1, keepdims=True) + EPS) * g


def _dot(a, b):
    return jnp.dot(a, b, preferred_element_type=F32)


def _dot_nt(a, b):
    return lax.dot_general(a, b, (((1,), (1,)), ((), ())), preferred_element_type=F32)


def _ffn_step(x, g_ref, wg_ref, wu_ref, wd_ref):
    xn = _rms(x, g_ref[...]).astype(BF16)
    gate = _dot(xn, wg_ref[...])
    up = _dot(xn, wu_ref[...])
    h = (gate * jax.nn.sigmoid(gate) * up).astype(BF16)
    return x + 0.5 * _dot(h, wd_ref[...])


def _ffn_tail(x, g_ref, wg_ref, wu_ref, wd_ref, gf_ref, o_ref, final_norm):
    y = _ffn_step(x, g_ref, wg_ref, wu_ref, wd_ref)
    if final_norm:
        y = _rms(y, gf_ref[...])
    o_ref[...] = y


def _ffn_body(x_ref, g_ref, wg_ref, wu_ref, wd_ref, gf_ref, o_ref, *, final_norm):
    _ffn_tail(x_ref[...], g_ref, wg_ref, wu_ref, wd_ref, gf_ref, o_ref, final_norm)


def _proj_ffn_body(x_ref, a_ref, wa_ref, g_ref, wg_ref, wu_ref, wd_ref, gf_ref, o_ref, *, final_norm):
    x = x_ref[...] + _dot(a_ref[...].astype(BF16), wa_ref[...])
    _ffn_tail(x, g_ref, wg_ref, wu_ref, wd_ref, gf_ref, o_ref, final_norm)


def _ffn(x, g, wg, wu, wd, g_final, final_norm, attn=None, w_attn=None):
    rows = x.shape[0]
    tm = min(ROW_TILE, rows)
    row_spec = pl.BlockSpec((tm, D_MODEL), lambda i: (i, 0))
    ffn_specs = [_const_spec((1, D_MODEL)), _const_spec((D_MODEL, D_FF)), _const_spec((D_MODEL, D_FF)),
                 _const_spec((D_FF, D_MODEL)), _const_spec((1, D_MODEL))]
    if attn is None:
        body, lead_specs, lead = _ffn_body, [row_spec], (x,)
    else:
        body, lead_specs, lead = _proj_ffn_body, [row_spec, row_spec, _const_spec((D_MODEL, D_MODEL))], (x, attn, w_attn)
    return pl.pallas_call(
        functools.partial(body, final_norm=final_norm),
        grid=(rows // tm,),
        in_specs=lead_specs + ffn_specs,
        out_specs=row_spec,
        out_shape=jax.ShapeDtypeStruct((rows, D_MODEL), F32),
        compiler_params=_params(1),
    )(*lead, g, wg, wu, wd, g_final)


def _rope(z, cos, sin_signed):
    half = HEAD_DIM // 2
    lane = lax.broadcasted_iota(jnp.int32, z.shape, 1)
    first_half = (lane % HEAD_DIM) < half
    partner = jnp.where(first_half, pltpu.roll(z, LANES - half, axis=1), pltpu.roll(z, half, axis=1))
    return z * cos + partner * sin_signed


def _project_pieces(x, g_ref, w_ref, cos_ref, sin_ref, store_u, store_gate, q_ref, k_ref, v_ref):
    o_gate, o_q, o_k, o_v = LRU_WIDTH, 2 * LRU_WIDTH, 2 * LRU_WIDTH + Q_WIDTH, 2 * LRU_WIDTH + Q_WIDTH + KV_WIDTH
    xn = []

    def normed():
        if not xn:
            xn.append(_rms(x(), g_ref[...]).astype(BF16))
        return xn[0]

    def rope_into(ref, z):
        for j in range(z.shape[1] // LANES):
            cols = slice(j * LANES, (j + 1) * LANES)
            ref[:, cols] = _rope(z[:, cols], cos_ref[...], sin_ref[...])

    def store_v(z):
        v_ref[...] = z

    return [lambda: rope_into(q_ref, _dot(normed(), w_ref[:, o_q:o_k])),
            lambda: rope_into(k_ref, _dot(normed(), w_ref[:, o_k:o_v])),
            lambda: store_v(_dot(normed(), w_ref[:, o_v:])),
            lambda: store_u(_dot(normed(), w_ref[:, :o_gate])),
            lambda: store_gate(_dot(normed(), w_ref[:, o_gate:o_q]))]


def _proj_body(x_ref, g_ref, w_ref, cos_ref, sin_ref, u_ref, gate_ref, q_ref, k_ref, v_ref):
    def store_u(z):
        u_ref[...] = z

    def store_gate(z):
        gate_ref[...] = z

    for piece in _project_pieces(lambda: x_ref[...], g_ref, w_ref, cos_ref, sin_ref, store_u, store_gate,
                                 q_ref, k_ref, v_ref):
        piece()


def _proj(x, g, w_in, cos, sin):
    rows = x.shape[0]
    tm = min(ROW_TILE, rows, cos.shape[0])
    pos_blocks = cos.shape[0] // tm

    def row_spec(width):
        return pl.BlockSpec((tm, width), lambda i: (i, 0))

    pos_spec = pl.BlockSpec((tm, LANES), lambda i: (i % pos_blocks, 0))
    widths = (LRU_WIDTH, LRU_WIDTH, Q_WIDTH, KV_WIDTH, KV_WIDTH)
    return pl.pallas_call(
        _proj_body,
        grid=(rows // tm,),
        in_specs=[row_spec(D_MODEL), _const_spec((1, D_MODEL)), _const_spec((D_MODEL, IN_COLS)), pos_spec, pos_spec],
        out_specs=[row_spec(w) for w in widths],
        out_shape=[jax.ShapeDtypeStruct((rows, w), F32) for w in widths],
        compiler_params=_params(1),
    )(x, g, w_in, cos, sin)


def _rope_tables(pos):
    half = HEAD_DIM // 2
    inv = ROPE_THETA ** (-np.arange(half, dtype=np.float64) / half)
    ang = pos.astype(np.float64)[:, None] * inv[None, :]
    cos = np.cos(ang)
    sin = np.sin(ang)
    reps = LANES // HEAD_DIM
    return (jnp.asarray(np.tile(np.concatenate([cos, cos], axis=-1), (1, reps)), dtype=F32),
            jnp.asarray(np.tile(np.concatenate([-sin, sin], axis=-1), (1, reps)), dtype=F32))


def _softplus(x):
    return jnp.maximum(x, 0.0) + jnp.log1p(jnp.exp(-jnp.abs(x)))


def _lru_coeffs(conv, wab, bab, lam):
    w = conv.shape[1]
    gates = _dot(conv.astype(BF16), wab) + bab
    r = jax.nn.sigmoid(gates[:, :w])
    gi = jax.nn.sigmoid(gates[:, w:])
    log_a = -LRU_C * r * _softplus(-lam)
    a = jnp.exp(log_a)
    b = jnp.sqrt(-jnp.tanh(log_a) * (a * a + 1.0)) * (gi * conv)
    return a, b


def _segment_scan(a, b, seg):
    step = 1
    while step < seg:
        a, b = _scan_step(a, b, seg, step)
        step *= 2
    return a, b


def _scan_step(a, b, seg, step):
    pos = lax.broadcasted_iota(jnp.int32, a.shape, 0) % seg
    live = pos >= step
    a_prev = pltpu.roll(a, step, axis=0)
    b_prev = pltpu.roll(b, step, axis=0)
    return jnp.where(live, a * a_prev, a), jnp.where(live, a * b_prev + b, b)


def _lru_hidden(conv, h_in, seg, wab, bab, lam):
    a, b = _lru_coeffs(conv, wab, bab, lam)
    a_cum, h_local = _segment_scan(a, b, seg)
    return a_cum * h_in + h_local


def _lane_chunks(width):
    return [slice(c * LANES, (c + 1) * LANES) for c in range(width // LANES)]


def _ffn_pieces(normed, wg_ref, wu_ref, wd_ref, state):
    cols = [slice(f, min(f + FFN_CHUNK, D_FF)) for f in range(0, D_FF, FFN_CHUNK)]

    def gate_up(f):
        return _dot(normed(), wg_ref[:, cols[f]]), _dot(normed(), wu_ref[:, cols[f]])

    def piece(f):
        def run():
            gate, up = state.pop("gate_up") if "gate_up" in state else gate_up(f)
            if f + 1 < len(cols):
                state["gate_up"] = gate_up(f + 1)
            part = _dot((gate * jax.nn.sigmoid(gate) * up).astype(BF16), wd_ref[cols[f], :])
            state["acc"] = part if "acc" not in state else state["acc"] + part
        return run

    return [piece(f) for f in range(len(cols))]


def _interleave(primary, secondary):
    due = [((i + 1) * len(primary)) // (len(secondary) + 1) for i in range(len(secondary))]
    pending = list(zip(due, secondary))
    for i, piece in enumerate(primary):
        while pending and pending[0][0] <= i:
            pending.pop(0)[1]()
        piece()
    for _, piece in pending:
        piece()


def _ffn_proj_lru_body(x_ref, g1_ref, wg_ref, wu_ref, wd_ref, gmix_ref, win_ref, cos_ref, sin_ref,
                       cw_ref, cb_ref, wab_ref, bab_ref, lam_ref, gout_ref,
                       x_out_ref, q_ref, k_ref, v_ref, lru_ref, hlast_ref, utail_ref,
                       ug_ref, ext_ref, h_ref, hs_ref, *, tt, tiles_per_seq):
    pad = SUBLANES
    s = pl.program_id(0)
    lead_slot = s % 2
    lag_slot = 1 - lead_slot

    @pl.when(s == 0)
    def _():
        ug_ref[1] = jnp.zeros((2, tt, LRU_WIDTH), F32)

    @pl.when((s == 0) | (lax.rem(s - 1, tiles_per_seq) == 0))
    def _():
        ext_ref[0:pad, :] = jnp.zeros((pad, LRU_WIDTH), F32)
        h_ref[...] = jnp.zeros_like(h_ref)

    state = {"sumsq": jnp.zeros((tt, 1), F32)}
    scan_steps = [1 << i for i in range(tt.bit_length() - 1)]

    def lru_pieces(c, cols):
        def coeffs():
            ext_ref[pad:pad + tt, cols] = ug_ref[lag_slot, 0, :, cols]
            conv = cb_ref[:, cols]
            for j in range(CONV_WIDTH):
                start = pad - (CONV_WIDTH - 1) + j
                conv = conv + ext_ref[start:start + tt, cols] * cw_ref[j:j + 1, cols]
            ext_ref[0:pad, cols] = ext_ref[tt:tt + pad, cols]
            utail_ref[0, :, cols] = ext_ref[0:pad, cols]
            state[c] = _lru_coeffs(conv, wab_ref[c], bab_ref[c], lam_ref[:, cols])

        def scan(steps):
            def run():
                for step in steps:
                    state[c] = _scan_step(*state[c], tt, step)
            return run

        def finish():
            a_cum, h_local = state.pop(c)
            hs_ref[:, cols] = a_cum * h_ref[:, cols] + h_local
            h_ref[:, cols] = hs_ref[tt - 1:tt, cols]
            hlast_ref[0, :, cols] = h_ref[:, cols]
            y = hs_ref[:, cols] * jax.nn.gelu(ug_ref[lag_slot, 1, :, cols])
            hs_ref[:, cols] = y
            state["sumsq"] = state["sumsq"] + jnp.sum(y * y, axis=-1, keepdims=True)

        half = len(scan_steps) // 2
        return [coeffs, scan(scan_steps[:half]), scan(scan_steps[half:]), finish]

    vector_pieces = [p for c, cols in enumerate(_lane_chunks(LRU_WIDTH)) for p in lru_pieces(c, cols)]

    def ffn_in():
        if "xn" not in state:
            state["xn"] = _rms(x_ref[...], g1_ref[...]).astype(BF16)
        return state["xn"]

    def ffn_out():
        if "x" not in state:
            state["x"] = x_ref[...] + 0.5 * state.pop("acc")
            x_out_ref[...] = state["x"]
        return state["x"]

    def store_u(z):
        ug_ref[lead_slot, 0] = z

    def store_gate(z):
        ug_ref[lead_slot, 1] = z

    matmul_pieces = _ffn_pieces(ffn_in, wg_ref, wu_ref, wd_ref, state)
    matmul_pieces += _project_pieces(ffn_out, gmix_ref, win_ref, cos_ref, sin_ref, store_u, store_gate,
                                     q_ref, k_ref, v_ref)

    assert len(vector_pieces) == len(matmul_pieces)
    for vector_piece, matmul_piece in zip(vector_pieces, matmul_pieces):
        vector_piece()
        matmul_piece()
    lru_ref[...] = hs_ref[...] * lax.rsqrt(state["sumsq"] * (1.0 / LRU_WIDTH) + EPS) * gout_ref[...]


def _ffn_proj_lru(x, seq, g1, wg, wu, wd, g_mix, w_in, cos, sin, conv_w, conv_b, wab, bab, lam, g_out):
    rows = x.shape[0]
    n = rows // seq
    tt = min(LRU_TILE, seq)
    tiles_per_seq = seq // tt
    tiles = rows // tt
    pos_blocks = cos.shape[0] // tt
    lead = lambda s: jnp.minimum(s, tiles - 1)
    lag = lambda s: jnp.maximum(s - 1, 0)

    def lead_spec(width):
        return pl.BlockSpec((tt, width), lambda s: (lead(s), 0))

    pos_spec = pl.BlockSpec((tt, LANES), lambda s: (lead(s) % pos_blocks, 0))
    seq_spec = lambda r: pl.BlockSpec((1, r, LRU_WIDTH), lambda s: (lag(s) // tiles_per_seq, 0, 0))
    widths = (D_MODEL, Q_WIDTH, KV_WIDTH, KV_WIDTH)
    return pl.pallas_call(
        functools.partial(_ffn_proj_lru_body, tt=tt, tiles_per_seq=tiles_per_seq),
        grid=(tiles + 1,),
        in_specs=[lead_spec(D_MODEL), _const_spec((1, D_MODEL)), _const_spec((D_MODEL, D_FF)), _const_spec((D_MODEL, D_FF)),
                  _const_spec((D_FF, D_MODEL)), _const_spec((1, D_MODEL)), _const_spec((D_MODEL, IN_COLS)),
                  pos_spec, pos_spec] + _lru_weight_specs(),
        out_specs=[lead_spec(w) for w in widths]
        + [pl.BlockSpec((tt, LRU_WIDTH), lambda s: (lag(s), 0)), seq_spec(1), seq_spec(SUBLANES)],
        out_shape=[jax.ShapeDtypeStruct((rows, w), F32) for w in widths]
        + [jax.ShapeDtypeStruct((rows, LRU_WIDTH), F32), jax.ShapeDtypeStruct((n, 1, LRU_WIDTH), F32),
           jax.ShapeDtypeStruct((n, SUBLANES, LRU_WIDTH), F32)],
        scratch_shapes=[pltpu.VMEM((2, 2, tt, LRU_WIDTH), F32), pltpu.VMEM((tt + SUBLANES, LRU_WIDTH), F32),
                        pltpu.VMEM((1, LRU_WIDTH), F32), pltpu.VMEM((tt, LRU_WIDTH), F32)],
        compiler_params=_params(1),
    )(x, g1, wg, wu, wd, g_mix, w_in, cos, sin, conv_w, conv_b, wab, bab, lam, g_out)


def _lru_sample_body(u_ref, gate_ref, cpad_ref, h0_ref, cw_ref, cb_ref, wab_ref, bab_ref, lam_ref, gout_ref,
                     o_ref, hs_ref, *, seg):
    rows = u_ref.shape[0]
    pos = lax.broadcasted_iota(jnp.int32, (rows, LANES), 0) % seg
    sumsq = jnp.zeros((rows, 1), F32)
    for c, cols in enumerate(_lane_chunks(LRU_WIDTH)):
        u = u_ref[:, cols]
        cpad = cpad_ref[:, cols]
        conv = cb_ref[:, cols] + u * cw_ref[CONV_WIDTH - 1:CONV_WIDTH, cols]
        for back in range(1, CONV_WIDTH):
            shifted = jnp.where(pos >= back, pltpu.roll(u, back, axis=0),
                                pltpu.roll(cpad, (back - seg) % rows, axis=0))
            conv = conv + shifted * cw_ref[CONV_WIDTH - 1 - back:CONV_WIDTH - back, cols]
        h = _lru_hidden(conv, h0_ref[:, cols], seg, wab_ref[c], bab_ref[c], lam_ref[:, cols])
        hs_ref[:, cols] = h
        y = h * jax.nn.gelu(gate_ref[:, cols])
        o_ref[:, cols] = y
        sumsq = sumsq + jnp.sum(y * y, axis=-1, keepdims=True)
    o_ref[...] = o_ref[...] * lax.rsqrt(sumsq * (1.0 / LRU_WIDTH) + EPS) * gout_ref[...]


def _lru_weight_specs():
    chunks = LRU_WIDTH // LANES
    return [_const_spec((CONV_WIDTH, LRU_WIDTH)), _const_spec((1, LRU_WIDTH)),
            _const_spec((chunks, LANES, 2 * LANES)), _const_spec((chunks, 1, 2 * LANES)),
            _const_spec((1, LRU_WIDTH)), _const_spec((1, LRU_WIDTH))]


def _lru_gate_chunks(w_a, b_a, w_i, b_i):
    chunks = LRU_WIDTH // LANES
    per = LRU_BLOCKS // chunks
    wa = w_a.reshape(chunks, per, *w_a.shape[1:])
    wi = w_i.reshape(chunks, per, *w_i.shape[1:])
    wab = jnp.stack([jnp.concatenate([_block_diag(wa[c]), _block_diag(wi[c])], axis=1) for c in range(chunks)])
    bab = jnp.concatenate([b_a.reshape(chunks, 1, LANES), b_i.reshape(chunks, 1, LANES)], axis=2)
    return wab.astype(BF16), bab


def _lru_sample(u, gate, conv_pad, h0_rep, seg, conv_w, conv_b, wab, bab, lam, g_out):
    rows = u.shape[0]
    tm = min(ROW_TILE, rows)
    row_spec = pl.BlockSpec((tm, LRU_WIDTH), lambda i: (i, 0))
    return pl.pallas_call(
        functools.partial(_lru_sample_body, seg=seg),
        grid=(rows // tm,),
        in_specs=[row_spec, row_spec, row_spec, row_spec] + _lru_weight_specs(),
        out_specs=[row_spec, row_spec],
        out_shape=[jax.ShapeDtypeStruct((rows, LRU_WIDTH), F32)] * 2,
        compiler_params=_params(1),
    )(u, gate, conv_pad, h0_rep, conv_w, conv_b, wab, bab, lam, g_out)


def _block_diag(w):
    nb, bi, bj = w.shape
    eye = jnp.eye(nb, dtype=w.dtype)
    return jnp.einsum('gij,gh->gihj', w, eye).reshape(nb * bi, nb * bj)


def _bdot_nt(a, b):
    return lax.dot_general(a, b, (((2,), (2,)), ((0,), (0,))), preferred_element_type=F32)


def _bdot(a, b):
    return lax.dot_general(a, b, (((2,), (1,)), ((0,), (0,))), preferred_element_type=F32)


def _swa_pieces(q, keys, vals, sink_ref, mask, g_out, store):
    group = ATTN_HEADS // KV_HEADS
    straight = [h for h in range(ATTN_HEADS) if (h % 2) == (h // group)]
    swapped = [h for h in range(ATTN_HEADS) if (h % 2) != (h // group)]
    scale = HEAD_DIM ** -0.5
    out_half = {}

    def low_lanes(shape):
        return lax.broadcasted_iota(jnp.int32, shape, 2) < HEAD_DIM

    def head_group(heads, swap):
        held = {}

        def arrange(z):
            return pltpu.roll(z, HEAD_DIM, axis=2) if swap else z

        def scores():
            qv = q()
            n, r, _ = qv.shape
            low = low_lanes((n, r, LANES))
            zero = jnp.zeros((n, r, LANES), F32)
            qs = jnp.concatenate(
                [jnp.where(low if h % 2 == 0 else ~low, qv[:, :, (h // 2) * LANES:(h // 2 + 1) * LANES], zero)
                 for h in heads], axis=1)
            held["s"] = _bdot_nt(qs.astype(BF16), arrange(keys()).astype(BF16)) * scale

        def probabilities():
            s = held.pop("s")
            r = s.shape[1] // len(heads)
            visible = mask()[None]
            probs = []
            for i, h in enumerate(heads):
                sh = jnp.where(visible, s[:, i * r:(i + 1) * r, :], -jnp.inf)
                sink = sink_ref[h]
                m = jnp.maximum(jnp.max(sh, axis=-1, keepdims=True), sink)
                e = jnp.exp(sh - m)
                denom = jnp.sum(e, axis=-1, keepdims=True) + jnp.exp(sink - m)
                probs.append(e / denom)
            held["p"] = jnp.concatenate(probs, axis=1).astype(BF16)

        def weighted_values():
            p = held.pop("p")
            r = p.shape[1] // len(heads)
            o = _bdot(p, arrange(vals()).astype(BF16))
            for i, h in enumerate(heads):
                out_half[h] = o[:, i * r:(i + 1) * r, :]

        return [scores, probabilities, weighted_values]

    def finish():
        low = low_lanes(out_half[0].shape)
        out = jnp.concatenate([jnp.where(low, out_half[2 * j], out_half[2 * j + 1])
                               for j in range(Q_WIDTH // LANES)], axis=2)
        store(_rms(out, g_out))

    return head_group(straight, False) + head_group(swapped, True) + [finish]


def _band_mask(first_block):
    i = lax.broadcasted_iota(jnp.int32, (WINDOW, 2 * WINDOW), 0)
    j = lax.broadcasted_iota(jnp.int32, (WINDOW, 2 * WINDOW), 1)
    dist = i + WINDOW - j
    return (dist >= 0) & (dist < WINDOW) & (jnp.logical_not(first_block) | (j >= WINDOW))


def _swa_sample_body(sink_ref, q_ref, k_ref, v_ref, bk_ref, bv_ref, gout_ref, o_ref, nk_ref, nv_ref, *, s_len, past):
    def mask():
        qp = past + lax.broadcasted_iota(jnp.int32, (s_len, WINDOW + s_len), 0)
        col = lax.broadcasted_iota(jnp.int32, (s_len, WINDOW + s_len), 1)
        kp = jnp.where(col < WINDOW, past - WINDOW + col, past + col - WINDOW)
        dist = qp - kp
        return (dist >= 0) & (dist < WINDOW) & (kp >= 0)

    keys = jnp.concatenate([bk_ref[...], k_ref[...]], axis=1)
    vals = jnp.concatenate([bv_ref[...], v_ref[...]], axis=1)

    def store(o):
        o_ref[...] = o

    for piece in _swa_pieces(lambda: q_ref[...], lambda: keys, lambda: vals, sink_ref, mask, gout_ref[...], store):
        piece()
    nk_ref[...] = keys[:, s_len:, :]
    nv_ref[...] = vals[:, s_len:, :]


def _swa_sample(q, k, v, buf_k, buf_v, sink, g_out, past):
    n, s_len, _ = q.shape
    sb = min(SAMPLE_SEQ_TILE, n)

    def seq_spec(rows, width):
        return pl.BlockSpec((sb, rows, width), lambda i: (i, 0, 0))

    buf_spec = seq_spec(WINDOW, KV_WIDTH)
    return pl.pallas_call(
        functools.partial(_swa_sample_body, s_len=s_len, past=past),
        grid=(n // sb,),
        in_specs=[pl.BlockSpec(memory_space=pltpu.SMEM), seq_spec(s_len, Q_WIDTH), seq_spec(s_len, KV_WIDTH),
                  seq_spec(s_len, KV_WIDTH), buf_spec, buf_spec, _const_spec((1, Q_WIDTH))],
        out_specs=[seq_spec(s_len, Q_WIDTH), buf_spec, buf_spec],
        out_shape=[jax.ShapeDtypeStruct((n, s_len, Q_WIDTH), F32),
                   jax.ShapeDtypeStruct((n, WINDOW, KV_WIDTH), F32), jax.ShapeDtypeStruct((n, WINDOW, KV_WIDTH), F32)],
        compiler_params=_params(1),
    )(sink, q, k, v, buf_k, buf_v, g_out)


def _softmax(s):
    e = jnp.exp(s - jnp.max(s, axis=-1, keepdims=True))
    return e / jnp.sum(e, axis=-1, keepdims=True)


def _mix_and_query(x_ref, lru_ref, attn_ref, wout_ref, gx_ref, wcq_ref):
    x = (x_ref[...] + _dot(lru_ref[...].astype(BF16), wout_ref[:LRU_WIDTH, :])
         + _dot(attn_ref[...].astype(BF16), wout_ref[LRU_WIDTH:, :]))
    return x, _dot(_rms(x, gx_ref[...]).astype(BF16), wcq_ref[...])


def _swa_mix_ffn_body(sink_ref, q_ref, k_ref, v_ref, kp_ref, vp_ref, gattn_ref,
                      x_ref, lru_ref, wout_ref, gx_ref, wcq_ref, mk_ref, mv_ref, wco_ref,
                      g2_ref, wg_ref, wu_ref, wd_ref, gf_ref, o_ref, attn_ref,
                      *, tt, tiles, tiles_per_seq, final_norm):
    s = pl.program_id(0)
    refs = (sink_ref, q_ref, k_ref, v_ref, kp_ref, vp_ref, gattn_ref, x_ref, lru_ref, wout_ref, gx_ref, wcq_ref,
            mk_ref, mv_ref, wco_ref, g2_ref, wg_ref, wu_ref, wd_ref, gf_ref, o_ref, attn_ref)
    step = functools.partial(_swa_mix_ffn_step, *refs, tt=tt, tiles_per_seq=tiles_per_seq, final_norm=final_norm)

    @pl.when(s == 0)
    def _():
        step(attend=True, layer=False)

    @pl.when((s > 0) & (s < tiles))
    def _():
        step(attend=True, layer=True)

    @pl.when(s == tiles)
    def _():
        step(attend=False, layer=True)


def _swa_mix_ffn_step(sink_ref, q_ref, k_ref, v_ref, kp_ref, vp_ref, gattn_ref,
                      x_ref, lru_ref, wout_ref, gx_ref, wcq_ref, mk_ref, mv_ref, wco_ref,
                      g2_ref, wg_ref, wu_ref, wd_ref, gf_ref, o_ref, attn_ref,
                      *, tt, tiles_per_seq, final_norm, attend, layer):
    s = pl.program_id(0)
    lead_slot = s % 2
    lag_slot = 1 - lead_slot

    blocks = tt // WINDOW
    first_pos_block = lax.rem(s, tiles_per_seq) * blocks
    attention_pieces = []
    for j in range(blocks):
        rows = slice(j * WINDOW, (j + 1) * WINDOW)

        def band(ref, prev_ref, j=j, rows=rows):
            prev = prev_ref[...] if j == 0 else ref[(j - 1) * WINDOW:j * WINDOW, :]
            return jnp.concatenate([prev, ref[rows, :]], axis=0)[None]

        def store(o, rows=rows):
            attn_ref[lead_slot, rows, :] = o[0]

        attention_pieces += _swa_pieces(
            lambda rows=rows: q_ref[rows, :][None], functools.partial(band, k_ref, kp_ref),
            functools.partial(band, v_ref, vp_ref), sink_ref,
            lambda j=j: _band_mask(first_pos_block + j == 0), gattn_ref[...], store)

    state = {}
    scale = X_HEAD_DIM ** -0.5

    def mix():
        state["x"] = (x_ref[...] + _dot(lru_ref[...].astype(BF16), wout_ref[:LRU_WIDTH, :])
                      + _dot(attn_ref[lag_slot].astype(BF16), wout_ref[LRU_WIDTH:, :]))

    def query():
        state["q"] = _dot(_rms(state["x"], gx_ref[...]).astype(BF16), wcq_ref[...])

    def memory_head(h):
        def run():
            cols = slice(h * X_HEAD_DIM, (h + 1) * X_HEAD_DIM)
            sc = _dot_nt(state["q"][:, cols].astype(BF16), mk_ref[0, :, cols].astype(BF16)) * scale
            state["o", h] = _dot(_softmax(sc).astype(BF16), mv_ref[0, :, cols].astype(BF16))
        return run

    def memory_out():
        o = jnp.concatenate([state.pop(("o", h)) for h in range(X_HEADS)], axis=1)
        state["x"] = state["x"] + _dot(o.astype(BF16), wco_ref[...])
        state["xn"] = _rms(state["x"], g2_ref[...]).astype(BF16)

    def finish():
        y = state["x"] + 0.5 * state["acc"]
        o_ref[...] = _rms(y, gf_ref[...]) if final_norm else y

    layer_pieces = ([mix, query] + [memory_head(h) for h in range(X_HEADS)] + [memory_out]
                    + _ffn_pieces(lambda: state["xn"], wg_ref, wu_ref, wd_ref, state) + [finish])
    _interleave(layer_pieces if layer else [], attention_pieces if attend else [])


def _swa_mix_ffn(q, k, v, sink, g_attn, x, lru_out, seq, w_out, g_x, w_cq, mk, mv, w_co, g2, wg, wu, wd, g_final,
                 final_norm):
    rows = x.shape[0]
    tt = min(LRU_TILE, seq)
    tiles_per_seq = seq // tt
    tiles = rows // tt
    blocks = tt // WINDOW
    lead = lambda s: jnp.minimum(s, tiles - 1)
    lag = lambda s: jnp.maximum(s - 1, 0)

    def lead_spec(width):
        return pl.BlockSpec((tt, width), lambda s: (lead(s), 0))

    def lag_spec(width):
        return pl.BlockSpec((tt, width), lambda s: (lag(s), 0))

    prev_spec = pl.BlockSpec((WINDOW, KV_WIDTH), lambda s: (jnp.maximum(lead(s) * blocks - 1, 0), 0))
    mem_spec = pl.BlockSpec((1, N_MEM, D_MODEL), lambda s: (lag(s) // tiles_per_seq, 0, 0))
    w_spec = _const_spec((D_MODEL, D_MODEL))
    return pl.pallas_call(
        functools.partial(_swa_mix_ffn_body, tt=tt, tiles=tiles, tiles_per_seq=tiles_per_seq, final_norm=final_norm),
        grid=(tiles + 1,),
        in_specs=[pl.BlockSpec(memory_space=pltpu.SMEM), lead_spec(Q_WIDTH), lead_spec(KV_WIDTH), lead_spec(KV_WIDTH),
                  prev_spec, prev_spec, _const_spec((1, Q_WIDTH)),
                  lag_spec(D_MODEL), lag_spec(LRU_WIDTH), w_spec, _const_spec((1, D_MODEL)), w_spec,
                  mem_spec, mem_spec, w_spec,
                  _const_spec((1, D_MODEL)), _const_spec((D_MODEL, D_FF)), _const_spec((D_MODEL, D_FF)),
                  _const_spec((D_FF, D_MODEL)), _const_spec((1, D_MODEL))],
        out_specs=lag_spec(D_MODEL),
        out_shape=jax.ShapeDtypeStruct((rows, D_MODEL), F32),
        scratch_shapes=[pltpu.VMEM((2, tt, Q_WIDTH), F32)],
        compiler_params=_params(1),
    )(sink, q, k, v, k, v, g_attn, x, lru_out, w_out, g_x, w_cq, mk, mv, w_co, g2, wg, wu, wd, g_final)


def _mix_query_body(x_ref, lru_ref, attn_ref, wout_ref, gx_ref, wcq_ref, x_out_ref, q_ref):
    x_out_ref[...], q_ref[...] = _mix_and_query(x_ref, lru_ref, attn_ref, wout_ref, gx_ref, wcq_ref)


def _mix_query(x, lru_out, attn_out, w_out, g_x, w_cq):
    rows = x.shape[0]
    tm = min(ROW_TILE, rows)

    def row_spec(width):
        return pl.BlockSpec((tm, width), lambda i: (i, 0))

    w_spec = _const_spec((D_MODEL, D_MODEL))
    return pl.pallas_call(
        _mix_query_body,
        grid=(rows // tm,),
        in_specs=[row_spec(D_MODEL), row_spec(LRU_WIDTH), row_spec(Q_WIDTH), w_spec, _const_spec((1, D_MODEL)), w_spec],
        out_specs=[row_spec(D_MODEL), row_spec(D_MODEL)],
        out_shape=[jax.ShapeDtypeStruct((rows, D_MODEL), F32)] * 2,
        compiler_params=_params(1),
    )(x, lru_out, attn_out, w_out, g_x, w_cq)


def _xattn_cache_body(q_ref, mk_ref, mv_ref, o_ref):
    sb, steps, _ = q_ref.shape
    blocks = D_MODEL // LANES
    chunks = X_HEAD_DIM // LANES
    width = mk_ref.shape[1]
    cls = lax.broadcasted_iota(jnp.int32, (sb, steps, width), 2) % blocks
    block_cls = [(j % chunks) * X_HEADS + j // chunks for j in range(blocks)]
    scale = X_HEAD_DIM ** -0.5

    qs = jnp.concatenate([q_ref[:, :, j * LANES:(j + 1) * LANES] for j in range(blocks)], axis=1)
    s = _bdot_nt(qs.astype(BF16), mk_ref[...].astype(BF16)) * scale
    part = jnp.zeros((sb, steps, width), F32)
    for j in range(blocks):
        part = part + jnp.where(cls == block_cls[j], s[:, j * steps:(j + 1) * steps, :], 0.0)
    score = part + pltpu.roll(part, width - X_HEADS, axis=2)
    top = jnp.zeros((sb, steps, width), F32)
    for h in range(X_HEADS):
        mine = cls == h
        top = jnp.where(mine, jnp.max(jnp.where(mine, score, -jnp.inf), axis=-1, keepdims=True), top)
    e = jnp.where(cls < X_HEADS, jnp.exp(score - top), 0.0)
    denom = jnp.ones((sb, steps, width), F32)
    for h in range(X_HEADS):
        mine = cls == h
        denom = jnp.where(mine, jnp.sum(jnp.where(mine, e, 0.0), axis=-1, keepdims=True), denom)
    p = e / denom
    p = p + pltpu.roll(p, X_HEADS, axis=2)
    ps = jnp.concatenate([jnp.where(cls == block_cls[j], p, 0.0) for j in range(blocks)], axis=1)
    o = _bdot(ps.astype(BF16), mv_ref[...].astype(BF16))
    for j in range(blocks):
        o_ref[:, :, j * LANES:(j + 1) * LANES] = o[:, j * steps:(j + 1) * steps, :]


def _interleave_chunks(cache):
    n = cache.shape[0]
    chunks = X_HEAD_DIM // LANES
    c = cache.reshape(n, N_MEM, X_HEADS, chunks, LANES)
    return jnp.transpose(c, (0, 1, 3, 2, 4)).reshape(n, N_MEM * chunks * X_HEADS, LANES)


def _xattn_cache(q, cache_k, cache_v):
    n, s_len, _ = q.shape
    sb = min(XATTN_SEQ_TILE, n)
    q_spec = pl.BlockSpec((sb, s_len, D_MODEL), lambda i: (i, 0, 0))
    rows = cache_k.shape[1]
    mem_spec = pl.BlockSpec((sb, rows, LANES), lambda i: (i, 0, 0))
    return pl.pallas_call(
        _xattn_cache_body,
        grid=(n // sb,),
        in_specs=[q_spec, mem_spec, mem_spec],
        out_specs=q_spec,
        out_shape=jax.ShapeDtypeStruct((n, s_len, D_MODEL), F32),
        compiler_params=_params(1),
    )(q, cache_k, cache_v)


def _mem_kv_body(mem_ref, g_ref, wk_ref, wv_ref, k_ref, v_ref):
    mm = _rms(mem_ref[...], g_ref[...]).astype(BF16)
    k_ref[...] = _dot(mm, wk_ref[...])
    v_ref[...] = _dot(mm, wv_ref[...])


def _mem_kv(mem, g, w_ck, w_cv):
    rows = mem.shape[0]
    tm = min(ROW_TILE, rows)
    row_spec = pl.BlockSpec((tm, D_MODEL), lambda i: (i, 0))
    w_spec = _const_spec((D_MODEL, D_MODEL))
    return pl.pallas_call(
        _mem_kv_body,
        grid=(rows // tm,),
        in_specs=[row_spec, _const_spec((1, D_MODEL)), w_spec, w_spec],
        out_specs=[row_spec, row_spec],
        out_shape=[jax.ShapeDtypeStruct((rows, D_MODEL), F32)] * 2,
        compiler_params=_params(1),
    )(mem, g, w_ck, w_cv)


def kernel(x_prompt, x_sample, mem_prompt, cache_mem_k, cache_mem_v, cache_swa_k, cache_swa_v, state_conv, state_lru_h,
           g_ffn1, w1_gate, w1_up, w1_down, g_mix, w_in, conv_w, conv_b, w_a, b_a, w_i, b_i, lam, sink,
           g_lru_out, g_attn_out, w_out, g_xattn, g_mem, w_cq, w_ck, w_cv, w_co, g_ffn2, w2_gate, w2_up, w2_down,
           g_final):
    nbp, seq, _ = x_prompt.shape
    nbs, dec_seq, _ = x_sample.shape
    depth = g_ffn1.shape[0]
    past = PAST_LEN
    cos_p, sin_p = _rope_tables(np.arange(seq, dtype=np.int32))
    sample_pos_rows = min(ROW_TILE, nbs * dec_seq)
    cos_s, sin_s = _rope_tables(np.tile(past + np.arange(dec_seq, dtype=np.int32), sample_pos_rows // dec_seq))

    xp = x_prompt.reshape(nbp * seq, D_MODEL)
    xs = x_sample.reshape(nbs * dec_seq, D_MODEL)
    g_fin = g_final.reshape(1, D_MODEL)
    row = lambda a: a.reshape(1, -1)
    outs = [[] for _ in range(10)]
    for l in range(depth):
        last = l == depth - 1
        bf = lambda a: a[l].astype(BF16)
        w1g, w1u, w1d, w2g, w2u, w2d = bf(w1_gate), bf(w1_up), bf(w1_down), bf(w2_gate), bf(w2_up), bf(w2_down)
        win, wout, wcq, wck, wcv, wco = bf(w_in), bf(w_out), bf(w_cq), bf(w_ck), bf(w_cv), bf(w_co)
        wab, bab = _lru_gate_chunks(w_a[l], b_a[l], w_i[l], b_i[l])
        lru_w = (conv_w[l], row(conv_b[l]), wab, bab, row(lam[l]), row(g_lru_out[l]))

        mk_p, mv_p = _mem_kv(mem_prompt.reshape(nbp * N_MEM, D_MODEL), row(g_mem[l]), wck, wcv)
        mk_p = mk_p.reshape(nbp, N_MEM, D_MODEL)
        mv_p = mv_p.reshape(nbp, N_MEM, D_MODEL)
        xp, q, k, v, lru_out, h_last, u_tail = _ffn_proj_lru(xp, seq, row(g_ffn1[l]), w1g, w1u, w1d, row(g_mix[l]), win,
                                                             cos_p, sin_p, *lru_w)
        k3 = k.reshape(nbp, seq, KV_WIDTH)
        v3 = v.reshape(nbp, seq, KV_WIDTH)
        xp = _swa_mix_ffn(q, k, v, sink[l], row(g_attn_out[l]), xp, lru_out, seq, wout, row(g_xattn[l]), wcq,
                          mk_p, mv_p, wco, row(g_ffn2[l]), w2g, w2u, w2d, g_fin, last)
        outs[0].append(mk_p.reshape(nbp, N_MEM, X_HEADS, X_HEAD_DIM))
        outs[1].append(mv_p.reshape(nbp, N_MEM, X_HEADS, X_HEAD_DIM))
        outs[2].append(k3[:, -WINDOW:].reshape(nbp, WINDOW, KV_HEADS, HEAD_DIM))
        outs[3].append(v3[:, -WINDOW:].reshape(nbp, WINDOW, KV_HEADS, HEAD_DIM))
        outs[4].append(u_tail[:, -(CONV_WIDTH - 1):])
        outs[5].append(h_last.reshape(nbp, LRU_WIDTH))

        xs = _ffn(xs, row(g_ffn1[l]), w1g, w1u, w1d, g_fin, False)
        u, gate, q, k, v = _proj(xs, row(g_mix[l]), win, cos_s, sin_s)
        conv_pad = jnp.pad(state_conv[l], ((0, 0), (dec_seq - (CONV_WIDTH - 1), 0), (0, 0)))
        h0_rep = jnp.repeat(state_lru_h[l], dec_seq, axis=0)
        lru_out, hs = _lru_sample(u, gate, conv_pad.reshape(nbs * dec_seq, LRU_WIDTH), h0_rep, dec_seq, *lru_w)
        per_seq = lambda a: a.reshape(nbs, dec_seq, a.shape[-1])
        attn_out, new_k, new_v = _swa_sample(per_seq(q), per_seq(k), per_seq(v),
                                             cache_swa_k[l].reshape(nbs, WINDOW, KV_WIDTH),
                                             cache_swa_v[l].reshape(nbs, WINDOW, KV_WIDTH), sink[l],
                                             row(g_attn_out[l]), past)
        xs, xq = _mix_query(xs, lru_out, attn_out.reshape(nbs * dec_seq, Q_WIDTH), wout, row(g_xattn[l]), wcq)
        xo = _xattn_cache(per_seq(xq), _interleave_chunks(cache_mem_k[l]), _interleave_chunks(cache_mem_v[l]))
        xs = _ffn(xs, row(g_ffn2[l]), w2g, w2u, w2d, g_fin, last, xo.reshape(nbs * dec_seq, D_MODEL), wco)
        outs[6].append(new_k.reshape(nbs, WINDOW, KV_HEADS, HEAD_DIM))
        outs[7].append(new_v.reshape(nbs, WINDOW, KV_HEADS, HEAD_DIM))
        outs[8].append(u.reshape(nbs, dec_seq, LRU_WIDTH)[:, -(CONV_WIDTH - 1):])
        outs[9].append(hs.reshape(nbs, dec_seq, LRU_WIDTH)[:, -1])

    return (xp.reshape(nbp, seq, D_MODEL), xs.reshape(nbs, dec_seq, D_MODEL)) + tuple(jnp.stack(o) for o in outs)
```

```python
import functools

import jax
import jax.numpy as jnp
import numpy as np
from jax import lax
from jax.experimental import pallas as pl
from jax.experimental.pallas import tpu as pltpu

F32 = jnp.float32
BF16 = jnp.bfloat16

D_MODEL = 1024
LRU_WIDTH = 512
LRU_BLOCKS = 8
CONV_WIDTH = 4
LRU_C = 8.0
ATTN_HEADS = 8
HEAD_DIM = 64
KV_HEADS = 2
WINDOW = 128
PAST_LEN = 8192
ROPE_THETA = 10000.0
N_MEM = 256
X_HEADS = 4
X_HEAD_DIM = 256
D_FF = 2816
EPS = 1e-6
Q_WIDTH = ATTN_HEADS * HEAD_DIM
KV_WIDTH = KV_HEADS * HEAD_DIM
IN_COLS = 2 * LRU_WIDTH + Q_WIDTH + 2 * KV_WIDTH

LANES = 128
SUBLANES = 8
VMEM_LIMIT = 56 * 1024 * 1024

ROW_TILE = 512
LRU_TILE = 512
FFN_CHUNK = 256
SAMPLE_SEQ_TILE = 32
XATTN_SEQ_TILE = 8
XATTN_RIDE_TILE = 2


def _params(n_axes):
    return pltpu.CompilerParams(dimension_semantics=("arbitrary",) * n_axes, vmem_limit_bytes=VMEM_LIMIT)


def _const_spec(shape):
    return pl.BlockSpec(shape, lambda *_: (0,) * len(shape), pipeline_mode=pl.Buffered(1))


def _rms(x, g):
    return x * lax.rsqrt(jnp.mean(x * x, axis=-1, keepdims=True) + EPS) * g


def _dot(a, b):
    return jnp.dot(a, b, preferred_element_type=F32)


def _dot_nt(a, b):
    return lax.dot_general(a, b, (((1,), (1,)), ((), ())), preferred_element_type=F32)


def _ffn_step(x, g_ref, wg_ref, wu_ref, wd_ref):
    xn = _rms(x, g_ref[...]).astype(BF16)
    gate = _dot(xn, wg_ref[...])
    up = _dot(xn, wu_ref[...])
    h = (gate * jax.nn.sigmoid(gate) * up).astype(BF16)
    return x + 0.5 * _dot(h, wd_ref[...])


def _ffn_tail(x, g_ref, wg_ref, wu_ref, wd_ref, gf_ref, o_ref, final_norm):
    y = _ffn_step(x, g_ref, wg_ref, wu_ref, wd_ref)
    if final_norm:
        y = _rms(y, gf_ref[...])
    o_ref[...] = y


def _ffn_body(x_ref, g_ref, wg_ref, wu_ref, wd_ref, gf_ref, o_ref, *, final_norm):
    _ffn_tail(x_ref[...], g_ref, wg_ref, wu_ref, wd_ref, gf_ref, o_ref, final_norm)


def _proj_ffn_body(x_ref, a_ref, wa_ref, g_ref, wg_ref, wu_ref, wd_ref, gf_ref, o_ref, *, final_norm):
    x = x_ref[...] + _dot(a_ref[...].astype(BF16), wa_ref[...])
    _ffn_tail(x, g_ref, wg_ref, wu_ref, wd_ref, gf_ref, o_ref, final_norm)


def _ffn(x, g, wg, wu, wd, g_final, final_norm, attn=None, w_attn=None):
    rows = x.shape[0]
    tm = min(ROW_TILE, rows)
    row_spec = pl.BlockSpec((tm, D_MODEL), lambda i: (i, 0))
    ffn_specs = [_const_spec((1, D_MODEL)), _const_spec((D_MODEL, D_FF)), _const_spec((D_MODEL, D_FF)),
                 _const_spec((D_FF, D_MODEL)), _const_spec((1, D_MODEL))]
    if attn is None:
        body, lead_specs, lead = _ffn_body, [row_spec], (x,)
    else:
        body, lead_specs, lead = _proj_ffn_body, [row_spec, row_spec, _const_spec((D_MODEL, D_MODEL))], (x, attn, w_attn)
    return pl.pallas_call(
        functools.partial(body, final_norm=final_norm),
        grid=(rows // tm,),
        in_specs=lead_specs + ffn_specs,
        out_specs=row_spec,
        out_shape=jax.ShapeDtypeStruct((rows, D_MODEL), F32),
        compiler_params=_params(1),
    )(*lead, g, wg, wu, wd, g_final)


def _rope(z, cos, sin_signed):
    half = HEAD_DIM // 2
    lane = lax.broadcasted_iota(jnp.int32, z.shape, 1)
    first_half = (lane % HEAD_DIM) < half
    partner = jnp.where(first_half, pltpu.roll(z, LANES - half, axis=1), pltpu.roll(z, half, axis=1))
    return z * cos + partner * sin_signed


def _project_pieces(x, g_ref, w_ref, cos_ref, sin_ref, store_u, store_gate, q_ref, k_ref, v_ref):
    o_gate, o_q, o_k, o_v = LRU_WIDTH, 2 * LRU_WIDTH, 2 * LRU_WIDTH + Q_WIDTH, 2 * LRU_WIDTH + Q_WIDTH + KV_WIDTH
    xn = []

    def normed():
        if not xn:
            xn.append(_rms(x(), g_ref[...]).astype(BF16))
        return xn[0]

    def rope_into(ref, z):
        for j in range(z.shape[1] // LANES):
            cols = slice(j * LANES, (j + 1) * LANES)
            ref[:, cols] = _rope(z[:, cols], cos_ref[...], sin_ref[...])

    def store_v(z):
        v_ref[...] = z

    return [lambda: rope_into(q_ref, _dot(normed(), w_ref[:, o_q:o_k])),
            lambda: rope_into(k_ref, _dot(normed(), w_ref[:, o_k:o_v])),
            lambda: store_v(_dot(normed(), w_ref[:, o_v:])),
            lambda: store_u(_dot(normed(), w_ref[:, :o_gate])),
            lambda: store_gate(_dot(normed(), w_ref[:, o_gate:o_q]))]


def _proj_body(x_ref, g_ref, w_ref, cos_ref, sin_ref, u_ref, gate_ref, q_ref, k_ref, v_ref):
    def store_u(z):
        u_ref[...] = z

    def store_gate(z):
        gate_ref[...] = z

    for piece in _project_pieces(lambda: x_ref[...], g_ref, w_ref, cos_ref, sin_ref, store_u, store_gate,
                                 q_ref, k_ref, v_ref):
        piece()


def _proj(x, g, w_in, cos, sin):
    rows = x.shape[0]
    tm = min(ROW_TILE, rows, cos.shape[0])
    pos_blocks = cos.shape[0] // tm

    def row_spec(width):
        return pl.BlockSpec((tm, width), lambda i: (i, 0))

    pos_spec = pl.BlockSpec((tm, LANES), lambda i: (i % pos_blocks, 0))
    widths = (LRU_WIDTH, LRU_WIDTH, Q_WIDTH, KV_WIDTH, KV_WIDTH)
    return pl.pallas_call(
        _proj_body,
        grid=(rows // tm,),
        in_specs=[row_spec(D_MODEL), _const_spec((1, D_MODEL)), _const_spec((D_MODEL, IN_COLS)), pos_spec, pos_spec],
        out_specs=[row_spec(w) for w in widths],
        out_shape=[jax.ShapeDtypeStruct((rows, w), F32) for w in widths],
        compiler_params=_params(1),
    )(x, g, w_in, cos, sin)


def _rope_tables(pos):
    half = HEAD_DIM // 2
    inv = ROPE_THETA ** (-np.arange(half, dtype=np.float64) / half)
    ang = pos.astype(np.float64)[:, None] * inv[None, :]
    cos = np.cos(ang)
    sin = np.sin(ang)
    reps = LANES // HEAD_DIM
    return (jnp.asarray(np.tile(np.concatenate([cos, cos], axis=-1), (1, reps)), dtype=F32),
            jnp.asarray(np.tile(np.concatenate([-sin, sin], axis=-1), (1, reps)), dtype=F32))


def _softplus(x):
    return jnp.maximum(x, 0.0) + jnp.log1p(jnp.exp(-jnp.abs(x)))


def _lru_coeffs(conv, wab, bab, lam):
    w = conv.shape[1]
    gates = _dot(conv.astype(BF16), wab) + bab
    r = jax.nn.sigmoid(gates[:, :w])
    gi = jax.nn.sigmoid(gates[:, w:])
    log_a = -LRU_C * r * _softplus(-lam)
    a = jnp.exp(log_a)
    b = jnp.sqrt(-jnp.tanh(log_a) * (a * a + 1.0)) * (gi * conv)
    return a, b


def _segment_scan(a, b, seg):
    step = 1
    while step < seg:
        a, b = _scan_step(a, b, seg, step)
        step *= 2
    return a, b


def _scan_step(a, b, seg, step):
    pos = lax.broadcasted_iota(jnp.int32, a.shape, 0) % seg
    live = pos >= step
    a_prev = pltpu.roll(a, step, axis=0)
    b_prev = pltpu.roll(b, step, axis=0)
    return jnp.where(live, a * a_prev, a), jnp.where(live, a * b_prev + b, b)


def _lru_hidden(conv, h_in, seg, wab, bab, lam):
    a, b = _lru_coeffs(conv, wab, bab, lam)
    a_cum, h_local = _segment_scan(a, b, seg)
    return a_cum * h_in + h_local


def _lane_chunks(width):
    return [slice(c * LANES, (c + 1) * LANES) for c in range(width // LANES)]


def _ffn_pieces(normed, wg_ref, wu_ref, wd_ref, state):
    cols = [slice(f, min(f + FFN_CHUNK, D_FF)) for f in range(0, D_FF, FFN_CHUNK)]

    def gate_up(f):
        return _dot(normed(), wg_ref[:, cols[f]]), _dot(normed(), wu_ref[:, cols[f]])

    def piece(f):
        def run():
            gate, up = state.pop("gate_up") if "gate_up" in state else gate_up(f)
            if f + 1 < len(cols):
                state["gate_up"] = gate_up(f + 1)
            part = _dot((gate * jax.nn.sigmoid(gate) * up).astype(BF16), wd_ref[cols[f], :])
            state["acc"] = part if "acc" not in state else state["acc"] + part
        return run

    return [piece(f) for f in range(len(cols))]


def _interleave(primary, secondary):
    due = [((i + 1) * len(primary)) // (len(secondary) + 1) for i in range(len(secondary))]
    pending = list(zip(due, secondary))
    for i, piece in enumerate(primary):
        while pending and pending[0][0] <= i:
            pending.pop(0)[1]()
        piece()
    for _, piece in pending:
        piece()


def _ffn_proj_lru_body(x_ref, g1_ref, wg_ref, wu_ref, wd_ref, gmix_ref, win_ref, cos_ref, sin_ref,
                       cw_ref, cb_ref, wab_ref, bab_ref, lam_ref, gout_ref,
                       x_out_ref, q_ref, k_ref, v_ref, lru_ref, hlast_ref, utail_ref,
                       ug_ref, ext_ref, h_ref, hs_ref, *, tt, tiles_per_seq):
    pad = SUBLANES
    s = pl.program_id(0)
    lead_slot = s % 2
    lag_slot = 1 - lead_slot

    @pl.when(s == 0)
    def _():
        ug_ref[1] = jnp.zeros((2, tt, LRU_WIDTH), F32)

    @pl.when((s == 0) | (lax.rem(s - 1, tiles_per_seq) == 0))
    def _():
        ext_ref[0:pad, :] = jnp.zeros((pad, LRU_WIDTH), F32)
        h_ref[...] = jnp.zeros_like(h_ref)

    state = {"sumsq": jnp.zeros((tt, 1), F32)}
    scan_steps = [1 << i for i in range(tt.bit_length() - 1)]

    def lru_pieces(c, cols):
        def coeffs():
            ext_ref[pad:pad + tt, cols] = ug_ref[lag_slot, 0, :, cols]
            conv = cb_ref[:, cols]
            for j in range(CONV_WIDTH):
                start = pad - (CONV_WIDTH - 1) + j
                conv = conv + ext_ref[start:start + tt, cols] * cw_ref[j:j + 1, cols]
            ext_ref[0:pad, cols] = ext_ref[tt:tt + pad, cols]
            utail_ref[0, :, cols] = ext_ref[0:pad, cols]
            state[c] = _lru_coeffs(conv, wab_ref[c], bab_ref[c], lam_ref[:, cols])

        def scan(steps):
            def run():
                for step in steps:
                    state[c] = _scan_step(*state[c], tt, step)
            return run

        def finish():
            a_cum, h_local = state.pop(c)
            hs_ref[:, cols] = a_cum * h_ref[:, cols] + h_local
            h_ref[:, cols] = hs_ref[tt - 1:tt, cols]
            hlast_ref[0, :, cols] = h_ref[:, cols]
            y = hs_ref[:, cols] * jax.nn.gelu(ug_ref[lag_slot, 1, :, cols])
            hs_ref[:, cols] = y
            state["sumsq"] = state["sumsq"] + jnp.sum(y * y, axis=-1, keepdims=True)

        half = len(scan_steps) // 2
        return [coeffs, scan(scan_steps[:half]), scan(scan_steps[half:]), finish]

    vector_pieces = [p for c, cols in enumerate(_lane_chunks(LRU_WIDTH)) for p in lru_pieces(c, cols)]

    def ffn_in():
        if "xn" not in state:
            state["xn"] = _rms(x_ref[...], g1_ref[...]).astype(BF16)
        return state["xn"]

    def ffn_out():
        if "x" not in state:
            state["x"] = x_ref[...] + 0.5 * state.pop("acc")
            x_out_ref[...] = state["x"]
        return state["x"]

    def store_u(z):
        ug_ref[lead_slot, 0] = z

    def store_gate(z):
        ug_ref[lead_slot, 1] = z

    matmul_pieces = _ffn_pieces(ffn_in, wg_ref, wu_ref, wd_ref, state)
    matmul_pieces += _project_pieces(ffn_out, gmix_ref, win_ref, cos_ref, sin_ref, store_u, store_gate,
                                     q_ref, k_ref, v_ref)

    assert len(vector_pieces) == len(matmul_pieces)
    for vector_piece, matmul_piece in zip(vector_pieces, matmul_pieces):
        vector_piece()
        matmul_piece()
    lru_ref[...] = hs_ref[...] * lax.rsqrt(state["sumsq"] * (1.0 / LRU_WIDTH) + EPS) * gout_ref[...]


def _ffn_proj_lru(x, seq, g1, wg, wu, wd, g_mix, w_in, cos, sin, conv_w, conv_b, wab, bab, lam, g_out):
    rows = x.shape[0]
    n = rows // seq
    tt = min(LRU_TILE, seq)
    tiles_per_seq = seq // tt
    tiles = rows // tt
    pos_blocks = cos.shape[0] // tt
    lead = lambda s: jnp.minimum(s, tiles - 1)
    lag = lambda s: jnp.maximum(s - 1, 0)

    def lead_spec(width):
        return pl.BlockSpec((tt, width), lambda s: (lead(s), 0))

    pos_spec = pl.BlockSpec((tt, LANES), lambda s: (lead(s) % pos_blocks, 0))
    seq_spec = lambda r: pl.BlockSpec((1, r, LRU_WIDTH), lambda s: (lag(s) // tiles_per_seq, 0, 0))
    widths = (D_MODEL, Q_WIDTH, KV_WIDTH, KV_WIDTH)
    return pl.pallas_call(
        functools.partial(_ffn_proj_lru_body, tt=tt, tiles_per_seq=tiles_per_seq),
        grid=(tiles + 1,),
        in_specs=[lead_spec(D_MODEL), _const_spec((1, D_MODEL)), _const_spec((D_MODEL, D_FF)), _const_spec((D_MODEL, D_FF)),
                  _const_spec((D_FF, D_MODEL)), _const_spec((1, D_MODEL)), _const_spec((D_MODEL, IN_COLS)),
                  pos_spec, pos_spec] + _lru_weight_specs(),
        out_specs=[lead_spec(w) for w in widths]
        + [pl.BlockSpec((tt, LRU_WIDTH), lambda s: (lag(s), 0)), seq_spec(1), seq_spec(SUBLANES)],
        out_shape=[jax.ShapeDtypeStruct((rows, w), F32) for w in widths]
        + [jax.ShapeDtypeStruct((rows, LRU_WIDTH), F32), jax.ShapeDtypeStruct((n, 1, LRU_WIDTH), F32),
           jax.ShapeDtypeStruct((n, SUBLANES, LRU_WIDTH), F32)],
        scratch_shapes=[pltpu.VMEM((2, 2, tt, LRU_WIDTH), F32), pltpu.VMEM((tt + SUBLANES, LRU_WIDTH), F32),
                        pltpu.VMEM((1, LRU_WIDTH), F32), pltpu.VMEM((tt, LRU_WIDTH), F32)],
        compiler_params=_params(1),
    )(x, g1, wg, wu, wd, g_mix, w_in, cos, sin, conv_w, conv_b, wab, bab, lam, g_out)


def _lru_sample_body(u_ref, gate_ref, cpad_ref, h0_ref, cw_ref, cb_ref, wab_ref, bab_ref, lam_ref, gout_ref,
                     o_ref, hs_ref, *, seg):
    rows = u_ref.shape[0]
    pos = lax.broadcasted_iota(jnp.int32, (rows, LANES), 0) % seg
    sumsq = jnp.zeros((rows, 1), F32)
    for c, cols in enumerate(_lane_chunks(LRU_WIDTH)):
        u = u_ref[:, cols]
        cpad = cpad_ref[:, cols]
        conv = cb_ref[:, cols] + u * cw_ref[CONV_WIDTH - 1:CONV_WIDTH, cols]
        for back in range(1, CONV_WIDTH):
            shifted = jnp.where(pos >= back, pltpu.roll(u, back, axis=0),
                                pltpu.roll(cpad, (back - seg) % rows, axis=0))
            conv = conv + shifted * cw_ref[CONV_WIDTH - 1 - back:CONV_WIDTH - back, cols]
        h = _lru_hidden(conv, h0_ref[:, cols], seg, wab_ref[c], bab_ref[c], lam_ref[:, cols])
        hs_ref[:, cols] = h
        y = h * jax.nn.gelu(gate_ref[:, cols])
        o_ref[:, cols] = y
        sumsq = sumsq + jnp.sum(y * y, axis=-1, keepdims=True)
    o_ref[...] = o_ref[...] * lax.rsqrt(sumsq * (1.0 / LRU_WIDTH) + EPS) * gout_ref[...]


def _lru_weight_specs():
    chunks = LRU_WIDTH // LANES
    return [_const_spec((CONV_WIDTH, LRU_WIDTH)), _const_spec((1, LRU_WIDTH)),
            _const_spec((chunks, LANES, 2 * LANES)), _const_spec((chunks, 1, 2 * LANES)),
            _const_spec((1, LRU_WIDTH)), _const_spec((1, LRU_WIDTH))]


def _lru_gate_chunks(w_a, b_a, w_i, b_i):
    chunks = LRU_WIDTH // LANES
    per = LRU_BLOCKS // chunks
    wa = w_a.reshape(chunks, per, *w_a.shape[1:])
    wi = w_i.reshape(chunks, per, *w_i.shape[1:])
    wab = jnp.stack([jnp.concatenate([_block_diag(wa[c]), _block_diag(wi[c])], axis=1) for c in range(chunks)])
    bab = jnp.concatenate([b_a.reshape(chunks, 1, LANES), b_i.reshape(chunks, 1, LANES)], axis=2)
    return wab.astype(BF16), bab


def _lru_sample(u, gate, conv_pad, h0_rep, seg, conv_w, conv_b, wab, bab, lam, g_out):
    rows = u.shape[0]
    tm = min(ROW_TILE, rows)
    row_spec = pl.BlockSpec((tm, LRU_WIDTH), lambda i: (i, 0))
    return pl.pallas_call(
        functools.partial(_lru_sample_body, seg=seg),
        grid=(rows // tm,),
        in_specs=[row_spec, row_spec, row_spec, row_spec] + _lru_weight_specs(),
        out_specs=[row_spec, row_spec],
        out_shape=[jax.ShapeDtypeStruct((rows, LRU_WIDTH), F32)] * 2,
        compiler_params=_params(1),
    )(u, gate, conv_pad, h0_rep, conv_w, conv_b, wab, bab, lam, g_out)


def _block_diag(w):
    nb, bi, bj = w.shape
    eye = jnp.eye(nb, dtype=w.dtype)
    return jnp.einsum('gij,gh->gihj', w, eye).reshape(nb * bi, nb * bj)


def _bdot_nt(a, b):
    return lax.dot_general(a, b, (((2,), (2,)), ((0,), (0,))), preferred_element_type=F32)


def _bdot(a, b):
    return lax.dot_general(a, b, (((2,), (1,)), ((0,), (0,))), preferred_element_type=F32)


def _swa_pieces(q, keys, vals, sink_ref, mask, g_out, store):
    group = ATTN_HEADS // KV_HEADS
    straight = [h for h in range(ATTN_HEADS) if (h % 2) == (h // group)]
    swapped = [h for h in range(ATTN_HEADS) if (h % 2) != (h // group)]
    scale = HEAD_DIM ** -0.5
    out_half = {}

    def low_lanes(shape):
        return lax.broadcasted_iota(jnp.int32, shape, 2) < HEAD_DIM

    def head_group(heads, swap):
        held = {}

        def arrange(z):
            return pltpu.roll(z, HEAD_DIM, axis=2) if swap else z

        def scores():
            qv = q()
            n, r, _ = qv.shape
            low = low_lanes((n, r, LANES))
            zero = jnp.zeros((n, r, LANES), F32)
            qs = jnp.concatenate(
                [jnp.where(low if h % 2 == 0 else ~low, qv[:, :, (h // 2) * LANES:(h // 2 + 1) * LANES], zero)
                 for h in heads], axis=1)
            held["s"] = _bdot_nt(qs.astype(BF16), arrange(keys()).astype(BF16)) * scale

        def probabilities():
            s = held.pop("s")
            r = s.shape[1] // len(heads)
            visible = mask()[None]
            probs = []
            for i, h in enumerate(heads):
                sh = jnp.where(visible, s[:, i * r:(i + 1) * r, :], -jnp.inf)
                sink = sink_ref[h]
                m = jnp.maximum(jnp.max(sh, axis=-1, keepdims=True), sink)
                e = jnp.exp(sh - m)
                denom = jnp.sum(e, axis=-1, keepdims=True) + jnp.exp(sink - m)
                probs.append(e / denom)
            held["p"] = jnp.concatenate(probs, axis=1).astype(BF16)

        def weighted_values():
            p = held.pop("p")
            r = p.shape[1] // len(heads)
            o = _bdot(p, arrange(vals()).astype(BF16))
            for i, h in enumerate(heads):
                out_half[h] = o[:, i * r:(i + 1) * r, :]

        return [scores, probabilities, weighted_values]

    def finish():
        low = low_lanes(out_half[0].shape)
        out = jnp.concatenate([jnp.where(low, out_half[2 * j], out_half[2 * j + 1])
                               for j in range(Q_WIDTH // LANES)], axis=2)
        store(_rms(out, g_out))

    return head_group(straight, False) + head_group(swapped, True) + [finish]


def _band_mask(first_block):
    i = lax.broadcasted_iota(jnp.int32, (WINDOW, 2 * WINDOW), 0)
    j = lax.broadcasted_iota(jnp.int32, (WINDOW, 2 * WINDOW), 1)
    dist = i + WINDOW - j
    return (dist >= 0) & (dist < WINDOW) & (jnp.logical_not(first_block) | (j >= WINDOW))


def _swa_sample_body(sink_ref, q_ref, k_ref, v_ref, bk_ref, bv_ref, gout_ref, o_ref, nk_ref, nv_ref, *, s_len, past):
    def mask():
        qp = past + lax.broadcasted_iota(jnp.int32, (s_len, WINDOW + s_len), 0)
        col = lax.broadcasted_iota(jnp.int32, (s_len, WINDOW + s_len), 1)
        kp = jnp.where(col < WINDOW, past - WINDOW + col, past + col - WINDOW)
        dist = qp - kp
        return (dist >= 0) & (dist < WINDOW) & (kp >= 0)

    keys = jnp.concatenate([bk_ref[...], k_ref[...]], axis=1)
    vals = jnp.concatenate([bv_ref[...], v_ref[...]], axis=1)

    def store(o):
        o_ref[...] = o

    for piece in _swa_pieces(lambda: q_ref[...], lambda: keys, lambda: vals, sink_ref, mask, gout_ref[...], store):
        piece()
    nk_ref[...] = keys[:, s_len:, :]
    nv_ref[...] = vals[:, s_len:, :]


def _swa_sample(q, k, v, buf_k, buf_v, sink, g_out, past):
    n, s_len, _ = q.shape
    sb = min(SAMPLE_SEQ_TILE, n)

    def seq_spec(rows, width):
        return pl.BlockSpec((sb, rows, width), lambda i: (i, 0, 0))

    buf_spec = seq_spec(WINDOW, KV_WIDTH)
    return pl.pallas_call(
        functools.partial(_swa_sample_body, s_len=s_len, past=past),
        grid=(n // sb,),
        in_specs=[pl.BlockSpec(memory_space=pltpu.SMEM), seq_spec(s_len, Q_WIDTH), seq_spec(s_len, KV_WIDTH),
                  seq_spec(s_len, KV_WIDTH), buf_spec, buf_spec, _const_spec((1, Q_WIDTH))],
        out_specs=[seq_spec(s_len, Q_WIDTH), buf_spec, buf_spec],
        out_shape=[jax.ShapeDtypeStruct((n, s_len, Q_WIDTH), F32),
                   jax.ShapeDtypeStruct((n, WINDOW, KV_WIDTH), F32), jax.ShapeDtypeStruct((n, WINDOW, KV_WIDTH), F32)],
        compiler_params=_params(1),
    )(sink, q, k, v, buf_k, buf_v, g_out)


def _softmax(s):
    e = jnp.exp(s - jnp.max(s, axis=-1, keepdims=True))
    return e / jnp.sum(e, axis=-1, keepdims=True)


def _mix_and_query(x_ref, lru_ref, attn_ref, wout_ref, gx_ref, wcq_ref):
    x = (x_ref[...] + _dot(lru_ref[...].astype(BF16), wout_ref[:LRU_WIDTH, :])
         + _dot(attn_ref[...].astype(BF16), wout_ref[LRU_WIDTH:, :]))
    return x, _dot(_rms(x, gx_ref[...]).astype(BF16), wcq_ref[...])


def _swa_mix_ffn_body(sink_ref, q_ref, k_ref, v_ref, kp_ref, vp_ref, gattn_ref,
                      x_ref, lru_ref, wout_ref, gx_ref, wcq_ref, mk_ref, mv_ref, wco_ref,
                      g2_ref, wg_ref, wu_ref, wd_ref, gf_ref, sq_ref, sk_ref, sv_ref, o_ref, so_ref, attn_ref,
                      *, tt, tiles, tiles_per_seq, final_norm):
    s = pl.program_id(0)
    refs = (sink_ref, q_ref, k_ref, v_ref, kp_ref, vp_ref, gattn_ref, x_ref, lru_ref, wout_ref, gx_ref, wcq_ref,
            mk_ref, mv_ref, wco_ref, g2_ref, wg_ref, wu_ref, wd_ref, gf_ref, sq_ref, sk_ref, sv_ref,
            o_ref, so_ref, attn_ref)
    step = functools.partial(_swa_mix_ffn_step, *refs, tt=tt, tiles_per_seq=tiles_per_seq, final_norm=final_norm)

    @pl.when(s == 0)
    def _():
        step(attend=True, layer=False)

    @pl.when((s > 0) & (s < tiles))
    def _():
        step(attend=True, layer=True)

    @pl.when(s == tiles)
    def _():
        step(attend=False, layer=True)


def _swa_mix_ffn_step(sink_ref, q_ref, k_ref, v_ref, kp_ref, vp_ref, gattn_ref,
                      x_ref, lru_ref, wout_ref, gx_ref, wcq_ref, mk_ref, mv_ref, wco_ref,
                      g2_ref, wg_ref, wu_ref, wd_ref, gf_ref, sq_ref, sk_ref, sv_ref, o_ref, so_ref, attn_ref,
                      *, tt, tiles_per_seq, final_norm, attend, layer):
    s = pl.program_id(0)
    lead_slot = s % 2
    lag_slot = 1 - lead_slot

    blocks = tt // WINDOW
    first_pos_block = lax.rem(s, tiles_per_seq) * blocks
    attention_pieces = []
    for j in range(blocks):
        rows = slice(j * WINDOW, (j + 1) * WINDOW)

        def band(ref, prev_ref, j=j, rows=rows):
            prev = prev_ref[...] if j == 0 else ref[(j - 1) * WINDOW:j * WINDOW, :]
            return jnp.concatenate([prev, ref[rows, :]], axis=0)[None]

        def store(o, rows=rows):
            attn_ref[lead_slot, rows, :] = o[0]

        attention_pieces += _swa_pieces(
            lambda rows=rows: q_ref[rows, :][None], functools.partial(band, k_ref, kp_ref),
            functools.partial(band, v_ref, vp_ref), sink_ref,
            lambda j=j: _band_mask(first_pos_block + j == 0), gattn_ref[...], store)

    state = {}
    scale = X_HEAD_DIM ** -0.5

    def mix():
        state["x"] = (x_ref[...] + _dot(lru_ref[...].astype(BF16), wout_ref[:LRU_WIDTH, :])
                      + _dot(attn_ref[lag_slot].astype(BF16), wout_ref[LRU_WIDTH:, :]))

    def query():
        state["q"] = _dot(_rms(state["x"], gx_ref[...]).astype(BF16), wcq_ref[...])

    def memory_head(h):
        def run():
            cols = slice(h * X_HEAD_DIM, (h + 1) * X_HEAD_DIM)
            sc = _dot_nt(state["q"][:, cols].astype(BF16), mk_ref[0, :, cols].astype(BF16)) * scale
            state["o", h] = _dot(_softmax(sc).astype(BF16), mv_ref[0, :, cols].astype(BF16))
        return run

    def memory_out():
        o = jnp.concatenate([state.pop(("o", h)) for h in range(X_HEADS)], axis=1)
        state["x"] = state["x"] + _dot(o.astype(BF16), wco_ref[...])
        state["xn"] = _rms(state["x"], g2_ref[...]).astype(BF16)

    def finish():
        y = state["x"] + 0.5 * state["acc"]
        o_ref[...] = _rms(y, gf_ref[...]) if final_norm else y

    layer_pieces = ([mix, query] + [memory_head(h) for h in range(X_HEADS)] + [memory_out]
                    + _ffn_pieces(lambda: state["xn"], wg_ref, wu_ref, wd_ref, state) + [finish])
    side_pieces = []
    if attend:
        rider = _xattn_cache_pieces(sq_ref, sk_ref, sv_ref, so_ref)
        cut = [(i * len(attention_pieces)) // len(rider) for i in range(len(rider))] + [len(attention_pieces)]
        for i, piece in enumerate(rider):
            side_pieces += attention_pieces[cut[i]:cut[i + 1]] + [piece]
    _interleave(layer_pieces if layer else [], side_pieces)


def _rider_count(prompt_rows, seq, sample_seqs):
    tiles = prompt_rows // min(LRU_TILE, seq)
    return min(tiles * XATTN_RIDE_TILE, sample_seqs)


def _swa_mix_ffn(q, k, v, sink, g_attn, x, lru_out, seq, w_out, g_x, w_cq, mk, mv, w_co, g2, wg, wu, wd, g_final,
                 final_norm, sample_q, cache_k, cache_v):
    rows = x.shape[0]
    tt = min(LRU_TILE, seq)
    tiles_per_seq = seq // tt
    tiles = rows // tt
    blocks = tt // WINDOW
    lead = lambda s: jnp.minimum(s, tiles - 1)
    lag = lambda s: jnp.maximum(s - 1, 0)
    n_sample, s_len, _ = sample_q.shape
    ride = _rider_count(rows, seq, n_sample)
    ride_tile = ride // tiles
    first_ride_block = (n_sample - ride) // ride_tile
    ride_q_spec = pl.BlockSpec((ride_tile, s_len, D_MODEL), lambda s: (first_ride_block + lead(s), 0, 0))
    ride_mem_spec = pl.BlockSpec((ride_tile, cache_k.shape[1], LANES), lambda s: (first_ride_block + lead(s), 0, 0))

    def lead_spec(width):
        return pl.BlockSpec((tt, width), lambda s: (lead(s), 0))

    def lag_spec(width):
        return pl.BlockSpec((tt, width), lambda s: (lag(s), 0))

    prev_spec = pl.BlockSpec((WINDOW, KV_WIDTH), lambda s: (jnp.maximum(lead(s) * blocks - 1, 0), 0))
    mem_spec = pl.BlockSpec((1, N_MEM, D_MODEL), lambda s: (lag(s) // tiles_per_seq, 0, 0),
                            pipeline_mode=pl.Buffered(1))
    w_spec = _const_spec((D_MODEL, D_MODEL))
    return pl.pallas_call(
        functools.partial(_swa_mix_ffn_body, tt=tt, tiles=tiles, tiles_per_seq=tiles_per_seq, final_norm=final_norm),
        grid=(tiles + 1,),
        in_specs=[pl.BlockSpec(memory_space=pltpu.SMEM), lead_spec(Q_WIDTH), lead_spec(KV_WIDTH), lead_spec(KV_WIDTH),
                  prev_spec, prev_spec, _const_spec((1, Q_WIDTH)),
                  lag_spec(D_MODEL), lag_spec(LRU_WIDTH), w_spec, _const_spec((1, D_MODEL)), w_spec,
                  mem_spec, mem_spec, w_spec,
                  _const_spec((1, D_MODEL)), _const_spec((D_MODEL, D_FF)), _const_spec((D_MODEL, D_FF)),
                  _const_spec((D_FF, D_MODEL)), _const_spec((1, D_MODEL)),
                  ride_q_spec, ride_mem_spec, ride_mem_spec],
        out_specs=[lag_spec(D_MODEL), pl.BlockSpec((ride_tile, s_len, D_MODEL), lambda s: (lead(s), 0, 0))],
        out_shape=[jax.ShapeDtypeStruct((rows, D_MODEL), F32), jax.ShapeDtypeStruct((ride, s_len, D_MODEL), F32)],
        scratch_shapes=[pltpu.VMEM((2, tt, Q_WIDTH), F32)],
        compiler_params=_params(1),
    )(sink, q, k, v, k, v, g_attn, x, lru_out, w_out, g_x, w_cq, mk, mv, w_co, g2, wg, wu, wd, g_final,
      sample_q, cache_k, cache_v)


def _mix_query_body(x_ref, lru_ref, attn_ref, wout_ref, gx_ref, wcq_ref, x_out_ref, q_ref):
    x_out_ref[...], q_ref[...] = _mix_and_query(x_ref, lru_ref, attn_ref, wout_ref, gx_ref, wcq_ref)


def _mix_query(x, lru_out, attn_out, w_out, g_x, w_cq):
    rows = x.shape[0]
    tm = min(ROW_TILE, rows)

    def row_spec(width):
        return pl.BlockSpec((tm, width), lambda i: (i, 0))

    w_spec = _const_spec((D_MODEL, D_MODEL))
    return pl.pallas_call(
        _mix_query_body,
        grid=(rows // tm,),
        in_specs=[row_spec(D_MODEL), row_spec(LRU_WIDTH), row_spec(Q_WIDTH), w_spec, _const_spec((1, D_MODEL)), w_spec],
        out_specs=[row_spec(D_MODEL), row_spec(D_MODEL)],
        out_shape=[jax.ShapeDtypeStruct((rows, D_MODEL), F32)] * 2,
        compiler_params=_params(1),
    )(x, lru_out, attn_out, w_out, g_x, w_cq)


def _xattn_cache_pieces(q_ref, mk_ref, mv_ref, o_ref):
    sb, steps, _ = q_ref.shape
    blocks = D_MODEL // LANES
    chunks = X_HEAD_DIM // LANES
    width = mk_ref.shape[1]
    block_cls = [(j % chunks) * X_HEADS + j // chunks for j in range(blocks)]
    scale = X_HEAD_DIM ** -0.5
    held = {}

    def lane_class():
        return lax.broadcasted_iota(jnp.int32, (sb, steps, width), 2) % blocks

    def scores():
        qs = jnp.concatenate([q_ref[:, :, j * LANES:(j + 1) * LANES] for j in range(blocks)], axis=1)
        held["s"] = _bdot_nt(qs.astype(BF16), mk_ref[...].astype(BF16)) * scale

    def probabilities():
        s = held.pop("s")
        cls = lane_class()
        part = jnp.zeros((sb, steps, width), F32)
        for j in range(blocks):
            part = part + jnp.where(cls == block_cls[j], s[:, j * steps:(j + 1) * steps, :], 0.0)
        score = part + pltpu.roll(part, width - X_HEADS, axis=2)
        top = jnp.zeros((sb, steps, width), F32)
        for h in range(X_HEADS):
            mine = cls == h
            top = jnp.where(mine, jnp.max(jnp.where(mine, score, -jnp.inf), axis=-1, keepdims=True), top)
        e = jnp.where(cls < X_HEADS, jnp.exp(score - top), 0.0)
        denom = jnp.ones((sb, steps, width), F32)
        for h in range(X_HEADS):
            mine = cls == h
            denom = jnp.where(mine, jnp.sum(jnp.where(mine, e, 0.0), axis=-1, keepdims=True), denom)
        p = e / denom
        p = p + pltpu.roll(p, X_HEADS, axis=2)
        held["p"] = jnp.concatenate([jnp.where(cls == block_cls[j], p, 0.0) for j in range(blocks)],
                                    axis=1).astype(BF16)

    def weighted_values():
        o = _bdot(held.pop("p"), mv_ref[...].astype(BF16))
        for j in range(blocks):
            o_ref[:, :, j * LANES:(j + 1) * LANES] = o[:, j * steps:(j + 1) * steps, :]

    return [scores, probabilities, weighted_values]


def _xattn_cache_body(q_ref, mk_ref, mv_ref, o_ref):
    for piece in _xattn_cache_pieces(q_ref, mk_ref, mv_ref, o_ref):
        piece()


def _interleave_chunks(cache):
    n = cache.shape[0]
    chunks = X_HEAD_DIM // LANES
    c = cache.reshape(n, N_MEM, X_HEADS, chunks, LANES)
    return jnp.transpose(c, (0, 1, 3, 2, 4)).reshape(n, N_MEM * chunks * X_HEADS, LANES)


def _xattn_cache(q, cache_k, cache_v, count):
    _, s_len, _ = q.shape
    sb = min(XATTN_SEQ_TILE, count)
    q_spec = pl.BlockSpec((sb, s_len, D_MODEL), lambda i: (i, 0, 0))
    rows = cache_k.shape[1]
    mem_spec = pl.BlockSpec((sb, rows, LANES), lambda i: (i, 0, 0))
    return pl.pallas_call(
        _xattn_cache_body,
        grid=(count // sb,),
        in_specs=[q_spec, mem_spec, mem_spec],
        out_specs=q_spec,
        out_shape=jax.ShapeDtypeStruct((count, s_len, D_MODEL), F32),
        compiler_params=_params(1),
    )(q, cache_k, cache_v)


def _mem_kv_body(mem_ref, g_ref, wk_ref, wv_ref, k_ref, v_ref):
    mm = _rms(mem_ref[...], g_ref[...]).astype(BF16)
    k_ref[...] = _dot(mm, wk_ref[...])
    v_ref[...] = _dot(mm, wv_ref[...])


def _mem_kv(mem, g, w_ck, w_cv):
    rows = mem.shape[0]
    tm = min(ROW_TILE, rows)
    row_spec = pl.BlockSpec((tm, D_MODEL), lambda i: (i, 0))
    w_spec = _const_spec((D_MODEL, D_MODEL))
    return pl.pallas_call(
        _mem_kv_body,
        grid=(rows // tm,),
        in_specs=[row_spec, _const_spec((1, D_MODEL)), w_spec, w_spec],
        out_specs=[row_spec, row_spec],
        out_shape=[jax.ShapeDtypeStruct((rows, D_MODEL), F32)] * 2,
        compiler_params=_params(1),
    )(mem, g, w_ck, w_cv)


def kernel(x_prompt, x_sample, mem_prompt, cache_mem_k, cache_mem_v, cache_swa_k, cache_swa_v, state_conv, state_lru_h,
           g_ffn1, w1_gate, w1_up, w1_down, g_mix, w_in, conv_w, conv_b, w_a, b_a, w_i, b_i, lam, sink,
           g_lru_out, g_attn_out, w_out, g_xattn, g_mem, w_cq, w_ck, w_cv, w_co, g_ffn2, w2_gate, w2_up, w2_down,
           g_final):
    nbp, seq, _ = x_prompt.shape
    nbs, dec_seq, _ = x_sample.shape
    depth = g_ffn1.shape[0]
    past = PAST_LEN
    cos_p, sin_p = _rope_tables(np.arange(seq, dtype=np.int32))
    sample_pos_rows = min(ROW_TILE, nbs * dec_seq)
    cos_s, sin_s = _rope_tables(np.tile(past + np.arange(dec_seq, dtype=np.int32), sample_pos_rows // dec_seq))

    xp = x_prompt.reshape(nbp * seq, D_MODEL)
    xs = x_sample.reshape(nbs * dec_seq, D_MODEL)
    g_fin = g_final.reshape(1, D_MODEL)
    row = lambda a: a.reshape(1, -1)
    outs = [[] for _ in range(10)]
    for l in range(depth):
        last = l == depth - 1
        bf = lambda a: a[l].astype(BF16)
        w1g, w1u, w1d, w2g, w2u, w2d = bf(w1_gate), bf(w1_up), bf(w1_down), bf(w2_gate), bf(w2_up), bf(w2_down)
        win, wout, wcq, wck, wcv, wco = bf(w_in), bf(w_out), bf(w_cq), bf(w_ck), bf(w_cv), bf(w_co)
        wab, bab = _lru_gate_chunks(w_a[l], b_a[l], w_i[l], b_i[l])
        lru_w = (conv_w[l], row(conv_b[l]), wab, bab, row(lam[l]), row(g_lru_out[l]))

        xs = _ffn(xs, row(g_ffn1[l]), w1g, w1u, w1d, g_fin, False)
        u_s, gate, q, k, v = _proj(xs, row(g_mix[l]), win, cos_s, sin_s)
        conv_pad = jnp.pad(state_conv[l], ((0, 0), (dec_seq - (CONV_WIDTH - 1), 0), (0, 0)))
        h0_rep = jnp.repeat(state_lru_h[l], dec_seq, axis=0)
        lru_out, hs = _lru_sample(u_s, gate, conv_pad.reshape(nbs * dec_seq, LRU_WIDTH), h0_rep, dec_seq, *lru_w)
        per_seq = lambda a: a.reshape(nbs, dec_seq, a.shape[-1])
        attn_out, new_k, new_v = _swa_sample(per_seq(q), per_seq(k), per_seq(v),
                                             cache_swa_k[l].reshape(nbs, WINDOW, KV_WIDTH),
                                             cache_swa_v[l].reshape(nbs, WINDOW, KV_WIDTH), sink[l],
                                             row(g_attn_out[l]), past)
        xs, xq = _mix_query(xs, lru_out, attn_out.reshape(nbs * dec_seq, Q_WIDTH), wout, row(g_xattn[l]), wcq)
        xq, cache_k, cache_v = per_seq(xq), _interleave_chunks(cache_mem_k[l]), _interleave_chunks(cache_mem_v[l])

        mk_p, mv_p = _mem_kv(mem_prompt.reshape(nbp * N_MEM, D_MODEL), row(g_mem[l]), wck, wcv)
        mk_p = mk_p.reshape(nbp, N_MEM, D_MODEL)
        mv_p = mv_p.reshape(nbp, N_MEM, D_MODEL)
        xp, q, k, v, lru_out, h_last, u_tail = _ffn_proj_lru(xp, seq, row(g_ffn1[l]), w1g, w1u, w1d, row(g_mix[l]), win,
                                                             cos_p, sin_p, *lru_w)
        k3 = k.reshape(nbp, seq, KV_WIDTH)
        v3 = v.reshape(nbp, seq, KV_WIDTH)
        xp, xo_ride = _swa_mix_ffn(q, k, v, sink[l], row(g_attn_out[l]), xp, lru_out, seq, wout, row(g_xattn[l]), wcq,
                                   mk_p, mv_p, wco, row(g_ffn2[l]), w2g, w2u, w2d, g_fin, last, xq, cache_k, cache_v)
        outs[0].append(mk_p.reshape(nbp, N_MEM, X_HEADS, X_HEAD_DIM))
        outs[1].append(mv_p.reshape(nbp, N_MEM, X_HEADS, X_HEAD_DIM))
        outs[2].append(k3[:, -WINDOW:].reshape(nbp, WINDOW, KV_HEADS, HEAD_DIM))
        outs[3].append(v3[:, -WINDOW:].reshape(nbp, WINDOW, KV_HEADS, HEAD_DIM))
        outs[4].append(u_tail[:, -(CONV_WIDTH - 1):])
        outs[5].append(h_last.reshape(nbp, LRU_WIDTH))

        own = nbs - xo_ride.shape[0]
        xo = xo_ride if own == 0 else jnp.concatenate([_xattn_cache(xq, cache_k, cache_v, own), xo_ride], axis=0)
        xs = _ffn(xs, row(g_ffn2[l]), w2g, w2u, w2d, g_fin, last, xo.reshape(nbs * dec_seq, D_MODEL), wco)
        outs[6].append(new_k.reshape(nbs, WINDOW, KV_HEADS, HEAD_DIM))
        outs[7].append(new_v.reshape(nbs, WINDOW, KV_HEADS, HEAD_DIM))
        outs[8].append(u_s.reshape(nbs, dec_seq, LRU_WIDTH)[:, -(CONV_WIDTH - 1):])
        outs[9].append(hs.reshape(nbs, dec_seq, LRU_WIDTH)[:, -1])

    return (xp.reshape(nbp, seq, D_MODEL), xs.reshape(nbs, dec_seq, D_MODEL)) + tuple(jnp.stack(o) for o in outs)
```

```python
import functools

import jax
import jax.numpy as jnp
import numpy as np
from jax import lax
from jax.experimental import pallas as pl
from jax.experimental.pallas import tpu as pltpu

F32 = jnp.float32
BF16 = jnp.bfloat16

D_MODEL = 1024
LRU_WIDTH = 512
LRU_BLOCKS = 8
CONV_WIDTH = 4
LRU_C = 8.0
ATTN_HEADS = 8
HEAD_DIM = 64
KV_HEADS = 2
WINDOW = 128
PAST_LEN = 8192
ROPE_THETA = 10000.0
N_MEM = 256
X_HEADS = 4
X_HEAD_DIM = 256
D_FF = 2816
EPS = 1e-6
Q_WIDTH = ATTN_HEADS * HEAD_DIM
KV_WIDTH = KV_HEADS * HEAD_DIM
IN_COLS = 2 * LRU_WIDTH + Q_WIDTH + 2 * KV_WIDTH

LANES = 128
SUBLANES = 8
VMEM_LIMIT = 56 * 1024 * 1024

ROW_TILE = 512
LRU_TILE = 512
FFN_CHUNK = 256
SAMPLE_SEQ_TILE = 32
XATTN_SEQ_TILE = 8


def _params(n_axes):
    return pltpu.CompilerParams(dimension_semantics=("arbitrary",) * n_axes, vmem_limit_bytes=VMEM_LIMIT)


def _const_spec(shape):
    return pl.BlockSpec(shape, lambda *_: (0,) * len(shape), pipeline_mode=pl.Buffered(1))


def _rms(x, g):
    return x * lax.rsqrt(jnp.mean(x * x, axis=-1, keepdims=True) + EPS) * g


def _dot(a, b):
    return jnp.dot(a, b, preferred_element_type=F32)


def _dot_nt(a, b):
    return lax.dot_general(a, b, (((1,), (1,)), ((), ())), preferred_element_type=F32)


def _ffn_step(x, g_ref, wg_ref, wu_ref, wd_ref):
    xn = _rms(x, g_ref[...]).astype(BF16)
    gate = _dot(xn, wg_ref[...])
    up = _dot(xn, wu_ref[...])
    h = (gate * jax.nn.sigmoid(gate) * up).astype(BF16)
    return x + 0.5 * _dot(h, wd_ref[...])


def _ffn_tail(x, g_ref, wg_ref, wu_ref, wd_ref, gf_ref, o_ref, final_norm):
    y = _ffn_step(x, g_ref, wg_ref, wu_ref, wd_ref)
    if final_norm:
        y = _rms(y, gf_ref[...])
    o_ref[...] = y


def _ffn_body(x_ref, g_ref, wg_ref, wu_ref, wd_ref, gf_ref, o_ref, *, final_norm):
    _ffn_tail(x_ref[...], g_ref, wg_ref, wu_ref, wd_ref, gf_ref, o_ref, final_norm)


def _proj_ffn_body(x_ref, a_ref, wa_ref, g_ref, wg_ref, wu_ref, wd_ref, gf_ref, o_ref, *, final_norm):
    x = x_ref[...] + _dot(a_ref[...].astype(BF16), wa_ref[...])
    _ffn_tail(x, g_ref, wg_ref, wu_ref, wd_ref, gf_ref, o_ref, final_norm)


def _ffn(x, g, wg, wu, wd, g_final, final_norm, attn=None, w_attn=None):
    rows = x.shape[0]
    tm = min(ROW_TILE, rows)
    row_spec = pl.BlockSpec((tm, D_MODEL), lambda i: (i, 0))
    ffn_specs = [_const_spec((1, D_MODEL)), _const_spec((D_MODEL, D_FF)), _const_spec((D_MODEL, D_FF)),
                 _const_spec((D_FF, D_MODEL)), _const_spec((1, D_MODEL))]
    if attn is None:
        body, lead_specs, lead = _ffn_body, [row_spec], (x,)
    else:
        body, lead_specs, lead = _proj_ffn_body, [row_spec, row_spec, _const_spec((D_MODEL, D_MODEL))], (x, attn, w_attn)
    return pl.pallas_call(
        functools.partial(body, final_norm=final_norm),
        grid=(rows // tm,),
        in_specs=lead_specs + ffn_specs,
        out_specs=row_spec,
        out_shape=jax.ShapeDtypeStruct((rows, D_MODEL), F32),
        compiler_params=_params(1),
    )(*lead, g, wg, wu, wd, g_final)


def _rope(z, cos, sin_signed):
    half = HEAD_DIM // 2
    lane = lax.broadcasted_iota(jnp.int32, z.shape, 1)
    first_half = (lane % HEAD_DIM) < half
    partner = jnp.where(first_half, pltpu.roll(z, LANES - half, axis=1), pltpu.roll(z, half, axis=1))
    return z * cos + partner * sin_signed


def _project_pieces(x, g_ref, w_ref, cos_ref, sin_ref, store_u, store_gate, q_ref, k_ref, v_ref):
    o_gate, o_q, o_k, o_v = LRU_WIDTH, 2 * LRU_WIDTH, 2 * LRU_WIDTH + Q_WIDTH, 2 * LRU_WIDTH + Q_WIDTH + KV_WIDTH
    xn = []

    def normed():
        if not xn:
            xn.append(_rms(x(), g_ref[...]).astype(BF16))
        return xn[0]

    def rope_into(ref, z):
        for j in range(z.shape[1] // LANES):
            cols = slice(j * LANES, (j + 1) * LANES)
            ref[:, cols] = _rope(z[:, cols], cos_ref[...], sin_ref[...])

    def store_v(z):
        v_ref[...] = z

    return [lambda: rope_into(q_ref, _dot(normed(), w_ref[:, o_q:o_k])),
            lambda: rope_into(k_ref, _dot(normed(), w_ref[:, o_k:o_v])),
            lambda: store_v(_dot(normed(), w_ref[:, o_v:])),
            lambda: store_u(_dot(normed(), w_ref[:, :o_gate])),
            lambda: store_gate(_dot(normed(), w_ref[:, o_gate:o_q]))]


def _proj_body(x_ref, g_ref, w_ref, cos_ref, sin_ref, u_ref, gate_ref, q_ref, k_ref, v_ref):
    def store_u(z):
        u_ref[...] = z

    def store_gate(z):
        gate_ref[...] = z

    for piece in _project_pieces(lambda: x_ref[...], g_ref, w_ref, cos_ref, sin_ref, store_u, store_gate,
                                 q_ref, k_ref, v_ref):
        piece()


def _proj(x, g, w_in, cos, sin):
    rows = x.shape[0]
    tm = min(ROW_TILE, rows, cos.shape[0])
    pos_blocks = cos.shape[0] // tm

    def row_spec(width):
        return pl.BlockSpec((tm, width), lambda i: (i, 0))

    pos_spec = pl.BlockSpec((tm, LANES), lambda i: (i % pos_blocks, 0))
    widths = (LRU_WIDTH, LRU_WIDTH, Q_WIDTH, KV_WIDTH, KV_WIDTH)
    return pl.pallas_call(
        _proj_body,
        grid=(rows // tm,),
        in_specs=[row_spec(D_MODEL), _const_spec((1, D_MODEL)), _const_spec((D_MODEL, IN_COLS)), pos_spec, pos_spec],
        out_specs=[row_spec(w) for w in widths],
        out_shape=[jax.ShapeDtypeStruct((rows, w), F32) for w in widths],
        compiler_params=_params(1),
    )(x, g, w_in, cos, sin)


def _rope_tables(pos):
    half = HEAD_DIM // 2
    inv = ROPE_THETA ** (-np.arange(half, dtype=np.float64) / half)
    ang = pos.astype(np.float64)[:, None] * inv[None, :]
    cos = np.cos(ang)
    sin = np.sin(ang)
    reps = LANES // HEAD_DIM
    return (jnp.asarray(np.tile(np.concatenate([cos, cos], axis=-1), (1, reps)), dtype=F32),
            jnp.asarray(np.tile(np.concatenate([-sin, sin], axis=-1), (1, reps)), dtype=F32))


def _softplus(x):
    return jnp.maximum(x, 0.0) + jnp.log1p(jnp.exp(-jnp.abs(x)))


def _lru_coeffs(conv, wab, bab, lam):
    w = conv.shape[1]
    gates = _dot(conv.astype(BF16), wab) + bab
    r = jax.nn.sigmoid(gates[:, :w])
    gi = jax.nn.sigmoid(gates[:, w:])
    log_a = -LRU_C * r * _softplus(-lam)
    a = jnp.exp(log_a)
    b = jnp.sqrt(-jnp.tanh(log_a) * (a * a + 1.0)) * (gi * conv)
    return a, b


def _segment_scan(a, b, seg):
    step = 1
    while step < seg:
        a, b = _scan_step(a, b, seg, step)
        step *= 2
    return a, b


def _scan_step(a, b, seg, step):
    pos = lax.broadcasted_iota(jnp.int32, a.shape, 0) % seg
    live = pos >= step
    a_prev = pltpu.roll(a, step, axis=0)
    b_prev = pltpu.roll(b, step, axis=0)
    return jnp.where(live, a * a_prev, a), jnp.where(live, a * b_prev + b, b)


def _lru_hidden(conv, h_in, seg, wab, bab, lam):
    a, b = _lru_coeffs(conv, wab, bab, lam)
    a_cum, h_local = _segment_scan(a, b, seg)
    return a_cum * h_in + h_local


def _lane_chunks(width):
    return [slice(c * LANES, (c + 1) * LANES) for c in range(width // LANES)]


def _ffn_pieces(normed, wg_ref, wu_ref, wd_ref, state):
    cols = [slice(f, min(f + FFN_CHUNK, D_FF)) for f in range(0, D_FF, FFN_CHUNK)]

    def gate_up(f):
        return _dot(normed(), wg_ref[:, cols[f]]), _dot(normed(), wu_ref[:, cols[f]])

    def piece(f):
        def run():
            gate, up = state.pop("gate_up") if "gate_up" in state else gate_up(f)
            if f + 1 < len(cols):
                state["gate_up"] = gate_up(f + 1)
            part = _dot((gate * jax.nn.sigmoid(gate) * up).astype(BF16), wd_ref[cols[f], :])
            state["acc"] = part if "acc" not in state else state["acc"] + part
        return run

    return [piece(f) for f in range(len(cols))]


def _interleave(primary, secondary):
    due = [((i + 1) * len(primary)) // (len(secondary) + 1) for i in range(len(secondary))]
    pending = list(zip(due, secondary))
    for i, piece in enumerate(primary):
        while pending and pending[0][0] <= i:
            pending.pop(0)[1]()
        piece()
    for _, piece in pending:
        piece()


def _ffn_proj_lru_body(x_ref, g1_ref, wg_ref, wu_ref, wd_ref, gmix_ref, win_ref, cos_ref, sin_ref,
                       cw_ref, cb_ref, wab_ref, bab_ref, lam_ref, gout_ref,
                       x_out_ref, q_ref, k_ref, v_ref, lru_ref, hlast_ref, utail_ref,
                       ug_ref, ext_ref, h_ref, hs_ref, *, tt, tiles_per_seq):
    pad = SUBLANES
    s = pl.program_id(0)
    lead_slot = s % 2
    lag_slot = 1 - lead_slot

    @pl.when(s == 0)
    def _():
        ug_ref[1] = jnp.zeros((2, tt, LRU_WIDTH), F32)

    @pl.when((s == 0) | (lax.rem(s - 1, tiles_per_seq) == 0))
    def _():
        ext_ref[0:pad, :] = jnp.zeros((pad, LRU_WIDTH), F32)
        h_ref[...] = jnp.zeros_like(h_ref)

    state = {"sumsq": jnp.zeros((tt, 1), F32)}
    scan_steps = [1 << i for i in range(tt.bit_length() - 1)]

    def lru_pieces(c, cols):
        def coeffs():
            ext_ref[pad:pad + tt, cols] = ug_ref[lag_slot, 0, :, cols]
            conv = cb_ref[:, cols]
            for j in range(CONV_WIDTH):
                start = pad - (CONV_WIDTH - 1) + j
                conv = conv + ext_ref[start:start + tt, cols] * cw_ref[j:j + 1, cols]
            ext_ref[0:pad, cols] = ext_ref[tt:tt + pad, cols]
            utail_ref[0, :, cols] = ext_ref[0:pad, cols]
            state[c] = _lru_coeffs(conv, wab_ref[c], bab_ref[c], lam_ref[:, cols])

        def scan(steps):
            def run():
                for step in steps:
                    state[c] = _scan_step(*state[c], tt, step)
            return run

        def finish():
            a_cum, h_local = state.pop(c)
            hs_ref[:, cols] = a_cum * h_ref[:, cols] + h_local
            h_ref[:, cols] = hs_ref[tt - 1:tt, cols]
            hlast_ref[0, :, cols] = h_ref[:, cols]
            y = hs_ref[:, cols] * jax.nn.gelu(ug_ref[lag_slot, 1, :, cols])
            hs_ref[:, cols] = y
            state["sumsq"] = state["sumsq"] + jnp.sum(y * y, axis=-1, keepdims=True)

        half = len(scan_steps) // 2
        return [coeffs, scan(scan_steps[:half]), scan(scan_steps[half:]), finish]

    vector_pieces = [p for c, cols in enumerate(_lane_chunks(LRU_WIDTH)) for p in lru_pieces(c, cols)]

    def ffn_in():
        if "xn" not in state:
            state["xn"] = _rms(x_ref[...], g1_ref[...]).astype(BF16)
        return state["xn"]

    def ffn_out():
        if "x" not in state:
            state["x"] = x_ref[...] + 0.5 * state.pop("acc")
            x_out_ref[...] = state["x"]
        return state["x"]

    def store_u(z):
        ug_ref[lead_slot, 0] = z

    def store_gate(z):
        ug_ref[lead_slot, 1] = z

    matmul_pieces = _ffn_pieces(ffn_in, wg_ref, wu_ref, wd_ref, state)
    matmul_pieces += _project_pieces(ffn_out, gmix_ref, win_ref, cos_ref, sin_ref, store_u, store_gate,
                                     q_ref, k_ref, v_ref)

    assert len(vector_pieces) == len(matmul_pieces)
    for vector_piece, matmul_piece in zip(vector_pieces, matmul_pieces):
        vector_piece()
        matmul_piece()
    lru_ref[...] = hs_ref[...] * lax.rsqrt(state["sumsq"] * (1.0 / LRU_WIDTH) + EPS) * gout_ref[...]


def _ffn_proj_lru(x, seq, g1, wg, wu, wd, g_mix, w_in, cos, sin, conv_w, conv_b, wab, bab, lam, g_out):
    rows = x.shape[0]
    n = rows // seq
    tt = min(LRU_TILE, seq)
    tiles_per_seq = seq // tt
    tiles = rows // tt
    pos_blocks = cos.shape[0] // tt
    lead = lambda s: jnp.minimum(s, tiles - 1)
    lag = lambda s: jnp.maximum(s - 1, 0)

    def lead_spec(width):
        return pl.BlockSpec((tt, width), lambda s: (lead(s), 0))

    pos_spec = pl.BlockSpec((tt, LANES), lambda s: (lead(s) % pos_blocks, 0))
    seq_spec = lambda r: pl.BlockSpec((1, r, LRU_WIDTH), lambda s: (lag(s) // tiles_per_seq, 0, 0))
    widths = (D_MODEL, Q_WIDTH, KV_WIDTH, KV_WIDTH)
    return pl.pallas_call(
        functools.partial(_ffn_proj_lru_body, tt=tt, tiles_per_seq=tiles_per_seq),
        grid=(tiles + 1,),
        in_specs=[lead_spec(D_MODEL), _const_spec((1, D_MODEL)), _const_spec((D_MODEL, D_FF)), _const_spec((D_MODEL, D_FF)),
                  _const_spec((D_FF, D_MODEL)), _const_spec((1, D_MODEL)), _const_spec((D_MODEL, IN_COLS)),
                  pos_spec, pos_spec] + _lru_weight_specs(),
        out_specs=[lead_spec(w) for w in widths]
        + [pl.BlockSpec((tt, LRU_WIDTH), lambda s: (lag(s), 0)), seq_spec(1), seq_spec(SUBLANES)],
        out_shape=[jax.ShapeDtypeStruct((rows, w), F32) for w in widths]
        + [jax.ShapeDtypeStruct((rows, LRU_WIDTH), F32), jax.ShapeDtypeStruct((n, 1, LRU_WIDTH), F32),
           jax.ShapeDtypeStruct((n, SUBLANES, LRU_WIDTH), F32)],
        scratch_shapes=[pltpu.VMEM((2, 2, tt, LRU_WIDTH), F32), pltpu.VMEM((tt + SUBLANES, LRU_WIDTH), F32),
                        pltpu.VMEM((1, LRU_WIDTH), F32), pltpu.VMEM((tt, LRU_WIDTH), F32)],
        compiler_params=_params(1),
    )(x, g1, wg, wu, wd, g_mix, w_in, cos, sin, conv_w, conv_b, wab, bab, lam, g_out)


def _lru_sample_body(u_ref, gate_ref, cpad_ref, h0_ref, cw_ref, cb_ref, wab_ref, bab_ref, lam_ref, gout_ref,
                     o_ref, hs_ref, *, seg):
    rows = u_ref.shape[0]
    pos = lax.broadcasted_iota(jnp.int32, (rows, LANES), 0) % seg
    sumsq = jnp.zeros((rows, 1), F32)
    for c, cols in enumerate(_lane_chunks(LRU_WIDTH)):
        u = u_ref[:, cols]
        cpad = cpad_ref[:, cols]
        conv = cb_ref[:, cols] + u * cw_ref[CONV_WIDTH - 1:CONV_WIDTH, cols]
        for back in range(1, CONV_WIDTH):
            shifted = jnp.where(pos >= back, pltpu.roll(u, back, axis=0),
                                pltpu.roll(cpad, (back - seg) % rows, axis=0))
            conv = conv + shifted * cw_ref[CONV_WIDTH - 1 - back:CONV_WIDTH - back, cols]
        h = _lru_hidden(conv, h0_ref[:, cols], seg, wab_ref[c], bab_ref[c], lam_ref[:, cols])
        hs_ref[:, cols] = h
        y = h * jax.nn.gelu(gate_ref[:, cols])
        o_ref[:, cols] = y
        sumsq = sumsq + jnp.sum(y * y, axis=-1, keepdims=True)
    o_ref[...] = o_ref[...] * lax.rsqrt(sumsq * (1.0 / LRU_WIDTH) + EPS) * gout_ref[...]


def _lru_weight_specs():
    chunks = LRU_WIDTH // LANES
    return [_const_spec((CONV_WIDTH, LRU_WIDTH)), _const_spec((1, LRU_WIDTH)),
            _const_spec((chunks, LANES, 2 * LANES)), _const_spec((chunks, 1, 2 * LANES)),
            _const_spec((1, LRU_WIDTH)), _const_spec((1, LRU_WIDTH))]


def _lru_gate_chunks(w_a, b_a, w_i, b_i):
    chunks = LRU_WIDTH // LANES
    per = LRU_BLOCKS // chunks
    wa = w_a.reshape(chunks, per, *w_a.shape[1:])
    wi = w_i.reshape(chunks, per, *w_i.shape[1:])
    wab = jnp.stack([jnp.concatenate([_block_diag(wa[c]), _block_diag(wi[c])], axis=1) for c in range(chunks)])
    bab = jnp.concatenate([b_a.reshape(chunks, 1, LANES), b_i.reshape(chunks, 1, LANES)], axis=2)
    return wab.astype(BF16), bab


def _block_diag(w):
    nb, bi, bj = w.shape
    eye = jnp.eye(nb, dtype=w.dtype)
    return jnp.einsum('gij,gh->gihj', w, eye).reshape(nb * bi, nb * bj)


def _bdot_nt(a, b):
    return lax.dot_general(a, b, (((2,), (2,)), ((0,), (0,))), preferred_element_type=F32)


def _bdot(a, b):
    return lax.dot_general(a, b, (((2,), (1,)), ((0,), (0,))), preferred_element_type=F32)


def _swa_pieces(q, keys, vals, sink_ref, mask, g_out, store):
    group = ATTN_HEADS // KV_HEADS
    straight = [h for h in range(ATTN_HEADS) if (h % 2) == (h // group)]
    swapped = [h for h in range(ATTN_HEADS) if (h % 2) != (h // group)]
    scale = HEAD_DIM ** -0.5
    out_half = {}

    def low_lanes(shape):
        return lax.broadcasted_iota(jnp.int32, shape, 2) < HEAD_DIM

    def head_group(heads, swap):
        held = {}

        def arrange(z):
            return pltpu.roll(z, HEAD_DIM, axis=2) if swap else z

        def scores():
            qv = q()
            n, r, _ = qv.shape
            low = low_lanes((n, r, LANES))
            zero = jnp.zeros((n, r, LANES), F32)
            qs = jnp.concatenate(
                [jnp.where(low if h % 2 == 0 else ~low, qv[:, :, (h // 2) * LANES:(h // 2 + 1) * LANES], zero)
                 for h in heads], axis=1)
            held["s"] = _bdot_nt(qs.astype(BF16), arrange(keys()).astype(BF16)) * scale

        def probabilities():
            s = held.pop("s")
            r = s.shape[1] // len(heads)
            visible = mask()[None]
            probs = []
            for i, h in enumerate(heads):
                sh = jnp.where(visible, s[:, i * r:(i + 1) * r, :], -jnp.inf)
                sink = sink_ref[h]
                m = jnp.maximum(jnp.max(sh, axis=-1, keepdims=True), sink)
                e = jnp.exp(sh - m)
                denom = jnp.sum(e, axis=-1, keepdims=True) + jnp.exp(sink - m)
                probs.append(e / denom)
            held["p"] = jnp.concatenate(probs, axis=1).astype(BF16)

        def weighted_values():
            p = held.pop("p")
            r = p.shape[1] // len(heads)
            o = _bdot(p, arrange(vals()).astype(BF16))
            for i, h in enumerate(heads):
                out_half[h] = o[:, i * r:(i + 1) * r, :]

        return [scores, probabilities, weighted_values]

    def finish():
        low = low_lanes(out_half[0].shape)
        out = jnp.concatenate([jnp.where(low, out_half[2 * j], out_half[2 * j + 1])
                               for j in range(Q_WIDTH // LANES)], axis=2)
        store(_rms(out, g_out))

    return head_group(straight, False) + head_group(swapped, True) + [finish]


def _band_mask(first_block):
    i = lax.broadcasted_iota(jnp.int32, (WINDOW, 2 * WINDOW), 0)
    j = lax.broadcasted_iota(jnp.int32, (WINDOW, 2 * WINDOW), 1)
    dist = i + WINDOW - j
    return (dist >= 0) & (dist < WINDOW) & (jnp.logical_not(first_block) | (j >= WINDOW))


def _swa_sample_body(sink_ref, q_ref, k_ref, v_ref, bk_ref, bv_ref, gout_ref, o_ref, nk_ref, nv_ref, *, s_len, past):
    def mask():
        qp = past + lax.broadcasted_iota(jnp.int32, (s_len, WINDOW + s_len), 0)
        col = lax.broadcasted_iota(jnp.int32, (s_len, WINDOW + s_len), 1)
        kp = jnp.where(col < WINDOW, past - WINDOW + col, past + col - WINDOW)
        dist = qp - kp
        return (dist >= 0) & (dist < WINDOW) & (kp >= 0)

    keys = jnp.concatenate([bk_ref[...], k_ref[...]], axis=1)
    vals = jnp.concatenate([bv_ref[...], v_ref[...]], axis=1)

    def store(o):
        o_ref[...] = o.reshape(o_ref.shape)

    for piece in _swa_pieces(lambda: q_ref[...], lambda: keys, lambda: vals, sink_ref, mask, gout_ref[...], store):
        piece()
    nk_ref[...] = keys[:, s_len:, :]
    nv_ref[...] = vals[:, s_len:, :]


def _softmax(s):
    e = jnp.exp(s - jnp.max(s, axis=-1, keepdims=True))
    return e / jnp.sum(e, axis=-1, keepdims=True)


def _mix_and_query(x_ref, lru_ref, attn_ref, wout_ref, gx_ref, wcq_ref):
    x = (x_ref[...] + _dot(lru_ref[...].astype(BF16), wout_ref[:LRU_WIDTH, :])
         + _dot(attn_ref[...].astype(BF16), wout_ref[LRU_WIDTH:, :]))
    return x, _dot(_rms(x, gx_ref[...]).astype(BF16), wcq_ref[...])


def _swa_mix_ffn_body(sink_ref, q_ref, k_ref, v_ref, kp_ref, vp_ref, gattn_ref,
                      x_ref, lru_ref, wout_ref, gx_ref, wcq_ref, mk_ref, mv_ref, wco_ref,
                      g2_ref, wg_ref, wu_ref, wd_ref, gf_ref, o_ref, attn_ref,
                      *, tt, tiles, tiles_per_seq, final_norm):
    s = pl.program_id(0)
    refs = (sink_ref, q_ref, k_ref, v_ref, kp_ref, vp_ref, gattn_ref, x_ref, lru_ref, wout_ref, gx_ref, wcq_ref,
            mk_ref, mv_ref, wco_ref, g2_ref, wg_ref, wu_ref, wd_ref, gf_ref, o_ref, attn_ref)
    step = functools.partial(_swa_mix_ffn_step, *refs, tt=tt, tiles_per_seq=tiles_per_seq, final_norm=final_norm)

    @pl.when(s == 0)
    def _():
        step(attend=True, layer=False)

    @pl.when((s > 0) & (s < tiles))
    def _():
        step(attend=True, layer=True)

    @pl.when(s == tiles)
    def _():
        step(attend=False, layer=True)


def _swa_mix_ffn_step(sink_ref, q_ref, k_ref, v_ref, kp_ref, vp_ref, gattn_ref,
                      x_ref, lru_ref, wout_ref, gx_ref, wcq_ref, mk_ref, mv_ref, wco_ref,
                      g2_ref, wg_ref, wu_ref, wd_ref, gf_ref, o_ref, attn_ref,
                      *, tt, tiles_per_seq, final_norm, attend, layer):
    s = pl.program_id(0)
    lead_slot = s % 2
    lag_slot = 1 - lead_slot

    blocks = tt // WINDOW
    first_pos_block = lax.rem(s, tiles_per_seq) * blocks
    attention_pieces = []
    for j in range(blocks):
        rows = slice(j * WINDOW, (j + 1) * WINDOW)

        def band(ref, prev_ref, j=j, rows=rows):
            prev = prev_ref[...] if j == 0 else ref[(j - 1) * WINDOW:j * WINDOW, :]
            return jnp.concatenate([prev, ref[rows, :]], axis=0)[None]

        def store(o, rows=rows):
            attn_ref[lead_slot, rows, :] = o[0]

        attention_pieces += _swa_pieces(
            lambda rows=rows: q_ref[rows, :][None], functools.partial(band, k_ref, kp_ref),
            functools.partial(band, v_ref, vp_ref), sink_ref,
            lambda j=j: _band_mask(first_pos_block + j == 0), gattn_ref[...], store)

    state = {}
    scale = X_HEAD_DIM ** -0.5

    def mix():
        state["x"] = (x_ref[...] + _dot(lru_ref[...].astype(BF16), wout_ref[:LRU_WIDTH, :])
                      + _dot(attn_ref[lag_slot].astype(BF16), wout_ref[LRU_WIDTH:, :]))

    def query():
        state["q"] = _dot(_rms(state["x"], gx_ref[...]).astype(BF16), wcq_ref[...])

    def memory_head(h):
        def run():
            cols = slice(h * X_HEAD_DIM, (h + 1) * X_HEAD_DIM)
            sc = _dot_nt(state["q"][:, cols].astype(BF16), mk_ref[0, :, cols].astype(BF16)) * scale
            state["o", h] = _dot(_softmax(sc).astype(BF16), mv_ref[0, :, cols].astype(BF16))
        return run

    def memory_out():
        o = jnp.concatenate([state.pop(("o", h)) for h in range(X_HEADS)], axis=1)
        state["x"] = state["x"] + _dot(o.astype(BF16), wco_ref[...])
        state["xn"] = _rms(state["x"], g2_ref[...]).astype(BF16)

    def finish():
        y = state["x"] + 0.5 * state["acc"]
        o_ref[...] = _rms(y, gf_ref[...]) if final_norm else y

    layer_pieces = ([mix, query] + [memory_head(h) for h in range(X_HEADS)] + [memory_out]
                    + _ffn_pieces(lambda: state["xn"], wg_ref, wu_ref, wd_ref, state) + [finish])
    _interleave(layer_pieces if layer else [], attention_pieces if attend else [])


def _swa_mix_ffn(q, k, v, sink, g_attn, x, lru_out, seq, w_out, g_x, w_cq, mk, mv, w_co, g2, wg, wu, wd, g_final,
                 final_norm):
    rows = x.shape[0]
    tt = min(LRU_TILE, seq)
    tiles_per_seq = seq // tt
    tiles = rows // tt
    blocks = tt // WINDOW
    lead = lambda s: jnp.minimum(s, tiles - 1)
    lag = lambda s: jnp.maximum(s - 1, 0)

    def lead_spec(width):
        return pl.BlockSpec((tt, width), lambda s: (lead(s), 0))

    def lag_spec(width):
        return pl.BlockSpec((tt, width), lambda s: (lag(s), 0))

    prev_spec = pl.BlockSpec((WINDOW, KV_WIDTH), lambda s: (jnp.maximum(lead(s) * blocks - 1, 0), 0))
    mem_spec = pl.BlockSpec((1, N_MEM, D_MODEL), lambda s: (lag(s) // tiles_per_seq, 0, 0))
    w_spec = _const_spec((D_MODEL, D_MODEL))
    return pl.pallas_call(
        functools.partial(_swa_mix_ffn_body, tt=tt, tiles=tiles, tiles_per_seq=tiles_per_seq, final_norm=final_norm),
        grid=(tiles + 1,),
        in_specs=[pl.BlockSpec(memory_space=pltpu.SMEM), lead_spec(Q_WIDTH), lead_spec(KV_WIDTH), lead_spec(KV_WIDTH),
                  prev_spec, prev_spec, _const_spec((1, Q_WIDTH)),
                  lag_spec(D_MODEL), lag_spec(LRU_WIDTH), w_spec, _const_spec((1, D_MODEL)), w_spec,
                  mem_spec, mem_spec, w_spec,
                  _const_spec((1, D_MODEL)), _const_spec((D_MODEL, D_FF)), _const_spec((D_MODEL, D_FF)),
                  _const_spec((D_FF, D_MODEL)), _const_spec((1, D_MODEL))],
        out_specs=lag_spec(D_MODEL),
        out_shape=jax.ShapeDtypeStruct((rows, D_MODEL), F32),
        scratch_shapes=[pltpu.VMEM((2, tt, Q_WIDTH), F32)],
        compiler_params=_params(1),
    )(sink, q, k, v, k, v, g_attn, x, lru_out, w_out, g_x, w_cq, mk, mv, w_co, g2, wg, wu, wd, g_final)


def _sample_mixer_body(sink_ref, u_ref, gate_ref, cpad_ref, h0_ref, q_ref, k_ref, v_ref, bk_ref, bv_ref, x_ref,
                       cw_ref, cb_ref, wab_ref, bab_ref, lam_ref, glru_ref, gattn_ref, wout_ref, gx_ref, wcq_ref,
                       x_out_ref, xq_ref, hs_ref, nk_ref, nv_ref, lru_ref, attn_ref, *, seg, past):
    _lru_sample_body(u_ref, gate_ref, cpad_ref, h0_ref, cw_ref, cb_ref, wab_ref, bab_ref, lam_ref, glru_ref,
                     lru_ref, hs_ref, seg=seg)
    _swa_sample_body(sink_ref, q_ref, k_ref, v_ref, bk_ref, bv_ref, gattn_ref, attn_ref, nk_ref, nv_ref,
                     s_len=seg, past=past)
    x_out_ref[...], xq_ref[...] = _mix_and_query(x_ref, lru_ref, attn_ref, wout_ref, gx_ref, wcq_ref)


def _sample_mixer(u, gate, conv_pad, h0_rep, q, k, v, buf_k, buf_v, x, sink, lru_w, g_attn, w_out, g_x, w_cq, past):
    n, seg, _ = q.shape
    sb = min(SAMPLE_SEQ_TILE, n)
    rows = sb * seg

    def row_spec(width):
        return pl.BlockSpec((rows, width), lambda i: (i, 0))

    def seq_spec(steps, width):
        return pl.BlockSpec((sb, steps, width), lambda i: (i, 0, 0))

    buf_spec = seq_spec(WINDOW, KV_WIDTH)
    w_spec = _const_spec((D_MODEL, D_MODEL))
    total = n * seg
    return pl.pallas_call(
        functools.partial(_sample_mixer_body, seg=seg, past=past),
        grid=(n // sb,),
        in_specs=[pl.BlockSpec(memory_space=pltpu.SMEM)] + [row_spec(LRU_WIDTH)] * 4
        + [seq_spec(seg, Q_WIDTH), seq_spec(seg, KV_WIDTH), seq_spec(seg, KV_WIDTH), buf_spec, buf_spec,
           row_spec(D_MODEL)] + _lru_weight_specs()
        + [_const_spec((1, Q_WIDTH)), w_spec, _const_spec((1, D_MODEL)), w_spec],
        out_specs=[row_spec(D_MODEL), row_spec(D_MODEL), row_spec(LRU_WIDTH), buf_spec, buf_spec],
        out_shape=[jax.ShapeDtypeStruct((total, D_MODEL), F32), jax.ShapeDtypeStruct((total, D_MODEL), F32),
                   jax.ShapeDtypeStruct((total, LRU_WIDTH), F32),
                   jax.ShapeDtypeStruct((n, WINDOW, KV_WIDTH), F32), jax.ShapeDtypeStruct((n, WINDOW, KV_WIDTH), F32)],
        scratch_shapes=[pltpu.VMEM((rows, LRU_WIDTH), F32), pltpu.VMEM((rows, Q_WIDTH), F32)],
        compiler_params=_params(1),
    )(sink, u, gate, conv_pad, h0_rep, q, k, v, buf_k, buf_v, x, *lru_w, g_attn, w_out, g_x, w_cq)


def _xattn_cache_pieces(q_ref, mk_ref, mv_ref, o_ref):
    sb, steps, _ = q_ref.shape
    blocks = D_MODEL // LANES
    chunks = X_HEAD_DIM // LANES
    width = mk_ref.shape[1]
    block_cls = [(j % chunks) * X_HEADS + j // chunks for j in range(blocks)]
    scale = X_HEAD_DIM ** -0.5
    held = {}

    def lane_class():
        return lax.broadcasted_iota(jnp.int32, (sb, steps, width), 2) % blocks

    def scores():
        qs = jnp.concatenate([q_ref[:, :, j * LANES:(j + 1) * LANES] for j in range(blocks)], axis=1)
        held["s"] = _bdot_nt(qs.astype(BF16), mk_ref[...].astype(BF16)) * scale

    def probabilities():
        s = held.pop("s")
        cls = lane_class()
        part = jnp.zeros((sb, steps, width), F32)
        for j in range(blocks):
            part = part + jnp.where(cls == block_cls[j], s[:, j * steps:(j + 1) * steps, :], 0.0)
        score = part + pltpu.roll(part, width - X_HEADS, axis=2)
        top = jnp.zeros((sb, steps, width), F32)
        for h in range(X_HEADS):
            mine = cls == h
            top = jnp.where(mine, jnp.max(jnp.where(mine, score, -jnp.inf), axis=-1, keepdims=True), top)
        e = jnp.where(cls < X_HEADS, jnp.exp(score - top), 0.0)
        denom = jnp.ones((sb, steps, width), F32)
        for h in range(X_HEADS):
            mine = cls == h
            denom = jnp.where(mine, jnp.sum(jnp.where(mine, e, 0.0), axis=-1, keepdims=True), denom)
        p = e / denom
        p = p + pltpu.roll(p, X_HEADS, axis=2)
        held["p"] = jnp.concatenate([jnp.where(cls == block_cls[j], p, 0.0) for j in range(blocks)],
                                    axis=1).astype(BF16)

    def weighted_values():
        o = _bdot(held.pop("p"), mv_ref[...].astype(BF16))
        for j in range(blocks):
            o_ref[:, :, j * LANES:(j + 1) * LANES] = o[:, j * steps:(j + 1) * steps, :]

    return [scores, probabilities, weighted_values]


def _xattn_cache_body(q_ref, mk_ref, mv_ref, o_ref):
    for piece in _xattn_cache_pieces(q_ref, mk_ref, mv_ref, o_ref):
        piece()


def _interleave_chunks(cache):
    n = cache.shape[0]
    chunks = X_HEAD_DIM // LANES
    c = cache.reshape(n, N_MEM, X_HEADS, chunks, LANES)
    return jnp.transpose(c, (0, 1, 3, 2, 4)).reshape(n, N_MEM * chunks * X_HEADS, LANES)


def _xattn_cache(q, cache_k, cache_v):
    count, s_len, _ = q.shape
    sb = min(XATTN_SEQ_TILE, count)
    q_spec = pl.BlockSpec((sb, s_len, D_MODEL), lambda i: (i, 0, 0))
    rows = cache_k.shape[1]
    mem_spec = pl.BlockSpec((sb, rows, LANES), lambda i: (i, 0, 0))
    return pl.pallas_call(
        _xattn_cache_body,
        grid=(count // sb,),
        in_specs=[q_spec, mem_spec, mem_spec],
        out_specs=q_spec,
        out_shape=jax.ShapeDtypeStruct((count, s_len, D_MODEL), F32),
        compiler_params=_params(1),
    )(q, cache_k, cache_v)


def _mem_kv_body(mem_ref, g_ref, wk_ref, wv_ref, k_ref, v_ref):
    mm = _rms(mem_ref[...], g_ref[...]).astype(BF16)
    k_ref[...] = _dot(mm, wk_ref[...])
    v_ref[...] = _dot(mm, wv_ref[...])


def _mem_kv(mem, g, w_ck, w_cv):
    rows = mem.shape[0]
    tm = min(ROW_TILE, rows)
    row_spec = pl.BlockSpec((tm, D_MODEL), lambda i: (i, 0))
    w_spec = _const_spec((D_MODEL, D_MODEL))
    return pl.pallas_call(
        _mem_kv_body,
        grid=(rows // tm,),
        in_specs=[row_spec, _const_spec((1, D_MODEL)), w_spec, w_spec],
        out_specs=[row_spec, row_spec],
        out_shape=[jax.ShapeDtypeStruct((rows, D_MODEL), F32)] * 2,
        compiler_params=_params(1),
    )(mem, g, w_ck, w_cv)


def kernel(x_prompt, x_sample, mem_prompt, cache_mem_k, cache_mem_v, cache_swa_k, cache_swa_v, state_conv, state_lru_h,
           g_ffn1, w1_gate, w1_up, w1_down, g_mix, w_in, conv_w, conv_b, w_a, b_a, w_i, b_i, lam, sink,
           g_lru_out, g_attn_out, w_out, g_xattn, g_mem, w_cq, w_ck, w_cv, w_co, g_ffn2, w2_gate, w2_up, w2_down,
           g_final):
    nbp, seq, _ = x_prompt.shape
    nbs, dec_seq, _ = x_sample.shape
    depth = g_ffn1.shape[0]
    past = PAST_LEN
    cos_p, sin_p = _rope_tables(np.arange(seq, dtype=np.int32))
    sample_pos_rows = min(ROW_TILE, nbs * dec_seq)
    cos_s, sin_s = _rope_tables(np.tile(past + np.arange(dec_seq, dtype=np.int32), sample_pos_rows // dec_seq))

    xp = x_prompt.reshape(nbp * seq, D_MODEL)
    xs = x_sample.reshape(nbs * dec_seq, D_MODEL)
    g_fin = g_final.reshape(1, D_MODEL)
    row = lambda a: a.reshape(1, -1)
    outs = [[] for _ in range(10)]
    for l in range(depth):
        last = l == depth - 1
        bf = lambda a: a[l].astype(BF16)
        w1g, w1u, w1d, w2g, w2u, w2d = bf(w1_gate), bf(w1_up), bf(w1_down), bf(w2_gate), bf(w2_up), bf(w2_down)
        win, wout, wcq, wck, wcv, wco = bf(w_in), bf(w_out), bf(w_cq), bf(w_ck), bf(w_cv), bf(w_co)
        wab, bab = _lru_gate_chunks(w_a[l], b_a[l], w_i[l], b_i[l])
        lru_w = (conv_w[l], row(conv_b[l]), wab, bab, row(lam[l]), row(g_lru_out[l]))

        xs = _ffn(xs, row(g_ffn1[l]), w1g, w1u, w1d, g_fin, False)
        u_s, gate, q, k, v = _proj(xs, row(g_mix[l]), win, cos_s, sin_s)
        conv_pad = jnp.pad(state_conv[l], ((0, 0), (dec_seq - (CONV_WIDTH - 1), 0), (0, 0)))
        h0_rep = jnp.repeat(state_lru_h[l], dec_seq, axis=0)
        per_seq = lambda a: a.reshape(nbs, dec_seq, a.shape[-1])
        xs, xq, hs, new_k, new_v = _sample_mixer(
            u_s, gate, conv_pad.reshape(nbs * dec_seq, LRU_WIDTH), h0_rep, per_seq(q), per_seq(k), per_seq(v),
            cache_swa_k[l].reshape(nbs, WINDOW, KV_WIDTH), cache_swa_v[l].reshape(nbs, WINDOW, KV_WIDTH), xs,
            sink[l], lru_w, row(g_attn_out[l]), wout, row(g_xattn[l]), wcq, past)
        xq, cache_k, cache_v = per_seq(xq), _interleave_chunks(cache_mem_k[l]), _interleave_chunks(cache_mem_v[l])

        mk_p, mv_p = _mem_kv(mem_prompt.reshape(nbp * N_MEM, D_MODEL), row(g_mem[l]), wck, wcv)
        mk_p = mk_p.reshape(nbp, N_MEM, D_MODEL)
        mv_p = mv_p.reshape(nbp, N_MEM, D_MODEL)
        xp, q, k, v, lru_out, h_last, u_tail = _ffn_proj_lru(xp, seq, row(g_ffn1[l]), w1g, w1u, w1d, row(g_mix[l]), win,
                                                             cos_p, sin_p, *lru_w)
        k3 = k.reshape(nbp, seq, KV_WIDTH)
        v3 = v.reshape(nbp, seq, KV_WIDTH)
        xp = _swa_mix_ffn(q, k, v, sink[l], row(g_attn_out[l]), xp, lru_out, seq, wout, row(g_xattn[l]), wcq,
                          mk_p, mv_p, wco, row(g_ffn2[l]), w2g, w2u, w2d, g_fin, last)
        outs[0].append(mk_p.reshape(nbp, N_MEM, X_HEADS, X_HEAD_DIM))
        outs[1].append(mv_p.reshape(nbp, N_MEM, X_HEADS, X_HEAD_DIM))
        outs[2].append(k3[:, -WINDOW:].reshape(nbp, WINDOW, KV_HEADS, HEAD_DIM))
        outs[3].append(v3[:, -WINDOW:].reshape(nbp, WINDOW, KV_HEADS, HEAD_DIM))
        outs[4].append(u_tail[:, -(CONV_WIDTH - 1):])
        outs[5].append(h_last.reshape(nbp, LRU_WIDTH))

        xo = _xattn_cache(xq, cache_k, cache_v)
        xs = _ffn(xs, row(g_ffn2[l]), w2g, w2u, w2d, g_fin, last, xo.reshape(nbs * dec_seq, D_MODEL), wco)
        outs[6].append(new_k.reshape(nbs, WINDOW, KV_HEADS, HEAD_DIM))
        outs[7].append(new_v.reshape(nbs, WINDOW, KV_HEADS, HEAD_DIM))
        outs[8].append(u_s.reshape(nbs, dec_seq, LRU_WIDTH)[:, -(CONV_WIDTH - 1):])
        outs[9].append(hs.reshape(nbs, dec_seq, LRU_WIDTH)[:, -1])

    return (xp.reshape(nbp, seq, D_MODEL), xs.reshape(nbs, dec_seq, D_MODEL)) + tuple(jnp.stack(o) for o in outs)
```

```python
import functools

import jax
import jax.numpy as jnp
import numpy as np
from jax import lax
from jax.experimental import pallas as pl
from jax.experimental.pallas import tpu as pltpu

F32 = jnp.float32
BF16 = jnp.bfloat16

D_MODEL = 1024
LRU_WIDTH = 512
LRU_BLOCKS = 8
CONV_WIDTH = 4
LRU_C = 8.0
ATTN_HEADS = 8
HEAD_DIM = 64
KV_HEADS = 2
WINDOW = 128
PAST_LEN = 8192
ROPE_THETA = 10000.0
N_MEM = 256
X_HEADS = 4
X_HEAD_DIM = 256
D_FF = 2816
EPS = 1e-6
Q_WIDTH = ATTN_HEADS * HEAD_DIM
KV_WIDTH = KV_HEADS * HEAD_DIM
IN_COLS = 2 * LRU_WIDTH + Q_WIDTH + 2 * KV_WIDTH

LANES = 128
SUBLANES = 8
VMEM_LIMIT = 56 * 1024 * 1024

ROW_TILE = 512
LRU_TILE = 512
FFN_CHUNK = 256
SAMPLE_SEQ_TILE = 32
XATTN_SEQ_TILE = 8


def _params(n_axes):
    return pltpu.CompilerParams(dimension_semantics=("arbitrary",) * n_axes, vmem_limit_bytes=VMEM_LIMIT)


def _const_spec(shape):
    return pl.BlockSpec(shape, lambda *_: (0,) * len(shape), pipeline_mode=pl.Buffered(1))


def _rms(x, g):
    return x * lax.rsqrt(jnp.mean(x * x, axis=-1, keepdims=True) + EPS) * g


def _dot(a, b):
    return jnp.dot(a, b, preferred_element_type=F32)


def _dot_nt(a, b):
    return lax.dot_general(a, b, (((1,), (1,)), ((), ())), preferred_element_type=F32)


def _ffn_step(x, g_ref, wg_ref, wu_ref, wd_ref):
    xn = _rms(x, g_ref[...]).astype(BF16)
    gate = _dot(xn, wg_ref[...])
    up = _dot(xn, wu_ref[...])
    h = (gate * jax.nn.sigmoid(gate) * up).astype(BF16)
    return x + 0.5 * _dot(h, wd_ref[...])


def _ffn_tail(x, g_ref, wg_ref, wu_ref, wd_ref, gf_ref, o_ref, final_norm):
    y = _ffn_step(x, g_ref, wg_ref, wu_ref, wd_ref)
    if final_norm:
        y = _rms(y, gf_ref[...])
    o_ref[...] = y


def _ffn_body(x_ref, g_ref, wg_ref, wu_ref, wd_ref, gf_ref, o_ref, *, final_norm):
    _ffn_tail(x_ref[...], g_ref, wg_ref, wu_ref, wd_ref, gf_ref, o_ref, final_norm)


def _proj_ffn_body(x_ref, a_ref, wa_ref, g_ref, wg_ref, wu_ref, wd_ref, gf_ref, o_ref, *, final_norm):
    x = x_ref[...] + _dot(a_ref[...].astype(BF16), wa_ref[...])
    _ffn_tail(x, g_ref, wg_ref, wu_ref, wd_ref, gf_ref, o_ref, final_norm)


def _ffn(x, g, wg, wu, wd, g_final, final_norm, attn=None, w_attn=None):
    rows = x.shape[0]
    tm = min(ROW_TILE, rows)
    row_spec = pl.BlockSpec((tm, D_MODEL), lambda i: (i, 0))
    ffn_specs = [_const_spec((1, D_MODEL)), _const_spec((D_MODEL, D_FF)), _const_spec((D_MODEL, D_FF)),
                 _const_spec((D_FF, D_MODEL)), _const_spec((1, D_MODEL))]
    if attn is None:
        body, lead_specs, lead = _ffn_body, [row_spec], (x,)
    else:
        body, lead_specs, lead = _proj_ffn_body, [row_spec, row_spec, _const_spec((D_MODEL, D_MODEL))], (x, attn, w_attn)
    return pl.pallas_call(
        functools.partial(body, final_norm=final_norm),
        grid=(rows // tm,),
        in_specs=lead_specs + ffn_specs,
        out_specs=row_spec,
        out_shape=jax.ShapeDtypeStruct((rows, D_MODEL), F32),
        compiler_params=_params(1),
    )(*lead, g, wg, wu, wd, g_final)


def _rope(z, cos, sin_signed):
    half = HEAD_DIM // 2
    lane = lax.broadcasted_iota(jnp.int32, z.shape, 1)
    first_half = (lane % HEAD_DIM) < half
    partner = jnp.where(first_half, pltpu.roll(z, LANES - half, axis=1), pltpu.roll(z, half, axis=1))
    return z * cos + partner * sin_signed


def _project_pieces(x, g_ref, w_ref, cos_ref, sin_ref, store_u, store_gate, q_ref, k_ref, v_ref):
    o_gate, o_q, o_k, o_v = LRU_WIDTH, 2 * LRU_WIDTH, 2 * LRU_WIDTH + Q_WIDTH, 2 * LRU_WIDTH + Q_WIDTH + KV_WIDTH
    xn = []

    def normed():
        if not xn:
            xn.append(_rms(x(), g_ref[...]).astype(BF16))
        return xn[0]

    def rope_into(ref, z):
        for j in range(z.shape[1] // LANES):
            cols = slice(j * LANES, (j + 1) * LANES)
            ref[:, cols] = _rope(z[:, cols], cos_ref[...], sin_ref[...])

    def store_v(z):
        v_ref[...] = z

    return [lambda: rope_into(q_ref, _dot(normed(), w_ref[:, o_q:o_k])),
            lambda: rope_into(k_ref, _dot(normed(), w_ref[:, o_k:o_v])),
            lambda: store_v(_dot(normed(), w_ref[:, o_v:])),
            lambda: store_u(_dot(normed(), w_ref[:, :o_gate])),
            lambda: store_gate(_dot(normed(), w_ref[:, o_gate:o_q]))]


def _proj_body(x_ref, g_ref, w_ref, cos_ref, sin_ref, u_ref, gate_ref, q_ref, k_ref, v_ref):
    def store_u(z):
        u_ref[...] = z

    def store_gate(z):
        gate_ref[...] = z

    for piece in _project_pieces(lambda: x_ref[...], g_ref, w_ref, cos_ref, sin_ref, store_u, store_gate,
                                 q_ref, k_ref, v_ref):
        piece()


def _proj(x, g, w_in, cos, sin):
    rows = x.shape[0]
    tm = min(ROW_TILE, rows, cos.shape[0])
    pos_blocks = cos.shape[0] // tm

    def row_spec(width):
        return pl.BlockSpec((tm, width), lambda i: (i, 0))

    pos_spec = pl.BlockSpec((tm, LANES), lambda i: (i % pos_blocks, 0))
    widths = (LRU_WIDTH, LRU_WIDTH, Q_WIDTH, KV_WIDTH, KV_WIDTH)
    return pl.pallas_call(
        _proj_body,
        grid=(rows // tm,),
        in_specs=[row_spec(D_MODEL), _const_spec((1, D_MODEL)), _const_spec((D_MODEL, IN_COLS)), pos_spec, pos_spec],
        out_specs=[row_spec(w) for w in widths],
        out_shape=[jax.ShapeDtypeStruct((rows, w), F32) for w in widths],
        compiler_params=_params(1),
    )(x, g, w_in, cos, sin)


def _rope_tables(pos):
    half = HEAD_DIM // 2
    inv = ROPE_THETA ** (-np.arange(half, dtype=np.float64) / half)
    ang = pos.astype(np.float64)[:, None] * inv[None, :]
    cos = np.cos(ang)
    sin = np.sin(ang)
    reps = LANES // HEAD_DIM
    return (jnp.asarray(np.tile(np.concatenate([cos, cos], axis=-1), (1, reps)), dtype=F32),
            jnp.asarray(np.tile(np.concatenate([-sin, sin], axis=-1), (1, reps)), dtype=F32))


def _softplus(x):
    return jnp.maximum(x, 0.0) + jnp.log1p(jnp.exp(-jnp.abs(x)))


def _lru_coeffs(conv, wab, bab, lam):
    w = conv.shape[1]
    gates = _dot(conv.astype(BF16), wab) + bab
    r = jax.nn.sigmoid(gates[:, :w])
    gi = jax.nn.sigmoid(gates[:, w:])
    log_a = -LRU_C * r * _softplus(-lam)
    a = jnp.exp(log_a)
    b = jnp.sqrt(-jnp.tanh(log_a) * (a * a + 1.0)) * (gi * conv)
    return a, b


def _segment_scan(a, b, seg):
    step = 1
    while step < seg:
        a, b = _scan_step(a, b, seg, step)
        step *= 2
    return a, b


def _scan_step(a, b, seg, step):
    pos = lax.broadcasted_iota(jnp.int32, a.shape, 0) % seg
    live = pos >= step
    a_prev = pltpu.roll(a, step, axis=0)
    b_prev = pltpu.roll(b, step, axis=0)
    return jnp.where(live, a * a_prev, a), jnp.where(live, a * b_prev + b, b)


def _lru_hidden(conv, h_in, seg, wab, bab, lam):
    a, b = _lru_coeffs(conv, wab, bab, lam)
    a_cum, h_local = _segment_scan(a, b, seg)
    return a_cum * h_in + h_local


def _lane_chunks(width):
    return [slice(c * LANES, (c + 1) * LANES) for c in range(width // LANES)]


def _ffn_pieces(normed, wg_ref, wu_ref, wd_ref, state):
    cols = [slice(f, min(f + FFN_CHUNK, D_FF)) for f in range(0, D_FF, FFN_CHUNK)]

    def gate_up(f):
        return _dot(normed(), wg_ref[:, cols[f]]), _dot(normed(), wu_ref[:, cols[f]])

    def piece(f):
        def run():
            gate, up = state.pop("gate_up") if "gate_up" in state else gate_up(f)
            if f + 1 < len(cols):
                state["gate_up"] = gate_up(f + 1)
            part = _dot((gate * jax.nn.sigmoid(gate) * up).astype(BF16), wd_ref[cols[f], :])
            state["acc"] = part if "acc" not in state else state["acc"] + part
        return run

    return [piece(f) for f in range(len(cols))]


def _interleave(primary, secondary):
    due = [((i + 1) * len(primary)) // (len(secondary) + 1) for i in range(len(secondary))]
    pending = list(zip(due, secondary))
    for i, piece in enumerate(primary):
        while pending and pending[0][0] <= i:
            pending.pop(0)[1]()
        piece()
    for _, piece in pending:
        piece()


def _ffn_proj_lru_body(x_ref, g1_ref, wg_ref, wu_ref, wd_ref, gmix_ref, win_ref, cos_ref, sin_ref,
                       cw_ref, cb_ref, wab_ref, bab_ref, lam_ref, gout_ref,
                       x_out_ref, q_ref, k_ref, v_ref, lru_ref, hlast_ref, utail_ref,
                       ug_ref, ext_ref, h_ref, hs_ref, *, tt, tiles, tiles_per_seq):
    s = pl.program_id(0)
    refs = (x_ref, g1_ref, wg_ref, wu_ref, wd_ref, gmix_ref, win_ref, cos_ref, sin_ref,
            cw_ref, cb_ref, wab_ref, bab_ref, lam_ref, gout_ref,
            x_out_ref, q_ref, k_ref, v_ref, lru_ref, hlast_ref, utail_ref, ug_ref, ext_ref, h_ref, hs_ref)

    @pl.when(s == 0)
    def _():
        ug_ref[1] = jnp.zeros((2, tt, LRU_WIDTH), F32)

    @pl.when((s == 0) | (lax.rem(s - 1, tiles_per_seq) == 0))
    def _():
        ext_ref[0:SUBLANES, :] = jnp.zeros((SUBLANES, LRU_WIDTH), F32)
        h_ref[...] = jnp.zeros_like(h_ref)

    @pl.when(s < tiles)
    def _():
        _ffn_proj_lru_step(*refs, tt=tt, project=True)

    @pl.when(s == tiles)
    def _():
        _ffn_proj_lru_step(*refs, tt=tt, project=False)


def _ffn_proj_lru_step(x_ref, g1_ref, wg_ref, wu_ref, wd_ref, gmix_ref, win_ref, cos_ref, sin_ref,
                       cw_ref, cb_ref, wab_ref, bab_ref, lam_ref, gout_ref,
                       x_out_ref, q_ref, k_ref, v_ref, lru_ref, hlast_ref, utail_ref,
                       ug_ref, ext_ref, h_ref, hs_ref, *, tt, project):
    pad = SUBLANES
    lead_slot = pl.program_id(0) % 2
    lag_slot = 1 - lead_slot

    state = {"sumsq": jnp.zeros((tt, 1), F32)}
    scan_steps = [1 << i for i in range(tt.bit_length() - 1)]

    def lru_pieces(c, cols):
        def coeffs():
            ext_ref[pad:pad + tt, cols] = ug_ref[lag_slot, 0, :, cols]
            conv = cb_ref[:, cols]
            for j in range(CONV_WIDTH):
                start = pad - (CONV_WIDTH - 1) + j
                conv = conv + ext_ref[start:start + tt, cols] * cw_ref[j:j + 1, cols]
            ext_ref[0:pad, cols] = ext_ref[tt:tt + pad, cols]
            utail_ref[0, :, cols] = ext_ref[0:pad, cols]
            state[c] = _lru_coeffs(conv, wab_ref[c], bab_ref[c], lam_ref[:, cols])

        def scan(steps):
            def run():
                for step in steps:
                    state[c] = _scan_step(*state[c], tt, step)
            return run

        def finish():
            a_cum, h_local = state.pop(c)
            hs_ref[:, cols] = a_cum * h_ref[:, cols] + h_local
            h_ref[:, cols] = hs_ref[tt - 1:tt, cols]
            hlast_ref[0, :, cols] = h_ref[:, cols]
            y = hs_ref[:, cols] * jax.nn.gelu(ug_ref[lag_slot, 1, :, cols])
            hs_ref[:, cols] = y
            state["sumsq"] = state["sumsq"] + jnp.sum(y * y, axis=-1, keepdims=True)

        half = len(scan_steps) // 2
        return [coeffs, scan(scan_steps[:half]), scan(scan_steps[half:]), finish]

    vector_pieces = [p for c, cols in enumerate(_lane_chunks(LRU_WIDTH)) for p in lru_pieces(c, cols)]

    def ffn_in():
        if "xn" not in state:
            state["xn"] = _rms(x_ref[...], g1_ref[...]).astype(BF16)
        return state["xn"]

    def ffn_out():
        if "x" not in state:
            state["x"] = x_ref[...] + 0.5 * state.pop("acc")
            x_out_ref[...] = state["x"]
        return state["x"]

    def store_u(z):
        ug_ref[lead_slot, 0] = z

    def store_gate(z):
        ug_ref[lead_slot, 1] = z

    matmul_pieces = _ffn_pieces(ffn_in, wg_ref, wu_ref, wd_ref, state)
    matmul_pieces += _project_pieces(ffn_out, gmix_ref, win_ref, cos_ref, sin_ref, store_u, store_gate,
                                     q_ref, k_ref, v_ref)

    assert len(vector_pieces) == len(matmul_pieces)
    for vector_piece, matmul_piece in zip(vector_pieces, matmul_pieces):
        vector_piece()
        if project:
            matmul_piece()
    lru_ref[...] = hs_ref[...] * lax.rsqrt(state["sumsq"] * (1.0 / LRU_WIDTH) + EPS) * gout_ref[...]


def _ffn_proj_lru(x, seq, g1, wg, wu, wd, g_mix, w_in, cos, sin, conv_w, conv_b, wab, bab, lam, g_out):
    rows = x.shape[0]
    n = rows // seq
    tt = min(LRU_TILE, seq)
    tiles_per_seq = seq // tt
    tiles = rows // tt
    pos_blocks = cos.shape[0] // tt
    lead = lambda s: jnp.minimum(s, tiles - 1)
    lag = lambda s: jnp.maximum(s - 1, 0)

    def lead_spec(width):
        return pl.BlockSpec((tt, width), lambda s: (lead(s), 0))

    pos_spec = pl.BlockSpec((tt, LANES), lambda s: (lead(s) % pos_blocks, 0))
    seq_spec = lambda r: pl.BlockSpec((1, r, LRU_WIDTH), lambda s: (lag(s) // tiles_per_seq, 0, 0))
    widths = (D_MODEL, Q_WIDTH, KV_WIDTH, KV_WIDTH)
    return pl.pallas_call(
        functools.partial(_ffn_proj_lru_body, tt=tt, tiles=tiles, tiles_per_seq=tiles_per_seq),
        grid=(tiles + 1,),
        in_specs=[lead_spec(D_MODEL), _const_spec((1, D_MODEL)), _const_spec((D_MODEL, D_FF)), _const_spec((D_MODEL, D_FF)),
                  _const_spec((D_FF, D_MODEL)), _const_spec((1, D_MODEL)), _const_spec((D_MODEL, IN_COLS)),
                  pos_spec, pos_spec] + _lru_weight_specs(),
        out_specs=[lead_spec(w) for w in widths]
        + [pl.BlockSpec((tt, LRU_WIDTH), lambda s: (lag(s), 0)), seq_spec(1), seq_spec(SUBLANES)],
        out_shape=[jax.ShapeDtypeStruct((rows, w), F32) for w in widths]
        + [jax.ShapeDtypeStruct((rows, LRU_WIDTH), F32), jax.ShapeDtypeStruct((n, 1, LRU_WIDTH), F32),
           jax.ShapeDtypeStruct((n, SUBLANES, LRU_WIDTH), F32)],
        scratch_shapes=[pltpu.VMEM((2, 2, tt, LRU_WIDTH), F32), pltpu.VMEM((tt + SUBLANES, LRU_WIDTH), F32),
                        pltpu.VMEM((1, LRU_WIDTH), F32), pltpu.VMEM((tt, LRU_WIDTH), F32)],
        compiler_params=_params(1),
    )(x, g1, wg, wu, wd, g_mix, w_in, cos, sin, conv_w, conv_b, wab, bab, lam, g_out)


def _lru_sample_body(u_ref, gate_ref, cpad_ref, h0_ref, cw_ref, cb_ref, wab_ref, bab_ref, lam_ref, gout_ref,
                     o_ref, hs_ref, *, seg):
    rows = u_ref.shape[0]
    pos = lax.broadcasted_iota(jnp.int32, (rows, LANES), 0) % seg
    sumsq = jnp.zeros((rows, 1), F32)
    for c, cols in enumerate(_lane_chunks(LRU_WIDTH)):
        u = u_ref[:, cols]
        cpad = cpad_ref[:, cols]
        conv = cb_ref[:, cols] + u * cw_ref[CONV_WIDTH - 1:CONV_WIDTH, cols]
        for back in range(1, CONV_WIDTH):
            shifted = jnp.where(pos >= back, pltpu.roll(u, back, axis=0),
                                pltpu.roll(cpad, (back - seg) % rows, axis=0))
            conv = conv + shifted * cw_ref[CONV_WIDTH - 1 - back:CONV_WIDTH - back, cols]
        h = _lru_hidden(conv, h0_ref[:, cols], seg, wab_ref[c], bab_ref[c], lam_ref[:, cols])
        hs_ref[:, cols] = h
        y = h * jax.nn.gelu(gate_ref[:, cols])
        o_ref[:, cols] = y
        sumsq = sumsq + jnp.sum(y * y, axis=-1, keepdims=True)
    o_ref[...] = o_ref[...] * lax.rsqrt(sumsq * (1.0 / LRU_WIDTH) + EPS) * gout_ref[...]


def _lru_weight_specs():
    chunks = LRU_WIDTH // LANES
    return [_const_spec((CONV_WIDTH, LRU_WIDTH)), _const_spec((1, LRU_WIDTH)),
            _const_spec((chunks, LANES, 2 * LANES)), _const_spec((chunks, 1, 2 * LANES)),
            _const_spec((1, LRU_WIDTH)), _const_spec((1, LRU_WIDTH))]


def _lru_gate_chunks(w_a, b_a, w_i, b_i):
    chunks = LRU_WIDTH // LANES
    per = LRU_BLOCKS // chunks
    wa = w_a.reshape(chunks, per, *w_a.shape[1:])
    wi = w_i.reshape(chunks, per, *w_i.shape[1:])
    wab = jnp.stack([jnp.concatenate([_block_diag(wa[c]), _block_diag(wi[c])], axis=1) for c in range(chunks)])
    bab = jnp.concatenate([b_a.reshape(chunks, 1, LANES), b_i.reshape(chunks, 1, LANES)], axis=2)
    return wab.astype(BF16), bab


def _block_diag(w):
    nb, bi, bj = w.shape
    eye = jnp.eye(nb, dtype=w.dtype)
    return jnp.einsum('gij,gh->gihj', w, eye).reshape(nb * bi, nb * bj)


def _bdot_nt(a, b):
    return lax.dot_general(a, b, (((2,), (2,)), ((0,), (0,))), preferred_element_type=F32)


def _bdot(a, b):
    return lax.dot_general(a, b, (((2,), (1,)), ((0,), (0,))), preferred_element_type=F32)


def _swa_pieces(q, keys, vals, sink_ref, mask, g_out, store):
    group = ATTN_HEADS // KV_HEADS
    straight = [h for h in range(ATTN_HEADS) if (h % 2) == (h // group)]
    swapped = [h for h in range(ATTN_HEADS) if (h % 2) != (h // group)]
    scale = HEAD_DIM ** -0.5
    out_half = {}

    def low_lanes(shape):
        return lax.broadcasted_iota(jnp.int32, shape, 2) < HEAD_DIM

    def head_group(heads, swap):
        held = {}

        def arrange(z):
            return pltpu.roll(z, HEAD_DIM, axis=2) if swap else z

        def scores():
            qv = q()
            n, r, _ = qv.shape
            low = low_lanes((n, r, LANES))
            zero = jnp.zeros((n, r, LANES), F32)
            qs = jnp.concatenate(
                [jnp.where(low if h % 2 == 0 else ~low, qv[:, :, (h // 2) * LANES:(h // 2 + 1) * LANES], zero)
                 for h in heads], axis=1)
            held["s"] = _bdot_nt(qs.astype(BF16), arrange(keys()).astype(BF16)) * scale

        def probabilities():
            s = held.pop("s")
            r = s.shape[1] // len(heads)
            visible = mask()[None]
            probs = []
            for i, h in enumerate(heads):
                sh = jnp.where(visible, s[:, i * r:(i + 1) * r, :], -jnp.inf)
                sink = sink_ref[h]
                m = jnp.maximum(jnp.max(sh, axis=-1, keepdims=True), sink)
                e = jnp.exp(sh - m)
                denom = jnp.sum(e, axis=-1, keepdims=True) + jnp.exp(sink - m)
                probs.append(e / denom)
            held["p"] = jnp.concatenate(probs, axis=1).astype(BF16)

        def weighted_values():
            p = held.pop("p")
            r = p.shape[1] // len(heads)
            o = _bdot(p, arrange(vals()).astype(BF16))
            for i, h in enumerate(heads):
                out_half[h] = o[:, i * r:(i + 1) * r, :]

        return [scores, probabilities, weighted_values]

    def finish():
        low = low_lanes(out_half[0].shape)
        out = jnp.concatenate([jnp.where(low, out_half[2 * j], out_half[2 * j + 1])
                               for j in range(Q_WIDTH // LANES)], axis=2)
        store(_rms(out, g_out))

    return head_group(straight, False) + head_group(swapped, True) + [finish]


def _band_mask(first_block):
    i = lax.broadcasted_iota(jnp.int32, (WINDOW, 2 * WINDOW), 0)
    j = lax.broadcasted_iota(jnp.int32, (WINDOW, 2 * WINDOW), 1)
    dist = i + WINDOW - j
    return (dist >= 0) & (dist < WINDOW) & (jnp.logical_not(first_block) | (j >= WINDOW))


def _swa_sample_body(sink_ref, q_ref, k_ref, v_ref, bk_ref, bv_ref, gout_ref, o_ref, nk_ref, nv_ref, *, s_len, past):
    def mask():
        qp = past + lax.broadcasted_iota(jnp.int32, (s_len, WINDOW + s_len), 0)
        col = lax.broadcasted_iota(jnp.int32, (s_len, WINDOW + s_len), 1)
        kp = jnp.where(col < WINDOW, past - WINDOW + col, past + col - WINDOW)
        dist = qp - kp
        return (dist >= 0) & (dist < WINDOW) & (kp >= 0)

    keys = jnp.concatenate([bk_ref[...], k_ref[...]], axis=1)
    vals = jnp.concatenate([bv_ref[...], v_ref[...]], axis=1)

    def store(o):
        o_ref[...] = o.reshape(o_ref.shape)

    for piece in _swa_pieces(lambda: q_ref[...], lambda: keys, lambda: vals, sink_ref, mask, gout_ref[...], store):
        piece()
    nk_ref[...] = keys[:, s_len:, :]
    nv_ref[...] = vals[:, s_len:, :]


def _softmax(s):
    e = jnp.exp(s - jnp.max(s, axis=-1, keepdims=True))
    return e / jnp.sum(e, axis=-1, keepdims=True)


def _mix_and_query(x_ref, lru_ref, attn_ref, wout_ref, gx_ref, wcq_ref):
    x = (x_ref[...] + _dot(lru_ref[...].astype(BF16), wout_ref[:LRU_WIDTH, :])
         + _dot(attn_ref[...].astype(BF16), wout_ref[LRU_WIDTH:, :]))
    return x, _dot(_rms(x, gx_ref[...]).astype(BF16), wcq_ref[...])


def _swa_mix_ffn_body(sink_ref, q_ref, k_ref, v_ref, kp_ref, vp_ref, gattn_ref,
                      x_ref, lru_ref, wout_ref, gx_ref, wcq_ref, mk_ref, mv_ref, wco_ref,
                      g2_ref, wg_ref, wu_ref, wd_ref, gf_ref, o_ref, attn_ref,
                      *, tt, tiles, tiles_per_seq, final_norm):
    s = pl.program_id(0)
    refs = (sink_ref, q_ref, k_ref, v_ref, kp_ref, vp_ref, gattn_ref, x_ref, lru_ref, wout_ref, gx_ref, wcq_ref,
            mk_ref, mv_ref, wco_ref, g2_ref, wg_ref, wu_ref, wd_ref, gf_ref, o_ref, attn_ref)
    step = functools.partial(_swa_mix_ffn_step, *refs, tt=tt, tiles_per_seq=tiles_per_seq, final_norm=final_norm)

    @pl.when(s == 0)
    def _():
        step(attend=True, layer=False)

    @pl.when((s > 0) & (s < tiles))
    def _():
        step(attend=True, layer=True)

    @pl.when(s == tiles)
    def _():
        step(attend=False, layer=True)


def _swa_mix_ffn_step(sink_ref, q_ref, k_ref, v_ref, kp_ref, vp_ref, gattn_ref,
                      x_ref, lru_ref, wout_ref, gx_ref, wcq_ref, mk_ref, mv_ref, wco_ref,
                      g2_ref, wg_ref, wu_ref, wd_ref, gf_ref, o_ref, attn_ref,
                      *, tt, tiles_per_seq, final_norm, attend, layer):
    s = pl.program_id(0)
    lead_slot = s % 2
    lag_slot = 1 - lead_slot

    blocks = tt // WINDOW
    first_pos_block = lax.rem(s, tiles_per_seq) * blocks
    attention_pieces = []
    for j in range(blocks):
        rows = slice(j * WINDOW, (j + 1) * WINDOW)

        def band(ref, prev_ref, j=j, rows=rows):
            prev = prev_ref[...] if j == 0 else ref[(j - 1) * WINDOW:j * WINDOW, :]
            return jnp.concatenate([prev, ref[rows, :]], axis=0)[None]

        def store(o, rows=rows):
            attn_ref[lead_slot, rows, :] = o[0]

        attention_pieces += _swa_pieces(
            lambda rows=rows: q_ref[rows, :][None], functools.partial(band, k_ref, kp_ref),
            functools.partial(band, v_ref, vp_ref), sink_ref,
            lambda j=j: _band_mask(first_pos_block + j == 0), gattn_ref[...], store)

    state = {}
    scale = X_HEAD_DIM ** -0.5

    def mix():
        state["x"] = (x_ref[...] + _dot(lru_ref[...].astype(BF16), wout_ref[:LRU_WIDTH, :])
                      + _dot(attn_ref[lag_slot].astype(BF16), wout_ref[LRU_WIDTH:, :]))

    def query():
        state["q"] = _dot(_rms(state["x"], gx_ref[...]).astype(BF16), wcq_ref[...])

    def memory_head(h):
        def run():
            cols = slice(h * X_HEAD_DIM, (h + 1) * X_HEAD_DIM)
            sc = _dot_nt(state["q"][:, cols].astype(BF16), mk_ref[0, :, cols].astype(BF16)) * scale
            state["o", h] = _dot(_softmax(sc).astype(BF16), mv_ref[0, :, cols].astype(BF16))
        return run

    def memory_out():
        o = jnp.concatenate([state.pop(("o", h)) for h in range(X_HEADS)], axis=1)
        state["x"] = state["x"] + _dot(o.astype(BF16), wco_ref[...])
        state["xn"] = _rms(state["x"], g2_ref[...]).astype(BF16)

    def finish():
        y = state["x"] + 0.5 * state["acc"]
        o_ref[...] = _rms(y, gf_ref[...]) if final_norm else y

    layer_pieces = ([mix, query] + [memory_head(h) for h in range(X_HEADS)] + [memory_out]
                    + _ffn_pieces(lambda: state["xn"], wg_ref, wu_ref, wd_ref, state) + [finish])
    _interleave(layer_pieces if layer else [], attention_pieces if attend else [])


def _swa_mix_ffn(q, k, v, sink, g_attn, x, lru_out, seq, w_out, g_x, w_cq, mk, mv, w_co, g2, wg, wu, wd, g_final,
                 final_norm):
    rows = x.shape[0]
    tt = min(LRU_TILE, seq)
    tiles_per_seq = seq // tt
    tiles = rows // tt
    blocks = tt // WINDOW
    lead = lambda s: jnp.minimum(s, tiles - 1)
    lag = lambda s: jnp.maximum(s - 1, 0)

    def lead_spec(width):
        return pl.BlockSpec((tt, width), lambda s: (lead(s), 0))

    def lag_spec(width):
        return pl.BlockSpec((tt, width), lambda s: (lag(s), 0))

    prev_spec = pl.BlockSpec((WINDOW, KV_WIDTH), lambda s: (jnp.maximum(lead(s) * blocks - 1, 0), 0))
    mem_spec = pl.BlockSpec((1, N_MEM, D_MODEL), lambda s: (lag(s) // tiles_per_seq, 0, 0))
    w_spec = _const_spec((D_MODEL, D_MODEL))
    return pl.pallas_call(
        functools.partial(_swa_mix_ffn_body, tt=tt, tiles=tiles, tiles_per_seq=tiles_per_seq, final_norm=final_norm),
        grid=(tiles + 1,),
        in_specs=[pl.BlockSpec(memory_space=pltpu.SMEM), lead_spec(Q_WIDTH), lead_spec(KV_WIDTH), lead_spec(KV_WIDTH),
                  prev_spec, prev_spec, _const_spec((1, Q_WIDTH)),
                  lag_spec(D_MODEL), lag_spec(LRU_WIDTH), w_spec, _const_spec((1, D_MODEL)), w_spec,
                  mem_spec, mem_spec, w_spec,
                  _const_spec((1, D_MODEL)), _const_spec((D_MODEL, D_FF)), _const_spec((D_MODEL, D_FF)),
                  _const_spec((D_FF, D_MODEL)), _const_spec((1, D_MODEL))],
        out_specs=lag_spec(D_MODEL),
        out_shape=jax.ShapeDtypeStruct((rows, D_MODEL), F32),
        scratch_shapes=[pltpu.VMEM((2, tt, Q_WIDTH), F32)],
        compiler_params=_params(1),
    )(sink, q, k, v, k, v, g_attn, x, lru_out, w_out, g_x, w_cq, mk, mv, w_co, g2, wg, wu, wd, g_final)


def _sample_mixer_body(sink_ref, u_ref, gate_ref, cpad_ref, h0_ref, q_ref, k_ref, v_ref, bk_ref, bv_ref, x_ref,
                       cw_ref, cb_ref, wab_ref, bab_ref, lam_ref, glru_ref, gattn_ref, wout_ref, gx_ref, wcq_ref,
                       x_out_ref, xq_ref, hs_ref, nk_ref, nv_ref, lru_ref, attn_ref, *, seg, past):
    _lru_sample_body(u_ref, gate_ref, cpad_ref, h0_ref, cw_ref, cb_ref, wab_ref, bab_ref, lam_ref, glru_ref,
                     lru_ref, hs_ref, seg=seg)
    _swa_sample_body(sink_ref, q_ref, k_ref, v_ref, bk_ref, bv_ref, gattn_ref, attn_ref, nk_ref, nv_ref,
                     s_len=seg, past=past)
    x_out_ref[...], xq_ref[...] = _mix_and_query(x_ref, lru_ref, attn_ref, wout_ref, gx_ref, wcq_ref)


def _sample_mixer(u, gate, conv_pad, h0_rep, q, k, v, buf_k, buf_v, x, sink, lru_w, g_attn, w_out, g_x, w_cq, past):
    n, seg, _ = q.shape
    sb = min(SAMPLE_SEQ_TILE, n)
    rows = sb * seg

    def row_spec(width):
        return pl.BlockSpec((rows, width), lambda i: (i, 0))

    def seq_spec(steps, width):
        return pl.BlockSpec((sb, steps, width), lambda i: (i, 0, 0))

    buf_spec = seq_spec(WINDOW, KV_WIDTH)
    w_spec = _const_spec((D_MODEL, D_MODEL))
    total = n * seg
    return pl.pallas_call(
        functools.partial(_sample_mixer_body, seg=seg, past=past),
        grid=(n // sb,),
        in_specs=[pl.BlockSpec(memory_space=pltpu.SMEM)] + [row_spec(LRU_WIDTH)] * 4
        + [seq_spec(seg, Q_WIDTH), seq_spec(seg, KV_WIDTH), seq_spec(seg, KV_WIDTH), buf_spec, buf_spec,
           row_spec(D_MODEL)] + _lru_weight_specs()
        + [_const_spec((1, Q_WIDTH)), w_spec, _const_spec((1, D_MODEL)), w_spec],
        out_specs=[row_spec(D_MODEL), row_spec(D_MODEL), row_spec(LRU_WIDTH), buf_spec, buf_spec],
        out_shape=[jax.ShapeDtypeStruct((total, D_MODEL), F32), jax.ShapeDtypeStruct((total, D_MODEL), F32),
                   jax.ShapeDtypeStruct((total, LRU_WIDTH), F32),
                   jax.ShapeDtypeStruct((n, WINDOW, KV_WIDTH), F32), jax.ShapeDtypeStruct((n, WINDOW, KV_WIDTH), F32)],
        scratch_shapes=[pltpu.VMEM((rows, LRU_WIDTH), F32), pltpu.VMEM((rows, Q_WIDTH), F32)],
        compiler_params=_params(1),
    )(sink, u, gate, conv_pad, h0_rep, q, k, v, buf_k, buf_v, x, *lru_w, g_attn, w_out, g_x, w_cq)


def _xattn_cache_pieces(q_ref, mk_ref, mv_ref, o_ref):
    sb, steps, _ = q_ref.shape
    blocks = D_MODEL // LANES
    chunks = X_HEAD_DIM // LANES
    width = mk_ref.shape[1]
    block_cls = [(j % chunks) * X_HEADS + j // chunks for j in range(blocks)]
    scale = X_HEAD_DIM ** -0.5
    held = {}

    def lane_class():
        return lax.broadcasted_iota(jnp.int32, (sb, steps, width), 2) % blocks

    def scores():
        qs = jnp.concatenate([q_ref[:, :, j * LANES:(j + 1) * LANES] for j in range(blocks)], axis=1)
        held["s"] = _bdot_nt(qs.astype(BF16), mk_ref[...].astype(BF16)) * scale

    def probabilities():
        s = held.pop("s")
        cls = lane_class()
        part = jnp.zeros((sb, steps, width), F32)
        for j in range(blocks):
            part = part + jnp.where(cls == block_cls[j], s[:, j * steps:(j + 1) * steps, :], 0.0)
        score = part + pltpu.roll(part, width - X_HEADS, axis=2)
        top = jnp.zeros((sb, steps, width), F32)
        for h in range(X_HEADS):
            mine = cls == h
            top = jnp.where(mine, jnp.max(jnp.where(mine, score, -jnp.inf), axis=-1, keepdims=True), top)
        e = jnp.where(cls < X_HEADS, jnp.exp(score - top), 0.0)
        denom = jnp.ones((sb, steps, width), F32)
        for h in range(X_HEADS):
            mine = cls == h
            denom = jnp.where(mine, jnp.sum(jnp.where(mine, e, 0.0), axis=-1, keepdims=True), denom)
        p = e / denom
        p = p + pltpu.roll(p, X_HEADS, axis=2)
        held["p"] = jnp.concatenate([jnp.where(cls == block_cls[j], p, 0.0) for j in range(blocks)],
                                    axis=1).astype(BF16)

    def weighted_values():
        o = _bdot(held.pop("p"), mv_ref[...].astype(BF16))
        for j in range(blocks):
            o_ref[:, :, j * LANES:(j + 1) * LANES] = o[:, j * steps:(j + 1) * steps, :]

    return [scores, probabilities, weighted_values]


def _xattn_cache_body(q_ref, mk_ref, mv_ref, o_ref):
    for piece in _xattn_cache_pieces(q_ref, mk_ref, mv_ref, o_ref):
        piece()


def _interleave_chunks(cache):
    n = cache.shape[0]
    chunks = X_HEAD_DIM // LANES
    c = cache.reshape(n, N_MEM, X_HEADS, chunks, LANES)
    return jnp.transpose(c, (0, 1, 3, 2, 4)).reshape(n, N_MEM * chunks * X_HEADS, LANES)


def _xattn_cache(q, cache_k, cache_v):
    count, s_len, _ = q.shape
    sb = min(XATTN_SEQ_TILE, count)
    q_spec = pl.BlockSpec((sb, s_len, D_MODEL), lambda i: (i, 0, 0))
    rows = cache_k.shape[1]
    mem_spec = pl.BlockSpec((sb, rows, LANES), lambda i: (i, 0, 0))
    return pl.pallas_call(
        _xattn_cache_body,
        grid=(count // sb,),
        in_specs=[q_spec, mem_spec, mem_spec],
        out_specs=q_spec,
        out_shape=jax.ShapeDtypeStruct((count, s_len, D_MODEL), F32),
        compiler_params=_params(1),
    )(q, cache_k, cache_v)


def _mem_kv_body(mem_ref, g_ref, wk_ref, wv_ref, k_ref, v_ref):
    mm = _rms(mem_ref[...], g_ref[...]).astype(BF16)
    k_ref[...] = _dot(mm, wk_ref[...].astype(BF16))
    v_ref[...] = _dot(mm, wv_ref[...].astype(BF16))


def _mem_kv(mem, g, w_ck, w_cv):
    rows = mem.shape[0]
    tm = min(ROW_TILE, rows)
    row_spec = pl.BlockSpec((tm, D_MODEL), lambda i: (i, 0))
    w_spec = _const_spec((D_MODEL, D_MODEL))
    return pl.pallas_call(
        _mem_kv_body,
        grid=(rows // tm,),
        in_specs=[row_spec, _const_spec((1, D_MODEL)), w_spec, w_spec],
        out_specs=[row_spec, row_spec],
        out_shape=[jax.ShapeDtypeStruct((rows, D_MODEL), F32)] * 2,
        compiler_params=_params(1),
    )(mem, g, w_ck, w_cv)


def kernel(x_prompt, x_sample, mem_prompt, cache_mem_k, cache_mem_v, cache_swa_k, cache_swa_v, state_conv, state_lru_h,
           g_ffn1, w1_gate, w1_up, w1_down, g_mix, w_in, conv_w, conv_b, w_a, b_a, w_i, b_i, lam, sink,
           g_lru_out, g_attn_out, w_out, g_xattn, g_mem, w_cq, w_ck, w_cv, w_co, g_ffn2, w2_gate, w2_up, w2_down,
           g_final):
    nbp, seq, _ = x_prompt.shape
    nbs, dec_seq, _ = x_sample.shape
    depth = g_ffn1.shape[0]
    past = PAST_LEN
    cos_p, sin_p = _rope_tables(np.arange(seq, dtype=np.int32))
    sample_pos_rows = min(ROW_TILE, nbs * dec_seq)
    cos_s, sin_s = _rope_tables(np.tile(past + np.arange(dec_seq, dtype=np.int32), sample_pos_rows // dec_seq))

    xp = x_prompt.reshape(nbp * seq, D_MODEL)
    xs = x_sample.reshape(nbs * dec_seq, D_MODEL)
    g_fin = g_final.reshape(1, D_MODEL)
    row = lambda a: a.reshape(1, -1)
    outs = [[] for _ in range(10)]
    for l in range(depth):
        last = l == depth - 1
        bf = lambda a: a[l].astype(BF16)
        w1g, w1u, w1d, w2g, w2u, w2d = bf(w1_gate), bf(w1_up), bf(w1_down), bf(w2_gate), bf(w2_up), bf(w2_down)
        win, wout, wcq, wco = bf(w_in), bf(w_out), bf(w_cq), bf(w_co)
        wab, bab = _lru_gate_chunks(w_a[l], b_a[l], w_i[l], b_i[l])
        lru_w = (conv_w[l], row(conv_b[l]), wab, bab, row(lam[l]), row(g_lru_out[l]))

        xs = _ffn(xs, row(g_ffn1[l]), w1g, w1u, w1d, g_fin, False)
        u_s, gate, q, k, v = _proj(xs, row(g_mix[l]), win, cos_s, sin_s)
        conv_pad = jnp.pad(state_conv[l], ((0, 0), (dec_seq - (CONV_WIDTH - 1), 0), (0, 0)))
        h0_rep = jnp.repeat(state_lru_h[l], dec_seq, axis=0)
        per_seq = lambda a: a.reshape(nbs, dec_seq, a.shape[-1])
        xs, xq, hs, new_k, new_v = _sample_mixer(
            u_s, gate, conv_pad.reshape(nbs * dec_seq, LRU_WIDTH), h0_rep, per_seq(q), per_seq(k), per_seq(v),
            cache_swa_k[l].reshape(nbs, WINDOW, KV_WIDTH), cache_swa_v[l].reshape(nbs, WINDOW, KV_WIDTH), xs,
            sink[l], lru_w, row(g_attn_out[l]), wout, row(g_xattn[l]), wcq, past)
        xq, cache_k, cache_v = per_seq(xq), _interleave_chunks(cache_mem_k[l]), _interleave_chunks(cache_mem_v[l])

        mk_p, mv_p = _mem_kv(mem_prompt.reshape(nbp * N_MEM, D_MODEL), row(g_mem[l]), w_ck[l], w_cv[l])
        mk_p = mk_p.reshape(nbp, N_MEM, D_MODEL)
        mv_p = mv_p.reshape(nbp, N_MEM, D_MODEL)
        xp, q, k, v, lru_out, h_last, u_tail = _ffn_proj_lru(xp, seq, row(g_ffn1[l]), w1g, w1u, w1d, row(g_mix[l]), win,
                                                             cos_p, sin_p, *lru_w)
        k3 = k.reshape(nbp, seq, KV_WIDTH)
        v3 = v.reshape(nbp, seq, KV_WIDTH)
        xp = _swa_mix_ffn(q, k, v, sink[l], row(g_attn_out[l]), xp, lru_out, seq, wout, row(g_xattn[l]), wcq,
                          mk_p, mv_p, wco, row(g_ffn2[l]), w2g, w2u, w2d, g_fin, last)
        outs[0].append(mk_p.reshape(nbp, N_MEM, X_HEADS, X_HEAD_DIM))
        outs[1].append(mv_p.reshape(nbp, N_MEM, X_HEADS, X_HEAD_DIM))
        outs[2].append(k3[:, -WINDOW:].reshape(nbp, WINDOW, KV_HEADS, HEAD_DIM))
        outs[3].append(v3[:, -WINDOW:].reshape(nbp, WINDOW, KV_HEADS, HEAD_DIM))
        outs[4].append(u_tail[:, -(CONV_WIDTH - 1):])
        outs[5].append(h_last.reshape(nbp, LRU_WIDTH))

        xo = _xattn_cache(xq, cache_k, cache_v)
        xs = _ffn(xs, row(g_ffn2[l]), w2g, w2u, w2d, g_fin, last, xo.reshape(nbs * dec_seq, D_MODEL), wco)
        outs[6].append(new_k.reshape(nbs, WINDOW, KV_HEADS, HEAD_DIM))
        outs[7].append(new_v.reshape(nbs, WINDOW, KV_HEADS, HEAD_DIM))
        outs[8].append(u_s.reshape(nbs, dec_seq, LRU_WIDTH)[:, -(CONV_WIDTH - 1):])
        outs[9].append(hs.reshape(nbs, dec_seq, LRU_WIDTH)[:, -1])

    return (xp.reshape(nbp, seq, D_MODEL), xs.reshape(nbs, dec_seq, D_MODEL)) + tuple(jnp.stack(o) for o in outs)
```

```python
import functools

import jax
import jax.numpy as jnp
import numpy as np
from jax import lax
from jax.experimental import pallas as pl
from jax.experimental.pallas import tpu as pltpu

F32 = jnp.float32
BF16 = jnp.bfloat16

D_MODEL = 1024
LRU_WIDTH = 512
LRU_BLOCKS = 8
CONV_WIDTH = 4
LRU_C = 8.0
ATTN_HEADS = 8
HEAD_DIM = 64
KV_HEADS = 2
WINDOW = 128
PAST_LEN = 8192
ROPE_THETA = 10000.0
N_MEM = 256
X_HEADS = 4
X_HEAD_DIM = 256
D_FF = 2816
EPS = 1e-6
Q_WIDTH = ATTN_HEADS * HEAD_DIM
KV_WIDTH = KV_HEADS * HEAD_DIM
IN_COLS = 2 * LRU_WIDTH + Q_WIDTH + 2 * KV_WIDTH

LANES = 128
SUBLANES = 8
VMEM_LIMIT = 56 * 1024 * 1024

ROW_TILE = 512
LRU_TILE = 512
FFN_CHUNK = 256
SAMPLE_SEQ_TILE = 32
XATTN_SEQ_TILE = 8


def _params(n_axes):
    return pltpu.CompilerParams(dimension_semantics=("arbitrary",) * n_axes, vmem_limit_bytes=VMEM_LIMIT)


def _const_spec(shape):
    return pl.BlockSpec(shape, lambda *_: (0,) * len(shape), pipeline_mode=pl.Buffered(1))


def _rms(x, g):
    return x * lax.rsqrt(jnp.mean(x * x, axis=-1, keepdims=True) + EPS) * g


def _dot(a, b):
    return jnp.dot(a, b, preferred_element_type=F32)


def _dot_nt(a, b):
    return lax.dot_general(a, b, (((1,), (1,)), ((), ())), preferred_element_type=F32)


def _ffn_step(x, g_ref, wg_ref, wu_ref, wd_ref):
    xn = _rms(x, g_ref[...]).astype(BF16)
    gate = _dot(xn, wg_ref[...])
    up = _dot(xn, wu_ref[...])
    h = (gate * jax.nn.sigmoid(gate) * up).astype(BF16)
    return x + 0.5 * _dot(h, wd_ref[...])


def _ffn_tail(x, g_ref, wg_ref, wu_ref, wd_ref, gf_ref, o_ref, final_norm):
    y = _ffn_step(x, g_ref, wg_ref, wu_ref, wd_ref)
    if final_norm:
        y = _rms(y, gf_ref[...])
    o_ref[...] = y


def _ffn_body(x_ref, g_ref, wg_ref, wu_ref, wd_ref, gf_ref, o_ref, *, final_norm):
    _ffn_tail(x_ref[...], g_ref, wg_ref, wu_ref, wd_ref, gf_ref, o_ref, final_norm)


def _proj_ffn_body(x_ref, a_ref, wa_ref, g_ref, wg_ref, wu_ref, wd_ref, gf_ref, o_ref, *, final_norm):
    x = x_ref[...] + _dot(a_ref[...].astype(BF16), wa_ref[...])
    _ffn_tail(x, g_ref, wg_ref, wu_ref, wd_ref, gf_ref, o_ref, final_norm)


def _ffn(x, g, wg, wu, wd, g_final, final_norm, attn=None, w_attn=None):
    rows = x.shape[0]
    tm = min(ROW_TILE, rows)
    row_spec = pl.BlockSpec((tm, D_MODEL), lambda i: (i, 0))
    ffn_specs = [_const_spec((1, D_MODEL)), _const_spec((D_MODEL, D_FF)), _const_spec((D_MODEL, D_FF)),
                 _const_spec((D_FF, D_MODEL)), _const_spec((1, D_MODEL))]
    if attn is None:
        body, lead_specs, lead = _ffn_body, [row_spec], (x,)
    else:
        body, lead_specs, lead = _proj_ffn_body, [row_spec, row_spec, _const_spec((D_MODEL, D_MODEL))], (x, attn, w_attn)
    return pl.pallas_call(
        functools.partial(body, final_norm=final_norm),
        grid=(rows // tm,),
        in_specs=lead_specs + ffn_specs,
        out_specs=row_spec,
        out_shape=jax.ShapeDtypeStruct((rows, D_MODEL), F32),
        compiler_params=_params(1),
    )(*lead, g, wg, wu, wd, g_final)


def _rope(z, cos, sin_signed):
    half = HEAD_DIM // 2
    lane = lax.broadcasted_iota(jnp.int32, z.shape, 1)
    first_half = (lane % HEAD_DIM) < half
    partner = jnp.where(first_half, pltpu.roll(z, LANES - half, axis=1), pltpu.roll(z, half, axis=1))
    return z * cos + partner * sin_signed


def _project_pieces(x, g_ref, w_ref, cos_ref, sin_ref, store_u, store_gate, q_ref, k_ref, v_ref):
    o_gate, o_q, o_k, o_v = LRU_WIDTH, 2 * LRU_WIDTH, 2 * LRU_WIDTH + Q_WIDTH, 2 * LRU_WIDTH + Q_WIDTH + KV_WIDTH
    xn = []

    def normed():
        if not xn:
            xn.append(_rms(x(), g_ref[...]).astype(BF16))
        return xn[0]

    def rope_into(ref, z):
        for j in range(z.shape[1] // LANES):
            cols = slice(j * LANES, (j + 1) * LANES)
            ref[:, cols] = _rope(z[:, cols], cos_ref[...], sin_ref[...])

    def store_v(z):
        v_ref[...] = z

    return [lambda: rope_into(q_ref, _dot(normed(), w_ref[:, o_q:o_k])),
            lambda: rope_into(k_ref, _dot(normed(), w_ref[:, o_k:o_v])),
            lambda: store_v(_dot(normed(), w_ref[:, o_v:])),
            lambda: store_u(_dot(normed(), w_ref[:, :o_gate])),
            lambda: store_gate(_dot(normed(), w_ref[:, o_gate:o_q]))]


def _proj_body(x_ref, g_ref, w_ref, cos_ref, sin_ref, u_ref, gate_ref, q_ref, k_ref, v_ref):
    def store_u(z):
        u_ref[...] = z

    def store_gate(z):
        gate_ref[...] = z

    for piece in _project_pieces(lambda: x_ref[...], g_ref, w_ref, cos_ref, sin_ref, store_u, store_gate,
                                 q_ref, k_ref, v_ref):
        piece()


def _proj(x, g, w_in, cos, sin):
    rows = x.shape[0]
    tm = min(ROW_TILE, rows, cos.shape[0])
    pos_blocks = cos.shape[0] // tm

    def row_spec(width):
        return pl.BlockSpec((tm, width), lambda i: (i, 0))

    pos_spec = pl.BlockSpec((tm, LANES), lambda i: (i % pos_blocks, 0))
    widths = (LRU_WIDTH, LRU_WIDTH, Q_WIDTH, KV_WIDTH, KV_WIDTH)
    return pl.pallas_call(
        _proj_body,
        grid=(rows // tm,),
        in_specs=[row_spec(D_MODEL), _const_spec((1, D_MODEL)), _const_spec((D_MODEL, IN_COLS)), pos_spec, pos_spec],
        out_specs=[row_spec(w) for w in widths],
        out_shape=[jax.ShapeDtypeStruct((rows, w), F32) for w in widths],
        compiler_params=_params(1),
    )(x, g, w_in, cos, sin)


def _rope_tables(pos):
    half = HEAD_DIM // 2
    inv = ROPE_THETA ** (-np.arange(half, dtype=np.float64) / half)
    ang = pos.astype(np.float64)[:, None] * inv[None, :]
    cos = np.cos(ang)
    sin = np.sin(ang)
    reps = LANES // HEAD_DIM
    return (jnp.asarray(np.tile(np.concatenate([cos, cos], axis=-1), (1, reps)), dtype=F32),
            jnp.asarray(np.tile(np.concatenate([-sin, sin], axis=-1), (1, reps)), dtype=F32))


def _softplus(x):
    return jnp.maximum(x, 0.0) + jnp.log1p(jnp.exp(-jnp.abs(x)))


def _lru_coeffs(conv, wab, bab, lam):
    w = conv.shape[1]
    gates = _dot(conv.astype(BF16), wab) + bab
    r = jax.nn.sigmoid(gates[:, :w])
    gi = jax.nn.sigmoid(gates[:, w:])
    log_a = -LRU_C * r * _softplus(-lam)
    a = jnp.exp(log_a)
    b = jnp.sqrt(-jnp.tanh(log_a) * (a * a + 1.0)) * (gi * conv)
    return a, b


def _segment_scan(a, b, seg):
    step = 1
    while step < seg:
        a, b = _scan_step(a, b, seg, step)
        step *= 2
    return a, b


def _scan_step(a, b, seg, step):
    pos = lax.broadcasted_iota(jnp.int32, a.shape, 0) % seg
    live = pos >= step
    a_prev = pltpu.roll(a, step, axis=0)
    b_prev = pltpu.roll(b, step, axis=0)
    return jnp.where(live, a * a_prev, a), jnp.where(live, a * b_prev + b, b)


def _lru_hidden(conv, h_in, seg, wab, bab, lam):
    a, b = _lru_coeffs(conv, wab, bab, lam)
    a_cum, h_local = _segment_scan(a, b, seg)
    return a_cum * h_in + h_local


def _lane_chunks(width):
    return [slice(c * LANES, (c + 1) * LANES) for c in range(width // LANES)]


def _ffn_pieces(normed, wg_ref, wu_ref, wd_ref, state):
    cols = [slice(f, min(f + FFN_CHUNK, D_FF)) for f in range(0, D_FF, FFN_CHUNK)]

    def gate_up(f):
        return _dot(normed(), wg_ref[:, cols[f]]), _dot(normed(), wu_ref[:, cols[f]])

    def piece(f):
        def run():
            gate, up = state.pop("gate_up") if "gate_up" in state else gate_up(f)
            if f + 1 < len(cols):
                state["gate_up"] = gate_up(f + 1)
            part = _dot((gate * jax.nn.sigmoid(gate) * up).astype(BF16), wd_ref[cols[f], :])
            state["acc"] = part if "acc" not in state else state["acc"] + part
        return run

    return [piece(f) for f in range(len(cols))]


def _interleave(primary, secondary):
    due = [((i + 1) * len(primary)) // (len(secondary) + 1) for i in range(len(secondary))]
    pending = list(zip(due, secondary))
    for i, piece in enumerate(primary):
        while pending and pending[0][0] <= i:
            pending.pop(0)[1]()
        piece()
    for _, piece in pending:
        piece()


def _ffn_proj_lru_body(x_ref, g1_ref, wg_ref, wu_ref, wd_ref, gmix_ref, win_ref, cos_ref, sin_ref,
                       cw_ref, cb_ref, wab_ref, bab_ref, lam_ref, gout_ref,
                       x_out_ref, q_ref, k_ref, v_ref, lru_ref, hlast_ref, utail_ref,
                       ug_ref, ext_ref, h_ref, hs_ref, *, tt, tiles, tiles_per_seq):
    s = pl.program_id(0)
    refs = (x_ref, g1_ref, wg_ref, wu_ref, wd_ref, gmix_ref, win_ref, cos_ref, sin_ref,
            cw_ref, cb_ref, wab_ref, bab_ref, lam_ref, gout_ref,
            x_out_ref, q_ref, k_ref, v_ref, lru_ref, hlast_ref, utail_ref, ug_ref, ext_ref, h_ref, hs_ref)

    @pl.when(s == 0)
    def _():
        ug_ref[1] = jnp.zeros((2, tt, LRU_WIDTH), F32)

    @pl.when((s == 0) | (lax.rem(s - 1, tiles_per_seq) == 0))
    def _():
        ext_ref[0:SUBLANES, :] = jnp.zeros((SUBLANES, LRU_WIDTH), F32)
        h_ref[...] = jnp.zeros_like(h_ref)

    @pl.when(s < tiles)
    def _():
        _ffn_proj_lru_step(*refs, tt=tt, project=True)

    @pl.when(s == tiles)
    def _():
        _ffn_proj_lru_step(*refs, tt=tt, project=False)


def _ffn_proj_lru_step(x_ref, g1_ref, wg_ref, wu_ref, wd_ref, gmix_ref, win_ref, cos_ref, sin_ref,
                       cw_ref, cb_ref, wab_ref, bab_ref, lam_ref, gout_ref,
                       x_out_ref, q_ref, k_ref, v_ref, lru_ref, hlast_ref, utail_ref,
                       ug_ref, ext_ref, h_ref, hs_ref, *, tt, project):
    pad = SUBLANES
    lead_slot = pl.program_id(0) % 2
    lag_slot = 1 - lead_slot

    state = {"sumsq": jnp.zeros((tt, 1), F32)}
    scan_steps = [1 << i for i in range(tt.bit_length() - 1)]

    def lru_pieces(c, cols):
        def coeffs():
            ext_ref[pad:pad + tt, cols] = ug_ref[lag_slot, 0, :, cols]
            conv = cb_ref[:, cols]
            for j in range(CONV_WIDTH):
                start = pad - (CONV_WIDTH - 1) + j
                conv = conv + ext_ref[start:start + tt, cols] * cw_ref[j:j + 1, cols]
            ext_ref[0:pad, cols] = ext_ref[tt:tt + pad, cols]
            utail_ref[0, :, cols] = ext_ref[0:pad, cols]
            state[c] = _lru_coeffs(conv, wab_ref[c], bab_ref[c], lam_ref[:, cols])

        def scan(steps):
            def run():
                for step in steps:
                    state[c] = _scan_step(*state[c], tt, step)
            return run

        def finish():
            a_cum, h_local = state.pop(c)
            hs_ref[:, cols] = a_cum * h_ref[:, cols] + h_local
            h_ref[:, cols] = hs_ref[tt - 1:tt, cols]
            hlast_ref[0, :, cols] = h_ref[:, cols]
            y = hs_ref[:, cols] * jax.nn.gelu(ug_ref[lag_slot, 1, :, cols])
            hs_ref[:, cols] = y
            state["sumsq"] = state["sumsq"] + jnp.sum(y * y, axis=-1, keepdims=True)

        half = len(scan_steps) // 2
        return [coeffs, scan(scan_steps[:half]), scan(scan_steps[half:]), finish]

    vector_pieces = [p for c, cols in enumerate(_lane_chunks(LRU_WIDTH)) for p in lru_pieces(c, cols)]

    def ffn_in():
        if "xn" not in state:
            state["xn"] = _rms(x_ref[...], g1_ref[...]).astype(BF16)
        return state["xn"]

    def ffn_out():
        if "x" not in state:
            state["x"] = x_ref[...] + 0.5 * state.pop("acc")
            x_out_ref[...] = state["x"]
        return state["x"]

    def store_u(z):
        ug_ref[lead_slot, 0] = z

    def store_gate(z):
        ug_ref[lead_slot, 1] = z

    matmul_pieces = _ffn_pieces(ffn_in, wg_ref, wu_ref, wd_ref, state)
    matmul_pieces += _project_pieces(ffn_out, gmix_ref, win_ref, cos_ref, sin_ref, store_u, store_gate,
                                     q_ref, k_ref, v_ref)

    assert len(vector_pieces) == len(matmul_pieces)
    for vector_piece, matmul_piece in zip(vector_pieces, matmul_pieces):
        vector_piece()
        if project:
            matmul_piece()
    lru_ref[...] = hs_ref[...] * lax.rsqrt(state["sumsq"] * (1.0 / LRU_WIDTH) + EPS) * gout_ref[...]


def _ffn_proj_lru(x, seq, g1, wg, wu, wd, g_mix, w_in, cos, sin, conv_w, conv_b, wab, bab, lam, g_out):
    rows = x.shape[0]
    n = rows // seq
    tt = min(LRU_TILE, seq)
    tiles_per_seq = seq // tt
    tiles = rows // tt
    pos_blocks = cos.shape[0] // tt
    lead = lambda s: jnp.minimum(s, tiles - 1)
    lag = lambda s: jnp.maximum(s - 1, 0)

    def lead_spec(width):
        return pl.BlockSpec((tt, width), lambda s: (lead(s), 0))

    pos_spec = pl.BlockSpec((tt, LANES), lambda s: (lead(s) % pos_blocks, 0))
    seq_spec = lambda r: pl.BlockSpec((1, r, LRU_WIDTH), lambda s: (lag(s) // tiles_per_seq, 0, 0))
    widths = (D_MODEL, Q_WIDTH, KV_WIDTH, KV_WIDTH)
    return pl.pallas_call(
        functools.partial(_ffn_proj_lru_body, tt=tt, tiles=tiles, tiles_per_seq=tiles_per_seq),
        grid=(tiles + 1,),
        in_specs=[lead_spec(D_MODEL), _const_spec((1, D_MODEL)), _const_spec((D_MODEL, D_FF)), _const_spec((D_MODEL, D_FF)),
                  _const_spec((D_FF, D_MODEL)), _const_spec((1, D_MODEL)), _const_spec((D_MODEL, IN_COLS)),
                  pos_spec, pos_spec] + _lru_weight_specs(),
        out_specs=[lead_spec(w) for w in widths]
        + [pl.BlockSpec((tt, LRU_WIDTH), lambda s: (lag(s), 0)), seq_spec(1), seq_spec(SUBLANES)],
        out_shape=[jax.ShapeDtypeStruct((rows, w), F32) for w in widths]
        + [jax.ShapeDtypeStruct((rows, LRU_WIDTH), F32), jax.ShapeDtypeStruct((n, 1, LRU_WIDTH), F32),
           jax.ShapeDtypeStruct((n, SUBLANES, LRU_WIDTH), F32)],
        scratch_shapes=[pltpu.VMEM((2, 2, tt, LRU_WIDTH), F32), pltpu.VMEM((tt + SUBLANES, LRU_WIDTH), F32),
                        pltpu.VMEM((1, LRU_WIDTH), F32), pltpu.VMEM((tt, LRU_WIDTH), F32)],
        compiler_params=_params(1),
    )(x, g1, wg, wu, wd, g_mix, w_in, cos, sin, conv_w, conv_b, wab, bab, lam, g_out)


def _lru_sample_body(u_ref, gate_ref, cpad_ref, h0_ref, cw_ref, cb_ref, wab_ref, bab_ref, lam_ref, gout_ref,
                     o_ref, hs_ref, *, seg):
    rows = u_ref.shape[0]
    pos = lax.broadcasted_iota(jnp.int32, (rows, LANES), 0) % seg
    sumsq = jnp.zeros((rows, 1), F32)
    for c, cols in enumerate(_lane_chunks(LRU_WIDTH)):
        u = u_ref[:, cols]
        cpad = cpad_ref[:, cols]
        conv = cb_ref[:, cols] + u * cw_ref[CONV_WIDTH - 1:CONV_WIDTH, cols]
        for back in range(1, CONV_WIDTH):
            shifted = jnp.where(pos >= back, pltpu.roll(u, back, axis=0),
                                pltpu.roll(cpad, (back - seg) % rows, axis=0))
            conv = conv + shifted * cw_ref[CONV_WIDTH - 1 - back:CONV_WIDTH - back, cols]
        h = _lru_hidden(conv, h0_ref[:, cols], seg, wab_ref[c], bab_ref[c], lam_ref[:, cols])
        hs_ref[:, cols] = h
        y = h * jax.nn.gelu(gate_ref[:, cols])
        o_ref[:, cols] = y
        sumsq = sumsq + jnp.sum(y * y, axis=-1, keepdims=True)
    o_ref[...] = o_ref[...] * lax.rsqrt(sumsq * (1.0 / LRU_WIDTH) + EPS) * gout_ref[...]


def _lru_weight_specs():
    chunks = LRU_WIDTH // LANES
    return [_const_spec((CONV_WIDTH, LRU_WIDTH)), _const_spec((1, LRU_WIDTH)),
            _const_spec((chunks, LANES, 2 * LANES)), _const_spec((chunks, 1, 2 * LANES)),
            _const_spec((1, LRU_WIDTH)), _const_spec((1, LRU_WIDTH))]


def _lru_gate_chunks(w_a, b_a, w_i, b_i):
    chunks = LRU_WIDTH // LANES
    per = LRU_BLOCKS // chunks
    wa = w_a.reshape(chunks, per, *w_a.shape[1:])
    wi = w_i.reshape(chunks, per, *w_i.shape[1:])
    wab = jnp.stack([jnp.concatenate([_block_diag(wa[c]), _block_diag(wi[c])], axis=1) for c in range(chunks)])
    bab = jnp.concatenate([b_a.reshape(chunks, 1, LANES), b_i.reshape(chunks, 1, LANES)], axis=2)
    return wab.astype(BF16), bab


def _block_diag(w):
    nb, bi, bj = w.shape
    eye = jnp.eye(nb, dtype=w.dtype)
    return jnp.einsum('gij,gh->gihj', w, eye).reshape(nb * bi, nb * bj)


def _bdot_nt(a, b):
    return lax.dot_general(a, b, (((2,), (2,)), ((0,), (0,))), preferred_element_type=F32)


def _bdot(a, b):
    return lax.dot_general(a, b, (((2,), (1,)), ((0,), (0,))), preferred_element_type=F32)


def _swa_pieces(q, score, weigh, sink_ref, mask, g_out, store):
    group = ATTN_HEADS // KV_HEADS
    straight = [h for h in range(ATTN_HEADS) if (h % 2) == (h // group)]
    swapped = [h for h in range(ATTN_HEADS) if (h % 2) != (h // group)]
    scale = HEAD_DIM ** -0.5
    out_half = {}

    def low_lanes(shape):
        return lax.broadcasted_iota(jnp.int32, shape, 2) < HEAD_DIM

    def head_group(heads, swap):
        held = {}

        def scores():
            qv = q()
            n, r, _ = qv.shape
            low = low_lanes((n, r, LANES))
            zero = jnp.zeros((n, r, LANES), F32)
            qs = jnp.concatenate(
                [jnp.where(low if h % 2 == 0 else ~low, qv[:, :, (h // 2) * LANES:(h // 2 + 1) * LANES], zero)
                 for h in heads], axis=1)
            held["s"] = score(qs.astype(BF16), swap) * scale

        def probabilities():
            s = held.pop("s")
            r = s.shape[1] // len(heads)
            visible = mask()[None]
            probs = []
            for i, h in enumerate(heads):
                sh = jnp.where(visible, s[:, i * r:(i + 1) * r, :], -jnp.inf)
                sink = sink_ref[h]
                m = jnp.maximum(jnp.max(sh, axis=-1, keepdims=True), sink)
                e = jnp.exp(sh - m)
                denom = jnp.sum(e, axis=-1, keepdims=True) + jnp.exp(sink - m)
                probs.append(e / denom)
            held["p"] = jnp.concatenate(probs, axis=1).astype(BF16)

        def weighted_values():
            p = held.pop("p")
            r = p.shape[1] // len(heads)
            o = weigh(p, swap)
            for i, h in enumerate(heads):
                out_half[h] = o[:, i * r:(i + 1) * r, :]

        return [scores, probabilities, weighted_values]

    def finish():
        low = low_lanes(out_half[0].shape)
        out = jnp.concatenate([jnp.where(low, out_half[2 * j], out_half[2 * j + 1])
                               for j in range(Q_WIDTH // LANES)], axis=2)
        store(_rms(out, g_out))

    return head_group(straight, False) + head_group(swapped, True) + [finish]


def _band_mask(first_block):
    i = lax.broadcasted_iota(jnp.int32, (WINDOW, 2 * WINDOW), 0)
    j = lax.broadcasted_iota(jnp.int32, (WINDOW, 2 * WINDOW), 1)
    dist = i + WINDOW - j
    return (dist >= 0) & (dist < WINDOW) & (jnp.logical_not(first_block) | (j >= WINDOW))


def _swa_sample_body(sink_ref, q_ref, k_ref, v_ref, bk_ref, bv_ref, gout_ref, o_ref, nk_ref, nv_ref, *, s_len, past):
    def mask():
        qp = past + lax.broadcasted_iota(jnp.int32, (s_len, WINDOW + s_len), 0)
        col = lax.broadcasted_iota(jnp.int32, (s_len, WINDOW + s_len), 1)
        kp = jnp.where(col < WINDOW, past - WINDOW + col, past + col - WINDOW)
        dist = qp - kp
        return (dist >= 0) & (dist < WINDOW) & (kp >= 0)

    def halves(z, axis, swap):
        return pltpu.roll(z, HEAD_DIM, axis=axis) if swap else z

    def score(qs, swap):
        carried = _bdot(qs, halves(bk_ref[...], 1, swap).astype(BF16))
        fresh = _bdot_nt(qs, halves(k_ref[...], 2, swap).astype(BF16))
        return jnp.concatenate([carried, fresh], axis=2)

    def weigh(p, swap):
        return (_bdot_nt(p[:, :, :WINDOW], halves(bv_ref[...], 1, swap).astype(BF16))
                + _bdot(p[:, :, WINDOW:], halves(v_ref[...], 2, swap).astype(BF16)))

    def store(o):
        o_ref[...] = o.reshape(o_ref.shape)

    for piece in _swa_pieces(lambda: q_ref[...], score, weigh, sink_ref, mask, gout_ref[...], store):
        piece()

    def slide(window_ref, fresh_ref, out_ref):
        n = window_ref.shape[0]
        tail = jnp.concatenate([jnp.zeros((n, WINDOW - s_len, KV_WIDTH), F32), fresh_ref[...]], axis=1)
        lane = lax.broadcasted_iota(jnp.int32, (n, KV_WIDTH, WINDOW), 2)
        out_ref[...] = jnp.where(lane >= WINDOW - s_len, jnp.swapaxes(tail, 1, 2),
                                 pltpu.roll(window_ref[...], WINDOW - s_len, axis=2))

    slide(bk_ref, k_ref, nk_ref)
    slide(bv_ref, v_ref, nv_ref)


def _softmax(s):
    e = jnp.exp(s - jnp.max(s, axis=-1, keepdims=True))
    return e / jnp.sum(e, axis=-1, keepdims=True)


def _mix_and_query(x_ref, lru_ref, attn_ref, wout_ref, gx_ref, wcq_ref):
    x = (x_ref[...] + _dot(lru_ref[...].astype(BF16), wout_ref[:LRU_WIDTH, :])
         + _dot(attn_ref[...].astype(BF16), wout_ref[LRU_WIDTH:, :]))
    return x, _dot(_rms(x, gx_ref[...]).astype(BF16), wcq_ref[...])


def _swa_mix_ffn_body(sink_ref, q_ref, k_ref, v_ref, kp_ref, vp_ref, gattn_ref,
                      x_ref, lru_ref, wout_ref, gx_ref, wcq_ref, mk_ref, mv_ref, wco_ref,
                      g2_ref, wg_ref, wu_ref, wd_ref, gf_ref, o_ref, attn_ref,
                      *, tt, tiles, tiles_per_seq, final_norm):
    s = pl.program_id(0)
    refs = (sink_ref, q_ref, k_ref, v_ref, kp_ref, vp_ref, gattn_ref, x_ref, lru_ref, wout_ref, gx_ref, wcq_ref,
            mk_ref, mv_ref, wco_ref, g2_ref, wg_ref, wu_ref, wd_ref, gf_ref, o_ref, attn_ref)
    step = functools.partial(_swa_mix_ffn_step, *refs, tt=tt, tiles_per_seq=tiles_per_seq, final_norm=final_norm)

    @pl.when(s == 0)
    def _():
        step(attend=True, layer=False)

    @pl.when((s > 0) & (s < tiles))
    def _():
        step(attend=True, layer=True)

    @pl.when(s == tiles)
    def _():
        step(attend=False, layer=True)


def _swa_mix_ffn_step(sink_ref, q_ref, k_ref, v_ref, kp_ref, vp_ref, gattn_ref,
                      x_ref, lru_ref, wout_ref, gx_ref, wcq_ref, mk_ref, mv_ref, wco_ref,
                      g2_ref, wg_ref, wu_ref, wd_ref, gf_ref, o_ref, attn_ref,
                      *, tt, tiles_per_seq, final_norm, attend, layer):
    s = pl.program_id(0)
    lead_slot = s % 2
    lag_slot = 1 - lead_slot

    blocks = tt // WINDOW
    first_pos_block = lax.rem(s, tiles_per_seq) * blocks
    attention_pieces = []
    for j in range(blocks):
        rows = slice(j * WINDOW, (j + 1) * WINDOW)

        def band(ref, prev_ref, swap, j=j, rows=rows):
            prev = prev_ref[...] if j == 0 else ref[(j - 1) * WINDOW:j * WINDOW, :]
            both = jnp.concatenate([prev, ref[rows, :]], axis=0)[None]
            return (pltpu.roll(both, HEAD_DIM, axis=2) if swap else both).astype(BF16)

        def score(qs, swap, band=band):
            return _bdot_nt(qs, band(k_ref, kp_ref, swap))

        def weigh(p, swap, band=band):
            return _bdot(p, band(v_ref, vp_ref, swap))

        def store(o, rows=rows):
            attn_ref[lead_slot, rows, :] = o[0]

        attention_pieces += _swa_pieces(lambda rows=rows: q_ref[rows, :][None], score, weigh, sink_ref,
                                        lambda j=j: _band_mask(first_pos_block + j == 0), gattn_ref[...], store)

    state = {}
    scale = X_HEAD_DIM ** -0.5

    def mix():
        state["x"] = (x_ref[...] + _dot(lru_ref[...].astype(BF16), wout_ref[:LRU_WIDTH, :])
                      + _dot(attn_ref[lag_slot].astype(BF16), wout_ref[LRU_WIDTH:, :]))

    def query():
        state["q"] = _dot(_rms(state["x"], gx_ref[...]).astype(BF16), wcq_ref[...])

    def memory_head(h):
        def run():
            cols = slice(h * X_HEAD_DIM, (h + 1) * X_HEAD_DIM)
            sc = _dot_nt(state["q"][:, cols].astype(BF16), mk_ref[0, :, cols].astype(BF16)) * scale
            state["o", h] = _dot(_softmax(sc).astype(BF16), mv_ref[0, :, cols].astype(BF16))
        return run

    def memory_out():
        o = jnp.concatenate([state.pop(("o", h)) for h in range(X_HEADS)], axis=1)
        state["x"] = state["x"] + _dot(o.astype(BF16), wco_ref[...])
        state["xn"] = _rms(state["x"], g2_ref[...]).astype(BF16)

    def finish():
        y = state["x"] + 0.5 * state["acc"]
        o_ref[...] = _rms(y, gf_ref[...]) if final_norm else y

    layer_pieces = ([mix, query] + [memory_head(h) for h in range(X_HEADS)] + [memory_out]
                    + _ffn_pieces(lambda: state["xn"], wg_ref, wu_ref, wd_ref, state) + [finish])
    _interleave(layer_pieces if layer else [], attention_pieces if attend else [])


def _swa_mix_ffn(q, k, v, sink, g_attn, x, lru_out, seq, w_out, g_x, w_cq, mk, mv, w_co, g2, wg, wu, wd, g_final,
                 final_norm):
    rows = x.shape[0]
    tt = min(LRU_TILE, seq)
    tiles_per_seq = seq // tt
    tiles = rows // tt
    blocks = tt // WINDOW
    lead = lambda s: jnp.minimum(s, tiles - 1)
    lag = lambda s: jnp.maximum(s - 1, 0)

    def lead_spec(width):
        return pl.BlockSpec((tt, width), lambda s: (lead(s), 0))

    def lag_spec(width):
        return pl.BlockSpec((tt, width), lambda s: (lag(s), 0))

    prev_spec = pl.BlockSpec((WINDOW, KV_WIDTH), lambda s: (jnp.maximum(lead(s) * blocks - 1, 0), 0))
    mem_spec = pl.BlockSpec((1, N_MEM, D_MODEL), lambda s: (lag(s) // tiles_per_seq, 0, 0))
    w_spec = _const_spec((D_MODEL, D_MODEL))
    return pl.pallas_call(
        functools.partial(_swa_mix_ffn_body, tt=tt, tiles=tiles, tiles_per_seq=tiles_per_seq, final_norm=final_norm),
        grid=(tiles + 1,),
        in_specs=[pl.BlockSpec(memory_space=pltpu.SMEM), lead_spec(Q_WIDTH), lead_spec(KV_WIDTH), lead_spec(KV_WIDTH),
                  prev_spec, prev_spec, _const_spec((1, Q_WIDTH)),
                  lag_spec(D_MODEL), lag_spec(LRU_WIDTH), w_spec, _const_spec((1, D_MODEL)), w_spec,
                  mem_spec, mem_spec, w_spec,
                  _const_spec((1, D_MODEL)), _const_spec((D_MODEL, D_FF)), _const_spec((D_MODEL, D_FF)),
                  _const_spec((D_FF, D_MODEL)), _const_spec((1, D_MODEL))],
        out_specs=lag_spec(D_MODEL),
        out_shape=jax.ShapeDtypeStruct((rows, D_MODEL), F32),
        scratch_shapes=[pltpu.VMEM((2, tt, Q_WIDTH), F32)],
        compiler_params=_params(1),
    )(sink, q, k, v, k, v, g_attn, x, lru_out, w_out, g_x, w_cq, mk, mv, w_co, g2, wg, wu, wd, g_final)


def _sample_mixer_body(sink_ref, u_ref, gate_ref, cpad_ref, h0_ref, q_ref, k_ref, v_ref, bk_ref, bv_ref, x_ref,
                       cw_ref, cb_ref, wab_ref, bab_ref, lam_ref, glru_ref, gattn_ref, wout_ref, gx_ref, wcq_ref,
                       x_out_ref, xq_ref, hs_ref, nk_ref, nv_ref, lru_ref, attn_ref, *, seg, past):
    _lru_sample_body(u_ref, gate_ref, cpad_ref, h0_ref, cw_ref, cb_ref, wab_ref, bab_ref, lam_ref, glru_ref,
                     lru_ref, hs_ref, seg=seg)
    _swa_sample_body(sink_ref, q_ref, k_ref, v_ref, bk_ref, bv_ref, gattn_ref, attn_ref, nk_ref, nv_ref,
                     s_len=seg, past=past)
    x_out_ref[...], xq_ref[...] = _mix_and_query(x_ref, lru_ref, attn_ref, wout_ref, gx_ref, wcq_ref)


def _channel_major(window):
    n = window.shape[0]
    return jnp.transpose(window, (0, 2, 3, 1)).reshape(n, KV_WIDTH, WINDOW)


def _position_major(window):
    n = window.shape[0]
    return jnp.transpose(window.reshape(n, KV_HEADS, HEAD_DIM, WINDOW), (0, 3, 1, 2))


def _sample_mixer(u, gate, conv_pad, h0_rep, q, k, v, buf_k, buf_v, x, sink, lru_w, g_attn, w_out, g_x, w_cq, past):
    n, seg, _ = q.shape
    sb = min(SAMPLE_SEQ_TILE, n)
    rows = sb * seg

    def row_spec(width):
        return pl.BlockSpec((rows, width), lambda i: (i, 0))

    def seq_spec(steps, width):
        return pl.BlockSpec((sb, steps, width), lambda i: (i, 0, 0))

    buf_spec = seq_spec(WINDOW, KV_WIDTH)
    w_spec = _const_spec((D_MODEL, D_MODEL))
    total = n * seg
    return pl.pallas_call(
        functools.partial(_sample_mixer_body, seg=seg, past=past),
        grid=(n // sb,),
        in_specs=[pl.BlockSpec(memory_space=pltpu.SMEM)] + [row_spec(LRU_WIDTH)] * 4
        + [seq_spec(seg, Q_WIDTH), seq_spec(seg, KV_WIDTH), seq_spec(seg, KV_WIDTH), buf_spec, buf_spec,
           row_spec(D_MODEL)] + _lru_weight_specs()
        + [_const_spec((1, Q_WIDTH)), w_spec, _const_spec((1, D_MODEL)), w_spec],
        out_specs=[row_spec(D_MODEL), row_spec(D_MODEL), row_spec(LRU_WIDTH), buf_spec, buf_spec],
        out_shape=[jax.ShapeDtypeStruct((total, D_MODEL), F32), jax.ShapeDtypeStruct((total, D_MODEL), F32),
                   jax.ShapeDtypeStruct((total, LRU_WIDTH), F32),
                   jax.ShapeDtypeStruct((n, WINDOW, KV_WIDTH), F32), jax.ShapeDtypeStruct((n, WINDOW, KV_WIDTH), F32)],
        scratch_shapes=[pltpu.VMEM((rows, LRU_WIDTH), F32), pltpu.VMEM((rows, Q_WIDTH), F32)],
        compiler_params=_params(1),
    )(sink, u, gate, conv_pad, h0_rep, q, k, v, buf_k, buf_v, x, *lru_w, g_attn, w_out, g_x, w_cq)


def _xattn_cache_pieces(q_ref, mk_ref, mv_ref, o_ref):
    sb, steps, _ = q_ref.shape
    blocks = D_MODEL // LANES
    chunks = X_HEAD_DIM // LANES
    width = mk_ref.shape[1]
    block_cls = [(j % chunks) * X_HEADS + j // chunks for j in range(blocks)]
    scale = X_HEAD_DIM ** -0.5
    held = {}

    def lane_class():
        return lax.broadcasted_iota(jnp.int32, (sb, steps, width), 2) % blocks

    def scores():
        qs = jnp.concatenate([q_ref[:, :, j * LANES:(j + 1) * LANES] for j in range(blocks)], axis=1)
        held["s"] = _bdot_nt(qs.astype(BF16), mk_ref[...].astype(BF16)) * scale

    def probabilities():
        s = held.pop("s")
        cls = lane_class()
        part = jnp.zeros((sb, steps, width), F32)
        for j in range(blocks):
            part = part + jnp.where(cls == block_cls[j], s[:, j * steps:(j + 1) * steps, :], 0.0)
        score = part + pltpu.roll(part, width - X_HEADS, axis=2)
        top = jnp.zeros((sb, steps, width), F32)
        for h in range(X_HEADS):
            mine = cls == h
            top = jnp.where(mine, jnp.max(jnp.where(mine, score, -jnp.inf), axis=-1, keepdims=True), top)
        e = jnp.where(cls < X_HEADS, jnp.exp(score - top), 0.0)
        denom = jnp.ones((sb, steps, width), F32)
        for h in range(X_HEADS):
            mine = cls == h
            denom = jnp.where(mine, jnp.sum(jnp.where(mine, e, 0.0), axis=-1, keepdims=True), denom)
        p = e / denom
        p = p + pltpu.roll(p, X_HEADS, axis=2)
        held["p"] = jnp.concatenate([jnp.where(cls == block_cls[j], p, 0.0) for j in range(blocks)],
                                    axis=1).astype(BF16)

    def weighted_values():
        o = _bdot(held.pop("p"), mv_ref[...].astype(BF16))
        for j in range(blocks):
            o_ref[:, :, j * LANES:(j + 1) * LANES] = o[:, j * steps:(j + 1) * steps, :]

    return [scores, probabilities, weighted_values]


def _xattn_cache_body(q_ref, mk_ref, mv_ref, o_ref):
    for piece in _xattn_cache_pieces(q_ref, mk_ref, mv_ref, o_ref):
        piece()


def _interleave_chunks(cache):
    n = cache.shape[0]
    chunks = X_HEAD_DIM // LANES
    c = cache.reshape(n, N_MEM, X_HEADS, chunks, LANES)
    return jnp.transpose(c, (0, 1, 3, 2, 4)).reshape(n, N_MEM * chunks * X_HEADS, LANES)


def _xattn_cache(q, cache_k, cache_v):
    count, s_len, _ = q.shape
    sb = min(XATTN_SEQ_TILE, count)
    q_spec = pl.BlockSpec((sb, s_len, D_MODEL), lambda i: (i, 0, 0))
    rows = cache_k.shape[1]
    mem_spec = pl.BlockSpec((sb, rows, LANES), lambda i: (i, 0, 0))
    return pl.pallas_call(
        _xattn_cache_body,
        grid=(count // sb,),
        in_specs=[q_spec, mem_spec, mem_spec],
        out_specs=q_spec,
        out_shape=jax.ShapeDtypeStruct((count, s_len, D_MODEL), F32),
        compiler_params=_params(1),
    )(q, cache_k, cache_v)


def _mem_kv_body(mem_ref, g_ref, wk_ref, wv_ref, k_ref, v_ref):
    mm = _rms(mem_ref[...], g_ref[...]).astype(BF16)
    k_ref[...] = _dot(mm, wk_ref[...].astype(BF16))
    v_ref[...] = _dot(mm, wv_ref[...].astype(BF16))


def _mem_kv(mem, g, w_ck, w_cv):
    rows = mem.shape[0]
    tm = min(ROW_TILE, rows)
    row_spec = pl.BlockSpec((tm, D_MODEL), lambda i: (i, 0))
    w_spec = _const_spec((D_MODEL, D_MODEL))
    return pl.pallas_call(
        _mem_kv_body,
        grid=(rows // tm,),
        in_specs=[row_spec, _const_spec((1, D_MODEL)), w_spec, w_spec],
        out_specs=[row_spec, row_spec],
        out_shape=[jax.ShapeDtypeStruct((rows, D_MODEL), F32)] * 2,
        compiler_params=_params(1),
    )(mem, g, w_ck, w_cv)


def kernel(x_prompt, x_sample, mem_prompt, cache_mem_k, cache_mem_v, cache_swa_k, cache_swa_v, state_conv, state_lru_h,
           g_ffn1, w1_gate, w1_up, w1_down, g_mix, w_in, conv_w, conv_b, w_a, b_a, w_i, b_i, lam, sink,
           g_lru_out, g_attn_out, w_out, g_xattn, g_mem, w_cq, w_ck, w_cv, w_co, g_ffn2, w2_gate, w2_up, w2_down,
           g_final):
    nbp, seq, _ = x_prompt.shape
    nbs, dec_seq, _ = x_sample.shape
    depth = g_ffn1.shape[0]
    past = PAST_LEN
    cos_p, sin_p = _rope_tables(np.arange(seq, dtype=np.int32))
    sample_pos_rows = min(ROW_TILE, nbs * dec_seq)
    cos_s, sin_s = _rope_tables(np.tile(past + np.arange(dec_seq, dtype=np.int32), sample_pos_rows // dec_seq))

    xp = x_prompt.reshape(nbp * seq, D_MODEL)
    xs = x_sample.reshape(nbs * dec_seq, D_MODEL)
    g_fin = g_final.reshape(1, D_MODEL)
    row = lambda a: a.reshape(1, -1)
    outs = [[] for _ in range(10)]
    for l in range(depth):
        last = l == depth - 1
        bf = lambda a: a[l].astype(BF16)
        w1g, w1u, w1d, w2g, w2u, w2d = bf(w1_gate), bf(w1_up), bf(w1_down), bf(w2_gate), bf(w2_up), bf(w2_down)
        win, wout, wcq, wco = bf(w_in), bf(w_out), bf(w_cq), bf(w_co)
        wab, bab = _lru_gate_chunks(w_a[l], b_a[l], w_i[l], b_i[l])
        lru_w = (conv_w[l], row(conv_b[l]), wab, bab, row(lam[l]), row(g_lru_out[l]))

        xs = _ffn(xs, row(g_ffn1[l]), w1g, w1u, w1d, g_fin, False)
        u_s, gate, q, k, v = _proj(xs, row(g_mix[l]), win, cos_s, sin_s)
        conv_pad = jnp.pad(state_conv[l], ((0, 0), (dec_seq - (CONV_WIDTH - 1), 0), (0, 0)))
        h0_rep = jnp.repeat(state_lru_h[l], dec_seq, axis=0)
        per_seq = lambda a: a.reshape(nbs, dec_seq, a.shape[-1])
        xs, xq, hs, new_k, new_v = _sample_mixer(
            u_s, gate, conv_pad.reshape(nbs * dec_seq, LRU_WIDTH), h0_rep, per_seq(q), per_seq(k), per_seq(v),
            _channel_major(cache_swa_k[l]), _channel_major(cache_swa_v[l]), xs,
            sink[l], lru_w, row(g_attn_out[l]), wout, row(g_xattn[l]), wcq, past)
        xq, cache_k, cache_v = per_seq(xq), _interleave_chunks(cache_mem_k[l]), _interleave_chunks(cache_mem_v[l])

        mk_p, mv_p = _mem_kv(mem_prompt.reshape(nbp * N_MEM, D_MODEL), row(g_mem[l]), w_ck[l], w_cv[l])
        mk_p = mk_p.reshape(nbp, N_MEM, D_MODEL)
        mv_p = mv_p.reshape(nbp, N_MEM, D_MODEL)
        xp, q, k, v, lru_out, h_last, u_tail = _ffn_proj_lru(xp, seq, row(g_ffn1[l]), w1g, w1u, w1d, row(g_mix[l]), win,
                                                             cos_p, sin_p, *lru_w)
        k3 = k.reshape(nbp, seq, KV_WIDTH)
        v3 = v.reshape(nbp, seq, KV_WIDTH)
        xp = _swa_mix_ffn(q, k, v, sink[l], row(g_attn_out[l]), xp, lru_out, seq, wout, row(g_xattn[l]), wcq,
                          mk_p, mv_p, wco, row(g_ffn2[l]), w2g, w2u, w2d, g_fin, last)
        outs[0].append(mk_p.reshape(nbp, N_MEM, X_HEADS, X_HEAD_DIM))
        outs[1].append(mv_p.reshape(nbp, N_MEM, X_HEADS, X_HEAD_DIM))
        outs[2].append(k3[:, -WINDOW:].reshape(nbp, WINDOW, KV_HEADS, HEAD_DIM))
        outs[3].append(v3[:, -WINDOW:].reshape(nbp, WINDOW, KV_HEADS, HEAD_DIM))
        outs[4].append(u_tail[:, -(CONV_WIDTH - 1):])
        outs[5].append(h_last.reshape(nbp, LRU_WIDTH))

        xo = _xattn_cache(xq, cache_k, cache_v)
        xs = _ffn(xs, row(g_ffn2[l]), w2g, w2u, w2d, g_fin, last, xo.reshape(nbs * dec_seq, D_MODEL), wco)
        outs[6].append(_position_major(new_k))
        outs[7].append(_position_major(new_v))
        outs[8].append(u_s.reshape(nbs, dec_seq, LRU_WIDTH)[:, -(CONV_WIDTH - 1):])
        outs[9].append(hs.reshape(nbs, dec_seq, LRU_WIDTH)[:, -1])

    return (xp.reshape(nbp, seq, D_MODEL), xs.reshape(nbs, dec_seq, D_MODEL)) + tuple(jnp.stack(o) for o in outs)
```

```python
import functools

import jax
import jax.numpy as jnp
import numpy as np
from jax import lax
from jax.experimental import pallas as pl
from jax.experimental.pallas import tpu as pltpu

F32 = jnp.float32
BF16 = jnp.bfloat16

D_MODEL = 1024
LRU_WIDTH = 512
LRU_BLOCKS = 8
CONV_WIDTH = 4
LRU_C = 8.0
ATTN_HEADS = 8
HEAD_DIM = 64
KV_HEADS = 2
WINDOW = 128
PAST_LEN = 8192
ROPE_THETA = 10000.0
N_MEM = 256
X_HEADS = 4
X_HEAD_DIM = 256
D_FF = 2816
EPS = 1e-6
Q_WIDTH = ATTN_HEADS * HEAD_DIM
KV_WIDTH = KV_HEADS * HEAD_DIM
IN_COLS = 2 * LRU_WIDTH + Q_WIDTH + 2 * KV_WIDTH

LANES = 128
SUBLANES = 8
VMEM_LIMIT = 56 * 1024 * 1024

ROW_TILE = 512
LRU_TILE = 512
FFN_CHUNK = 256
SAMPLE_SEQ_TILE = 32
XATTN_SEQ_TILE = 8


def _params(n_axes):
    return pltpu.CompilerParams(dimension_semantics=("arbitrary",) * n_axes, vmem_limit_bytes=VMEM_LIMIT)


def _const_spec(shape):
    return pl.BlockSpec(shape, lambda *_: (0,) * len(shape), pipeline_mode=pl.Buffered(1))


def _rms(x, g):
    return x * lax.rsqrt(jnp.mean(x * x, axis=-1, keepdims=True) + EPS) * g


def _dot(a, b):
    return jnp.dot(a, b, preferred_element_type=F32)


def _dot_nt(a, b):
    return lax.dot_general(a, b, (((1,), (1,)), ((), ())), preferred_element_type=F32)


def _ffn_step(x, g_ref, wg_ref, wu_ref, wd_ref):
    xn = _rms(x, g_ref[...]).astype(BF16)
    gate = _dot(xn, wg_ref[...])
    up = _dot(xn, wu_ref[...])
    h = (gate * jax.nn.sigmoid(gate) * up).astype(BF16)
    return x + 0.5 * _dot(h, wd_ref[...])


def _ffn_tail(x, g_ref, wg_ref, wu_ref, wd_ref, gf_ref, o_ref, final_norm):
    y = _ffn_step(x, g_ref, wg_ref, wu_ref, wd_ref)
    if final_norm:
        y = _rms(y, gf_ref[...])
    o_ref[...] = y


def _ffn_body(x_ref, g_ref, wg_ref, wu_ref, wd_ref, gf_ref, o_ref, *, final_norm):
    _ffn_tail(x_ref[...], g_ref, wg_ref, wu_ref, wd_ref, gf_ref, o_ref, final_norm)


def _proj_ffn_body(x_ref, a_ref, wa_ref, g_ref, wg_ref, wu_ref, wd_ref, gf_ref, o_ref, *, final_norm):
    x = x_ref[...] + _dot(a_ref[...].astype(BF16), wa_ref[...])
    _ffn_tail(x, g_ref, wg_ref, wu_ref, wd_ref, gf_ref, o_ref, final_norm)


def _ffn(x, g, wg, wu, wd, g_final, final_norm, attn=None, w_attn=None):
    rows = x.shape[0]
    tm = min(ROW_TILE, rows)
    row_spec = pl.BlockSpec((tm, D_MODEL), lambda i: (i, 0))
    ffn_specs = [_const_spec((1, D_MODEL)), _const_spec((D_MODEL, D_FF)), _const_spec((D_MODEL, D_FF)),
                 _const_spec((D_FF, D_MODEL)), _const_spec((1, D_MODEL))]
    if attn is None:
        body, lead_specs, lead = _ffn_body, [row_spec], (x,)
    else:
        body, lead_specs, lead = _proj_ffn_body, [row_spec, row_spec, _const_spec((D_MODEL, D_MODEL))], (x, attn, w_attn)
    return pl.pallas_call(
        functools.partial(body, final_norm=final_norm),
        grid=(rows // tm,),
        in_specs=lead_specs + ffn_specs,
        out_specs=row_spec,
        out_shape=jax.ShapeDtypeStruct((rows, D_MODEL), F32),
        compiler_params=_params(1),
    )(*lead, g, wg, wu, wd, g_final)


def _ffn_cast_body(x_ref, g_ref, wg_ref, wu_ref, wd_ref, og_ref, ou_ref, od_ref,
                   o_ref, wg16_ref, wu16_ref, wd16_ref, og16_ref, ou16_ref, od16_ref, xn_ref, acc_ref):
    f = pl.program_id(0)

    @pl.when(f == 0)
    def _():
        xn_ref[...] = _rms(x_ref[...], g_ref[...]).astype(BF16)
        acc_ref[...] = jnp.zeros_like(acc_ref)

    for src, dst in ((wg_ref, wg16_ref), (wu_ref, wu16_ref), (wd_ref, wd16_ref),
                     (og_ref, og16_ref), (ou_ref, ou16_ref), (od_ref, od16_ref)):
        dst[...] = src[...].astype(BF16)
    xn = xn_ref[...]
    gate = _dot(xn, wg16_ref[...])
    up = _dot(xn, wu16_ref[...])
    acc_ref[...] += _dot((gate * jax.nn.sigmoid(gate) * up).astype(BF16), wd16_ref[...])

    @pl.when(f == pl.num_programs(0) - 1)
    def _():
        o_ref[...] = x_ref[...] + 0.5 * acc_ref[...]


def _ffn_cast(x, g, wg, wu, wd, other_g, other_u, other_d):
    rows = x.shape[0]
    chunks = D_FF // FFN_CHUNK
    x_spec = _const_spec((rows, D_MODEL))
    col_spec = pl.BlockSpec((D_MODEL, FFN_CHUNK), lambda f: (0, f))
    row_spec = pl.BlockSpec((FFN_CHUNK, D_MODEL), lambda f: (f, 0))
    w_specs = [col_spec, col_spec, row_spec] * 2
    wide, tall = jax.ShapeDtypeStruct((D_MODEL, D_FF), BF16), jax.ShapeDtypeStruct((D_FF, D_MODEL), BF16)
    return pl.pallas_call(
        _ffn_cast_body,
        grid=(chunks,),
        in_specs=[x_spec, _const_spec((1, D_MODEL))] + w_specs,
        out_specs=[pl.BlockSpec((rows, D_MODEL), lambda f: (0, 0))] + w_specs,
        out_shape=[jax.ShapeDtypeStruct((rows, D_MODEL), F32)] + [wide, wide, tall] * 2,
        scratch_shapes=[pltpu.VMEM((rows, D_MODEL), BF16), pltpu.VMEM((rows, D_MODEL), F32)],
        compiler_params=_params(1),
    )(x, g, wg, wu, wd, other_g, other_u, other_d)


def _rope(z, cos, sin_signed):
    half = HEAD_DIM // 2
    lane = lax.broadcasted_iota(jnp.int32, z.shape, 1)
    first_half = (lane % HEAD_DIM) < half
    partner = jnp.where(first_half, pltpu.roll(z, LANES - half, axis=1), pltpu.roll(z, half, axis=1))
    return z * cos + partner * sin_signed


def _project_pieces(x, g_ref, w_ref, cos_ref, sin_ref, store_u, store_gate, q_ref, k_ref, v_ref):
    o_gate, o_q, o_k, o_v = LRU_WIDTH, 2 * LRU_WIDTH, 2 * LRU_WIDTH + Q_WIDTH, 2 * LRU_WIDTH + Q_WIDTH + KV_WIDTH
    xn = []

    def normed():
        if not xn:
            xn.append(_rms(x(), g_ref[...]).astype(BF16))
        return xn[0]

    def rope_into(ref, z):
        for j in range(z.shape[1] // LANES):
            cols = slice(j * LANES, (j + 1) * LANES)
            ref[:, cols] = _rope(z[:, cols], cos_ref[...], sin_ref[...])

    def store_v(z):
        v_ref[...] = z

    return [lambda: rope_into(q_ref, _dot(normed(), w_ref[:, o_q:o_k])),
            lambda: rope_into(k_ref, _dot(normed(), w_ref[:, o_k:o_v])),
            lambda: store_v(_dot(normed(), w_ref[:, o_v:])),
            lambda: store_u(_dot(normed(), w_ref[:, :o_gate])),
            lambda: store_gate(_dot(normed(), w_ref[:, o_gate:o_q]))]


def _proj_body(x_ref, g_ref, w_ref, cos_ref, sin_ref, u_ref, gate_ref, q_ref, k_ref, v_ref):
    def store_u(z):
        u_ref[...] = z

    def store_gate(z):
        gate_ref[...] = z

    for piece in _project_pieces(lambda: x_ref[...], g_ref, w_ref, cos_ref, sin_ref, store_u, store_gate,
                                 q_ref, k_ref, v_ref):
        piece()


def _proj(x, g, w_in, cos, sin):
    rows = x.shape[0]
    tm = min(ROW_TILE, rows, cos.shape[0])
    pos_blocks = cos.shape[0] // tm

    def row_spec(width):
        return pl.BlockSpec((tm, width), lambda i: (i, 0))

    pos_spec = pl.BlockSpec((tm, LANES), lambda i: (i % pos_blocks, 0))
    widths = (LRU_WIDTH, LRU_WIDTH, Q_WIDTH, KV_WIDTH, KV_WIDTH)
    return pl.pallas_call(
        _proj_body,
        grid=(rows // tm,),
        in_specs=[row_spec(D_MODEL), _const_spec((1, D_MODEL)), _const_spec((D_MODEL, IN_COLS)), pos_spec, pos_spec],
        out_specs=[row_spec(w) for w in widths],
        out_shape=[jax.ShapeDtypeStruct((rows, w), F32) for w in widths],
        compiler_params=_params(1),
    )(x, g, w_in, cos, sin)


def _rope_tables(pos):
    half = HEAD_DIM // 2
    inv = ROPE_THETA ** (-np.arange(half, dtype=np.float64) / half)
    ang = pos.astype(np.float64)[:, None] * inv[None, :]
    cos = np.cos(ang)
    sin = np.sin(ang)
    reps = LANES // HEAD_DIM
    return (jnp.asarray(np.tile(np.concatenate([cos, cos], axis=-1), (1, reps)), dtype=F32),
            jnp.asarray(np.tile(np.concatenate([-sin, sin], axis=-1), (1, reps)), dtype=F32))


def _softplus(x):
    return jnp.maximum(x, 0.0) + jnp.log1p(jnp.exp(-jnp.abs(x)))


def _lru_coeffs(conv, wab, bab, lam):
    w = conv.shape[1]
    gates = _dot(conv.astype(BF16), wab) + bab
    r = jax.nn.sigmoid(gates[:, :w])
    gi = jax.nn.sigmoid(gates[:, w:])
    log_a = -LRU_C * r * _softplus(-lam)
    a = jnp.exp(log_a)
    b = jnp.sqrt(-jnp.tanh(log_a) * (a * a + 1.0)) * (gi * conv)
    return a, b


def _segment_scan(a, b, seg):
    step = 1
    while step < seg:
        a, b = _scan_step(a, b, seg, step)
        step *= 2
    return a, b


def _scan_step(a, b, seg, step):
    pos = lax.broadcasted_iota(jnp.int32, a.shape, 0) % seg
    live = pos >= step
    a_prev = pltpu.roll(a, step, axis=0)
    b_prev = pltpu.roll(b, step, axis=0)
    return jnp.where(live, a * a_prev, a), jnp.where(live, a * b_prev + b, b)


def _lru_hidden(conv, h_in, seg, wab, bab, lam):
    a, b = _lru_coeffs(conv, wab, bab, lam)
    a_cum, h_local = _segment_scan(a, b, seg)
    return a_cum * h_in + h_local


def _lane_chunks(width):
    return [slice(c * LANES, (c + 1) * LANES) for c in range(width // LANES)]


def _ffn_pieces(normed, wg_ref, wu_ref, wd_ref, state):
    cols = [slice(f, min(f + FFN_CHUNK, D_FF)) for f in range(0, D_FF, FFN_CHUNK)]

    def gate_up(f):
        return _dot(normed(), wg_ref[:, cols[f]]), _dot(normed(), wu_ref[:, cols[f]])

    def piece(f):
        def run():
            gate, up = state.pop("gate_up") if "gate_up" in state else gate_up(f)
            if f + 1 < len(cols):
                state["gate_up"] = gate_up(f + 1)
            part = _dot((gate * jax.nn.sigmoid(gate) * up).astype(BF16), wd_ref[cols[f], :])
            state["acc"] = part if "acc" not in state else state["acc"] + part
        return run

    return [piece(f) for f in range(len(cols))]


def _interleave(primary, secondary):
    due = [((i + 1) * len(primary)) // (len(secondary) + 1) for i in range(len(secondary))]
    pending = list(zip(due, secondary))
    for i, piece in enumerate(primary):
        while pending and pending[0][0] <= i:
            pending.pop(0)[1]()
        piece()
    for _, piece in pending:
        piece()


def _ffn_proj_lru_body(x_ref, g1_ref, wg_ref, wu_ref, wd_ref, gmix_ref, win_ref, cos_ref, sin_ref,
                       cw_ref, cb_ref, wab_ref, bab_ref, lam_ref, gout_ref,
                       x_out_ref, q_ref, k_ref, v_ref, lru_ref, hlast_ref, utail_ref,
                       ug_ref, ext_ref, h_ref, hs_ref, *, tt, tiles, tiles_per_seq):
    s = pl.program_id(0)
    refs = (x_ref, g1_ref, wg_ref, wu_ref, wd_ref, gmix_ref, win_ref, cos_ref, sin_ref,
            cw_ref, cb_ref, wab_ref, bab_ref, lam_ref, gout_ref,
            x_out_ref, q_ref, k_ref, v_ref, lru_ref, hlast_ref, utail_ref, ug_ref, ext_ref, h_ref, hs_ref)

    @pl.when(s == 0)
    def _():
        ug_ref[1] = jnp.zeros((2, tt, LRU_WIDTH), F32)

    @pl.when((s == 0) | (lax.rem(s - 1, tiles_per_seq) == 0))
    def _():
        ext_ref[0:SUBLANES, :] = jnp.zeros((SUBLANES, LRU_WIDTH), F32)
        h_ref[...] = jnp.zeros_like(h_ref)

    @pl.when(s < tiles)
    def _():
        _ffn_proj_lru_step(*refs, tt=tt, project=True)

    @pl.when(s == tiles)
    def _():
        _ffn_proj_lru_step(*refs, tt=tt, project=False)


def _ffn_proj_lru_step(x_ref, g1_ref, wg_ref, wu_ref, wd_ref, gmix_ref, win_ref, cos_ref, sin_ref,
                       cw_ref, cb_ref, wab_ref, bab_ref, lam_ref, gout_ref,
                       x_out_ref, q_ref, k_ref, v_ref, lru_ref, hlast_ref, utail_ref,
                       ug_ref, ext_ref, h_ref, hs_ref, *, tt, project):
    pad = SUBLANES
    lead_slot = pl.program_id(0) % 2
    lag_slot = 1 - lead_slot

    state = {"sumsq": jnp.zeros((tt, 1), F32)}
    scan_steps = [1 << i for i in range(tt.bit_length() - 1)]

    def lru_pieces(c, cols):
        def coeffs():
            ext_ref[pad:pad + tt, cols] = ug_ref[lag_slot, 0, :, cols]
            conv = cb_ref[:, cols]
            for j in range(CONV_WIDTH):
                start = pad - (CONV_WIDTH - 1) + j
                conv = conv + ext_ref[start:start + tt, cols] * cw_ref[j:j + 1, cols]
            ext_ref[0:pad, cols] = ext_ref[tt:tt + pad, cols]
            utail_ref[0, :, cols] = ext_ref[0:pad, cols]
            state[c] = _lru_coeffs(conv, wab_ref[c], bab_ref[c], lam_ref[:, cols])

        def scan(steps):
            def run():
                for step in steps:
                    state[c] = _scan_step(*state[c], tt, step)
            return run

        def finish():
            a_cum, h_local = state.pop(c)
            hs_ref[:, cols] = a_cum * h_ref[:, cols] + h_local
            h_ref[:, cols] = hs_ref[tt - 1:tt, cols]
            hlast_ref[0, :, cols] = h_ref[:, cols]
            y = hs_ref[:, cols] * jax.nn.gelu(ug_ref[lag_slot, 1, :, cols])
            hs_ref[:, cols] = y
            state["sumsq"] = state["sumsq"] + jnp.sum(y * y, axis=-1, keepdims=True)

        half = len(scan_steps) // 2
        return [coeffs, scan(scan_steps[:half]), scan(scan_steps[half:]), finish]

    vector_pieces = [p for c, cols in enumerate(_lane_chunks(LRU_WIDTH)) for p in lru_pieces(c, cols)]

    def ffn_in():
        if "xn" not in state:
            state["xn"] = _rms(x_ref[...], g1_ref[...]).astype(BF16)
        return state["xn"]

    def ffn_out():
        if "x" not in state:
            state["x"] = x_ref[...] + 0.5 * state.pop("acc")
            x_out_ref[...] = state["x"]
        return state["x"]

    def store_u(z):
        ug_ref[lead_slot, 0] = z

    def store_gate(z):
        ug_ref[lead_slot, 1] = z

    matmul_pieces = _ffn_pieces(ffn_in, wg_ref, wu_ref, wd_ref, state)
    matmul_pieces += _project_pieces(ffn_out, gmix_ref, win_ref, cos_ref, sin_ref, store_u, store_gate,
                                     q_ref, k_ref, v_ref)

    assert len(vector_pieces) == len(matmul_pieces)
    for vector_piece, matmul_piece in zip(vector_pieces, matmul_pieces):
        vector_piece()
        if project:
            matmul_piece()
    lru_ref[...] = hs_ref[...] * lax.rsqrt(state["sumsq"] * (1.0 / LRU_WIDTH) + EPS) * gout_ref[...]


def _ffn_proj_lru(x, seq, g1, wg, wu, wd, g_mix, w_in, cos, sin, conv_w, conv_b, wab, bab, lam, g_out):
    rows = x.shape[0]
    n = rows // seq
    tt = min(LRU_TILE, seq)
    tiles_per_seq = seq // tt
    tiles = rows // tt
    pos_blocks = cos.shape[0] // tt
    lead = lambda s: jnp.minimum(s, tiles - 1)
    lag = lambda s: jnp.maximum(s - 1, 0)

    def lead_spec(width):
        return pl.BlockSpec((tt, width), lambda s: (lead(s), 0))

    pos_spec = pl.BlockSpec((tt, LANES), lambda s: (lead(s) % pos_blocks, 0))
    seq_spec = lambda r: pl.BlockSpec((1, r, LRU_WIDTH), lambda s: (lag(s) // tiles_per_seq, 0, 0))
    widths = (D_MODEL, Q_WIDTH, KV_WIDTH, KV_WIDTH)
    return pl.pallas_call(
        functools.partial(_ffn_proj_lru_body, tt=tt, tiles=tiles, tiles_per_seq=tiles_per_seq),
        grid=(tiles + 1,),
        in_specs=[lead_spec(D_MODEL), _const_spec((1, D_MODEL)), _const_spec((D_MODEL, D_FF)), _const_spec((D_MODEL, D_FF)),
                  _const_spec((D_FF, D_MODEL)), _const_spec((1, D_MODEL)), _const_spec((D_MODEL, IN_COLS)),
                  pos_spec, pos_spec] + _lru_weight_specs(),
        out_specs=[lead_spec(w) for w in widths]
        + [pl.BlockSpec((tt, LRU_WIDTH), lambda s: (lag(s), 0)), seq_spec(1), seq_spec(SUBLANES)],
        out_shape=[jax.ShapeDtypeStruct((rows, w), F32) for w in widths]
        + [jax.ShapeDtypeStruct((rows, LRU_WIDTH), F32), jax.ShapeDtypeStruct((n, 1, LRU_WIDTH), F32),
           jax.ShapeDtypeStruct((n, SUBLANES, LRU_WIDTH), F32)],
        scratch_shapes=[pltpu.VMEM((2, 2, tt, LRU_WIDTH), F32), pltpu.VMEM((tt + SUBLANES, LRU_WIDTH), F32),
                        pltpu.VMEM((1, LRU_WIDTH), F32), pltpu.VMEM((tt, LRU_WIDTH), F32)],
        compiler_params=_params(1),
    )(x, g1, wg, wu, wd, g_mix, w_in, cos, sin, conv_w, conv_b, wab, bab, lam, g_out)


def _lru_sample_body(u_ref, gate_ref, cpad_ref, h0_ref, cw_ref, cb_ref, wab_ref, bab_ref, lam_ref, gout_ref,
                     o_ref, hs_ref, *, seg):
    rows = u_ref.shape[0]
    pos = lax.broadcasted_iota(jnp.int32, (rows, LANES), 0) % seg
    sumsq = jnp.zeros((rows, 1), F32)
    for c, cols in enumerate(_lane_chunks(LRU_WIDTH)):
        u = u_ref[:, cols]
        cpad = cpad_ref[:, cols]
        conv = cb_ref[:, cols] + u * cw_ref[CONV_WIDTH - 1:CONV_WIDTH, cols]
        for back in range(1, CONV_WIDTH):
            shifted = jnp.where(pos >= back, pltpu.roll(u, back, axis=0),
                                pltpu.roll(cpad, (back - seg) % rows, axis=0))
            conv = conv + shifted * cw_ref[CONV_WIDTH - 1 - back:CONV_WIDTH - back, cols]
        h = _lru_hidden(conv, h0_ref[:, cols], seg, wab_ref[c], bab_ref[c], lam_ref[:, cols])
        hs_ref[:, cols] = h
        y = h * jax.nn.gelu(gate_ref[:, cols])
        o_ref[:, cols] = y
        sumsq = sumsq + jnp.sum(y * y, axis=-1, keepdims=True)
    o_ref[...] = o_ref[...] * lax.rsqrt(sumsq * (1.0 / LRU_WIDTH) + EPS) * gout_ref[...]


def _lru_weight_specs():
    chunks = LRU_WIDTH // LANES
    return [_const_spec((CONV_WIDTH, LRU_WIDTH)), _const_spec((1, LRU_WIDTH)),
            _const_spec((chunks, LANES, 2 * LANES)), _const_spec((chunks, 1, 2 * LANES)),
            _const_spec((1, LRU_WIDTH)), _const_spec((1, LRU_WIDTH))]


def _lru_gate_chunks(w_a, b_a, w_i, b_i):
    chunks = LRU_WIDTH // LANES
    per = LRU_BLOCKS // chunks
    wa = w_a.reshape(chunks, per, *w_a.shape[1:])
    wi = w_i.reshape(chunks, per, *w_i.shape[1:])
    wab = jnp.stack([jnp.concatenate([_block_diag(wa[c]), _block_diag(wi[c])], axis=1) for c in range(chunks)])
    bab = jnp.concatenate([b_a.reshape(chunks, 1, LANES), b_i.reshape(chunks, 1, LANES)], axis=2)
    return wab.astype(BF16), bab


def _block_diag(w):
    nb, bi, bj = w.shape
    eye = jnp.eye(nb, dtype=w.dtype)
    return jnp.einsum('gij,gh->gihj', w, eye).reshape(nb * bi, nb * bj)


def _bdot_nt(a, b):
    return lax.dot_general(a, b, (((2,), (2,)), ((0,), (0,))), preferred_element_type=F32)


def _bdot(a, b):
    return lax.dot_general(a, b, (((2,), (1,)), ((0,), (0,))), preferred_element_type=F32)


def _swa_pieces(q, score, weigh, sink_ref, mask, g_out, store):
    group = ATTN_HEADS // KV_HEADS
    straight = [h for h in range(ATTN_HEADS) if (h % 2) == (h // group)]
    swapped = [h for h in range(ATTN_HEADS) if (h % 2) != (h // group)]
    scale = HEAD_DIM ** -0.5
    out_half = {}

    def low_lanes(shape):
        return lax.broadcasted_iota(jnp.int32, shape, 2) < HEAD_DIM

    def head_group(heads, swap):
        held = {}

        def scores():
            qv = q()
            n, r, _ = qv.shape
            low = low_lanes((n, r, LANES))
            zero = jnp.zeros((n, r, LANES), F32)
            qs = jnp.concatenate(
                [jnp.where(low if h % 2 == 0 else ~low, qv[:, :, (h // 2) * LANES:(h // 2 + 1) * LANES], zero)
                 for h in heads], axis=1)
            held["s"] = score(qs.astype(BF16), swap) * scale

        def probabilities():
            s = held.pop("s")
            r = s.shape[1] // len(heads)
            visible = mask()[None]
            probs = []
            for i, h in enumerate(heads):
                sh = jnp.where(visible, s[:, i * r:(i + 1) * r, :], -jnp.inf)
                sink = sink_ref[h]
                m = jnp.maximum(jnp.max(sh, axis=-1, keepdims=True), sink)
                e = jnp.exp(sh - m)
                denom = jnp.sum(e, axis=-1, keepdims=True) + jnp.exp(sink - m)
                probs.append(e / denom)
            held["p"] = jnp.concatenate(probs, axis=1).astype(BF16)

        def weighted_values():
            p = held.pop("p")
            r = p.shape[1] // len(heads)
            o = weigh(p, swap)
            for i, h in enumerate(heads):
                out_half[h] = o[:, i * r:(i + 1) * r, :]

        return [scores, probabilities, weighted_values]

    def finish():
        low = low_lanes(out_half[0].shape)
        out = jnp.concatenate([jnp.where(low, out_half[2 * j], out_half[2 * j + 1])
                               for j in range(Q_WIDTH // LANES)], axis=2)
        store(_rms(out, g_out))

    return head_group(straight, False) + head_group(swapped, True) + [finish]


def _band_mask(first_block):
    i = lax.broadcasted_iota(jnp.int32, (WINDOW, 2 * WINDOW), 0)
    j = lax.broadcasted_iota(jnp.int32, (WINDOW, 2 * WINDOW), 1)
    dist = i + WINDOW - j
    return (dist >= 0) & (dist < WINDOW) & (jnp.logical_not(first_block) | (j >= WINDOW))


def _swa_sample_body(sink_ref, q_ref, k_ref, v_ref, bk_ref, bv_ref, gout_ref, o_ref, nk_ref, nv_ref, *, s_len, past):
    def mask():
        qp = past + lax.broadcasted_iota(jnp.int32, (s_len, WINDOW + s_len), 0)
        col = lax.broadcasted_iota(jnp.int32, (s_len, WINDOW + s_len), 1)
        kp = jnp.where(col < WINDOW, past - WINDOW + col, past + col - WINDOW)
        dist = qp - kp
        return (dist >= 0) & (dist < WINDOW) & (kp >= 0)

    def halves(z, axis, swap):
        return pltpu.roll(z, HEAD_DIM, axis=axis) if swap else z

    def score(qs, swap):
        carried = _bdot(qs, halves(bk_ref[...], 1, swap).astype(BF16))
        fresh = _bdot_nt(qs, halves(k_ref[...], 2, swap).astype(BF16))
        return jnp.concatenate([carried, fresh], axis=2)

    def weigh(p, swap):
        return (_bdot_nt(p[:, :, :WINDOW], halves(bv_ref[...], 1, swap).astype(BF16))
                + _bdot(p[:, :, WINDOW:], halves(v_ref[...], 2, swap).astype(BF16)))

    def store(o):
        o_ref[...] = o.reshape(o_ref.shape)

    for piece in _swa_pieces(lambda: q_ref[...], score, weigh, sink_ref, mask, gout_ref[...], store):
        piece()

    def slide(window_ref, fresh_ref, out_ref):
        n = window_ref.shape[0]
        tail = jnp.concatenate([jnp.zeros((n, WINDOW - s_len, KV_WIDTH), F32), fresh_ref[...]], axis=1)
        lane = lax.broadcasted_iota(jnp.int32, (n, KV_WIDTH, WINDOW), 2)
        out_ref[...] = jnp.where(lane >= WINDOW - s_len, jnp.swapaxes(tail, 1, 2),
                                 pltpu.roll(window_ref[...], WINDOW - s_len, axis=2))

    slide(bk_ref, k_ref, nk_ref)
    slide(bv_ref, v_ref, nv_ref)


def _softmax(s):
    e = jnp.exp(s - jnp.max(s, axis=-1, keepdims=True))
    return e / jnp.sum(e, axis=-1, keepdims=True)


def _mix_and_query(x_ref, lru_ref, attn_ref, wout_ref, gx_ref, wcq_ref):
    x = (x_ref[...] + _dot(lru_ref[...].astype(BF16), wout_ref[:LRU_WIDTH, :])
         + _dot(attn_ref[...].astype(BF16), wout_ref[LRU_WIDTH:, :]))
    return x, _dot(_rms(x, gx_ref[...]).astype(BF16), wcq_ref[...])


def _swa_mix_ffn_body(sink_ref, q_ref, k_ref, v_ref, kp_ref, vp_ref, gattn_ref,
                      x_ref, lru_ref, wout_ref, gx_ref, wcq_ref, mk_ref, mv_ref, wco_ref,
                      g2_ref, wg_ref, wu_ref, wd_ref, gf_ref, o_ref, attn_ref,
                      *, tt, tiles, tiles_per_seq, final_norm):
    s = pl.program_id(0)
    refs = (sink_ref, q_ref, k_ref, v_ref, kp_ref, vp_ref, gattn_ref, x_ref, lru_ref, wout_ref, gx_ref, wcq_ref,
            mk_ref, mv_ref, wco_ref, g2_ref, wg_ref, wu_ref, wd_ref, gf_ref, o_ref, attn_ref)
    step = functools.partial(_swa_mix_ffn_step, *refs, tt=tt, tiles_per_seq=tiles_per_seq, final_norm=final_norm)

    @pl.when(s == 0)
    def _():
        step(attend=True, layer=False)

    @pl.when((s > 0) & (s < tiles))
    def _():
        step(attend=True, layer=True)

    @pl.when(s == tiles)
    def _():
        step(attend=False, layer=True)


def _swa_mix_ffn_step(sink_ref, q_ref, k_ref, v_ref, kp_ref, vp_ref, gattn_ref,
                      x_ref, lru_ref, wout_ref, gx_ref, wcq_ref, mk_ref, mv_ref, wco_ref,
                      g2_ref, wg_ref, wu_ref, wd_ref, gf_ref, o_ref, attn_ref,
                      *, tt, tiles_per_seq, final_norm, attend, layer):
    s = pl.program_id(0)
    lead_slot = s % 2
    lag_slot = 1 - lead_slot

    blocks = tt // WINDOW
    first_pos_block = lax.rem(s, tiles_per_seq) * blocks
    attention_pieces = []
    for j in range(blocks):
        rows = slice(j * WINDOW, (j + 1) * WINDOW)

        def band(ref, prev_ref, swap, j=j, rows=rows):
            prev = prev_ref[...] if j == 0 else ref[(j - 1) * WINDOW:j * WINDOW, :]
            both = jnp.concatenate([prev, ref[rows, :]], axis=0)[None]
            return (pltpu.roll(both, HEAD_DIM, axis=2) if swap else both).astype(BF16)

        def score(qs, swap, band=band):
            return _bdot_nt(qs, band(k_ref, kp_ref, swap))

        def weigh(p, swap, band=band):
            return _bdot(p, band(v_ref, vp_ref, swap))

        def store(o, rows=rows):
            attn_ref[lead_slot, rows, :] = o[0]

        attention_pieces += _swa_pieces(lambda rows=rows: q_ref[rows, :][None], score, weigh, sink_ref,
                                        lambda j=j: _band_mask(first_pos_block + j == 0), gattn_ref[...], store)

    state = {}
    scale = X_HEAD_DIM ** -0.5

    def mix():
        state["x"] = (x_ref[...] + _dot(lru_ref[...].astype(BF16), wout_ref[:LRU_WIDTH, :])
                      + _dot(attn_ref[lag_slot].astype(BF16), wout_ref[LRU_WIDTH:, :]))

    def query():
        state["q"] = _dot(_rms(state["x"], gx_ref[...]).astype(BF16), wcq_ref[...])

    def memory_head(h):
        def run():
            cols = slice(h * X_HEAD_DIM, (h + 1) * X_HEAD_DIM)
            sc = _dot_nt(state["q"][:, cols].astype(BF16), mk_ref[0, :, cols].astype(BF16)) * scale
            state["o", h] = _dot(_softmax(sc).astype(BF16), mv_ref[0, :, cols].astype(BF16))
        return run

    def memory_out():
        o = jnp.concatenate([state.pop(("o", h)) for h in range(X_HEADS)], axis=1)
        state["x"] = state["x"] + _dot(o.astype(BF16), wco_ref[...])
        state["xn"] = _rms(state["x"], g2_ref[...]).astype(BF16)

    def finish():
        y = state["x"] + 0.5 * state["acc"]
        o_ref[...] = _rms(y, gf_ref[...]) if final_norm else y

    layer_pieces = ([mix, query] + [memory_head(h) for h in range(X_HEADS)] + [memory_out]
                    + _ffn_pieces(lambda: state["xn"], wg_ref, wu_ref, wd_ref, state) + [finish])
    _interleave(layer_pieces if layer else [], attention_pieces if attend else [])


def _swa_mix_ffn(q, k, v, sink, g_attn, x, lru_out, seq, w_out, g_x, w_cq, mk, mv, w_co, g2, wg, wu, wd, g_final,
                 final_norm):
    rows = x.shape[0]
    tt = min(LRU_TILE, seq)
    tiles_per_seq = seq // tt
    tiles = rows // tt
    blocks = tt // WINDOW
    lead = lambda s: jnp.minimum(s, tiles - 1)
    lag = lambda s: jnp.maximum(s - 1, 0)

    def lead_spec(width):
        return pl.BlockSpec((tt, width), lambda s: (lead(s), 0))

    def lag_spec(width):
        return pl.BlockSpec((tt, width), lambda s: (lag(s), 0))

    prev_spec = pl.BlockSpec((WINDOW, KV_WIDTH), lambda s: (jnp.maximum(lead(s) * blocks - 1, 0), 0))
    mem_spec = pl.BlockSpec((1, N_MEM, D_MODEL), lambda s: (lag(s) // tiles_per_seq, 0, 0))
    w_spec = _const_spec((D_MODEL, D_MODEL))
    return pl.pallas_call(
        functools.partial(_swa_mix_ffn_body, tt=tt, tiles=tiles, tiles_per_seq=tiles_per_seq, final_norm=final_norm),
        grid=(tiles + 1,),
        in_specs=[pl.BlockSpec(memory_space=pltpu.SMEM), lead_spec(Q_WIDTH), lead_spec(KV_WIDTH), lead_spec(KV_WIDTH),
                  prev_spec, prev_spec, _const_spec((1, Q_WIDTH)),
                  lag_spec(D_MODEL), lag_spec(LRU_WIDTH), w_spec, _const_spec((1, D_MODEL)), w_spec,
                  mem_spec, mem_spec, w_spec,
                  _const_spec((1, D_MODEL)), _const_spec((D_MODEL, D_FF)), _const_spec((D_MODEL, D_FF)),
                  _const_spec((D_FF, D_MODEL)), _const_spec((1, D_MODEL))],
        out_specs=lag_spec(D_MODEL),
        out_shape=jax.ShapeDtypeStruct((rows, D_MODEL), F32),
        scratch_shapes=[pltpu.VMEM((2, tt, Q_WIDTH), F32)],
        compiler_params=_params(1),
    )(sink, q, k, v, k, v, g_attn, x, lru_out, w_out, g_x, w_cq, mk, mv, w_co, g2, wg, wu, wd, g_final)


def _sample_mixer_body(sink_ref, u_ref, gate_ref, cpad_ref, h0_ref, q_ref, k_ref, v_ref, bk_ref, bv_ref, x_ref,
                       cw_ref, cb_ref, wab_ref, bab_ref, lam_ref, glru_ref, gattn_ref, wout_ref, gx_ref, wcq_ref,
                       x_out_ref, xq_ref, hs_ref, nk_ref, nv_ref, lru_ref, attn_ref, *, seg, past):
    _lru_sample_body(u_ref, gate_ref, cpad_ref, h0_ref, cw_ref, cb_ref, wab_ref, bab_ref, lam_ref, glru_ref,
                     lru_ref, hs_ref, seg=seg)
    _swa_sample_body(sink_ref, q_ref, k_ref, v_ref, bk_ref, bv_ref, gattn_ref, attn_ref, nk_ref, nv_ref,
                     s_len=seg, past=past)
    x_out_ref[...], xq_ref[...] = _mix_and_query(x_ref, lru_ref, attn_ref, wout_ref, gx_ref, wcq_ref)


def _channel_major(window):
    n = window.shape[0]
    return jnp.transpose(window, (0, 2, 3, 1)).reshape(n, KV_WIDTH, WINDOW)


def _position_major(window):
    n = window.shape[0]
    return jnp.transpose(window.reshape(n, KV_HEADS, HEAD_DIM, WINDOW), (0, 3, 1, 2))


def _sample_mixer(u, gate, conv_pad, h0_rep, q, k, v, buf_k, buf_v, x, sink, lru_w, g_attn, w_out, g_x, w_cq, past):
    n, seg, _ = q.shape
    sb = min(SAMPLE_SEQ_TILE, n)
    rows = sb * seg

    def row_spec(width):
        return pl.BlockSpec((rows, width), lambda i: (i, 0))

    def seq_spec(steps, width):
        return pl.BlockSpec((sb, steps, width), lambda i: (i, 0, 0))

    buf_spec = seq_spec(WINDOW, KV_WIDTH)
    w_spec = _const_spec((D_MODEL, D_MODEL))
    total = n * seg
    return pl.pallas_call(
        functools.partial(_sample_mixer_body, seg=seg, past=past),
        grid=(n // sb,),
        in_specs=[pl.BlockSpec(memory_space=pltpu.SMEM)] + [row_spec(LRU_WIDTH)] * 4
        + [seq_spec(seg, Q_WIDTH), seq_spec(seg, KV_WIDTH), seq_spec(seg, KV_WIDTH), buf_spec, buf_spec,
           row_spec(D_MODEL)] + _lru_weight_specs()
        + [_const_spec((1, Q_WIDTH)), w_spec, _const_spec((1, D_MODEL)), w_spec],
        out_specs=[row_spec(D_MODEL), row_spec(D_MODEL), row_spec(LRU_WIDTH), buf_spec, buf_spec],
        out_shape=[jax.ShapeDtypeStruct((total, D_MODEL), F32), jax.ShapeDtypeStruct((total, D_MODEL), F32),
                   jax.ShapeDtypeStruct((total, LRU_WIDTH), F32),
                   jax.ShapeDtypeStruct((n, WINDOW, KV_WIDTH), F32), jax.ShapeDtypeStruct((n, WINDOW, KV_WIDTH), F32)],
        scratch_shapes=[pltpu.VMEM((rows, LRU_WIDTH), F32), pltpu.VMEM((rows, Q_WIDTH), F32)],
        compiler_params=_params(1),
    )(sink, u, gate, conv_pad, h0_rep, q, k, v, buf_k, buf_v, x, *lru_w, g_attn, w_out, g_x, w_cq)


def _xattn_cache_pieces(q_ref, mk_ref, mv_ref, o_ref):
    sb, steps, _ = q_ref.shape
    blocks = D_MODEL // LANES
    chunks = X_HEAD_DIM // LANES
    width = mk_ref.shape[1]
    block_cls = [(j % chunks) * X_HEADS + j // chunks for j in range(blocks)]
    scale = X_HEAD_DIM ** -0.5
    held = {}

    def lane_class():
        return lax.broadcasted_iota(jnp.int32, (sb, steps, width), 2) % blocks

    def scores():
        qs = jnp.concatenate([q_ref[:, :, j * LANES:(j + 1) * LANES] for j in range(blocks)], axis=1)
        held["s"] = _bdot_nt(qs.astype(BF16), mk_ref[...].astype(BF16)) * scale

    def probabilities():
        s = held.pop("s")
        cls = lane_class()
        part = jnp.zeros((sb, steps, width), F32)
        for j in range(blocks):
            part = part + jnp.where(cls == block_cls[j], s[:, j * steps:(j + 1) * steps, :], 0.0)
        score = part + pltpu.roll(part, width - X_HEADS, axis=2)
        top = jnp.zeros((sb, steps, width), F32)
        for h in range(X_HEADS):
            mine = cls == h
            top = jnp.where(mine, jnp.max(jnp.where(mine, score, -jnp.inf), axis=-1, keepdims=True), top)
        e = jnp.where(cls < X_HEADS, jnp.exp(score - top), 0.0)
        denom = jnp.ones((sb, steps, width), F32)
        for h in range(X_HEADS):
            mine = cls == h
            denom = jnp.where(mine, jnp.sum(jnp.where(mine, e, 0.0), axis=-1, keepdims=True), denom)
        p = e / denom
        p = p + pltpu.roll(p, X_HEADS, axis=2)
        held["p"] = jnp.concatenate([jnp.where(cls == block_cls[j], p, 0.0) for j in range(blocks)],
                                    axis=1).astype(BF16)

    def weighted_values():
        o = _bdot(held.pop("p"), mv_ref[...].astype(BF16))
        for j in range(blocks):
            o_ref[:, :, j * LANES:(j + 1) * LANES] = o[:, j * steps:(j + 1) * steps, :]

    return [scores, probabilities, weighted_values]


def _xattn_cache_body(q_ref, mk_ref, mv_ref, o_ref):
    for piece in _xattn_cache_pieces(q_ref, mk_ref, mv_ref, o_ref):
        piece()


def _interleave_chunks(cache):
    n = cache.shape[0]
    chunks = X_HEAD_DIM // LANES
    c = cache.reshape(n, N_MEM, X_HEADS, chunks, LANES)
    return jnp.transpose(c, (0, 1, 3, 2, 4)).reshape(n, N_MEM * chunks * X_HEADS, LANES)


def _xattn_cache(q, cache_k, cache_v):
    count, s_len, _ = q.shape
    sb = min(XATTN_SEQ_TILE, count)
    q_spec = pl.BlockSpec((sb, s_len, D_MODEL), lambda i: (i, 0, 0))
    rows = cache_k.shape[1]
    mem_spec = pl.BlockSpec((sb, rows, LANES), lambda i: (i, 0, 0))
    return pl.pallas_call(
        _xattn_cache_body,
        grid=(count // sb,),
        in_specs=[q_spec, mem_spec, mem_spec],
        out_specs=q_spec,
        out_shape=jax.ShapeDtypeStruct((count, s_len, D_MODEL), F32),
        compiler_params=_params(1),
    )(q, cache_k, cache_v)


def _mem_kv_body(mem_ref, g_ref, wk_ref, wv_ref, k_ref, v_ref):
    mm = _rms(mem_ref[...], g_ref[...]).astype(BF16)
    k_ref[...] = _dot(mm, wk_ref[...].astype(BF16))
    v_ref[...] = _dot(mm, wv_ref[...].astype(BF16))


def _mem_kv(mem, g, w_ck, w_cv):
    rows = mem.shape[0]
    tm = min(ROW_TILE, rows)
    row_spec = pl.BlockSpec((tm, D_MODEL), lambda i: (i, 0))
    w_spec = _const_spec((D_MODEL, D_MODEL))
    return pl.pallas_call(
        _mem_kv_body,
        grid=(rows // tm,),
        in_specs=[row_spec, _const_spec((1, D_MODEL)), w_spec, w_spec],
        out_specs=[row_spec, row_spec],
        out_shape=[jax.ShapeDtypeStruct((rows, D_MODEL), F32)] * 2,
        compiler_params=_params(1),
    )(mem, g, w_ck, w_cv)


def kernel(x_prompt, x_sample, mem_prompt, cache_mem_k, cache_mem_v, cache_swa_k, cache_swa_v, state_conv, state_lru_h,
           g_ffn1, w1_gate, w1_up, w1_down, g_mix, w_in, conv_w, conv_b, w_a, b_a, w_i, b_i, lam, sink,
           g_lru_out, g_attn_out, w_out, g_xattn, g_mem, w_cq, w_ck, w_cv, w_co, g_ffn2, w2_gate, w2_up, w2_down,
           g_final):
    nbp, seq, _ = x_prompt.shape
    nbs, dec_seq, _ = x_sample.shape
    depth = g_ffn1.shape[0]
    past = PAST_LEN
    cos_p, sin_p = _rope_tables(np.arange(seq, dtype=np.int32))
    sample_pos_rows = min(ROW_TILE, nbs * dec_seq)
    cos_s, sin_s = _rope_tables(np.tile(past + np.arange(dec_seq, dtype=np.int32), sample_pos_rows // dec_seq))

    xp = x_prompt.reshape(nbp * seq, D_MODEL)
    xs = x_sample.reshape(nbs * dec_seq, D_MODEL)
    g_fin = g_final.reshape(1, D_MODEL)
    row = lambda a: a.reshape(1, -1)
    outs = [[] for _ in range(10)]
    for l in range(depth):
        last = l == depth - 1
        bf = lambda a: a[l].astype(BF16)
        win, wout, wcq, wco = bf(w_in), bf(w_out), bf(w_cq), bf(w_co)
        wab, bab = _lru_gate_chunks(w_a[l], b_a[l], w_i[l], b_i[l])
        lru_w = (conv_w[l], row(conv_b[l]), wab, bab, row(lam[l]), row(g_lru_out[l]))

        xs, w1g, w1u, w1d, w2g, w2u, w2d = _ffn_cast(xs, row(g_ffn1[l]), w1_gate[l], w1_up[l], w1_down[l],
                                                     w2_gate[l], w2_up[l], w2_down[l])
        u_s, gate, q, k, v = _proj(xs, row(g_mix[l]), win, cos_s, sin_s)
        conv_pad = jnp.pad(state_conv[l], ((0, 0), (dec_seq - (CONV_WIDTH - 1), 0), (0, 0)))
        h0_rep = jnp.repeat(state_lru_h[l], dec_seq, axis=0)
        per_seq = lambda a: a.reshape(nbs, dec_seq, a.shape[-1])
        xs, xq, hs, new_k, new_v = _sample_mixer(
            u_s, gate, conv_pad.reshape(nbs * dec_seq, LRU_WIDTH), h0_rep, per_seq(q), per_seq(k), per_seq(v),
            _channel_major(cache_swa_k[l]), _channel_major(cache_swa_v[l]), xs,
            sink[l], lru_w, row(g_attn_out[l]), wout, row(g_xattn[l]), wcq, past)
        xq, cache_k, cache_v = per_seq(xq), _interleave_chunks(cache_mem_k[l]), _interleave_chunks(cache_mem_v[l])

        mk_p, mv_p = _mem_kv(mem_prompt.reshape(nbp * N_MEM, D_MODEL), row(g_mem[l]), w_ck[l], w_cv[l])
        mk_p = mk_p.reshape(nbp, N_MEM, D_MODEL)
        mv_p = mv_p.reshape(nbp, N_MEM, D_MODEL)
        xp, q, k, v, lru_out, h_last, u_tail = _ffn_proj_lru(xp, seq, row(g_ffn1[l]), w1g, w1u, w1d, row(g_mix[l]), win,
                                                             cos_p, sin_p, *lru_w)
        k3 = k.reshape(nbp, seq, KV_WIDTH)
        v3 = v.reshape(nbp, seq, KV_WIDTH)
        xp = _swa_mix_ffn(q, k, v, sink[l], row(g_attn_out[l]), xp, lru_out, seq, wout, row(g_xattn[l]), wcq,
                          mk_p, mv_p, wco, row(g_ffn2[l]), w2g, w2u, w2d, g_fin, last)
        outs[0].append(mk_p.reshape(nbp, N_MEM, X_HEADS, X_HEAD_DIM))
        outs[1].append(mv_p.reshape(nbp, N_MEM, X_HEADS, X_HEAD_DIM))
        outs[2].append(k3[:, -WINDOW:].reshape(nbp, WINDOW, KV_HEADS, HEAD_DIM))
        outs[3].append(v3[:, -WINDOW:].reshape(nbp, WINDOW, KV_HEADS, HEAD_DIM))
        outs[4].append(u_tail[:, -(CONV_WIDTH - 1):])
        outs[5].append(h_last.reshape(nbp, LRU_WIDTH))

        xo = _xattn_cache(xq, cache_k, cache_v)
        xs = _ffn(xs, row(g_ffn2[l]), w2g, w2u, w2d, g_fin, last, xo.reshape(nbs * dec_seq, D_MODEL), wco)
        outs[6].append(_position_major(new_k))
        outs[7].append(_position_major(new_v))
        outs[8].append(u_s.reshape(nbs, dec_seq, LRU_WIDTH)[:, -(CONV_WIDTH - 1):])
        outs[9].append(hs.reshape(nbs, dec_seq, LRU_WIDTH)[:, -1])

    return (xp.reshape(nbp, seq, D_MODEL), xs.reshape(nbs, dec_seq, D_MODEL)) + tuple(jnp.stack(o) for o in outs)
```

```python
import functools

import jax
import jax.numpy as jnp
import numpy as np
from jax import lax
from jax.experimental import pallas as pl
from jax.experimental.pallas import tpu as pltpu

F32 = jnp.float32
BF16 = jnp.bfloat16

D_MODEL = 1024
LRU_WIDTH = 512
LRU_BLOCKS = 8
CONV_WIDTH = 4
LRU_C = 8.0
ATTN_HEADS = 8
HEAD_DIM = 64
KV_HEADS = 2
WINDOW = 128
PAST_LEN = 8192
ROPE_THETA = 10000.0
N_MEM = 256
X_HEADS = 4
X_HEAD_DIM = 256
D_FF = 2816
EPS = 1e-6
Q_WIDTH = ATTN_HEADS * HEAD_DIM
KV_WIDTH = KV_HEADS * HEAD_DIM
IN_COLS = 2 * LRU_WIDTH + Q_WIDTH + 2 * KV_WIDTH

LANES = 128
SUBLANES = 8
VMEM_LIMIT = 56 * 1024 * 1024

ROW_TILE = 512
LRU_TILE = 512
FFN_CHUNK = 256
SAMPLE_SEQ_TILE = 32
XATTN_SEQ_TILE = 8


def _params(n_axes):
    return pltpu.CompilerParams(dimension_semantics=("arbitrary",) * n_axes, vmem_limit_bytes=VMEM_LIMIT)


def _const_spec(shape):
    return pl.BlockSpec(shape, lambda *_: (0,) * len(shape), pipeline_mode=pl.Buffered(1))


def _rms(x, g):
    return x * lax.rsqrt(jnp.mean(x * x, axis=-1, keepdims=True) + EPS) * g


def _dot(a, b):
    return jnp.dot(a, b, preferred_element_type=F32)


def _dot_nt(a, b):
    return lax.dot_general(a, b, (((1,), (1,)), ((), ())), preferred_element_type=F32)


def _ffn_step(x, g_ref, wg_ref, wu_ref, wd_ref):
    xn = _rms(x, g_ref[...]).astype(BF16)
    gate = _dot(xn, wg_ref[...])
    up = _dot(xn, wu_ref[...])
    h = (gate * jax.nn.sigmoid(gate) * up).astype(BF16)
    return x + 0.5 * _dot(h, wd_ref[...])


def _ffn_tail(x, g_ref, wg_ref, wu_ref, wd_ref, gf_ref, o_ref, final_norm):
    y = _ffn_step(x, g_ref, wg_ref, wu_ref, wd_ref)
    if final_norm:
        y = _rms(y, gf_ref[...])
    o_ref[...] = y


def _ffn_body(x_ref, g_ref, wg_ref, wu_ref, wd_ref, gf_ref, o_ref, *, final_norm):
    _ffn_tail(x_ref[...], g_ref, wg_ref, wu_ref, wd_ref, gf_ref, o_ref, final_norm)


def _proj_ffn_body(x_ref, a_ref, wa_ref, g_ref, wg_ref, wu_ref, wd_ref, gf_ref, o_ref, *, final_norm):
    x = x_ref[...] + _dot(a_ref[...].astype(BF16), wa_ref[...])
    _ffn_tail(x, g_ref, wg_ref, wu_ref, wd_ref, gf_ref, o_ref, final_norm)


def _ffn(x, g, wg, wu, wd, g_final, final_norm, attn=None, w_attn=None):
    rows = x.shape[0]
    tm = min(ROW_TILE, rows)
    row_spec = pl.BlockSpec((tm, D_MODEL), lambda i: (i, 0))
    ffn_specs = [_const_spec((1, D_MODEL)), _const_spec((D_MODEL, D_FF)), _const_spec((D_MODEL, D_FF)),
                 _const_spec((D_FF, D_MODEL)), _const_spec((1, D_MODEL))]
    if attn is None:
        body, lead_specs, lead = _ffn_body, [row_spec], (x,)
    else:
        body, lead_specs, lead = _proj_ffn_body, [row_spec, row_spec, _const_spec((D_MODEL, D_MODEL))], (x, attn, w_attn)
    return pl.pallas_call(
        functools.partial(body, final_norm=final_norm),
        grid=(rows // tm,),
        in_specs=lead_specs + ffn_specs,
        out_specs=row_spec,
        out_shape=jax.ShapeDtypeStruct((rows, D_MODEL), F32),
        compiler_params=_params(1),
    )(*lead, g, wg, wu, wd, g_final)


def _ffn_cast_body(x_ref, g_ref, wg_ref, wu_ref, wd_ref, og_ref, ou_ref, od_ref, *rest, extra_chunks):
    n_extra = len(extra_chunks)
    extra_in, rest = rest[:n_extra], rest[n_extra:]
    o_ref, wg16_ref, wu16_ref, wd16_ref, og16_ref, ou16_ref, od16_ref = rest[:7]
    extra_out, (xn_ref, acc_ref) = rest[7:7 + n_extra], rest[7 + n_extra:]
    f = pl.program_id(0)

    @pl.when(f == 0)
    def _():
        xn_ref[...] = _rms(x_ref[...], g_ref[...]).astype(BF16)
        acc_ref[...] = jnp.zeros_like(acc_ref)

    for src, dst in ((wg_ref, wg16_ref), (wu_ref, wu16_ref), (wd_ref, wd16_ref),
                     (og_ref, og16_ref), (ou_ref, ou16_ref), (od_ref, od16_ref)):
        dst[...] = src[...].astype(BF16)
    for src, dst, count in zip(extra_in, extra_out, extra_chunks):
        @pl.when(f < count)
        def _(src=src, dst=dst):
            dst[...] = src[...].astype(BF16)
    xn = xn_ref[...]
    gate = _dot(xn, wg16_ref[...])
    up = _dot(xn, wu16_ref[...])
    acc_ref[...] += _dot((gate * jax.nn.sigmoid(gate) * up).astype(BF16), wd16_ref[...])

    @pl.when(f == pl.num_programs(0) - 1)
    def _():
        o_ref[...] = x_ref[...] + 0.5 * acc_ref[...]


def _ffn_cast(x, g, wg, wu, wd, other_g, other_u, other_d, extra):
    rows = x.shape[0]
    chunks = D_FF // FFN_CHUNK
    x_spec = _const_spec((rows, D_MODEL))
    col_spec = pl.BlockSpec((D_MODEL, FFN_CHUNK), lambda f: (0, f))
    row_spec = pl.BlockSpec((FFN_CHUNK, D_MODEL), lambda f: (f, 0))
    w_specs = [col_spec, col_spec, row_spec] * 2
    extra_chunks = tuple(w.shape[1] // FFN_CHUNK for w in extra)
    assert max(extra_chunks) <= chunks
    extra_specs = [pl.BlockSpec((D_MODEL, FFN_CHUNK), lambda f, last=c - 1: (0, jnp.minimum(f, last)))
                   for c in extra_chunks]
    wide, tall = jax.ShapeDtypeStruct((D_MODEL, D_FF), BF16), jax.ShapeDtypeStruct((D_FF, D_MODEL), BF16)
    return pl.pallas_call(
        functools.partial(_ffn_cast_body, extra_chunks=extra_chunks),
        grid=(chunks,),
        in_specs=[x_spec, _const_spec((1, D_MODEL))] + w_specs + extra_specs,
        out_specs=[pl.BlockSpec((rows, D_MODEL), lambda f: (0, 0))] + w_specs + extra_specs,
        out_shape=[jax.ShapeDtypeStruct((rows, D_MODEL), F32)] + [wide, wide, tall] * 2
        + [jax.ShapeDtypeStruct(w.shape, BF16) for w in extra],
        scratch_shapes=[pltpu.VMEM((rows, D_MODEL), BF16), pltpu.VMEM((rows, D_MODEL), F32)],
        compiler_params=_params(1),
    )(x, g, wg, wu, wd, other_g, other_u, other_d, *extra)


def _rope(z, cos, sin_signed):
    half = HEAD_DIM // 2
    lane = lax.broadcasted_iota(jnp.int32, z.shape, 1)
    first_half = (lane % HEAD_DIM) < half
    partner = jnp.where(first_half, pltpu.roll(z, LANES - half, axis=1), pltpu.roll(z, half, axis=1))
    return z * cos + partner * sin_signed


def _project_pieces(x, g_ref, w_ref, cos_ref, sin_ref, store_u, store_gate, q_ref, k_ref, v_ref):
    o_gate, o_q, o_k, o_v = LRU_WIDTH, 2 * LRU_WIDTH, 2 * LRU_WIDTH + Q_WIDTH, 2 * LRU_WIDTH + Q_WIDTH + KV_WIDTH
    xn = []

    def normed():
        if not xn:
            xn.append(_rms(x(), g_ref[...]).astype(BF16))
        return xn[0]

    def rope_into(ref, z):
        for j in range(z.shape[1] // LANES):
            cols = slice(j * LANES, (j + 1) * LANES)
            ref[:, cols] = _rope(z[:, cols], cos_ref[...], sin_ref[...])

    def store_v(z):
        v_ref[...] = z

    return [lambda: rope_into(q_ref, _dot(normed(), w_ref[:, o_q:o_k])),
            lambda: rope_into(k_ref, _dot(normed(), w_ref[:, o_k:o_v])),
            lambda: store_v(_dot(normed(), w_ref[:, o_v:])),
            lambda: store_u(_dot(normed(), w_ref[:, :o_gate])),
            lambda: store_gate(_dot(normed(), w_ref[:, o_gate:o_q]))]


def _proj_body(x_ref, g_ref, w_ref, cos_ref, sin_ref, u_ref, gate_ref, q_ref, k_ref, v_ref):
    def store_u(z):
        u_ref[...] = z

    def store_gate(z):
        gate_ref[...] = z

    for piece in _project_pieces(lambda: x_ref[...], g_ref, w_ref, cos_ref, sin_ref, store_u, store_gate,
                                 q_ref, k_ref, v_ref):
        piece()


def _proj(x, g, w_in, cos, sin):
    rows = x.shape[0]
    tm = min(ROW_TILE, rows, cos.shape[0])
    pos_blocks = cos.shape[0] // tm

    def row_spec(width):
        return pl.BlockSpec((tm, width), lambda i: (i, 0))

    pos_spec = pl.BlockSpec((tm, LANES), lambda i: (i % pos_blocks, 0))
    widths = (LRU_WIDTH, LRU_WIDTH, Q_WIDTH, KV_WIDTH, KV_WIDTH)
    return pl.pallas_call(
        _proj_body,
        grid=(rows // tm,),
        in_specs=[row_spec(D_MODEL), _const_spec((1, D_MODEL)), _const_spec((D_MODEL, IN_COLS)), pos_spec, pos_spec],
        out_specs=[row_spec(w) for w in widths],
        out_shape=[jax.ShapeDtypeStruct((rows, w), F32) for w in widths],
        compiler_params=_params(1),
    )(x, g, w_in, cos, sin)


def _rope_tables(pos):
    half = HEAD_DIM // 2
    inv = ROPE_THETA ** (-np.arange(half, dtype=np.float64) / half)
    ang = pos.astype(np.float64)[:, None] * inv[None, :]
    cos = np.cos(ang)
    sin = np.sin(ang)
    reps = LANES // HEAD_DIM
    return (jnp.asarray(np.tile(np.concatenate([cos, cos], axis=-1), (1, reps)), dtype=F32),
            jnp.asarray(np.tile(np.concatenate([-sin, sin], axis=-1), (1, reps)), dtype=F32))


def _softplus(x):
    return jnp.maximum(x, 0.0) + jnp.log1p(jnp.exp(-jnp.abs(x)))


def _lru_coeffs(conv, wab, bab, lam):
    w = conv.shape[1]
    gates = _dot(conv.astype(BF16), wab) + bab
    r = jax.nn.sigmoid(gates[:, :w])
    gi = jax.nn.sigmoid(gates[:, w:])
    log_a = -LRU_C * r * _softplus(-lam)
    a = jnp.exp(log_a)
    b = jnp.sqrt(-jnp.tanh(log_a) * (a * a + 1.0)) * (gi * conv)
    return a, b


def _segment_scan(a, b, seg):
    step = 1
    while step < seg:
        a, b = _scan_step(a, b, seg, step)
        step *= 2
    return a, b


def _scan_step(a, b, seg, step):
    pos = lax.broadcasted_iota(jnp.int32, a.shape, 0) % seg
    live = pos >= step
    a_prev = pltpu.roll(a, step, axis=0)
    b_prev = pltpu.roll(b, step, axis=0)
    return jnp.where(live, a * a_prev, a), jnp.where(live, a * b_prev + b, b)


def _lru_hidden(conv, h_in, seg, wab, bab, lam):
    a, b = _lru_coeffs(conv, wab, bab, lam)
    a_cum, h_local = _segment_scan(a, b, seg)
    return a_cum * h_in + h_local


def _lane_chunks(width):
    return [slice(c * LANES, (c + 1) * LANES) for c in range(width // LANES)]


def _ffn_pieces(normed, wg_ref, wu_ref, wd_ref, state):
    cols = [slice(f, min(f + FFN_CHUNK, D_FF)) for f in range(0, D_FF, FFN_CHUNK)]

    def gate_up(f):
        return _dot(normed(), wg_ref[:, cols[f]]), _dot(normed(), wu_ref[:, cols[f]])

    def piece(f):
        def run():
            gate, up = state.pop("gate_up") if "gate_up" in state else gate_up(f)
            if f + 1 < len(cols):
                state["gate_up"] = gate_up(f + 1)
            part = _dot((gate * jax.nn.sigmoid(gate) * up).astype(BF16), wd_ref[cols[f], :])
            state["acc"] = part if "acc" not in state else state["acc"] + part
        return run

    return [piece(f) for f in range(len(cols))]


def _interleave(primary, secondary):
    due = [((i + 1) * len(primary)) // (len(secondary) + 1) for i in range(len(secondary))]
    pending = list(zip(due, secondary))
    for i, piece in enumerate(primary):
        while pending and pending[0][0] <= i:
            pending.pop(0)[1]()
        piece()
    for _, piece in pending:
        piece()


def _ffn_proj_lru_body(x_ref, g1_ref, wg_ref, wu_ref, wd_ref, gmix_ref, win_ref, cos_ref, sin_ref,
                       cw_ref, cb_ref, wab_ref, bab_ref, lam_ref, gout_ref,
                       x_out_ref, q_ref, k_ref, v_ref, lru_ref, hlast_ref, utail_ref,
                       ug_ref, ext_ref, h_ref, hs_ref, *, tt, tiles, tiles_per_seq):
    s = pl.program_id(0)
    refs = (x_ref, g1_ref, wg_ref, wu_ref, wd_ref, gmix_ref, win_ref, cos_ref, sin_ref,
            cw_ref, cb_ref, wab_ref, bab_ref, lam_ref, gout_ref,
            x_out_ref, q_ref, k_ref, v_ref, lru_ref, hlast_ref, utail_ref, ug_ref, ext_ref, h_ref, hs_ref)

    @pl.when(s == 0)
    def _():
        ug_ref[1] = jnp.zeros((2, tt, LRU_WIDTH), F32)

    @pl.when((s == 0) | (lax.rem(s - 1, tiles_per_seq) == 0))
    def _():
        ext_ref[0:SUBLANES, :] = jnp.zeros((SUBLANES, LRU_WIDTH), F32)
        h_ref[...] = jnp.zeros_like(h_ref)

    @pl.when(s < tiles)
    def _():
        _ffn_proj_lru_step(*refs, tt=tt, project=True)

    @pl.when(s == tiles)
    def _():
        _ffn_proj_lru_step(*refs, tt=tt, project=False)


def _ffn_proj_lru_step(x_ref, g1_ref, wg_ref, wu_ref, wd_ref, gmix_ref, win_ref, cos_ref, sin_ref,
                       cw_ref, cb_ref, wab_ref, bab_ref, lam_ref, gout_ref,
                       x_out_ref, q_ref, k_ref, v_ref, lru_ref, hlast_ref, utail_ref,
                       ug_ref, ext_ref, h_ref, hs_ref, *, tt, project):
    pad = SUBLANES
    lead_slot = pl.program_id(0) % 2
    lag_slot = 1 - lead_slot

    state = {"sumsq": jnp.zeros((tt, 1), F32)}
    scan_steps = [1 << i for i in range(tt.bit_length() - 1)]

    def lru_pieces(c, cols):
        def coeffs():
            ext_ref[pad:pad + tt, cols] = ug_ref[lag_slot, 0, :, cols]
            conv = cb_ref[:, cols]
            for j in range(CONV_WIDTH):
                start = pad - (CONV_WIDTH - 1) + j
                conv = conv + ext_ref[start:start + tt, cols] * cw_ref[j:j + 1, cols]
            ext_ref[0:pad, cols] = ext_ref[tt:tt + pad, cols]
            utail_ref[0, :, cols] = ext_ref[0:pad, cols]
            state[c] = _lru_coeffs(conv, wab_ref[c], bab_ref[c], lam_ref[:, cols])

        def scan(steps):
            def run():
                for step in steps:
                    state[c] = _scan_step(*state[c], tt, step)
            return run

        def finish():
            a_cum, h_local = state.pop(c)
            hs_ref[:, cols] = a_cum * h_ref[:, cols] + h_local
            h_ref[:, cols] = hs_ref[tt - 1:tt, cols]
            hlast_ref[0, :, cols] = h_ref[:, cols]
            y = hs_ref[:, cols] * jax.nn.gelu(ug_ref[lag_slot, 1, :, cols])
            hs_ref[:, cols] = y
            state["sumsq"] = state["sumsq"] + jnp.sum(y * y, axis=-1, keepdims=True)

        half = len(scan_steps) // 2
        return [coeffs, scan(scan_steps[:half]), scan(scan_steps[half:]), finish]

    vector_pieces = [p for c, cols in enumerate(_lane_chunks(LRU_WIDTH)) for p in lru_pieces(c, cols)]

    def ffn_in():
        if "xn" not in state:
            state["xn"] = _rms(x_ref[...], g1_ref[...]).astype(BF16)
        return state["xn"]

    def ffn_out():
        if "x" not in state:
            state["x"] = x_ref[...] + 0.5 * state.pop("acc")
            x_out_ref[...] = state["x"]
        return state["x"]

    def store_u(z):
        ug_ref[lead_slot, 0] = z

    def store_gate(z):
        ug_ref[lead_slot, 1] = z

    matmul_pieces = _ffn_pieces(ffn_in, wg_ref, wu_ref, wd_ref, state)
    matmul_pieces += _project_pieces(ffn_out, gmix_ref, win_ref, cos_ref, sin_ref, store_u, store_gate,
                                     q_ref, k_ref, v_ref)

    assert len(vector_pieces) == len(matmul_pieces)
    for vector_piece, matmul_piece in zip(vector_pieces, matmul_pieces):
        vector_piece()
        if project:
            matmul_piece()
    lru_ref[...] = hs_ref[...] * lax.rsqrt(state["sumsq"] * (1.0 / LRU_WIDTH) + EPS) * gout_ref[...]


def _ffn_proj_lru(x, seq, g1, wg, wu, wd, g_mix, w_in, cos, sin, conv_w, conv_b, wab, bab, lam, g_out):
    rows = x.shape[0]
    n = rows // seq
    tt = min(LRU_TILE, seq)
    tiles_per_seq = seq // tt
    tiles = rows // tt
    pos_blocks = cos.shape[0] // tt
    lead = lambda s: jnp.minimum(s, tiles - 1)
    lag = lambda s: jnp.maximum(s - 1, 0)

    def lead_spec(width):
        return pl.BlockSpec((tt, width), lambda s: (lead(s), 0))

    pos_spec = pl.BlockSpec((tt, LANES), lambda s: (lead(s) % pos_blocks, 0))
    seq_spec = lambda r: pl.BlockSpec((1, r, LRU_WIDTH), lambda s: (lag(s) // tiles_per_seq, 0, 0))
    widths = (D_MODEL, Q_WIDTH, KV_WIDTH, KV_WIDTH)
    return pl.pallas_call(
        functools.partial(_ffn_proj_lru_body, tt=tt, tiles=tiles, tiles_per_seq=tiles_per_seq),
        grid=(tiles + 1,),
        in_specs=[lead_spec(D_MODEL), _const_spec((1, D_MODEL)), _const_spec((D_MODEL, D_FF)), _const_spec((D_MODEL, D_FF)),
                  _const_spec((D_FF, D_MODEL)), _const_spec((1, D_MODEL)), _const_spec((D_MODEL, IN_COLS)),
                  pos_spec, pos_spec] + _lru_weight_specs(),
        out_specs=[lead_spec(w) for w in widths]
        + [pl.BlockSpec((tt, LRU_WIDTH), lambda s: (lag(s), 0)), seq_spec(1), seq_spec(SUBLANES)],
        out_shape=[jax.ShapeDtypeStruct((rows, w), F32) for w in widths]
        + [jax.ShapeDtypeStruct((rows, LRU_WIDTH), F32), jax.ShapeDtypeStruct((n, 1, LRU_WIDTH), F32),
           jax.ShapeDtypeStruct((n, SUBLANES, LRU_WIDTH), F32)],
        scratch_shapes=[pltpu.VMEM((2, 2, tt, LRU_WIDTH), F32), pltpu.VMEM((tt + SUBLANES, LRU_WIDTH), F32),
                        pltpu.VMEM((1, LRU_WIDTH), F32), pltpu.VMEM((tt, LRU_WIDTH), F32)],
        compiler_params=_params(1),
    )(x, g1, wg, wu, wd, g_mix, w_in, cos, sin, conv_w, conv_b, wab, bab, lam, g_out)


def _lru_sample_body(u_ref, gate_ref, cpad_ref, h0_ref, cw_ref, cb_ref, wab_ref, bab_ref, lam_ref, gout_ref,
                     o_ref, hs_ref, *, seg):
    rows = u_ref.shape[0]
    pos = lax.broadcasted_iota(jnp.int32, (rows, LANES), 0) % seg
    sumsq = jnp.zeros((rows, 1), F32)
    for c, cols in enumerate(_lane_chunks(LRU_WIDTH)):
        u = u_ref[:, cols]
        cpad = cpad_ref[:, cols]
        conv = cb_ref[:, cols] + u * cw_ref[CONV_WIDTH - 1:CONV_WIDTH, cols]
        for back in range(1, CONV_WIDTH):
            shifted = jnp.where(pos >= back, pltpu.roll(u, back, axis=0),
                                pltpu.roll(cpad, (back - seg) % rows, axis=0))
            conv = conv + shifted * cw_ref[CONV_WIDTH - 1 - back:CONV_WIDTH - back, cols]
        h = _lru_hidden(conv, h0_ref[:, cols], seg, wab_ref[c], bab_ref[c], lam_ref[:, cols])
        hs_ref[:, cols] = h
        y = h * jax.nn.gelu(gate_ref[:, cols])
        o_ref[:, cols] = y
        sumsq = sumsq + jnp.sum(y * y, axis=-1, keepdims=True)
    o_ref[...] = o_ref[...] * lax.rsqrt(sumsq * (1.0 / LRU_WIDTH) + EPS) * gout_ref[...]


def _lru_weight_specs():
    chunks = LRU_WIDTH // LANES
    return [_const_spec((CONV_WIDTH, LRU_WIDTH)), _const_spec((1, LRU_WIDTH)),
            _const_spec((chunks, LANES, 2 * LANES)), _const_spec((chunks, 1, 2 * LANES)),
            _const_spec((1, LRU_WIDTH)), _const_spec((1, LRU_WIDTH))]


def _lru_gate_chunks(w_a, b_a, w_i, b_i):
    chunks = LRU_WIDTH // LANES
    per = LRU_BLOCKS // chunks
    wa = w_a.reshape(chunks, per, *w_a.shape[1:])
    wi = w_i.reshape(chunks, per, *w_i.shape[1:])
    wab = jnp.stack([jnp.concatenate([_block_diag(wa[c]), _block_diag(wi[c])], axis=1) for c in range(chunks)])
    bab = jnp.concatenate([b_a.reshape(chunks, 1, LANES), b_i.reshape(chunks, 1, LANES)], axis=2)
    return wab.astype(BF16), bab


def _block_diag(w):
    nb, bi, bj = w.shape
    eye = jnp.eye(nb, dtype=w.dtype)
    return jnp.einsum('gij,gh->gihj', w, eye).reshape(nb * bi, nb * bj)


def _bdot_nt(a, b):
    return lax.dot_general(a, b, (((2,), (2,)), ((0,), (0,))), preferred_element_type=F32)


def _bdot(a, b):
    return lax.dot_general(a, b, (((2,), (1,)), ((0,), (0,))), preferred_element_type=F32)


def _swa_pieces(q, score, weigh, sink_ref, mask, g_out, store):
    group = ATTN_HEADS // KV_HEADS
    straight = [h for h in range(ATTN_HEADS) if (h % 2) == (h // group)]
    swapped = [h for h in range(ATTN_HEADS) if (h % 2) != (h // group)]
    scale = HEAD_DIM ** -0.5
    out_half = {}

    def low_lanes(shape):
        return lax.broadcasted_iota(jnp.int32, shape, 2) < HEAD_DIM

    def head_group(heads, swap):
        held = {}

        def scores():
            qv = q()
            n, r, _ = qv.shape
            low = low_lanes((n, r, LANES))
            zero = jnp.zeros((n, r, LANES), F32)
            qs = jnp.concatenate(
                [jnp.where(low if h % 2 == 0 else ~low, qv[:, :, (h // 2) * LANES:(h // 2 + 1) * LANES], zero)
                 for h in heads], axis=1)
            held["s"] = score(qs.astype(BF16), swap) * scale

        def probabilities():
            s = held.pop("s")
            r = s.shape[1] // len(heads)
            visible = mask()[None]
            probs = []
            for i, h in enumerate(heads):
                sh = jnp.where(visible, s[:, i * r:(i + 1) * r, :], -jnp.inf)
                sink = sink_ref[h]
                m = jnp.maximum(jnp.max(sh, axis=-1, keepdims=True), sink)
                e = jnp.exp(sh - m)
                denom = jnp.sum(e, axis=-1, keepdims=True) + jnp.exp(sink - m)
                probs.append(e / denom)
            held["p"] = jnp.concatenate(probs, axis=1).astype(BF16)

        def weighted_values():
            p = held.pop("p")
            r = p.shape[1] // len(heads)
            o = weigh(p, swap)
            for i, h in enumerate(heads):
                out_half[h] = o[:, i * r:(i + 1) * r, :]

        return [scores, probabilities, weighted_values]

    def finish():
        low = low_lanes(out_half[0].shape)
        out = jnp.concatenate([jnp.where(low, out_half[2 * j], out_half[2 * j + 1])
                               for j in range(Q_WIDTH // LANES)], axis=2)
        store(_rms(out, g_out))

    return head_group(straight, False) + head_group(swapped, True) + [finish]


def _band_mask(first_block):
    i = lax.broadcasted_iota(jnp.int32, (WINDOW, 2 * WINDOW), 0)
    j = lax.broadcasted_iota(jnp.int32, (WINDOW, 2 * WINDOW), 1)
    dist = i + WINDOW - j
    return (dist >= 0) & (dist < WINDOW) & (jnp.logical_not(first_block) | (j >= WINDOW))


def _swa_sample_body(sink_ref, q_ref, k_ref, v_ref, bk_ref, bv_ref, gout_ref, o_ref, nk_ref, nv_ref, *, s_len, past):
    def mask():
        qp = past + lax.broadcasted_iota(jnp.int32, (s_len, WINDOW + s_len), 0)
        col = lax.broadcasted_iota(jnp.int32, (s_len, WINDOW + s_len), 1)
        kp = jnp.where(col < WINDOW, past - WINDOW + col, past + col - WINDOW)
        dist = qp - kp
        return (dist >= 0) & (dist < WINDOW) & (kp >= 0)

    def halves(z, axis, swap):
        return pltpu.roll(z, HEAD_DIM, axis=axis) if swap else z

    def score(qs, swap):
        carried = _bdot(qs, halves(bk_ref[...], 1, swap).astype(BF16))
        fresh = _bdot_nt(qs, halves(k_ref[...], 2, swap).astype(BF16))
        return jnp.concatenate([carried, fresh], axis=2)

    def weigh(p, swap):
        return (_bdot_nt(p[:, :, :WINDOW], halves(bv_ref[...], 1, swap).astype(BF16))
                + _bdot(p[:, :, WINDOW:], halves(v_ref[...], 2, swap).astype(BF16)))

    def store(o):
        o_ref[...] = o.reshape(o_ref.shape)

    for piece in _swa_pieces(lambda: q_ref[...], score, weigh, sink_ref, mask, gout_ref[...], store):
        piece()

    def slide(window_ref, fresh_ref, out_ref):
        n = window_ref.shape[0]
        tail = jnp.concatenate([jnp.zeros((n, WINDOW - s_len, KV_WIDTH), F32), fresh_ref[...]], axis=1)
        lane = lax.broadcasted_iota(jnp.int32, (n, KV_WIDTH, WINDOW), 2)
        out_ref[...] = jnp.where(lane >= WINDOW - s_len, jnp.swapaxes(tail, 1, 2),
                                 pltpu.roll(window_ref[...], WINDOW - s_len, axis=2))

    slide(bk_ref, k_ref, nk_ref)
    slide(bv_ref, v_ref, nv_ref)


def _softmax(s):
    e = jnp.exp(s - jnp.max(s, axis=-1, keepdims=True))
    return e / jnp.sum(e, axis=-1, keepdims=True)


def _mix_and_query(x_ref, lru_ref, attn_ref, wout_ref, gx_ref, wcq_ref):
    x = (x_ref[...] + _dot(lru_ref[...].astype(BF16), wout_ref[:LRU_WIDTH, :])
         + _dot(attn_ref[...].astype(BF16), wout_ref[LRU_WIDTH:, :]))
    return x, _dot(_rms(x, gx_ref[...]).astype(BF16), wcq_ref[...])


def _swa_mix_ffn_body(sink_ref, q_ref, k_ref, v_ref, kp_ref, vp_ref, gattn_ref,
                      x_ref, lru_ref, wout_ref, gx_ref, wcq_ref, mk_ref, mv_ref, wco_ref,
                      g2_ref, wg_ref, wu_ref, wd_ref, gf_ref, o_ref, attn_ref,
                      *, tt, tiles, tiles_per_seq, final_norm):
    s = pl.program_id(0)
    refs = (sink_ref, q_ref, k_ref, v_ref, kp_ref, vp_ref, gattn_ref, x_ref, lru_ref, wout_ref, gx_ref, wcq_ref,
            mk_ref, mv_ref, wco_ref, g2_ref, wg_ref, wu_ref, wd_ref, gf_ref, o_ref, attn_ref)
    step = functools.partial(_swa_mix_ffn_step, *refs, tt=tt, tiles_per_seq=tiles_per_seq, final_norm=final_norm)

    @pl.when(s == 0)
    def _():
        step(attend=True, layer=False)

    @pl.when((s > 0) & (s < tiles))
    def _():
        step(attend=True, layer=True)

    @pl.when(s == tiles)
    def _():
        step(attend=False, layer=True)


def _swa_mix_ffn_step(sink_ref, q_ref, k_ref, v_ref, kp_ref, vp_ref, gattn_ref,
                      x_ref, lru_ref, wout_ref, gx_ref, wcq_ref, mk_ref, mv_ref, wco_ref,
                      g2_ref, wg_ref, wu_ref, wd_ref, gf_ref, o_ref, attn_ref,
                      *, tt, tiles_per_seq, final_norm, attend, layer):
    s = pl.program_id(0)
    lead_slot = s % 2
    lag_slot = 1 - lead_slot

    blocks = tt // WINDOW
    first_pos_block = lax.rem(s, tiles_per_seq) * blocks
    attention_pieces = []
    for j in range(blocks):
        rows = slice(j * WINDOW, (j + 1) * WINDOW)

        def band(ref, prev_ref, swap, j=j, rows=rows):
            prev = prev_ref[...] if j == 0 else ref[(j - 1) * WINDOW:j * WINDOW, :]
            both = jnp.concatenate([prev, ref[rows, :]], axis=0)[None]
            return (pltpu.roll(both, HEAD_DIM, axis=2) if swap else both).astype(BF16)

        def score(qs, swap, band=band):
            return _bdot_nt(qs, band(k_ref, kp_ref, swap))

        def weigh(p, swap, band=band):
            return _bdot(p, band(v_ref, vp_ref, swap))

        def store(o, rows=rows):
            attn_ref[lead_slot, rows, :] = o[0]

        attention_pieces += _swa_pieces(lambda rows=rows: q_ref[rows, :][None], score, weigh, sink_ref,
                                        lambda j=j: _band_mask(first_pos_block + j == 0), gattn_ref[...], store)

    state = {}
    scale = X_HEAD_DIM ** -0.5

    def mix():
        state["x"] = (x_ref[...] + _dot(lru_ref[...].astype(BF16), wout_ref[:LRU_WIDTH, :])
                      + _dot(attn_ref[lag_slot].astype(BF16), wout_ref[LRU_WIDTH:, :]))

    def query():
        state["q"] = _dot(_rms(state["x"], gx_ref[...]).astype(BF16), wcq_ref[...])

    def memory_head(h):
        def run():
            cols = slice(h * X_HEAD_DIM, (h + 1) * X_HEAD_DIM)
            sc = _dot_nt(state["q"][:, cols].astype(BF16), mk_ref[0, :, cols].astype(BF16)) * scale
            state["o", h] = _dot(_softmax(sc).astype(BF16), mv_ref[0, :, cols].astype(BF16))
        return run

    def memory_out():
        o = jnp.concatenate([state.pop(("o", h)) for h in range(X_HEADS)], axis=1)
        state["x"] = state["x"] + _dot(o.astype(BF16), wco_ref[...])
        state["xn"] = _rms(state["x"], g2_ref[...]).astype(BF16)

    def finish():
        y = state["x"] + 0.5 * state["acc"]
        o_ref[...] = _rms(y, gf_ref[...]) if final_norm else y

    layer_pieces = ([mix, query] + [memory_head(h) for h in range(X_HEADS)] + [memory_out]
                    + _ffn_pieces(lambda: state["xn"], wg_ref, wu_ref, wd_ref, state) + [finish])
    _interleave(layer_pieces if layer else [], attention_pieces if attend else [])


def _swa_mix_ffn(q, k, v, sink, g_attn, x, lru_out, seq, w_out, g_x, w_cq, mk, mv, w_co, g2, wg, wu, wd, g_final,
                 final_norm):
    rows = x.shape[0]
    tt = min(LRU_TILE, seq)
    tiles_per_seq = seq // tt
    tiles = rows // tt
    blocks = tt // WINDOW
    lead = lambda s: jnp.minimum(s, tiles - 1)
    lag = lambda s: jnp.maximum(s - 1, 0)

    def lead_spec(width):
        return pl.BlockSpec((tt, width), lambda s: (lead(s), 0))

    def lag_spec(width):
        return pl.BlockSpec((tt, width), lambda s: (lag(s), 0))

    prev_spec = pl.BlockSpec((WINDOW, KV_WIDTH), lambda s: (jnp.maximum(lead(s) * blocks - 1, 0), 0))
    mem_spec = pl.BlockSpec((1, N_MEM, D_MODEL), lambda s: (lag(s) // tiles_per_seq, 0, 0))
    w_spec = _const_spec((D_MODEL, D_MODEL))
    return pl.pallas_call(
        functools.partial(_swa_mix_ffn_body, tt=tt, tiles=tiles, tiles_per_seq=tiles_per_seq, final_norm=final_norm),
        grid=(tiles + 1,),
        in_specs=[pl.BlockSpec(memory_space=pltpu.SMEM), lead_spec(Q_WIDTH), lead_spec(KV_WIDTH), lead_spec(KV_WIDTH),
                  prev_spec, prev_spec, _const_spec((1, Q_WIDTH)),
                  lag_spec(D_MODEL), lag_spec(LRU_WIDTH), w_spec, _const_spec((1, D_MODEL)), w_spec,
                  mem_spec, mem_spec, w_spec,
                  _const_spec((1, D_MODEL)), _const_spec((D_MODEL, D_FF)), _const_spec((D_MODEL, D_FF)),
                  _const_spec((D_FF, D_MODEL)), _const_spec((1, D_MODEL))],
        out_specs=lag_spec(D_MODEL),
        out_shape=jax.ShapeDtypeStruct((rows, D_MODEL), F32),
        scratch_shapes=[pltpu.VMEM((2, tt, Q_WIDTH), F32)],
        compiler_params=_params(1),
    )(sink, q, k, v, k, v, g_attn, x, lru_out, w_out, g_x, w_cq, mk, mv, w_co, g2, wg, wu, wd, g_final)


def _sample_mixer_body(sink_ref, u_ref, gate_ref, cpad_ref, h0_ref, q_ref, k_ref, v_ref, bk_ref, bv_ref, x_ref,
                       cw_ref, cb_ref, wab_ref, bab_ref, lam_ref, glru_ref, gattn_ref, wout_ref, gx_ref, wcq_ref,
                       x_out_ref, xq_ref, hs_ref, nk_ref, nv_ref, lru_ref, attn_ref, *, seg, past):
    _lru_sample_body(u_ref, gate_ref, cpad_ref, h0_ref, cw_ref, cb_ref, wab_ref, bab_ref, lam_ref, glru_ref,
                     lru_ref, hs_ref, seg=seg)
    _swa_sample_body(sink_ref, q_ref, k_ref, v_ref, bk_ref, bv_ref, gattn_ref, attn_ref, nk_ref, nv_ref,
                     s_len=seg, past=past)
    x_out_ref[...], xq_ref[...] = _mix_and_query(x_ref, lru_ref, attn_ref, wout_ref, gx_ref, wcq_ref)


def _channel_major(window):
    n = window.shape[0]
    return jnp.transpose(window, (0, 2, 3, 1)).reshape(n, KV_WIDTH, WINDOW)


def _position_major(window):
    n = window.shape[0]
    return jnp.transpose(window.reshape(n, KV_HEADS, HEAD_DIM, WINDOW), (0, 3, 1, 2))


def _sample_mixer(u, gate, conv_pad, h0_rep, q, k, v, buf_k, buf_v, x, sink, lru_w, g_attn, w_out, g_x, w_cq, past):
    n, seg, _ = q.shape
    sb = min(SAMPLE_SEQ_TILE, n)
    rows = sb * seg

    def row_spec(width):
        return pl.BlockSpec((rows, width), lambda i: (i, 0))

    def seq_spec(steps, width):
        return pl.BlockSpec((sb, steps, width), lambda i: (i, 0, 0))

    buf_spec = seq_spec(WINDOW, KV_WIDTH)
    w_spec = _const_spec((D_MODEL, D_MODEL))
    total = n * seg
    return pl.pallas_call(
        functools.partial(_sample_mixer_body, seg=seg, past=past),
        grid=(n // sb,),
        in_specs=[pl.BlockSpec(memory_space=pltpu.SMEM)] + [row_spec(LRU_WIDTH)] * 4
        + [seq_spec(seg, Q_WIDTH), seq_spec(seg, KV_WIDTH), seq_spec(seg, KV_WIDTH), buf_spec, buf_spec,
           row_spec(D_MODEL)] + _lru_weight_specs()
        + [_const_spec((1, Q_WIDTH)), w_spec, _const_spec((1, D_MODEL)), w_spec],
        out_specs=[row_spec(D_MODEL), row_spec(D_MODEL), row_spec(LRU_WIDTH), buf_spec, buf_spec],
        out_shape=[jax.ShapeDtypeStruct((total, D_MODEL), F32), jax.ShapeDtypeStruct((total, D_MODEL), F32),
                   jax.ShapeDtypeStruct((total, LRU_WIDTH), F32),
                   jax.ShapeDtypeStruct((n, WINDOW, KV_WIDTH), F32), jax.ShapeDtypeStruct((n, WINDOW, KV_WIDTH), F32)],
        scratch_shapes=[pltpu.VMEM((rows, LRU_WIDTH), F32), pltpu.VMEM((rows, Q_WIDTH), F32)],
        compiler_params=_params(1),
    )(sink, u, gate, conv_pad, h0_rep, q, k, v, buf_k, buf_v, x, *lru_w, g_attn, w_out, g_x, w_cq)


def _xattn_cache_pieces(q_ref, mk_ref, mv_ref, o_ref):
    sb, steps, _ = q_ref.shape
    blocks = D_MODEL // LANES
    chunks = X_HEAD_DIM // LANES
    width = mk_ref.shape[1]
    block_cls = [(j % chunks) * X_HEADS + j // chunks for j in range(blocks)]
    scale = X_HEAD_DIM ** -0.5
    held = {}

    def lane_class():
        return lax.broadcasted_iota(jnp.int32, (sb, steps, width), 2) % blocks

    def scores():
        qs = jnp.concatenate([q_ref[:, :, j * LANES:(j + 1) * LANES] for j in range(blocks)], axis=1)
        held["s"] = _bdot_nt(qs.astype(BF16), mk_ref[...].astype(BF16)) * scale

    def probabilities():
        s = held.pop("s")
        cls = lane_class()
        part = jnp.zeros((sb, steps, width), F32)
        for j in range(blocks):
            part = part + jnp.where(cls == block_cls[j], s[:, j * steps:(j + 1) * steps, :], 0.0)
        score = part + pltpu.roll(part, width - X_HEADS, axis=2)
        top = jnp.zeros((sb, steps, width), F32)
        for h in range(X_HEADS):
            mine = cls == h
            top = jnp.where(mine, jnp.max(jnp.where(mine, score, -jnp.inf), axis=-1, keepdims=True), top)
        e = jnp.where(cls < X_HEADS, jnp.exp(score - top), 0.0)
        denom = jnp.ones((sb, steps, width), F32)
        for h in range(X_HEADS):
            mine = cls == h
            denom = jnp.where(mine, jnp.sum(jnp.where(mine, e, 0.0), axis=-1, keepdims=True), denom)
        p = e / denom
        p = p + pltpu.roll(p, X_HEADS, axis=2)
        held["p"] = jnp.concatenate([jnp.where(cls == block_cls[j], p, 0.0) for j in range(blocks)],
                                    axis=1).astype(BF16)

    def weighted_values():
        o = _bdot(held.pop("p"), mv_ref[...].astype(BF16))
        for j in range(blocks):
            o_ref[:, :, j * LANES:(j + 1) * LANES] = o[:, j * steps:(j + 1) * steps, :]

    return [scores, probabilities, weighted_values]


def _xattn_cache_body(q_ref, mk_ref, mv_ref, o_ref):
    for piece in _xattn_cache_pieces(q_ref, mk_ref, mv_ref, o_ref):
        piece()


def _interleave_chunks(cache):
    n = cache.shape[0]
    chunks = X_HEAD_DIM // LANES
    c = cache.reshape(n, N_MEM, X_HEADS, chunks, LANES)
    return jnp.transpose(c, (0, 1, 3, 2, 4)).reshape(n, N_MEM * chunks * X_HEADS, LANES)


def _xattn_cache(q, cache_k, cache_v):
    count, s_len, _ = q.shape
    sb = min(XATTN_SEQ_TILE, count)
    q_spec = pl.BlockSpec((sb, s_len, D_MODEL), lambda i: (i, 0, 0))
    rows = cache_k.shape[1]
    mem_spec = pl.BlockSpec((sb, rows, LANES), lambda i: (i, 0, 0))
    return pl.pallas_call(
        _xattn_cache_body,
        grid=(count // sb,),
        in_specs=[q_spec, mem_spec, mem_spec],
        out_specs=q_spec,
        out_shape=jax.ShapeDtypeStruct((count, s_len, D_MODEL), F32),
        compiler_params=_params(1),
    )(q, cache_k, cache_v)


def _mem_kv_body(mem_ref, g_ref, wk_ref, wv_ref, k_ref, v_ref):
    mm = _rms(mem_ref[...], g_ref[...]).astype(BF16)
    k_ref[...] = _dot(mm, wk_ref[...].astype(BF16))
    v_ref[...] = _dot(mm, wv_ref[...].astype(BF16))


def _mem_kv(mem, g, w_ck, w_cv):
    rows = mem.shape[0]
    tm = min(ROW_TILE, rows)
    row_spec = pl.BlockSpec((tm, D_MODEL), lambda i: (i, 0))
    w_spec = _const_spec((D_MODEL, D_MODEL))
    return pl.pallas_call(
        _mem_kv_body,
        grid=(rows // tm,),
        in_specs=[row_spec, _const_spec((1, D_MODEL)), w_spec, w_spec],
        out_specs=[row_spec, row_spec],
        out_shape=[jax.ShapeDtypeStruct((rows, D_MODEL), F32)] * 2,
        compiler_params=_params(1),
    )(mem, g, w_ck, w_cv)


def kernel(x_prompt, x_sample, mem_prompt, cache_mem_k, cache_mem_v, cache_swa_k, cache_swa_v, state_conv, state_lru_h,
           g_ffn1, w1_gate, w1_up, w1_down, g_mix, w_in, conv_w, conv_b, w_a, b_a, w_i, b_i, lam, sink,
           g_lru_out, g_attn_out, w_out, g_xattn, g_mem, w_cq, w_ck, w_cv, w_co, g_ffn2, w2_gate, w2_up, w2_down,
           g_final):
    nbp, seq, _ = x_prompt.shape
    nbs, dec_seq, _ = x_sample.shape
    depth = g_ffn1.shape[0]
    past = PAST_LEN
    cos_p, sin_p = _rope_tables(np.arange(seq, dtype=np.int32))
    sample_pos_rows = min(ROW_TILE, nbs * dec_seq)
    cos_s, sin_s = _rope_tables(np.tile(past + np.arange(dec_seq, dtype=np.int32), sample_pos_rows // dec_seq))

    xp = x_prompt.reshape(nbp * seq, D_MODEL)
    xs = x_sample.reshape(nbs * dec_seq, D_MODEL)
    g_fin = g_final.reshape(1, D_MODEL)
    row = lambda a: a.reshape(1, -1)
    outs = [[] for _ in range(10)]
    for l in range(depth):
        last = l == depth - 1
        wab, bab = _lru_gate_chunks(w_a[l], b_a[l], w_i[l], b_i[l])
        lru_w = (conv_w[l], row(conv_b[l]), wab, bab, row(lam[l]), row(g_lru_out[l]))

        xs, w1g, w1u, w1d, w2g, w2u, w2d, win, wout, wcq, wco = _ffn_cast(
            xs, row(g_ffn1[l]), w1_gate[l], w1_up[l], w1_down[l], w2_gate[l], w2_up[l], w2_down[l],
            (w_in[l], w_out[l], w_cq[l], w_co[l]))
        u_s, gate, q, k, v = _proj(xs, row(g_mix[l]), win, cos_s, sin_s)
        conv_pad = jnp.pad(state_conv[l], ((0, 0), (dec_seq - (CONV_WIDTH - 1), 0), (0, 0)))
        h0_rep = jnp.repeat(state_lru_h[l], dec_seq, axis=0)
        per_seq = lambda a: a.reshape(nbs, dec_seq, a.shape[-1])
        xs, xq, hs, new_k, new_v = _sample_mixer(
            u_s, gate, conv_pad.reshape(nbs * dec_seq, LRU_WIDTH), h0_rep, per_seq(q), per_seq(k), per_seq(v),
            _channel_major(cache_swa_k[l]), _channel_major(cache_swa_v[l]), xs,
            sink[l], lru_w, row(g_attn_out[l]), wout, row(g_xattn[l]), wcq, past)
        xq, cache_k, cache_v = per_seq(xq), _interleave_chunks(cache_mem_k[l]), _interleave_chunks(cache_mem_v[l])

        mk_p, mv_p = _mem_kv(mem_prompt.reshape(nbp * N_MEM, D_MODEL), row(g_mem[l]), w_ck[l], w_cv[l])
        mk_p = mk_p.reshape(nbp, N_MEM, D_MODEL)
        mv_p = mv_p.reshape(nbp, N_MEM, D_MODEL)
        xp, q, k, v, lru_out, h_last, u_tail = _ffn_proj_lru(xp, seq, row(g_ffn1[l]), w1g, w1u, w1d, row(g_mix[l]), win,
                                                             cos_p, sin_p, *lru_w)
        k3 = k.reshape(nbp, seq, KV_WIDTH)
        v3 = v.reshape(nbp, seq, KV_WIDTH)
        xp = _swa_mix_ffn(q, k, v, sink[l], row(g_attn_out[l]), xp, lru_out, seq, wout, row(g_xattn[l]), wcq,
                          mk_p, mv_p, wco, row(g_ffn2[l]), w2g, w2u, w2d, g_fin, last)
        outs[0].append(mk_p.reshape(nbp, N_MEM, X_HEADS, X_HEAD_DIM))
        outs[1].append(mv_p.reshape(nbp, N_MEM, X_HEADS, X_HEAD_DIM))
        outs[2].append(k3[:, -WINDOW:].reshape(nbp, WINDOW, KV_HEADS, HEAD_DIM))
        outs[3].append(v3[:, -WINDOW:].reshape(nbp, WINDOW, KV_HEADS, HEAD_DIM))
        outs[4].append(u_tail[:, -(CONV_WIDTH - 1):])
        outs[5].append(h_last.reshape(nbp, LRU_WIDTH))

        xo = _xattn_cache(xq, cache_k, cache_v)
        xs = _ffn(xs, row(g_ffn2[l]), w2g, w2u, w2d, g_fin, last, xo.reshape(nbs * dec_seq, D_MODEL), wco)
        outs[6].append(_position_major(new_k))
        outs[7].append(_position_major(new_v))
        outs[8].append(u_s.reshape(nbs, dec_seq, LRU_WIDTH)[:, -(CONV_WIDTH - 1):])
        outs[9].append(hs.reshape(nbs, dec_seq, LRU_WIDTH)[:, -1])

    return (xp.reshape(nbp, seq, D_MODEL), xs.reshape(nbs, dec_seq, D_MODEL)) + tuple(jnp.stack(o) for o in outs)
```

```python
import functools

import jax
import jax.numpy as jnp
import numpy as np
from jax import lax
from jax.experimental import pallas as pl
from jax.experimental.pallas import tpu as pltpu

F32 = jnp.float32
BF16 = jnp.bfloat16

D_MODEL = 1024
LRU_WIDTH = 512
LRU_BLOCKS = 8
CONV_WIDTH = 4
LRU_C = 8.0
ATTN_HEADS = 8
HEAD_DIM = 64
KV_HEADS = 2
WINDOW = 128
PAST_LEN = 8192
ROPE_THETA = 10000.0
N_MEM = 256
X_HEADS = 4
X_HEAD_DIM = 256
D_FF = 2816
EPS = 1e-6
Q_WIDTH = ATTN_HEADS * HEAD_DIM
KV_WIDTH = KV_HEADS * HEAD_DIM
IN_COLS = 2 * LRU_WIDTH + Q_WIDTH + 2 * KV_WIDTH

LANES = 128
SUBLANES = 8
VMEM_LIMIT = 56 * 1024 * 1024

ROW_TILE = 512
LRU_TILE = 512
FFN_CHUNK = 256
SAMPLE_SEQ_TILE = 32
XATTN_SEQ_TILE = 8


def _params(n_axes):
    return pltpu.CompilerParams(dimension_semantics=("arbitrary",) * n_axes, vmem_limit_bytes=VMEM_LIMIT)


def _const_spec(shape):
    return pl.BlockSpec(shape, lambda *_: (0,) * len(shape), pipeline_mode=pl.Buffered(1))


def _rms(x, g):
    return x * lax.rsqrt(jnp.mean(x * x, axis=-1, keepdims=True) + EPS) * g


def _dot(a, b):
    return jnp.dot(a, b, preferred_element_type=F32)


def _dot_nt(a, b):
    return lax.dot_general(a, b, (((1,), (1,)), ((), ())), preferred_element_type=F32)


def _ffn_step(x, g_ref, wg_ref, wu_ref, wd_ref):
    xn = _rms(x, g_ref[...]).astype(BF16)
    gate = _dot(xn, wg_ref[...])
    up = _dot(xn, wu_ref[...])
    h = (gate * jax.nn.sigmoid(gate) * up).astype(BF16)
    return x + 0.5 * _dot(h, wd_ref[...])


def _ffn_tail(x, g_ref, wg_ref, wu_ref, wd_ref, gf_ref, o_ref, final_norm):
    y = _ffn_step(x, g_ref, wg_ref, wu_ref, wd_ref)
    if final_norm:
        y = _rms(y, gf_ref[...])
    o_ref[...] = y


def _ffn_body(x_ref, g_ref, wg_ref, wu_ref, wd_ref, gf_ref, o_ref, *, final_norm):
    _ffn_tail(x_ref[...], g_ref, wg_ref, wu_ref, wd_ref, gf_ref, o_ref, final_norm)


def _proj_ffn_body(x_ref, a_ref, wa_ref, g_ref, wg_ref, wu_ref, wd_ref, gf_ref, o_ref, *, final_norm):
    x = x_ref[...] + _dot(a_ref[...].astype(BF16), wa_ref[...])
    _ffn_tail(x, g_ref, wg_ref, wu_ref, wd_ref, gf_ref, o_ref, final_norm)


def _ffn(x, g, wg, wu, wd, g_final, final_norm, attn=None, w_attn=None):
    rows = x.shape[0]
    tm = min(ROW_TILE, rows)
    row_spec = pl.BlockSpec((tm, D_MODEL), lambda i: (i, 0))
    ffn_specs = [_const_spec((1, D_MODEL)), _const_spec((D_MODEL, D_FF)), _const_spec((D_MODEL, D_FF)),
                 _const_spec((D_FF, D_MODEL)), _const_spec((1, D_MODEL))]
    if attn is None:
        body, lead_specs, lead = _ffn_body, [row_spec], (x,)
    else:
        body, lead_specs, lead = _proj_ffn_body, [row_spec, row_spec, _const_spec((D_MODEL, D_MODEL))], (x, attn, w_attn)
    return pl.pallas_call(
        functools.partial(body, final_norm=final_norm),
        grid=(rows // tm,),
        in_specs=lead_specs + ffn_specs,
        out_specs=row_spec,
        out_shape=jax.ShapeDtypeStruct((rows, D_MODEL), F32),
        compiler_params=_params(1),
    )(*lead, g, wg, wu, wd, g_final)


def _ffn_cast_body(x_ref, g_ref, wg_ref, wu_ref, wd_ref, o_ref, wg16_ref, wu16_ref, wd16_ref, xn_ref, acc_ref):
    f = pl.program_id(0)

    @pl.when(f == 0)
    def _():
        xn_ref[...] = _rms(x_ref[...], g_ref[...]).astype(BF16)
        acc_ref[...] = jnp.zeros_like(acc_ref)

    for src, dst in ((wg_ref, wg16_ref), (wu_ref, wu16_ref), (wd_ref, wd16_ref)):
        dst[...] = src[...].astype(BF16)
    xn = xn_ref[...]
    gate = _dot(xn, wg16_ref[...])
    up = _dot(xn, wu16_ref[...])
    acc_ref[...] += _dot((gate * jax.nn.sigmoid(gate) * up).astype(BF16), wd16_ref[...])

    @pl.when(f == pl.num_programs(0) - 1)
    def _():
        o_ref[...] = x_ref[...] + 0.5 * acc_ref[...]


def _ffn_cast(x, g, wg, wu, wd):
    rows = x.shape[0]
    chunks = D_FF // FFN_CHUNK
    x_spec = _const_spec((rows, D_MODEL))
    col_spec = pl.BlockSpec((D_MODEL, FFN_CHUNK), lambda f: (0, f))
    row_spec = pl.BlockSpec((FFN_CHUNK, D_MODEL), lambda f: (f, 0))
    w_specs = [col_spec, col_spec, row_spec]
    wide, tall = jax.ShapeDtypeStruct((D_MODEL, D_FF), BF16), jax.ShapeDtypeStruct((D_FF, D_MODEL), BF16)
    return pl.pallas_call(
        _ffn_cast_body,
        grid=(chunks,),
        in_specs=[x_spec, _const_spec((1, D_MODEL))] + w_specs,
        out_specs=[pl.BlockSpec((rows, D_MODEL), lambda f: (0, 0))] + w_specs,
        out_shape=[jax.ShapeDtypeStruct((rows, D_MODEL), F32), wide, wide, tall],
        scratch_shapes=[pltpu.VMEM((rows, D_MODEL), BF16), pltpu.VMEM((rows, D_MODEL), F32)],
        compiler_params=_params(1),
    )(x, g, wg, wu, wd)


def _rope(z, cos, sin_signed):
    half = HEAD_DIM // 2
    lane = lax.broadcasted_iota(jnp.int32, z.shape, 1)
    first_half = (lane % HEAD_DIM) < half
    partner = jnp.where(first_half, pltpu.roll(z, LANES - half, axis=1), pltpu.roll(z, half, axis=1))
    return z * cos + partner * sin_signed


def _project_pieces(x, g_ref, w_ref, cos_ref, sin_ref, store_u, store_gate, q_ref, k_ref, v_ref):
    o_gate, o_q, o_k, o_v = LRU_WIDTH, 2 * LRU_WIDTH, 2 * LRU_WIDTH + Q_WIDTH, 2 * LRU_WIDTH + Q_WIDTH + KV_WIDTH
    xn = []

    def normed():
        if not xn:
            xn.append(_rms(x(), g_ref[...]).astype(BF16))
        return xn[0]

    def rope_into(ref, z):
        for j in range(z.shape[1] // LANES):
            cols = slice(j * LANES, (j + 1) * LANES)
            ref[:, cols] = _rope(z[:, cols], cos_ref[...], sin_ref[...])

    def store_v(z):
        v_ref[...] = z

    return [lambda: rope_into(q_ref, _dot(normed(), w_ref[:, o_q:o_k])),
            lambda: rope_into(k_ref, _dot(normed(), w_ref[:, o_k:o_v])),
            lambda: store_v(_dot(normed(), w_ref[:, o_v:])),
            lambda: store_u(_dot(normed(), w_ref[:, :o_gate])),
            lambda: store_gate(_dot(normed(), w_ref[:, o_gate:o_q]))]


def _proj_body(x_ref, g_ref, w_ref, cos_ref, sin_ref, u_ref, gate_ref, q_ref, k_ref, v_ref):
    def store_u(z):
        u_ref[...] = z

    def store_gate(z):
        gate_ref[...] = z

    for piece in _project_pieces(lambda: x_ref[...], g_ref, w_ref, cos_ref, sin_ref, store_u, store_gate,
                                 q_ref, k_ref, v_ref):
        piece()


def _proj(x, g, w_in, cos, sin):
    rows = x.shape[0]
    tm = min(ROW_TILE, rows, cos.shape[0])
    pos_blocks = cos.shape[0] // tm

    def row_spec(width):
        return pl.BlockSpec((tm, width), lambda i: (i, 0))

    pos_spec = pl.BlockSpec((tm, LANES), lambda i: (i % pos_blocks, 0))
    widths = (LRU_WIDTH, LRU_WIDTH, Q_WIDTH, KV_WIDTH, KV_WIDTH)
    return pl.pallas_call(
        _proj_body,
        grid=(rows // tm,),
        in_specs=[row_spec(D_MODEL), _const_spec((1, D_MODEL)), _const_spec((D_MODEL, IN_COLS)), pos_spec, pos_spec],
        out_specs=[row_spec(w) for w in widths],
        out_shape=[jax.ShapeDtypeStruct((rows, w), F32) for w in widths],
        compiler_params=_params(1),
    )(x, g, w_in, cos, sin)


def _rope_tables(pos):
    half = HEAD_DIM // 2
    inv = ROPE_THETA ** (-np.arange(half, dtype=np.float64) / half)
    ang = pos.astype(np.float64)[:, None] * inv[None, :]
    cos = np.cos(ang)
    sin = np.sin(ang)
    reps = LANES // HEAD_DIM
    return (jnp.asarray(np.tile(np.concatenate([cos, cos], axis=-1), (1, reps)), dtype=F32),
            jnp.asarray(np.tile(np.concatenate([-sin, sin], axis=-1), (1, reps)), dtype=F32))


def _softplus(x):
    return jnp.maximum(x, 0.0) + jnp.log1p(jnp.exp(-jnp.abs(x)))


def _lru_coeffs(conv, wab, bab, lam):
    w = conv.shape[1]
    gates = _dot(conv.astype(BF16), wab) + bab
    r = jax.nn.sigmoid(gates[:, :w])
    gi = jax.nn.sigmoid(gates[:, w:])
    log_a = -LRU_C * r * _softplus(-lam)
    a = jnp.exp(log_a)
    b = jnp.sqrt(-jnp.tanh(log_a) * (a * a + 1.0)) * (gi * conv)
    return a, b


def _segment_scan(a, b, seg):
    step = 1
    while step < seg:
        a, b = _scan_step(a, b, seg, step)
        step *= 2
    return a, b


def _scan_step(a, b, seg, step):
    pos = lax.broadcasted_iota(jnp.int32, a.shape, 0) % seg
    live = pos >= step
    a_prev = pltpu.roll(a, step, axis=0)
    b_prev = pltpu.roll(b, step, axis=0)
    return jnp.where(live, a * a_prev, a), jnp.where(live, a * b_prev + b, b)


def _lru_hidden(conv, h_in, seg, wab, bab, lam):
    a, b = _lru_coeffs(conv, wab, bab, lam)
    a_cum, h_local = _segment_scan(a, b, seg)
    return a_cum * h_in + h_local


def _lane_chunks(width):
    return [slice(c * LANES, (c + 1) * LANES) for c in range(width // LANES)]


def _ffn_pieces(normed, wg_ref, wu_ref, wd_ref, state):
    cols = [slice(f, min(f + FFN_CHUNK, D_FF)) for f in range(0, D_FF, FFN_CHUNK)]

    def gate_up(f):
        return _dot(normed(), wg_ref[:, cols[f]]), _dot(normed(), wu_ref[:, cols[f]])

    def piece(f):
        def run():
            gate, up = state.pop("gate_up") if "gate_up" in state else gate_up(f)
            if f + 1 < len(cols):
                state["gate_up"] = gate_up(f + 1)
            part = _dot((gate * jax.nn.sigmoid(gate) * up).astype(BF16), wd_ref[cols[f], :])
            state["acc"] = part if "acc" not in state else state["acc"] + part
        return run

    return [piece(f) for f in range(len(cols))]


def _interleave(primary, secondary):
    due = [((i + 1) * len(primary)) // (len(secondary) + 1) for i in range(len(secondary))]
    pending = list(zip(due, secondary))
    for i, piece in enumerate(primary):
        while pending and pending[0][0] <= i:
            pending.pop(0)[1]()
        piece()
    for _, piece in pending:
        piece()


def _ffn_proj_lru_body(x_ref, g1_ref, wg_ref, wu_ref, wd_ref, gmix_ref, win_ref, cos_ref, sin_ref,
                       cw_ref, cb_ref, wab_ref, bab_ref, lam_ref, gout_ref, og_ref, ou_ref, od_ref,
                       x_out_ref, q_ref, k_ref, v_ref, lru_ref, hlast_ref, utail_ref, og16_ref, ou16_ref, od16_ref,
                       ug_ref, ext_ref, h_ref, hs_ref, *, tt, tiles, tiles_per_seq):
    s = pl.program_id(0)

    @pl.when(s < D_FF // FFN_CHUNK)
    def _():
        for src, dst in ((og_ref, og16_ref), (ou_ref, ou16_ref), (od_ref, od16_ref)):
            dst[...] = src[...].astype(BF16)

    refs = (x_ref, g1_ref, wg_ref, wu_ref, wd_ref, gmix_ref, win_ref, cos_ref, sin_ref,
            cw_ref, cb_ref, wab_ref, bab_ref, lam_ref, gout_ref,
            x_out_ref, q_ref, k_ref, v_ref, lru_ref, hlast_ref, utail_ref, ug_ref, ext_ref, h_ref, hs_ref)

    @pl.when(s == 0)
    def _():
        ug_ref[1] = jnp.zeros((2, tt, LRU_WIDTH), F32)

    @pl.when((s == 0) | (lax.rem(s - 1, tiles_per_seq) == 0))
    def _():
        ext_ref[0:SUBLANES, :] = jnp.zeros((SUBLANES, LRU_WIDTH), F32)
        h_ref[...] = jnp.zeros_like(h_ref)

    @pl.when(s < tiles)
    def _():
        _ffn_proj_lru_step(*refs, tt=tt, project=True)

    @pl.when(s == tiles)
    def _():
        _ffn_proj_lru_step(*refs, tt=tt, project=False)


def _ffn_proj_lru_step(x_ref, g1_ref, wg_ref, wu_ref, wd_ref, gmix_ref, win_ref, cos_ref, sin_ref,
                       cw_ref, cb_ref, wab_ref, bab_ref, lam_ref, gout_ref,
                       x_out_ref, q_ref, k_ref, v_ref, lru_ref, hlast_ref, utail_ref,
                       ug_ref, ext_ref, h_ref, hs_ref, *, tt, project):
    pad = SUBLANES
    lead_slot = pl.program_id(0) % 2
    lag_slot = 1 - lead_slot

    state = {"sumsq": jnp.zeros((tt, 1), F32)}
    scan_steps = [1 << i for i in range(tt.bit_length() - 1)]

    def lru_pieces(c, cols):
        def coeffs():
            ext_ref[pad:pad + tt, cols] = ug_ref[lag_slot, 0, :, cols]
            conv = cb_ref[:, cols]
            for j in range(CONV_WIDTH):
                start = pad - (CONV_WIDTH - 1) + j
                conv = conv + ext_ref[start:start + tt, cols] * cw_ref[j:j + 1, cols]
            ext_ref[0:pad, cols] = ext_ref[tt:tt + pad, cols]
            utail_ref[0, :, cols] = ext_ref[0:pad, cols]
            state[c] = _lru_coeffs(conv, wab_ref[c], bab_ref[c], lam_ref[:, cols])

        def scan(steps):
            def run():
                for step in steps:
                    state[c] = _scan_step(*state[c], tt, step)
            return run

        def finish():
            a_cum, h_local = state.pop(c)
            hs_ref[:, cols] = a_cum * h_ref[:, cols] + h_local
            h_ref[:, cols] = hs_ref[tt - 1:tt, cols]
            hlast_ref[0, :, cols] = h_ref[:, cols]
            y = hs_ref[:, cols] * jax.nn.gelu(ug_ref[lag_slot, 1, :, cols])
            hs_ref[:, cols] = y
            state["sumsq"] = state["sumsq"] + jnp.sum(y * y, axis=-1, keepdims=True)

        half = len(scan_steps) // 2
        return [coeffs, scan(scan_steps[:half]), scan(scan_steps[half:]), finish]

    vector_pieces = [p for c, cols in enumerate(_lane_chunks(LRU_WIDTH)) for p in lru_pieces(c, cols)]

    def ffn_in():
        if "xn" not in state:
            state["xn"] = _rms(x_ref[...], g1_ref[...]).astype(BF16)
        return state["xn"]

    def ffn_out():
        if "x" not in state:
            state["x"] = x_ref[...] + 0.5 * state.pop("acc")
            x_out_ref[...] = state["x"]
        return state["x"]

    def store_u(z):
        ug_ref[lead_slot, 0] = z

    def store_gate(z):
        ug_ref[lead_slot, 1] = z

    matmul_pieces = _ffn_pieces(ffn_in, wg_ref, wu_ref, wd_ref, state)
    matmul_pieces += _project_pieces(ffn_out, gmix_ref, win_ref, cos_ref, sin_ref, store_u, store_gate,
                                     q_ref, k_ref, v_ref)

    assert len(vector_pieces) == len(matmul_pieces)
    for vector_piece, matmul_piece in zip(vector_pieces, matmul_pieces):
        vector_piece()
        if project:
            matmul_piece()
    lru_ref[...] = hs_ref[...] * lax.rsqrt(state["sumsq"] * (1.0 / LRU_WIDTH) + EPS) * gout_ref[...]


def _ffn_proj_lru(x, seq, g1, wg, wu, wd, g_mix, w_in, cos, sin, conv_w, conv_b, wab, bab, lam, g_out,
                  other_g, other_u, other_d):
    rows = x.shape[0]
    n = rows // seq
    tt = min(LRU_TILE, seq)
    tiles_per_seq = seq // tt
    tiles = rows // tt
    pos_blocks = cos.shape[0] // tt
    lead = lambda s: jnp.minimum(s, tiles - 1)
    lag = lambda s: jnp.maximum(s - 1, 0)

    def lead_spec(width):
        return pl.BlockSpec((tt, width), lambda s: (lead(s), 0))

    pos_spec = pl.BlockSpec((tt, LANES), lambda s: (lead(s) % pos_blocks, 0))
    seq_spec = lambda r: pl.BlockSpec((1, r, LRU_WIDTH), lambda s: (lag(s) // tiles_per_seq, 0, 0))
    widths = (D_MODEL, Q_WIDTH, KV_WIDTH, KV_WIDTH)
    last_chunk = D_FF // FFN_CHUNK - 1
    assert last_chunk <= tiles
    chunk = lambda s: jnp.minimum(s, last_chunk)
    col_spec = pl.BlockSpec((D_MODEL, FFN_CHUNK), lambda s: (0, chunk(s)))
    row_spec = pl.BlockSpec((FFN_CHUNK, D_MODEL), lambda s: (chunk(s), 0))
    wide, tall = jax.ShapeDtypeStruct((D_MODEL, D_FF), BF16), jax.ShapeDtypeStruct((D_FF, D_MODEL), BF16)
    return pl.pallas_call(
        functools.partial(_ffn_proj_lru_body, tt=tt, tiles=tiles, tiles_per_seq=tiles_per_seq),
        grid=(tiles + 1,),
        in_specs=[lead_spec(D_MODEL), _const_spec((1, D_MODEL)), _const_spec((D_MODEL, D_FF)), _const_spec((D_MODEL, D_FF)),
                  _const_spec((D_FF, D_MODEL)), _const_spec((1, D_MODEL)), _const_spec((D_MODEL, IN_COLS)),
                  pos_spec, pos_spec] + _lru_weight_specs() + [col_spec, col_spec, row_spec],
        out_specs=[lead_spec(w) for w in widths]
        + [pl.BlockSpec((tt, LRU_WIDTH), lambda s: (lag(s), 0)), seq_spec(1), seq_spec(SUBLANES)]
        + [col_spec, col_spec, row_spec],
        out_shape=[jax.ShapeDtypeStruct((rows, w), F32) for w in widths]
        + [jax.ShapeDtypeStruct((rows, LRU_WIDTH), F32), jax.ShapeDtypeStruct((n, 1, LRU_WIDTH), F32),
           jax.ShapeDtypeStruct((n, SUBLANES, LRU_WIDTH), F32), wide, wide, tall],
        scratch_shapes=[pltpu.VMEM((2, 2, tt, LRU_WIDTH), F32), pltpu.VMEM((tt + SUBLANES, LRU_WIDTH), F32),
                        pltpu.VMEM((1, LRU_WIDTH), F32), pltpu.VMEM((tt, LRU_WIDTH), F32)],
        compiler_params=_params(1),
    )(x, g1, wg, wu, wd, g_mix, w_in, cos, sin, conv_w, conv_b, wab, bab, lam, g_out, other_g, other_u, other_d)


def _lru_sample_body(u_ref, gate_ref, cpad_ref, h0_ref, cw_ref, cb_ref, wab_ref, bab_ref, lam_ref, gout_ref,
                     o_ref, hs_ref, *, seg):
    rows = u_ref.shape[0]
    pos = lax.broadcasted_iota(jnp.int32, (rows, LANES), 0) % seg
    sumsq = jnp.zeros((rows, 1), F32)
    for c, cols in enumerate(_lane_chunks(LRU_WIDTH)):
        u = u_ref[:, cols]
        cpad = cpad_ref[:, cols]
        conv = cb_ref[:, cols] + u * cw_ref[CONV_WIDTH - 1:CONV_WIDTH, cols]
        for back in range(1, CONV_WIDTH):
            shifted = jnp.where(pos >= back, pltpu.roll(u, back, axis=0),
                                pltpu.roll(cpad, (back - seg) % rows, axis=0))
            conv = conv + shifted * cw_ref[CONV_WIDTH - 1 - back:CONV_WIDTH - back, cols]
        h = _lru_hidden(conv, h0_ref[:, cols], seg, wab_ref[c], bab_ref[c], lam_ref[:, cols])
        hs_ref[:, cols] = h
        y = h * jax.nn.gelu(gate_ref[:, cols])
        o_ref[:, cols] = y
        sumsq = sumsq + jnp.sum(y * y, axis=-1, keepdims=True)
    o_ref[...] = o_ref[...] * lax.rsqrt(sumsq * (1.0 / LRU_WIDTH) + EPS) * gout_ref[...]


def _lru_weight_specs():
    chunks = LRU_WIDTH // LANES
    return [_const_spec((CONV_WIDTH, LRU_WIDTH)), _const_spec((1, LRU_WIDTH)),
            _const_spec((chunks, LANES, 2 * LANES)), _const_spec((chunks, 1, 2 * LANES)),
            _const_spec((1, LRU_WIDTH)), _const_spec((1, LRU_WIDTH))]


def _lru_gate_chunks(w_a, b_a, w_i, b_i):
    chunks = LRU_WIDTH // LANES
    per = LRU_BLOCKS // chunks
    wa = w_a.reshape(chunks, per, *w_a.shape[1:])
    wi = w_i.reshape(chunks, per, *w_i.shape[1:])
    wab = jnp.stack([jnp.concatenate([_block_diag(wa[c]), _block_diag(wi[c])], axis=1) for c in range(chunks)])
    bab = jnp.concatenate([b_a.reshape(chunks, 1, LANES), b_i.reshape(chunks, 1, LANES)], axis=2)
    return wab.astype(BF16), bab


def _block_diag(w):
    nb, bi, bj = w.shape
    eye = jnp.eye(nb, dtype=w.dtype)
    return jnp.einsum('gij,gh->gihj', w, eye).reshape(nb * bi, nb * bj)


def _bdot_nt(a, b):
    return lax.dot_general(a, b, (((2,), (2,)), ((0,), (0,))), preferred_element_type=F32)


def _bdot(a, b):
    return lax.dot_general(a, b, (((2,), (1,)), ((0,), (0,))), preferred_element_type=F32)


def _swa_pieces(q, score, weigh, sink_ref, mask, g_out, store):
    group = ATTN_HEADS // KV_HEADS
    straight = [h for h in range(ATTN_HEADS) if (h % 2) == (h // group)]
    swapped = [h for h in range(ATTN_HEADS) if (h % 2) != (h // group)]
    scale = HEAD_DIM ** -0.5
    out_half = {}

    def low_lanes(shape):
        return lax.broadcasted_iota(jnp.int32, shape, 2) < HEAD_DIM

    def head_group(heads, swap):
        held = {}

        def scores():
            qv = q()
            n, r, _ = qv.shape
            low = low_lanes((n, r, LANES))
            zero = jnp.zeros((n, r, LANES), F32)
            qs = jnp.concatenate(
                [jnp.where(low if h % 2 == 0 else ~low, qv[:, :, (h // 2) * LANES:(h // 2 + 1) * LANES], zero)
                 for h in heads], axis=1)
            held["s"] = score(qs.astype(BF16), swap) * scale

        def probabilities():
            s = held.pop("s")
            r = s.shape[1] // len(heads)
            visible = mask()[None]
            probs = []
            for i, h in enumerate(heads):
                sh = jnp.where(visible, s[:, i * r:(i + 1) * r, :], -jnp.inf)
                sink = sink_ref[h]
                m = jnp.maximum(jnp.max(sh, axis=-1, keepdims=True), sink)
                e = jnp.exp(sh - m)
                denom = jnp.sum(e, axis=-1, keepdims=True) + jnp.exp(sink - m)
                probs.append(e / denom)
            held["p"] = jnp.concatenate(probs, axis=1).astype(BF16)

        def weighted_values():
            p = held.pop("p")
            r = p.shape[1] // len(heads)
            o = weigh(p, swap)
            for i, h in enumerate(heads):
                out_half[h] = o[:, i * r:(i + 1) * r, :]

        return [scores, probabilities, weighted_values]

    def finish():
        low = low_lanes(out_half[0].shape)
        out = jnp.concatenate([jnp.where(low, out_half[2 * j], out_half[2 * j + 1])
                               for j in range(Q_WIDTH // LANES)], axis=2)
        store(_rms(out, g_out))

    return head_group(straight, False) + head_group(swapped, True) + [finish]


def _band_mask(first_block):
    i = lax.broadcasted_iota(jnp.int32, (WINDOW, 2 * WINDOW), 0)
    j = lax.broadcasted_iota(jnp.int32, (WINDOW, 2 * WINDOW), 1)
    dist = i + WINDOW - j
    return (dist >= 0) & (dist < WINDOW) & (jnp.logical_not(first_block) | (j >= WINDOW))


def _swa_sample_body(sink_ref, q_ref, k_ref, v_ref, bk_ref, bv_ref, gout_ref, o_ref, nk_ref, nv_ref, *, s_len, past):
    def mask():
        qp = past + lax.broadcasted_iota(jnp.int32, (s_len, WINDOW + s_len), 0)
        col = lax.broadcasted_iota(jnp.int32, (s_len, WINDOW + s_len), 1)
        kp = jnp.where(col < WINDOW, past - WINDOW + col, past + col - WINDOW)
        dist = qp - kp
        return (dist >= 0) & (dist < WINDOW) & (kp >= 0)

    def halves(z, axis, swap):
        return pltpu.roll(z, HEAD_DIM, axis=axis) if swap else z

    def score(qs, swap):
        carried = _bdot(qs, halves(bk_ref[...], 1, swap).astype(BF16))
        fresh = _bdot_nt(qs, halves(k_ref[...], 2, swap).astype(BF16))
        return jnp.concatenate([carried, fresh], axis=2)

    def weigh(p, swap):
        return (_bdot_nt(p[:, :, :WINDOW], halves(bv_ref[...], 1, swap).astype(BF16))
                + _bdot(p[:, :, WINDOW:], halves(v_ref[...], 2, swap).astype(BF16)))

    def store(o):
        o_ref[...] = o.reshape(o_ref.shape)

    for piece in _swa_pieces(lambda: q_ref[...], score, weigh, sink_ref, mask, gout_ref[...], store):
        piece()

    def slide(window_ref, fresh_ref, out_ref):
        n = window_ref.shape[0]
        tail = jnp.concatenate([jnp.zeros((n, WINDOW - s_len, KV_WIDTH), F32), fresh_ref[...]], axis=1)
        lane = lax.broadcasted_iota(jnp.int32, (n, KV_WIDTH, WINDOW), 2)
        out_ref[...] = jnp.where(lane >= WINDOW - s_len, jnp.swapaxes(tail, 1, 2),
                                 pltpu.roll(window_ref[...], WINDOW - s_len, axis=2))

    slide(bk_ref, k_ref, nk_ref)
    slide(bv_ref, v_ref, nv_ref)


def _softmax(s):
    e = jnp.exp(s - jnp.max(s, axis=-1, keepdims=True))
    return e / jnp.sum(e, axis=-1, keepdims=True)


def _mix_and_query(x_ref, lru_ref, attn_ref, wout_ref, gx_ref, wcq_ref):
    x = (x_ref[...] + _dot(lru_ref[...].astype(BF16), wout_ref[:LRU_WIDTH, :])
         + _dot(attn_ref[...].astype(BF16), wout_ref[LRU_WIDTH:, :]))
    return x, _dot(_rms(x, gx_ref[...]).astype(BF16), wcq_ref[...])


def _swa_mix_ffn_body(sink_ref, q_ref, k_ref, v_ref, kp_ref, vp_ref, gattn_ref,
                      x_ref, lru_ref, wout_ref, gx_ref, wcq_ref, mk_ref, mv_ref, wco_ref,
                      g2_ref, wg_ref, wu_ref, wd_ref, gf_ref, o_ref, attn_ref,
                      *, tt, tiles, tiles_per_seq, final_norm):
    s = pl.program_id(0)
    refs = (sink_ref, q_ref, k_ref, v_ref, kp_ref, vp_ref, gattn_ref, x_ref, lru_ref, wout_ref, gx_ref, wcq_ref,
            mk_ref, mv_ref, wco_ref, g2_ref, wg_ref, wu_ref, wd_ref, gf_ref, o_ref, attn_ref)
    step = functools.partial(_swa_mix_ffn_step, *refs, tt=tt, tiles_per_seq=tiles_per_seq, final_norm=final_norm)

    @pl.when(s == 0)
    def _():
        step(attend=True, layer=False)

    @pl.when((s > 0) & (s < tiles))
    def _():
        step(attend=True, layer=True)

    @pl.when(s == tiles)
    def _():
        step(attend=False, layer=True)


def _swa_mix_ffn_step(sink_ref, q_ref, k_ref, v_ref, kp_ref, vp_ref, gattn_ref,
                      x_ref, lru_ref, wout_ref, gx_ref, wcq_ref, mk_ref, mv_ref, wco_ref,
                      g2_ref, wg_ref, wu_ref, wd_ref, gf_ref, o_ref, attn_ref,
                      *, tt, tiles_per_seq, final_norm, attend, layer):
    s = pl.program_id(0)
    lead_slot = s % 2
    lag_slot = 1 - lead_slot

    blocks = tt // WINDOW
    first_pos_block = lax.rem(s, tiles_per_seq) * blocks
    attention_pieces = []
    for j in range(blocks):
        rows = slice(j * WINDOW, (j + 1) * WINDOW)

        def band(ref, prev_ref, swap, j=j, rows=rows):
            prev = prev_ref[...] if j == 0 else ref[(j - 1) * WINDOW:j * WINDOW, :]
            both = jnp.concatenate([prev, ref[rows, :]], axis=0)[None]
            return (pltpu.roll(both, HEAD_DIM, axis=2) if swap else both).astype(BF16)

        def score(qs, swap, band=band):
            return _bdot_nt(qs, band(k_ref, kp_ref, swap))

        def weigh(p, swap, band=band):
            return _bdot(p, band(v_ref, vp_ref, swap))

        def store(o, rows=rows):
            attn_ref[lead_slot, rows, :] = o[0]

        attention_pieces += _swa_pieces(lambda rows=rows: q_ref[rows, :][None], score, weigh, sink_ref,
                                        lambda j=j: _band_mask(first_pos_block + j == 0), gattn_ref[...], store)

    state = {}
    scale = X_HEAD_DIM ** -0.5

    def mix():
        state["x"] = (x_ref[...] + _dot(lru_ref[...].astype(BF16), wout_ref[:LRU_WIDTH, :])
                      + _dot(attn_ref[lag_slot].astype(BF16), wout_ref[LRU_WIDTH:, :]))

    def query():
        state["q"] = _dot(_rms(state["x"], gx_ref[...]).astype(BF16), wcq_ref[...])

    def memory_head(h):
        def run():
            cols = slice(h * X_HEAD_DIM, (h + 1) * X_HEAD_DIM)
            sc = _dot_nt(state["q"][:, cols].astype(BF16), mk_ref[0, :, cols].astype(BF16)) * scale
            state["o", h] = _dot(_softmax(sc).astype(BF16), mv_ref[0, :, cols].astype(BF16))
        return run

    def memory_out():
        o = jnp.concatenate([state.pop(("o", h)) for h in range(X_HEADS)], axis=1)
        state["x"] = state["x"] + _dot(o.astype(BF16), wco_ref[...])
        state["xn"] = _rms(state["x"], g2_ref[...]).astype(BF16)

    def finish():
        y = state["x"] + 0.5 * state["acc"]
        o_ref[...] = _rms(y, gf_ref[...]) if final_norm else y

    layer_pieces = ([mix, query] + [memory_head(h) for h in range(X_HEADS)] + [memory_out]
                    + _ffn_pieces(lambda: state["xn"], wg_ref, wu_ref, wd_ref, state) + [finish])
    _interleave(layer_pieces if layer else [], attention_pieces if attend else [])


def _swa_mix_ffn(q, k, v, sink, g_attn, x, lru_out, seq, w_out, g_x, w_cq, mk, mv, w_co, g2, wg, wu, wd, g_final,
                 final_norm):
    rows = x.shape[0]
    tt = min(LRU_TILE, seq)
    tiles_per_seq = seq // tt
    tiles = rows // tt
    blocks = tt // WINDOW
    lead = lambda s: jnp.minimum(s, tiles - 1)
    lag = lambda s: jnp.maximum(s - 1, 0)

    def lead_spec(width):
        return pl.BlockSpec((tt, width), lambda s: (lead(s), 0))

    def lag_spec(width):
        return pl.BlockSpec((tt, width), lambda s: (lag(s), 0))

    prev_spec = pl.BlockSpec((WINDOW, KV_WIDTH), lambda s: (jnp.maximum(lead(s) * blocks - 1, 0), 0))
    mem_spec = pl.BlockSpec((1, N_MEM, D_MODEL), lambda s: (lag(s) // tiles_per_seq, 0, 0))
    w_spec = _const_spec((D_MODEL, D_MODEL))
    return pl.pallas_call(
        functools.partial(_swa_mix_ffn_body, tt=tt, tiles=tiles, tiles_per_seq=tiles_per_seq, final_norm=final_norm),
        grid=(tiles + 1,),
        in_specs=[pl.BlockSpec(memory_space=pltpu.SMEM), lead_spec(Q_WIDTH), lead_spec(KV_WIDTH), lead_spec(KV_WIDTH),
                  prev_spec, prev_spec, _const_spec((1, Q_WIDTH)),
                  lag_spec(D_MODEL), lag_spec(LRU_WIDTH), w_spec, _const_spec((1, D_MODEL)), w_spec,
                  mem_spec, mem_spec, w_spec,
                  _const_spec((1, D_MODEL)), _const_spec((D_MODEL, D_FF)), _const_spec((D_MODEL, D_FF)),
                  _const_spec((D_FF, D_MODEL)), _const_spec((1, D_MODEL))],
        out_specs=lag_spec(D_MODEL),
        out_shape=jax.ShapeDtypeStruct((rows, D_MODEL), F32),
        scratch_shapes=[pltpu.VMEM((2, tt, Q_WIDTH), F32)],
        compiler_params=_params(1),
    )(sink, q, k, v, k, v, g_attn, x, lru_out, w_out, g_x, w_cq, mk, mv, w_co, g2, wg, wu, wd, g_final)


def _sample_mixer_body(sink_ref, u_ref, gate_ref, cpad_ref, h0_ref, q_ref, k_ref, v_ref, bk_ref, bv_ref, x_ref,
                       cw_ref, cb_ref, wab_ref, bab_ref, lam_ref, glru_ref, gattn_ref, wout_ref, gx_ref, wcq_ref,
                       x_out_ref, xq_ref, hs_ref, nk_ref, nv_ref, lru_ref, attn_ref, *, seg, past):
    _lru_sample_body(u_ref, gate_ref, cpad_ref, h0_ref, cw_ref, cb_ref, wab_ref, bab_ref, lam_ref, glru_ref,
                     lru_ref, hs_ref, seg=seg)
    _swa_sample_body(sink_ref, q_ref, k_ref, v_ref, bk_ref, bv_ref, gattn_ref, attn_ref, nk_ref, nv_ref,
                     s_len=seg, past=past)
    x_out_ref[...], xq_ref[...] = _mix_and_query(x_ref, lru_ref, attn_ref, wout_ref, gx_ref, wcq_ref)


def _channel_major(window):
    n = window.shape[0]
    return jnp.transpose(window, (0, 2, 3, 1)).reshape(n, KV_WIDTH, WINDOW)


def _position_major(window):
    n = window.shape[0]
    return jnp.transpose(window.reshape(n, KV_HEADS, HEAD_DIM, WINDOW), (0, 3, 1, 2))


def _sample_mixer(u, gate, conv_pad, h0_rep, q, k, v, buf_k, buf_v, x, sink, lru_w, g_attn, w_out, g_x, w_cq, past):
    n, seg, _ = q.shape
    sb = min(SAMPLE_SEQ_TILE, n)
    rows = sb * seg

    def row_spec(width):
        return pl.BlockSpec((rows, width), lambda i: (i, 0))

    def seq_spec(steps, width):
        return pl.BlockSpec((sb, steps, width), lambda i: (i, 0, 0))

    buf_spec = seq_spec(WINDOW, KV_WIDTH)
    w_spec = _const_spec((D_MODEL, D_MODEL))
    total = n * seg
    return pl.pallas_call(
        functools.partial(_sample_mixer_body, seg=seg, past=past),
        grid=(n // sb,),
        in_specs=[pl.BlockSpec(memory_space=pltpu.SMEM)] + [row_spec(LRU_WIDTH)] * 4
        + [seq_spec(seg, Q_WIDTH), seq_spec(seg, KV_WIDTH), seq_spec(seg, KV_WIDTH), buf_spec, buf_spec,
           row_spec(D_MODEL)] + _lru_weight_specs()
        + [_const_spec((1, Q_WIDTH)), w_spec, _const_spec((1, D_MODEL)), w_spec],
        out_specs=[row_spec(D_MODEL), row_spec(D_MODEL), row_spec(LRU_WIDTH), buf_spec, buf_spec],
        out_shape=[jax.ShapeDtypeStruct((total, D_MODEL), F32), jax.ShapeDtypeStruct((total, D_MODEL), F32),
                   jax.ShapeDtypeStruct((total, LRU_WIDTH), F32),
                   jax.ShapeDtypeStruct((n, WINDOW, KV_WIDTH), F32), jax.ShapeDtypeStruct((n, WINDOW, KV_WIDTH), F32)],
        scratch_shapes=[pltpu.VMEM((rows, LRU_WIDTH), F32), pltpu.VMEM((rows, Q_WIDTH), F32)],
        compiler_params=_params(1),
    )(sink, u, gate, conv_pad, h0_rep, q, k, v, buf_k, buf_v, x, *lru_w, g_attn, w_out, g_x, w_cq)


def _xattn_cache_pieces(q_ref, mk_ref, mv_ref, o_ref):
    sb, steps, _ = q_ref.shape
    blocks = D_MODEL // LANES
    chunks = X_HEAD_DIM // LANES
    width = mk_ref.shape[1]
    block_cls = [(j % chunks) * X_HEADS + j // chunks for j in range(blocks)]
    scale = X_HEAD_DIM ** -0.5
    held = {}

    def lane_class():
        return lax.broadcasted_iota(jnp.int32, (sb, steps, width), 2) % blocks

    def scores():
        qs = jnp.concatenate([q_ref[:, :, j * LANES:(j + 1) * LANES] for j in range(blocks)], axis=1)
        held["s"] = _bdot_nt(qs.astype(BF16), mk_ref[...].astype(BF16)) * scale

    def probabilities():
        s = held.pop("s")
        cls = lane_class()
        part = jnp.zeros((sb, steps, width), F32)
        for j in range(blocks):
            part = part + jnp.where(cls == block_cls[j], s[:, j * steps:(j + 1) * steps, :], 0.0)
        score = part + pltpu.roll(part, width - X_HEADS, axis=2)
        top = jnp.zeros((sb, steps, width), F32)
        for h in range(X_HEADS):
            mine = cls == h
            top = jnp.where(mine, jnp.max(jnp.where(mine, score, -jnp.inf), axis=-1, keepdims=True), top)
        e = jnp.where(cls < X_HEADS, jnp.exp(score - top), 0.0)
        denom = jnp.ones((sb, steps, width), F32)
        for h in range(X_HEADS):
            mine = cls == h
            denom = jnp.where(mine, jnp.sum(jnp.where(mine, e, 0.0), axis=-1, keepdims=True), denom)
        p = e / denom
        p = p + pltpu.roll(p, X_HEADS, axis=2)
        held["p"] = jnp.concatenate([jnp.where(cls == block_cls[j], p, 0.0) for j in range(blocks)],
                                    axis=1).astype(BF16)

    def weighted_values():
        o = _bdot(held.pop("p"), mv_ref[...].astype(BF16))
        for j in range(blocks):
            o_ref[:, :, j * LANES:(j + 1) * LANES] = o[:, j * steps:(j + 1) * steps, :]

    return [scores, probabilities, weighted_values]


def _xattn_cache_body(q_ref, mk_ref, mv_ref, o_ref):
    for piece in _xattn_cache_pieces(q_ref, mk_ref, mv_ref, o_ref):
        piece()


def _interleave_chunks(cache):
    n = cache.shape[0]
    chunks = X_HEAD_DIM // LANES
    c = cache.reshape(n, N_MEM, X_HEADS, chunks, LANES)
    return jnp.transpose(c, (0, 1, 3, 2, 4)).reshape(n, N_MEM * chunks * X_HEADS, LANES)


def _xattn_cache(q, cache_k, cache_v):
    count, s_len, _ = q.shape
    sb = min(XATTN_SEQ_TILE, count)
    q_spec = pl.BlockSpec((sb, s_len, D_MODEL), lambda i: (i, 0, 0))
    rows = cache_k.shape[1]
    mem_spec = pl.BlockSpec((sb, rows, LANES), lambda i: (i, 0, 0))
    return pl.pallas_call(
        _xattn_cache_body,
        grid=(count // sb,),
        in_specs=[q_spec, mem_spec, mem_spec],
        out_specs=q_spec,
        out_shape=jax.ShapeDtypeStruct((count, s_len, D_MODEL), F32),
        compiler_params=_params(1),
    )(q, cache_k, cache_v)


def _mem_kv_body(mem_ref, g_ref, wk_ref, wv_ref, k_ref, v_ref):
    mm = _rms(mem_ref[...], g_ref[...]).astype(BF16)
    k_ref[...] = _dot(mm, wk_ref[...].astype(BF16))
    v_ref[...] = _dot(mm, wv_ref[...].astype(BF16))


def _mem_kv(mem, g, w_ck, w_cv):
    rows = mem.shape[0]
    tm = min(ROW_TILE, rows)
    row_spec = pl.BlockSpec((tm, D_MODEL), lambda i: (i, 0))
    w_spec = _const_spec((D_MODEL, D_MODEL))
    return pl.pallas_call(
        _mem_kv_body,
        grid=(rows // tm,),
        in_specs=[row_spec, _const_spec((1, D_MODEL)), w_spec, w_spec],
        out_specs=[row_spec, row_spec],
        out_shape=[jax.ShapeDtypeStruct((rows, D_MODEL), F32)] * 2,
        compiler_params=_params(1),
    )(mem, g, w_ck, w_cv)


def kernel(x_prompt, x_sample, mem_prompt, cache_mem_k, cache_mem_v, cache_swa_k, cache_swa_v, state_conv, state_lru_h,
           g_ffn1, w1_gate, w1_up, w1_down, g_mix, w_in, conv_w, conv_b, w_a, b_a, w_i, b_i, lam, sink,
           g_lru_out, g_attn_out, w_out, g_xattn, g_mem, w_cq, w_ck, w_cv, w_co, g_ffn2, w2_gate, w2_up, w2_down,
           g_final):
    nbp, seq, _ = x_prompt.shape
    nbs, dec_seq, _ = x_sample.shape
    depth = g_ffn1.shape[0]
    past = PAST_LEN
    cos_p, sin_p = _rope_tables(np.arange(seq, dtype=np.int32))
    sample_pos_rows = min(ROW_TILE, nbs * dec_seq)
    cos_s, sin_s = _rope_tables(np.tile(past + np.arange(dec_seq, dtype=np.int32), sample_pos_rows // dec_seq))

    xp = x_prompt.reshape(nbp * seq, D_MODEL)
    xs = x_sample.reshape(nbs * dec_seq, D_MODEL)
    g_fin = g_final.reshape(1, D_MODEL)
    row = lambda a: a.reshape(1, -1)
    outs = [[] for _ in range(10)]
    for l in range(depth):
        last = l == depth - 1
        bf = lambda a: a[l].astype(BF16)
        win, wout, wcq, wco = bf(w_in), bf(w_out), bf(w_cq), bf(w_co)
        wab, bab = _lru_gate_chunks(w_a[l], b_a[l], w_i[l], b_i[l])
        lru_w = (conv_w[l], row(conv_b[l]), wab, bab, row(lam[l]), row(g_lru_out[l]))

        xs, w1g, w1u, w1d = _ffn_cast(xs, row(g_ffn1[l]), w1_gate[l], w1_up[l], w1_down[l])
        u_s, gate, q, k, v = _proj(xs, row(g_mix[l]), win, cos_s, sin_s)
        conv_pad = jnp.pad(state_conv[l], ((0, 0), (dec_seq - (CONV_WIDTH - 1), 0), (0, 0)))
        h0_rep = jnp.repeat(state_lru_h[l], dec_seq, axis=0)
        per_seq = lambda a: a.reshape(nbs, dec_seq, a.shape[-1])
        xs, xq, hs, new_k, new_v = _sample_mixer(
            u_s, gate, conv_pad.reshape(nbs * dec_seq, LRU_WIDTH), h0_rep, per_seq(q), per_seq(k), per_seq(v),
            _channel_major(cache_swa_k[l]), _channel_major(cache_swa_v[l]), xs,
            sink[l], lru_w, row(g_attn_out[l]), wout, row(g_xattn[l]), wcq, past)
        xq, cache_k, cache_v = per_seq(xq), _interleave_chunks(cache_mem_k[l]), _interleave_chunks(cache_mem_v[l])

        mk_p, mv_p = _mem_kv(mem_prompt.reshape(nbp * N_MEM, D_MODEL), row(g_mem[l]), w_ck[l], w_cv[l])
        mk_p = mk_p.reshape(nbp, N_MEM, D_MODEL)
        mv_p = mv_p.reshape(nbp, N_MEM, D_MODEL)
        xp, q, k, v, lru_out, h_last, u_tail, w2g, w2u, w2d = _ffn_proj_lru(
            xp, seq, row(g_ffn1[l]), w1g, w1u, w1d, row(g_mix[l]), win, cos_p, sin_p, *lru_w,
            w2_gate[l], w2_up[l], w2_down[l])
        k3 = k.reshape(nbp, seq, KV_WIDTH)
        v3 = v.reshape(nbp, seq, KV_WIDTH)
        xp = _swa_mix_ffn(q, k, v, sink[l], row(g_attn_out[l]), xp, lru_out, seq, wout, row(g_xattn[l]), wcq,
                          mk_p, mv_p, wco, row(g_ffn2[l]), w2g, w2u, w2d, g_fin, last)
        outs[0].append(mk_p.reshape(nbp, N_MEM, X_HEADS, X_HEAD_DIM))
        outs[1].append(mv_p.reshape(nbp, N_MEM, X_HEADS, X_HEAD_DIM))
        outs[2].append(k3[:, -WINDOW:].reshape(nbp, WINDOW, KV_HEADS, HEAD_DIM))
        outs[3].append(v3[:, -WINDOW:].reshape(nbp, WINDOW, KV_HEADS, HEAD_DIM))
        outs[4].append(u_tail[:, -(CONV_WIDTH - 1):])
        outs[5].append(h_last.reshape(nbp, LRU_WIDTH))

        xo = _xattn_cache(xq, cache_k, cache_v)
        xs = _ffn(xs, row(g_ffn2[l]), w2g, w2u, w2d, g_fin, last, xo.reshape(nbs * dec_seq, D_MODEL), wco)
        outs[6].append(_position_major(new_k))
        outs[7].append(_position_major(new_v))
        outs[8].append(u_s.reshape(nbs, dec_seq, LRU_WIDTH)[:, -(CONV_WIDTH - 1):])
        outs[9].append(hs.reshape(nbs, dec_seq, LRU_WIDTH)[:, -1])

    return (xp.reshape(nbp, seq, D_MODEL), xs.reshape(nbs, dec_seq, D_MODEL)) + tuple(jnp.stack(o) for o in outs)
```

```python
import functools

import jax
import jax.numpy as jnp
import numpy as np
from jax import lax
from jax.experimental import pallas as pl
from jax.experimental.pallas import tpu as pltpu

F32 = jnp.float32
BF16 = jnp.bfloat16

D_MODEL = 1024
LRU_WIDTH = 512
LRU_BLOCKS = 8
CONV_WIDTH = 4
LRU_C = 8.0
ATTN_HEADS = 8
HEAD_DIM = 64
KV_HEADS = 2
WINDOW = 128
PAST_LEN = 8192
ROPE_THETA = 10000.0
N_MEM = 256
X_HEADS = 4
X_HEAD_DIM = 256
D_FF = 2816
EPS = 1e-6
Q_WIDTH = ATTN_HEADS * HEAD_DIM
KV_WIDTH = KV_HEADS * HEAD_DIM
IN_COLS = 2 * LRU_WIDTH + Q_WIDTH + 2 * KV_WIDTH

LANES = 128
SUBLANES = 8
VMEM_LIMIT = 56 * 1024 * 1024

ROW_TILE = 512
LRU_TILE = 512
FFN_CHUNK = 256
SAMPLE_SEQ_TILE = 32
XATTN_SEQ_TILE = 8


def _params(n_axes):
    return pltpu.CompilerParams(dimension_semantics=("arbitrary",) * n_axes, vmem_limit_bytes=VMEM_LIMIT)


def _const_spec(shape):
    return pl.BlockSpec(shape, lambda *_: (0,) * len(shape), pipeline_mode=pl.Buffered(1))


def _rms(x, g):
    return x * lax.rsqrt(jnp.mean(x * x, axis=-1, keepdims=True) + EPS) * g


def _dot(a, b):
    return jnp.dot(a, b, preferred_element_type=F32)


def _dot_nt(a, b):
    return lax.dot_general(a, b, (((1,), (1,)), ((), ())), preferred_element_type=F32)


def _proj_ffn_body(x_ref, a_ref, wa_ref, g_ref, wg_ref, wu_ref, wd_ref, gf_ref, o_ref, x1_ref, xn_ref, acc_ref,
                   *, final_norm):
    f = pl.program_id(0)

    @pl.when(f == 0)
    def _():
        x = x_ref[...] + _dot(a_ref[...].astype(BF16), wa_ref[...])
        x1_ref[...] = x
        xn_ref[...] = _rms(x, g_ref[...]).astype(BF16)
        acc_ref[...] = jnp.zeros_like(acc_ref)

    xn = xn_ref[...]
    gate = _dot(xn, wg_ref[...])
    up = _dot(xn, wu_ref[...])
    acc_ref[...] += _dot((gate * jax.nn.sigmoid(gate) * up).astype(BF16), wd_ref[...])

    @pl.when(f == pl.num_programs(0) - 1)
    def _():
        y = x1_ref[...] + 0.5 * acc_ref[...]
        o_ref[...] = _rms(y, gf_ref[...]) if final_norm else y


def _ffn(x, g, wg, wu, wd, g_final, final_norm, attn, w_attn):
    rows = x.shape[0]
    x_spec = _const_spec((rows, D_MODEL))
    col_spec = pl.BlockSpec((D_MODEL, FFN_CHUNK), lambda f: (0, f))
    row_spec = pl.BlockSpec((FFN_CHUNK, D_MODEL), lambda f: (f, 0))
    return pl.pallas_call(
        functools.partial(_proj_ffn_body, final_norm=final_norm),
        grid=(D_FF // FFN_CHUNK,),
        in_specs=[x_spec, x_spec, _const_spec((D_MODEL, D_MODEL)), _const_spec((1, D_MODEL)), col_spec, col_spec,
                  row_spec, _const_spec((1, D_MODEL))],
        out_specs=pl.BlockSpec((rows, D_MODEL), lambda f: (0, 0)),
        out_shape=jax.ShapeDtypeStruct((rows, D_MODEL), F32),
        scratch_shapes=[pltpu.VMEM((rows, D_MODEL), F32), pltpu.VMEM((rows, D_MODEL), BF16),
                        pltpu.VMEM((rows, D_MODEL), F32)],
        compiler_params=_params(1),
    )(x, attn, w_attn, g, wg, wu, wd, g_final)


def _ffn_cast_body(x_ref, g_ref, wg_ref, wu_ref, wd_ref, o_ref, wg16_ref, wu16_ref, wd16_ref, xn_ref, acc_ref):
    f = pl.program_id(0)

    @pl.when(f == 0)
    def _():
        xn_ref[...] = _rms(x_ref[...], g_ref[...]).astype(BF16)
        acc_ref[...] = jnp.zeros_like(acc_ref)

    for src, dst in ((wg_ref, wg16_ref), (wu_ref, wu16_ref), (wd_ref, wd16_ref)):
        dst[...] = src[...].astype(BF16)
    xn = xn_ref[...]
    gate = _dot(xn, wg16_ref[...])
    up = _dot(xn, wu16_ref[...])
    acc_ref[...] += _dot((gate * jax.nn.sigmoid(gate) * up).astype(BF16), wd16_ref[...])

    @pl.when(f == pl.num_programs(0) - 1)
    def _():
        o_ref[...] = x_ref[...] + 0.5 * acc_ref[...]


def _ffn_cast(x, g, wg, wu, wd):
    rows = x.shape[0]
    chunks = D_FF // FFN_CHUNK
    x_spec = _const_spec((rows, D_MODEL))
    col_spec = pl.BlockSpec((D_MODEL, FFN_CHUNK), lambda f: (0, f))
    row_spec = pl.BlockSpec((FFN_CHUNK, D_MODEL), lambda f: (f, 0))
    w_specs = [col_spec, col_spec, row_spec]
    wide, tall = jax.ShapeDtypeStruct((D_MODEL, D_FF), BF16), jax.ShapeDtypeStruct((D_FF, D_MODEL), BF16)
    return pl.pallas_call(
        _ffn_cast_body,
        grid=(chunks,),
        in_specs=[x_spec, _const_spec((1, D_MODEL))] + w_specs,
        out_specs=[pl.BlockSpec((rows, D_MODEL), lambda f: (0, 0))] + w_specs,
        out_shape=[jax.ShapeDtypeStruct((rows, D_MODEL), F32), wide, wide, tall],
        scratch_shapes=[pltpu.VMEM((rows, D_MODEL), BF16), pltpu.VMEM((rows, D_MODEL), F32)],
        compiler_params=_params(1),
    )(x, g, wg, wu, wd)


def _rope(z, cos, sin_signed):
    half = HEAD_DIM // 2
    lane = lax.broadcasted_iota(jnp.int32, z.shape, 1)
    first_half = (lane % HEAD_DIM) < half
    partner = jnp.where(first_half, pltpu.roll(z, LANES - half, axis=1), pltpu.roll(z, half, axis=1))
    return z * cos + partner * sin_signed


def _project_pieces(x, g_ref, w_ref, cos_ref, sin_ref, store_u, store_gate, q_ref, k_ref, v_ref):
    o_gate, o_q, o_k, o_v = LRU_WIDTH, 2 * LRU_WIDTH, 2 * LRU_WIDTH + Q_WIDTH, 2 * LRU_WIDTH + Q_WIDTH + KV_WIDTH
    xn = []

    def normed():
        if not xn:
            xn.append(_rms(x(), g_ref[...]).astype(BF16))
        return xn[0]

    def rope_into(ref, z):
        for j in range(z.shape[1] // LANES):
            cols = slice(j * LANES, (j + 1) * LANES)
            ref[:, cols] = _rope(z[:, cols], cos_ref[...], sin_ref[...])

    def store_v(z):
        v_ref[...] = z

    return [lambda: rope_into(q_ref, _dot(normed(), w_ref[:, o_q:o_k])),
            lambda: rope_into(k_ref, _dot(normed(), w_ref[:, o_k:o_v])),
            lambda: store_v(_dot(normed(), w_ref[:, o_v:])),
            lambda: store_u(_dot(normed(), w_ref[:, :o_gate])),
            lambda: store_gate(_dot(normed(), w_ref[:, o_gate:o_q]))]


def _proj_body(x_ref, g_ref, w_ref, cos_ref, sin_ref, u_ref, gate_ref, q_ref, k_ref, v_ref):
    def store_u(z):
        u_ref[...] = z

    def store_gate(z):
        gate_ref[...] = z

    for piece in _project_pieces(lambda: x_ref[...], g_ref, w_ref, cos_ref, sin_ref, store_u, store_gate,
                                 q_ref, k_ref, v_ref):
        piece()


def _proj(x, g, w_in, cos, sin):
    rows = x.shape[0]
    tm = min(ROW_TILE, rows, cos.shape[0])
    pos_blocks = cos.shape[0] // tm

    def row_spec(width):
        return pl.BlockSpec((tm, width), lambda i: (i, 0))

    pos_spec = pl.BlockSpec((tm, LANES), lambda i: (i % pos_blocks, 0))
    widths = (LRU_WIDTH, LRU_WIDTH, Q_WIDTH, KV_WIDTH, KV_WIDTH)
    return pl.pallas_call(
        _proj_body,
        grid=(rows // tm,),
        in_specs=[row_spec(D_MODEL), _const_spec((1, D_MODEL)), _const_spec((D_MODEL, IN_COLS)), pos_spec, pos_spec],
        out_specs=[row_spec(w) for w in widths],
        out_shape=[jax.ShapeDtypeStruct((rows, w), F32) for w in widths],
        compiler_params=_params(1),
    )(x, g, w_in, cos, sin)


def _rope_tables(pos):
    half = HEAD_DIM // 2
    inv = ROPE_THETA ** (-np.arange(half, dtype=np.float64) / half)
    ang = pos.astype(np.float64)[:, None] * inv[None, :]
    cos = np.cos(ang)
    sin = np.sin(ang)
    reps = LANES // HEAD_DIM
    return (jnp.asarray(np.tile(np.concatenate([cos, cos], axis=-1), (1, reps)), dtype=F32),
            jnp.asarray(np.tile(np.concatenate([-sin, sin], axis=-1), (1, reps)), dtype=F32))


def _softplus(x):
    return jnp.maximum(x, 0.0) + jnp.log1p(jnp.exp(-jnp.abs(x)))


def _lru_coeffs(conv, wab, bab, lam):
    w = conv.shape[1]
    gates = _dot(conv.astype(BF16), wab) + bab
    r = jax.nn.sigmoid(gates[:, :w])
    gi = jax.nn.sigmoid(gates[:, w:])
    log_a = -LRU_C * r * _softplus(-lam)
    a = jnp.exp(log_a)
    b = jnp.sqrt(-jnp.tanh(log_a) * (a * a + 1.0)) * (gi * conv)
    return a, b


def _segment_scan(a, b, seg):
    step = 1
    while step < seg:
        a, b = _scan_step(a, b, seg, step)
        step *= 2
    return a, b


def _scan_step(a, b, seg, step):
    pos = lax.broadcasted_iota(jnp.int32, a.shape, 0) % seg
    live = pos >= step
    a_prev = pltpu.roll(a, step, axis=0)
    b_prev = pltpu.roll(b, step, axis=0)
    return jnp.where(live, a * a_prev, a), jnp.where(live, a * b_prev + b, b)


def _lru_hidden(conv, h_in, seg, wab, bab, lam):
    a, b = _lru_coeffs(conv, wab, bab, lam)
    a_cum, h_local = _segment_scan(a, b, seg)
    return a_cum * h_in + h_local


def _lane_chunks(width):
    return [slice(c * LANES, (c + 1) * LANES) for c in range(width // LANES)]


def _ffn_pieces(normed, wg_ref, wu_ref, wd_ref, state):
    cols = [slice(f, min(f + FFN_CHUNK, D_FF)) for f in range(0, D_FF, FFN_CHUNK)]

    def gate_up(f):
        return _dot(normed(), wg_ref[:, cols[f]]), _dot(normed(), wu_ref[:, cols[f]])

    def piece(f):
        def run():
            gate, up = state.pop("gate_up") if "gate_up" in state else gate_up(f)
            if f + 1 < len(cols):
                state["gate_up"] = gate_up(f + 1)
            part = _dot((gate * jax.nn.sigmoid(gate) * up).astype(BF16), wd_ref[cols[f], :])
            state["acc"] = part if "acc" not in state else state["acc"] + part
        return run

    return [piece(f) for f in range(len(cols))]


def _interleave(primary, secondary):
    due = [((i + 1) * len(primary)) // (len(secondary) + 1) for i in range(len(secondary))]
    pending = list(zip(due, secondary))
    for i, piece in enumerate(primary):
        while pending and pending[0][0] <= i:
            pending.pop(0)[1]()
        piece()
    for _, piece in pending:
        piece()


def _ffn_proj_lru_body(x_ref, g1_ref, wg_ref, wu_ref, wd_ref, gmix_ref, win_ref, cos_ref, sin_ref,
                       cw_ref, cb_ref, wab_ref, bab_ref, lam_ref, gout_ref, og_ref, ou_ref, od_ref,
                       x_out_ref, q_ref, k_ref, v_ref, lru_ref, hlast_ref, utail_ref, og16_ref, ou16_ref, od16_ref,
                       ug_ref, ext_ref, h_ref, hs_ref, *, tt, tiles, tiles_per_seq):
    s = pl.program_id(0)

    @pl.when(s < D_FF // FFN_CHUNK)
    def _():
        for src, dst in ((og_ref, og16_ref), (ou_ref, ou16_ref), (od_ref, od16_ref)):
            dst[...] = src[...].astype(BF16)

    refs = (x_ref, g1_ref, wg_ref, wu_ref, wd_ref, gmix_ref, win_ref, cos_ref, sin_ref,
            cw_ref, cb_ref, wab_ref, bab_ref, lam_ref, gout_ref,
            x_out_ref, q_ref, k_ref, v_ref, lru_ref, hlast_ref, utail_ref, ug_ref, ext_ref, h_ref, hs_ref)

    @pl.when(s == 0)
    def _():
        ug_ref[1] = jnp.zeros((2, tt, LRU_WIDTH), F32)

    @pl.when((s == 0) | (lax.rem(s - 1, tiles_per_seq) == 0))
    def _():
        ext_ref[0:SUBLANES, :] = jnp.zeros((SUBLANES, LRU_WIDTH), F32)
        h_ref[...] = jnp.zeros_like(h_ref)

    @pl.when(s < tiles)
    def _():
        _ffn_proj_lru_step(*refs, tt=tt, project=True)

    @pl.when(s == tiles)
    def _():
        _ffn_proj_lru_step(*refs, tt=tt, project=False)


def _ffn_proj_lru_step(x_ref, g1_ref, wg_ref, wu_ref, wd_ref, gmix_ref, win_ref, cos_ref, sin_ref,
                       cw_ref, cb_ref, wab_ref, bab_ref, lam_ref, gout_ref,
                       x_out_ref, q_ref, k_ref, v_ref, lru_ref, hlast_ref, utail_ref,
                       ug_ref, ext_ref, h_ref, hs_ref, *, tt, project):
    pad = SUBLANES
    lead_slot = pl.program_id(0) % 2
    lag_slot = 1 - lead_slot

    state = {"sumsq": jnp.zeros((tt, 1), F32)}
    scan_steps = [1 << i for i in range(tt.bit_length() - 1)]

    def lru_pieces(c, cols):
        def coeffs():
            ext_ref[pad:pad + tt, cols] = ug_ref[lag_slot, 0, :, cols]
            conv = cb_ref[:, cols]
            for j in range(CONV_WIDTH):
                start = pad - (CONV_WIDTH - 1) + j
                conv = conv + ext_ref[start:start + tt, cols] * cw_ref[j:j + 1, cols]
            ext_ref[0:pad, cols] = ext_ref[tt:tt + pad, cols]
            utail_ref[0, :, cols] = ext_ref[0:pad, cols]
            state[c] = _lru_coeffs(conv, wab_ref[c], bab_ref[c], lam_ref[:, cols])

        def scan(steps):
            def run():
                for step in steps:
                    state[c] = _scan_step(*state[c], tt, step)
            return run

        def finish():
            a_cum, h_local = state.pop(c)
            hs_ref[:, cols] = a_cum * h_ref[:, cols] + h_local
            h_ref[:, cols] = hs_ref[tt - 1:tt, cols]
            hlast_ref[0, :, cols] = h_ref[:, cols]
            y = hs_ref[:, cols] * jax.nn.gelu(ug_ref[lag_slot, 1, :, cols])
            hs_ref[:, cols] = y
            state["sumsq"] = state["sumsq"] + jnp.sum(y * y, axis=-1, keepdims=True)

        half = len(scan_steps) // 2
        return [coeffs, scan(scan_steps[:half]), scan(scan_steps[half:]), finish]

    vector_pieces = [p for c, cols in enumerate(_lane_chunks(LRU_WIDTH)) for p in lru_pieces(c, cols)]

    def ffn_in():
        if "xn" not in state:
            state["xn"] = _rms(x_ref[...], g1_ref[...]).astype(BF16)
        return state["xn"]

    def ffn_out():
        if "x" not in state:
            state["x"] = x_ref[...] + 0.5 * state.pop("acc")
            x_out_ref[...] = state["x"]
        return state["x"]

    def store_u(z):
        ug_ref[lead_slot, 0] = z

    def store_gate(z):
        ug_ref[lead_slot, 1] = z

    matmul_pieces = _ffn_pieces(ffn_in, wg_ref, wu_ref, wd_ref, state)
    matmul_pieces += _project_pieces(ffn_out, gmix_ref, win_ref, cos_ref, sin_ref, store_u, store_gate,
                                     q_ref, k_ref, v_ref)

    assert len(vector_pieces) == len(matmul_pieces)
    for vector_piece, matmul_piece in zip(vector_pieces, matmul_pieces):
        vector_piece()
        if project:
            matmul_piece()
    lru_ref[...] = hs_ref[...] * lax.rsqrt(state["sumsq"] * (1.0 / LRU_WIDTH) + EPS) * gout_ref[...]


def _ffn_proj_lru(x, seq, g1, wg, wu, wd, g_mix, w_in, cos, sin, conv_w, conv_b, wab, bab, lam, g_out,
                  other_g, other_u, other_d):
    rows = x.shape[0]
    n = rows // seq
    tt = min(LRU_TILE, seq)
    tiles_per_seq = seq // tt
    tiles = rows // tt
    pos_blocks = cos.shape[0] // tt
    lead = lambda s: jnp.minimum(s, tiles - 1)
    lag = lambda s: jnp.maximum(s - 1, 0)

    def lead_spec(width):
        return pl.BlockSpec((tt, width), lambda s: (lead(s), 0))

    pos_spec = pl.BlockSpec((tt, LANES), lambda s: (lead(s) % pos_blocks, 0))
    seq_spec = lambda r: pl.BlockSpec((1, r, LRU_WIDTH), lambda s: (lag(s) // tiles_per_seq, 0, 0))
    widths = (D_MODEL, Q_WIDTH, KV_WIDTH, KV_WIDTH)
    last_chunk = D_FF // FFN_CHUNK - 1
    assert last_chunk <= tiles
    chunk = lambda s: jnp.minimum(s, last_chunk)
    col_spec = pl.BlockSpec((D_MODEL, FFN_CHUNK), lambda s: (0, chunk(s)))
    row_spec = pl.BlockSpec((FFN_CHUNK, D_MODEL), lambda s: (chunk(s), 0))
    wide, tall = jax.ShapeDtypeStruct((D_MODEL, D_FF), BF16), jax.ShapeDtypeStruct((D_FF, D_MODEL), BF16)
    return pl.pallas_call(
        functools.partial(_ffn_proj_lru_body, tt=tt, tiles=tiles, tiles_per_seq=tiles_per_seq),
        grid=(tiles + 1,),
        in_specs=[lead_spec(D_MODEL), _const_spec((1, D_MODEL)), _const_spec((D_MODEL, D_FF)), _const_spec((D_MODEL, D_FF)),
                  _const_spec((D_FF, D_MODEL)), _const_spec((1, D_MODEL)), _const_spec((D_MODEL, IN_COLS)),
                  pos_spec, pos_spec] + _lru_weight_specs() + [col_spec, col_spec, row_spec],
        out_specs=[lead_spec(w) for w in widths]
        + [pl.BlockSpec((tt, LRU_WIDTH), lambda s: (lag(s), 0)), seq_spec(1), seq_spec(SUBLANES)]
        + [col_spec, col_spec, row_spec],
        out_shape=[jax.ShapeDtypeStruct((rows, w), F32) for w in widths]
        + [jax.ShapeDtypeStruct((rows, LRU_WIDTH), F32), jax.ShapeDtypeStruct((n, 1, LRU_WIDTH), F32),
           jax.ShapeDtypeStruct((n, SUBLANES, LRU_WIDTH), F32), wide, wide, tall],
        scratch_shapes=[pltpu.VMEM((2, 2, tt, LRU_WIDTH), F32), pltpu.VMEM((tt + SUBLANES, LRU_WIDTH), F32),
                        pltpu.VMEM((1, LRU_WIDTH), F32), pltpu.VMEM((tt, LRU_WIDTH), F32)],
        compiler_params=_params(1),
    )(x, g1, wg, wu, wd, g_mix, w_in, cos, sin, conv_w, conv_b, wab, bab, lam, g_out, other_g, other_u, other_d)


def _lru_sample_body(u_ref, gate_ref, cpad_ref, h0_ref, cw_ref, cb_ref, wab_ref, bab_ref, lam_ref, gout_ref,
                     o_ref, hs_ref, *, seg):
    rows = u_ref.shape[0]
    pos = lax.broadcasted_iota(jnp.int32, (rows, LANES), 0) % seg
    sumsq = jnp.zeros((rows, 1), F32)
    for c, cols in enumerate(_lane_chunks(LRU_WIDTH)):
        u = u_ref[:, cols]
        cpad = cpad_ref[:, cols]
        conv = cb_ref[:, cols] + u * cw_ref[CONV_WIDTH - 1:CONV_WIDTH, cols]
        for back in range(1, CONV_WIDTH):
            shifted = jnp.where(pos >= back, pltpu.roll(u, back, axis=0),
                                pltpu.roll(cpad, (back - seg) % rows, axis=0))
            conv = conv + shifted * cw_ref[CONV_WIDTH - 1 - back:CONV_WIDTH - back, cols]
        h = _lru_hidden(conv, h0_ref[:, cols], seg, wab_ref[c], bab_ref[c], lam_ref[:, cols])
        hs_ref[:, cols] = h
        y = h * jax.nn.gelu(gate_ref[:, cols])
        o_ref[:, cols] = y
        sumsq = sumsq + jnp.sum(y * y, axis=-1, keepdims=True)
    o_ref[...] = o_ref[...] * lax.rsqrt(sumsq * (1.0 / LRU_WIDTH) + EPS) * gout_ref[...]


def _lru_weight_specs():
    chunks = LRU_WIDTH // LANES
    return [_const_spec((CONV_WIDTH, LRU_WIDTH)), _const_spec((1, LRU_WIDTH)),
            _const_spec((chunks, LANES, 2 * LANES)), _const_spec((chunks, 1, 2 * LANES)),
            _const_spec((1, LRU_WIDTH)), _const_spec((1, LRU_WIDTH))]


def _lru_gate_chunks(w_a, b_a, w_i, b_i):
    chunks = LRU_WIDTH // LANES
    per = LRU_BLOCKS // chunks
    wa = w_a.reshape(chunks, per, *w_a.shape[1:])
    wi = w_i.reshape(chunks, per, *w_i.shape[1:])
    wab = jnp.stack([jnp.concatenate([_block_diag(wa[c]), _block_diag(wi[c])], axis=1) for c in range(chunks)])
    bab = jnp.concatenate([b_a.reshape(chunks, 1, LANES), b_i.reshape(chunks, 1, LANES)], axis=2)
    return wab.astype(BF16), bab


def _block_diag(w):
    nb, bi, bj = w.shape
    eye = jnp.eye(nb, dtype=w.dtype)
    return jnp.einsum('gij,gh->gihj', w, eye).reshape(nb * bi, nb * bj)


def _bdot_nt(a, b):
    return lax.dot_general(a, b, (((2,), (2,)), ((0,), (0,))), preferred_element_type=F32)


def _bdot(a, b):
    return lax.dot_general(a, b, (((2,), (1,)), ((0,), (0,))), preferred_element_type=F32)


def _swa_pieces(q, score, weigh, sink_ref, mask, g_out, store):
    group = ATTN_HEADS // KV_HEADS
    straight = [h for h in range(ATTN_HEADS) if (h % 2) == (h // group)]
    swapped = [h for h in range(ATTN_HEADS) if (h % 2) != (h // group)]
    scale = HEAD_DIM ** -0.5
    out_half = {}

    def low_lanes(shape):
        return lax.broadcasted_iota(jnp.int32, shape, 2) < HEAD_DIM

    def head_group(heads, swap):
        held = {}

        def scores():
            qv = q()
            n, r, _ = qv.shape
            low = low_lanes((n, r, LANES))
            zero = jnp.zeros((n, r, LANES), F32)
            qs = jnp.concatenate(
                [jnp.where(low if h % 2 == 0 else ~low, qv[:, :, (h // 2) * LANES:(h // 2 + 1) * LANES], zero)
                 for h in heads], axis=1)
            held["s"] = score(qs.astype(BF16), swap) * scale

        def probabilities():
            s = held.pop("s")
            r = s.shape[1] // len(heads)
            visible = mask()[None]
            probs = []
            for i, h in enumerate(heads):
                sh = jnp.where(visible, s[:, i * r:(i + 1) * r, :], -jnp.inf)
                sink = sink_ref[h]
                m = jnp.maximum(jnp.max(sh, axis=-1, keepdims=True), sink)
                e = jnp.exp(sh - m)
                denom = jnp.sum(e, axis=-1, keepdims=True) + jnp.exp(sink - m)
                probs.append(e / denom)
            held["p"] = jnp.concatenate(probs, axis=1).astype(BF16)

        def weighted_values():
            p = held.pop("p")
            r = p.shape[1] // len(heads)
            o = weigh(p, swap)
            for i, h in enumerate(heads):
                out_half[h] = o[:, i * r:(i + 1) * r, :]

        return [scores, probabilities, weighted_values]

    def finish():
        low = low_lanes(out_half[0].shape)
        out = jnp.concatenate([jnp.where(low, out_half[2 * j], out_half[2 * j + 1])
                               for j in range(Q_WIDTH // LANES)], axis=2)
        store(_rms(out, g_out))

    return head_group(straight, False) + head_group(swapped, True) + [finish]


def _band_mask(first_block):
    i = lax.broadcasted_iota(jnp.int32, (WINDOW, 2 * WINDOW), 0)
    j = lax.broadcasted_iota(jnp.int32, (WINDOW, 2 * WINDOW), 1)
    dist = i + WINDOW - j
    return (dist >= 0) & (dist < WINDOW) & (jnp.logical_not(first_block) | (j >= WINDOW))


def _swa_sample_body(sink_ref, q_ref, k_ref, v_ref, bk_ref, bv_ref, gout_ref, o_ref, nk_ref, nv_ref, *, s_len, past):
    def mask():
        qp = past + lax.broadcasted_iota(jnp.int32, (s_len, WINDOW + s_len), 0)
        col = lax.broadcasted_iota(jnp.int32, (s_len, WINDOW + s_len), 1)
        kp = jnp.where(col < WINDOW, past - WINDOW + col, past + col - WINDOW)
        dist = qp - kp
        return (dist >= 0) & (dist < WINDOW) & (kp >= 0)

    def halves(z, axis, swap):
        return pltpu.roll(z, HEAD_DIM, axis=axis) if swap else z

    def score(qs, swap):
        carried = _bdot(qs, halves(bk_ref[...], 1, swap).astype(BF16))
        fresh = _bdot_nt(qs, halves(k_ref[...], 2, swap).astype(BF16))
        return jnp.concatenate([carried, fresh], axis=2)

    def weigh(p, swap):
        return (_bdot_nt(p[:, :, :WINDOW], halves(bv_ref[...], 1, swap).astype(BF16))
                + _bdot(p[:, :, WINDOW:], halves(v_ref[...], 2, swap).astype(BF16)))

    def store(o):
        o_ref[...] = o.reshape(o_ref.shape)

    for piece in _swa_pieces(lambda: q_ref[...], score, weigh, sink_ref, mask, gout_ref[...], store):
        piece()

    def slide(window_ref, fresh_ref, out_ref):
        n = window_ref.shape[0]
        tail = jnp.concatenate([jnp.zeros((n, WINDOW - s_len, KV_WIDTH), F32), fresh_ref[...]], axis=1)
        lane = lax.broadcasted_iota(jnp.int32, (n, KV_WIDTH, WINDOW), 2)
        out_ref[...] = jnp.where(lane >= WINDOW - s_len, jnp.swapaxes(tail, 1, 2),
                                 pltpu.roll(window_ref[...], WINDOW - s_len, axis=2))

    slide(bk_ref, k_ref, nk_ref)
    slide(bv_ref, v_ref, nv_ref)


def _softmax(s):
    e = jnp.exp(s - jnp.max(s, axis=-1, keepdims=True))
    return e / jnp.sum(e, axis=-1, keepdims=True)


def _mix_and_query(x_ref, lru_ref, attn_ref, wout_ref, gx_ref, wcq_ref):
    x = (x_ref[...] + _dot(lru_ref[...].astype(BF16), wout_ref[:LRU_WIDTH, :])
         + _dot(attn_ref[...].astype(BF16), wout_ref[LRU_WIDTH:, :]))
    return x, _dot(_rms(x, gx_ref[...]).astype(BF16), wcq_ref[...])


def _swa_mix_ffn_body(sink_ref, q_ref, k_ref, v_ref, kp_ref, vp_ref, gattn_ref,
                      x_ref, lru_ref, wout_ref, gx_ref, wcq_ref, mk_ref, mv_ref, wco_ref,
                      g2_ref, wg_ref, wu_ref, wd_ref, gf_ref, o_ref, attn_ref,
                      *, tt, tiles, tiles_per_seq, final_norm):
    s = pl.program_id(0)
    refs = (sink_ref, q_ref, k_ref, v_ref, kp_ref, vp_ref, gattn_ref, x_ref, lru_ref, wout_ref, gx_ref, wcq_ref,
            mk_ref, mv_ref, wco_ref, g2_ref, wg_ref, wu_ref, wd_ref, gf_ref, o_ref, attn_ref)
    step = functools.partial(_swa_mix_ffn_step, *refs, tt=tt, tiles_per_seq=tiles_per_seq, final_norm=final_norm)

    @pl.when(s == 0)
    def _():
        step(attend=True, layer=False)

    @pl.when((s > 0) & (s < tiles))
    def _():
        step(attend=True, layer=True)

    @pl.when(s == tiles)
    def _():
        step(attend=False, layer=True)


def _swa_mix_ffn_step(sink_ref, q_ref, k_ref, v_ref, kp_ref, vp_ref, gattn_ref,
                      x_ref, lru_ref, wout_ref, gx_ref, wcq_ref, mk_ref, mv_ref, wco_ref,
                      g2_ref, wg_ref, wu_ref, wd_ref, gf_ref, o_ref, attn_ref,
                      *, tt, tiles_per_seq, final_norm, attend, layer):
    s = pl.program_id(0)
    lead_slot = s % 2
    lag_slot = 1 - lead_slot

    blocks = tt // WINDOW
    first_pos_block = lax.rem(s, tiles_per_seq) * blocks
    attention_pieces = []
    for j in range(blocks):
        rows = slice(j * WINDOW, (j + 1) * WINDOW)

        def band(ref, prev_ref, swap, j=j, rows=rows):
            prev = prev_ref[...] if j == 0 else ref[(j - 1) * WINDOW:j * WINDOW, :]
            both = jnp.concatenate([prev, ref[rows, :]], axis=0)[None]
            return (pltpu.roll(both, HEAD_DIM, axis=2) if swap else both).astype(BF16)

        def score(qs, swap, band=band):
            return _bdot_nt(qs, band(k_ref, kp_ref, swap))

        def weigh(p, swap, band=band):
            return _bdot(p, band(v_ref, vp_ref, swap))

        def store(o, rows=rows):
            attn_ref[lead_slot, rows, :] = o[0]

        attention_pieces += _swa_pieces(lambda rows=rows: q_ref[rows, :][None], score, weigh, sink_ref,
                                        lambda j=j: _band_mask(first_pos_block + j == 0), gattn_ref[...], store)

    state = {}
    scale = X_HEAD_DIM ** -0.5

    def mix():
        state["x"] = (x_ref[...] + _dot(lru_ref[...].astype(BF16), wout_ref[:LRU_WIDTH, :])
                      + _dot(attn_ref[lag_slot].astype(BF16), wout_ref[LRU_WIDTH:, :]))

    def query():
        state["q"] = _dot(_rms(state["x"], gx_ref[...]).astype(BF16), wcq_ref[...])

    def memory_head(h):
        def run():
            cols = slice(h * X_HEAD_DIM, (h + 1) * X_HEAD_DIM)
            sc = _dot_nt(state["q"][:, cols].astype(BF16), mk_ref[0, :, cols].astype(BF16)) * scale
            state["o", h] = _dot(_softmax(sc).astype(BF16), mv_ref[0, :, cols].astype(BF16))
        return run

    def memory_out():
        o = jnp.concatenate([state.pop(("o", h)) for h in range(X_HEADS)], axis=1)
        state["x"] = state["x"] + _dot(o.astype(BF16), wco_ref[...])
        state["xn"] = _rms(state["x"], g2_ref[...]).astype(BF16)

    def finish():
        y = state["x"] + 0.5 * state["acc"]
        o_ref[...] = _rms(y, gf_ref[...]) if final_norm else y

    layer_pieces = ([mix, query] + [memory_head(h) for h in range(X_HEADS)] + [memory_out]
                    + _ffn_pieces(lambda: state["xn"], wg_ref, wu_ref, wd_ref, state) + [finish])
    _interleave(layer_pieces if layer else [], attention_pieces if attend else [])


def _swa_mix_ffn(q, k, v, sink, g_attn, x, lru_out, seq, w_out, g_x, w_cq, mk, mv, w_co, g2, wg, wu, wd, g_final,
                 final_norm):
    rows = x.shape[0]
    tt = min(LRU_TILE, seq)
    tiles_per_seq = seq // tt
    tiles = rows // tt
    blocks = tt // WINDOW
    lead = lambda s: jnp.minimum(s, tiles - 1)
    lag = lambda s: jnp.maximum(s - 1, 0)

    def lead_spec(width):
        return pl.BlockSpec((tt, width), lambda s: (lead(s), 0))

    def lag_spec(width):
        return pl.BlockSpec((tt, width), lambda s: (lag(s), 0))

    prev_spec = pl.BlockSpec((WINDOW, KV_WIDTH), lambda s: (jnp.maximum(lead(s) * blocks - 1, 0), 0))
    mem_spec = pl.BlockSpec((1, N_MEM, D_MODEL), lambda s: (lag(s) // tiles_per_seq, 0, 0))
    w_spec = _const_spec((D_MODEL, D_MODEL))
    return pl.pallas_call(
        functools.partial(_swa_mix_ffn_body, tt=tt, tiles=tiles, tiles_per_seq=tiles_per_seq, final_norm=final_norm),
        grid=(tiles + 1,),
        in_specs=[pl.BlockSpec(memory_space=pltpu.SMEM), lead_spec(Q_WIDTH), lead_spec(KV_WIDTH), lead_spec(KV_WIDTH),
                  prev_spec, prev_spec, _const_spec((1, Q_WIDTH)),
                  lag_spec(D_MODEL), lag_spec(LRU_WIDTH), w_spec, _const_spec((1, D_MODEL)), w_spec,
                  mem_spec, mem_spec, w_spec,
                  _const_spec((1, D_MODEL)), _const_spec((D_MODEL, D_FF)), _const_spec((D_MODEL, D_FF)),
                  _const_spec((D_FF, D_MODEL)), _const_spec((1, D_MODEL))],
        out_specs=lag_spec(D_MODEL),
        out_shape=jax.ShapeDtypeStruct((rows, D_MODEL), F32),
        scratch_shapes=[pltpu.VMEM((2, tt, Q_WIDTH), F32)],
        compiler_params=_params(1),
    )(sink, q, k, v, k, v, g_attn, x, lru_out, w_out, g_x, w_cq, mk, mv, w_co, g2, wg, wu, wd, g_final)


def _sample_mixer_body(sink_ref, u_ref, gate_ref, cpad_ref, h0_ref, q_ref, k_ref, v_ref, bk_ref, bv_ref, x_ref,
                       cw_ref, cb_ref, wab_ref, bab_ref, lam_ref, glru_ref, gattn_ref, wout_ref, gx_ref, wcq_ref,
                       x_out_ref, xq_ref, hs_ref, nk_ref, nv_ref, lru_ref, attn_ref, *, seg, past):
    _lru_sample_body(u_ref, gate_ref, cpad_ref, h0_ref, cw_ref, cb_ref, wab_ref, bab_ref, lam_ref, glru_ref,
                     lru_ref, hs_ref, seg=seg)
    _swa_sample_body(sink_ref, q_ref, k_ref, v_ref, bk_ref, bv_ref, gattn_ref, attn_ref, nk_ref, nv_ref,
                     s_len=seg, past=past)
    x_out_ref[...], xq_ref[...] = _mix_and_query(x_ref, lru_ref, attn_ref, wout_ref, gx_ref, wcq_ref)


def _channel_major(window):
    n = window.shape[0]
    return jnp.transpose(window, (0, 2, 3, 1)).reshape(n, KV_WIDTH, WINDOW)


def _position_major(window):
    n = window.shape[0]
    return jnp.transpose(window.reshape(n, KV_HEADS, HEAD_DIM, WINDOW), (0, 3, 1, 2))


def _sample_mixer(u, gate, conv_pad, h0_rep, q, k, v, buf_k, buf_v, x, sink, lru_w, g_attn, w_out, g_x, w_cq, past):
    n, seg, _ = q.shape
    sb = min(SAMPLE_SEQ_TILE, n)
    rows = sb * seg

    def row_spec(width):
        return pl.BlockSpec((rows, width), lambda i: (i, 0))

    def seq_spec(steps, width):
        return pl.BlockSpec((sb, steps, width), lambda i: (i, 0, 0))

    buf_spec = seq_spec(WINDOW, KV_WIDTH)
    w_spec = _const_spec((D_MODEL, D_MODEL))
    total = n * seg
    return pl.pallas_call(
        functools.partial(_sample_mixer_body, seg=seg, past=past),
        grid=(n // sb,),
        in_specs=[pl.BlockSpec(memory_space=pltpu.SMEM)] + [row_spec(LRU_WIDTH)] * 4
        + [seq_spec(seg, Q_WIDTH), seq_spec(seg, KV_WIDTH), seq_spec(seg, KV_WIDTH), buf_spec, buf_spec,
           row_spec(D_MODEL)] + _lru_weight_specs()
        + [_const_spec((1, Q_WIDTH)), w_spec, _const_spec((1, D_MODEL)), w_spec],
        out_specs=[row_spec(D_MODEL), row_spec(D_MODEL), row_spec(LRU_WIDTH), buf_spec, buf_spec],
        out_shape=[jax.ShapeDtypeStruct((total, D_MODEL), F32), jax.ShapeDtypeStruct((total, D_MODEL), F32),
                   jax.ShapeDtypeStruct((total, LRU_WIDTH), F32),
                   jax.ShapeDtypeStruct((n, WINDOW, KV_WIDTH), F32), jax.ShapeDtypeStruct((n, WINDOW, KV_WIDTH), F32)],
        scratch_shapes=[pltpu.VMEM((rows, LRU_WIDTH), F32), pltpu.VMEM((rows, Q_WIDTH), F32)],
        compiler_params=_params(1),
    )(sink, u, gate, conv_pad, h0_rep, q, k, v, buf_k, buf_v, x, *lru_w, g_attn, w_out, g_x, w_cq)


def _xattn_cache_pieces(q_ref, mk_ref, mv_ref, o_ref):
    sb, steps, _ = q_ref.shape
    blocks = D_MODEL // LANES
    chunks = X_HEAD_DIM // LANES
    width = mk_ref.shape[1]
    block_cls = [(j % chunks) * X_HEADS + j // chunks for j in range(blocks)]
    scale = X_HEAD_DIM ** -0.5
    held = {}

    def lane_class():
        return lax.broadcasted_iota(jnp.int32, (sb, steps, width), 2) % blocks

    def scores():
        qs = jnp.concatenate([q_ref[:, :, j * LANES:(j + 1) * LANES] for j in range(blocks)], axis=1)
        held["s"] = _bdot_nt(qs.astype(BF16), mk_ref[...].astype(BF16)) * scale

    def probabilities():
        s = held.pop("s")
        cls = lane_class()
        part = jnp.zeros((sb, steps, width), F32)
        for j in range(blocks):
            part = part + jnp.where(cls == block_cls[j], s[:, j * steps:(j + 1) * steps, :], 0.0)
        score = part + pltpu.roll(part, width - X_HEADS, axis=2)
        top = jnp.zeros((sb, steps, width), F32)
        for h in range(X_HEADS):
            mine = cls == h
            top = jnp.where(mine, jnp.max(jnp.where(mine, score, -jnp.inf), axis=-1, keepdims=True), top)
        e = jnp.where(cls < X_HEADS, jnp.exp(score - top), 0.0)
        denom = jnp.ones((sb, steps, width), F32)
        for h in range(X_HEADS):
            mine = cls == h
            denom = jnp.where(mine, jnp.sum(jnp.where(mine, e, 0.0), axis=-1, keepdims=True), denom)
        p = e / denom
        p = p + pltpu.roll(p, X_HEADS, axis=2)
        held["p"] = jnp.concatenate([jnp.where(cls == block_cls[j], p, 0.0) for j in range(blocks)],
                                    axis=1).astype(BF16)

    def weighted_values():
        o = _bdot(held.pop("p"), mv_ref[...].astype(BF16))
        for j in range(blocks):
            o_ref[:, :, j * LANES:(j + 1) * LANES] = o[:, j * steps:(j + 1) * steps, :]

    return [scores, probabilities, weighted_values]


def _xattn_cache_body(q_ref, mk_ref, mv_ref, o_ref):
    for piece in _xattn_cache_pieces(q_ref, mk_ref, mv_ref, o_ref):
        piece()


def _interleave_chunks(cache):
    n = cache.shape[0]
    chunks = X_HEAD_DIM // LANES
    c = cache.reshape(n, N_MEM, X_HEADS, chunks, LANES)
    return jnp.transpose(c, (0, 1, 3, 2, 4)).reshape(n, N_MEM * chunks * X_HEADS, LANES)


def _xattn_cache(q, cache_k, cache_v):
    count, s_len, _ = q.shape
    sb = min(XATTN_SEQ_TILE, count)
    q_spec = pl.BlockSpec((sb, s_len, D_MODEL), lambda i: (i, 0, 0))
    rows = cache_k.shape[1]
    mem_spec = pl.BlockSpec((sb, rows, LANES), lambda i: (i, 0, 0))
    return pl.pallas_call(
        _xattn_cache_body,
        grid=(count // sb,),
        in_specs=[q_spec, mem_spec, mem_spec],
        out_specs=q_spec,
        out_shape=jax.ShapeDtypeStruct((count, s_len, D_MODEL), F32),
        compiler_params=_params(1),
    )(q, cache_k, cache_v)


def _mem_kv_body(mem_ref, g_ref, wk_ref, wv_ref, k_ref, v_ref):
    mm = _rms(mem_ref[...], g_ref[...]).astype(BF16)
    k_ref[...] = _dot(mm, wk_ref[...].astype(BF16))
    v_ref[...] = _dot(mm, wv_ref[...].astype(BF16))


def _mem_kv(mem, g, w_ck, w_cv):
    rows = mem.shape[0]
    tm = min(ROW_TILE, rows)
    row_spec = pl.BlockSpec((tm, D_MODEL), lambda i: (i, 0))
    w_spec = _const_spec((D_MODEL, D_MODEL))
    return pl.pallas_call(
        _mem_kv_body,
        grid=(rows // tm,),
        in_specs=[row_spec, _const_spec((1, D_MODEL)), w_spec, w_spec],
        out_specs=[row_spec, row_spec],
        out_shape=[jax.ShapeDtypeStruct((rows, D_MODEL), F32)] * 2,
        compiler_params=_params(1),
    )(mem, g, w_ck, w_cv)


def kernel(x_prompt, x_sample, mem_prompt, cache_mem_k, cache_mem_v, cache_swa_k, cache_swa_v, state_conv, state_lru_h,
           g_ffn1, w1_gate, w1_up, w1_down, g_mix, w_in, conv_w, conv_b, w_a, b_a, w_i, b_i, lam, sink,
           g_lru_out, g_attn_out, w_out, g_xattn, g_mem, w_cq, w_ck, w_cv, w_co, g_ffn2, w2_gate, w2_up, w2_down,
           g_final):
    nbp, seq, _ = x_prompt.shape
    nbs, dec_seq, _ = x_sample.shape
    depth = g_ffn1.shape[0]
    past = PAST_LEN
    cos_p, sin_p = _rope_tables(np.arange(seq, dtype=np.int32))
    sample_pos_rows = min(ROW_TILE, nbs * dec_seq)
    cos_s, sin_s = _rope_tables(np.tile(past + np.arange(dec_seq, dtype=np.int32), sample_pos_rows // dec_seq))

    xp = x_prompt.reshape(nbp * seq, D_MODEL)
    xs = x_sample.reshape(nbs * dec_seq, D_MODEL)
    g_fin = g_final.reshape(1, D_MODEL)
    row = lambda a: a.reshape(1, -1)
    outs = [[] for _ in range(10)]
    for l in range(depth):
        last = l == depth - 1
        bf = lambda a: a[l].astype(BF16)
        win, wout, wcq, wco = bf(w_in), bf(w_out), bf(w_cq), bf(w_co)
        wab, bab = _lru_gate_chunks(w_a[l], b_a[l], w_i[l], b_i[l])
        lru_w = (conv_w[l], row(conv_b[l]), wab, bab, row(lam[l]), row(g_lru_out[l]))

        xs, w1g, w1u, w1d = _ffn_cast(xs, row(g_ffn1[l]), w1_gate[l], w1_up[l], w1_down[l])
        u_s, gate, q, k, v = _proj(xs, row(g_mix[l]), win, cos_s, sin_s)
        conv_pad = jnp.pad(state_conv[l], ((0, 0), (dec_seq - (CONV_WIDTH - 1), 0), (0, 0)))
        h0_rep = jnp.repeat(state_lru_h[l], dec_seq, axis=0)
        per_seq = lambda a: a.reshape(nbs, dec_seq, a.shape[-1])
        xs, xq, hs, new_k, new_v = _sample_mixer(
            u_s, gate, conv_pad.reshape(nbs * dec_seq, LRU_WIDTH), h0_rep, per_seq(q), per_seq(k), per_seq(v),
            _channel_major(cache_swa_k[l]), _channel_major(cache_swa_v[l]), xs,
            sink[l], lru_w, row(g_attn_out[l]), wout, row(g_xattn[l]), wcq, past)
        xq, cache_k, cache_v = per_seq(xq), _interleave_chunks(cache_mem_k[l]), _interleave_chunks(cache_mem_v[l])

        mk_p, mv_p = _mem_kv(mem_prompt.reshape(nbp * N_MEM, D_MODEL), row(g_mem[l]), w_ck[l], w_cv[l])
        mk_p = mk_p.reshape(nbp, N_MEM, D_MODEL)
        mv_p = mv_p.reshape(nbp, N_MEM, D_MODEL)
        xp, q, k, v, lru_out, h_last, u_tail, w2g, w2u, w2d = _ffn_proj_lru(
            xp, seq, row(g_ffn1[l]), w1g, w1u, w1d, row(g_mix[l]), win, cos_p, sin_p, *lru_w,
            w2_gate[l], w2_up[l], w2_down[l])
        k3 = k.reshape(nbp, seq, KV_WIDTH)
        v3 = v.reshape(nbp, seq, KV_WIDTH)
        xp = _swa_mix_ffn(q, k, v, sink[l], row(g_attn_out[l]), xp, lru_out, seq, wout, row(g_xattn[l]), wcq,
                          mk_p, mv_p, wco, row(g_ffn2[l]), w2g, w2u, w2d, g_fin, last)
        outs[0].append(mk_p.reshape(nbp, N_MEM, X_HEADS, X_HEAD_DIM))
        outs[1].append(mv_p.reshape(nbp, N_MEM, X_HEADS, X_HEAD_DIM))
        outs[2].append(k3[:, -WINDOW:].reshape(nbp, WINDOW, KV_HEADS, HEAD_DIM))
        outs[3].append(v3[:, -WINDOW:].reshape(nbp, WINDOW, KV_HEADS, HEAD_DIM))
        outs[4].append(u_tail[:, -(CONV_WIDTH - 1):])
        outs[5].append(h_last.reshape(nbp, LRU_WIDTH))

        xo = _xattn_cache(xq, cache_k, cache_v)
        xs = _ffn(xs, row(g_ffn2[l]), w2g, w2u, w2d, g_fin, last, xo.reshape(nbs * dec_seq, D_MODEL), wco)
        outs[6].append(_position_major(new_k))
        outs[7].append(_position_major(new_v))
        outs[8].append(u_s.reshape(nbs, dec_seq, LRU_WIDTH)[:, -(CONV_WIDTH - 1):])
        outs[9].append(hs.reshape(nbs, dec_seq, LRU_WIDTH)[:, -1])

    return (xp.reshape(nbp, seq, D_MODEL), xs.reshape(nbs, dec_seq, D_MODEL)) + tuple(jnp.stack(o) for o in outs)
```

```python
import functools

import jax
import jax.numpy as jnp
import numpy as np
from jax import lax
from jax.experimental import pallas as pl
from jax.experimental.pallas import tpu as pltpu

F32 = jnp.float32
BF16 = jnp.bfloat16

D_MODEL = 1024
LRU_WIDTH = 512
LRU_BLOCKS = 8
CONV_WIDTH = 4
LRU_C = 8.0
ATTN_HEADS = 8
HEAD_DIM = 64
KV_HEADS = 2
WINDOW = 128
PAST_LEN = 8192
ROPE_THETA = 10000.0
N_MEM = 256
X_HEADS = 4
X_HEAD_DIM = 256
D_FF = 2816
EPS = 1e-6
Q_WIDTH = ATTN_HEADS * HEAD_DIM
KV_WIDTH = KV_HEADS * HEAD_DIM
IN_COLS = 2 * LRU_WIDTH + Q_WIDTH + 2 * KV_WIDTH

LANES = 128
SUBLANES = 8
VMEM_LIMIT = 56 * 1024 * 1024

ROW_TILE = 512
LRU_TILE = 512
FFN_CHUNK = 256
SAMPLE_SEQ_TILE = 32
XATTN_SEQ_TILE = 4
XATTN_RING = 3


def _params(n_axes):
    return pltpu.CompilerParams(dimension_semantics=("arbitrary",) * n_axes, vmem_limit_bytes=VMEM_LIMIT)


def _const_spec(shape):
    return pl.BlockSpec(shape, lambda *_: (0,) * len(shape), pipeline_mode=pl.Buffered(1))


def _rms(x, g):
    return x * lax.rsqrt(jnp.mean(x * x, axis=-1, keepdims=True) + EPS) * g


def _dot(a, b):
    return jnp.dot(a, b, preferred_element_type=F32)


def _dot_nt(a, b):
    return lax.dot_general(a, b, (((1,), (1,)), ((), ())), preferred_element_type=F32)


def _proj_ffn_body(x_ref, a_ref, wa_ref, g_ref, wg_ref, wu_ref, wd_ref, gf_ref, o_ref, x1_ref, xn_ref, acc_ref,
                   *, final_norm):
    f = pl.program_id(0)

    @pl.when(f == 0)
    def _():
        x = x_ref[...] + _dot(a_ref[...].astype(BF16), wa_ref[...])
        x1_ref[...] = x
        xn_ref[...] = _rms(x, g_ref[...]).astype(BF16)
        acc_ref[...] = jnp.zeros_like(acc_ref)

    xn = xn_ref[...]
    gate = _dot(xn, wg_ref[...])
    up = _dot(xn, wu_ref[...])
    acc_ref[...] += _dot((gate * jax.nn.sigmoid(gate) * up).astype(BF16), wd_ref[...])

    @pl.when(f == pl.num_programs(0) - 1)
    def _():
        y = x1_ref[...] + 0.5 * acc_ref[...]
        o_ref[...] = _rms(y, gf_ref[...]) if final_norm else y


def _ffn(x, g, wg, wu, wd, g_final, final_norm, attn, w_attn):
    rows = x.shape[0]
    x_spec = _const_spec((rows, D_MODEL))
    col_spec = pl.BlockSpec((D_MODEL, FFN_CHUNK), lambda f: (0, f))
    row_spec = pl.BlockSpec((FFN_CHUNK, D_MODEL), lambda f: (f, 0))
    return pl.pallas_call(
        functools.partial(_proj_ffn_body, final_norm=final_norm),
        grid=(D_FF // FFN_CHUNK,),
        in_specs=[x_spec, x_spec, _const_spec((D_MODEL, D_MODEL)), _const_spec((1, D_MODEL)), col_spec, col_spec,
                  row_spec, _const_spec((1, D_MODEL))],
        out_specs=pl.BlockSpec((rows, D_MODEL), lambda f: (0, 0)),
        out_shape=jax.ShapeDtypeStruct((rows, D_MODEL), F32),
        scratch_shapes=[pltpu.VMEM((rows, D_MODEL), F32), pltpu.VMEM((rows, D_MODEL), BF16),
                        pltpu.VMEM((rows, D_MODEL), F32)],
        compiler_params=_params(1),
    )(x, attn, w_attn, g, wg, wu, wd, g_final)


def _ffn_cast_body(x_ref, g_ref, wg_ref, wu_ref, wd_ref, o_ref, wg16_ref, wu16_ref, wd16_ref, xn_ref, acc_ref):
    f = pl.program_id(0)

    @pl.when(f == 0)
    def _():
        xn_ref[...] = _rms(x_ref[...], g_ref[...]).astype(BF16)
        acc_ref[...] = jnp.zeros_like(acc_ref)

    for src, dst in ((wg_ref, wg16_ref), (wu_ref, wu16_ref), (wd_ref, wd16_ref)):
        dst[...] = src[...].astype(BF16)
    xn = xn_ref[...]
    gate = _dot(xn, wg16_ref[...])
    up = _dot(xn, wu16_ref[...])
    acc_ref[...] += _dot((gate * jax.nn.sigmoid(gate) * up).astype(BF16), wd16_ref[...])

    @pl.when(f == pl.num_programs(0) - 1)
    def _():
        o_ref[...] = x_ref[...] + 0.5 * acc_ref[...]


def _ffn_cast(x, g, wg, wu, wd):
    rows = x.shape[0]
    chunks = D_FF // FFN_CHUNK
    x_spec = _const_spec((rows, D_MODEL))
    col_spec = pl.BlockSpec((D_MODEL, FFN_CHUNK), lambda f: (0, f))
    row_spec = pl.BlockSpec((FFN_CHUNK, D_MODEL), lambda f: (f, 0))
    w_specs = [col_spec, col_spec, row_spec]
    wide, tall = jax.ShapeDtypeStruct((D_MODEL, D_FF), BF16), jax.ShapeDtypeStruct((D_FF, D_MODEL), BF16)
    return pl.pallas_call(
        _ffn_cast_body,
        grid=(chunks,),
        in_specs=[x_spec, _const_spec((1, D_MODEL))] + w_specs,
        out_specs=[pl.BlockSpec((rows, D_MODEL), lambda f: (0, 0))] + w_specs,
        out_shape=[jax.ShapeDtypeStruct((rows, D_MODEL), F32), wide, wide, tall],
        scratch_shapes=[pltpu.VMEM((rows, D_MODEL), BF16), pltpu.VMEM((rows, D_MODEL), F32)],
        compiler_params=_params(1),
    )(x, g, wg, wu, wd)


def _rope(z, cos, sin_signed):
    half = HEAD_DIM // 2
    lane = lax.broadcasted_iota(jnp.int32, z.shape, 1)
    first_half = (lane % HEAD_DIM) < half
    partner = jnp.where(first_half, pltpu.roll(z, LANES - half, axis=1), pltpu.roll(z, half, axis=1))
    return z * cos + partner * sin_signed


def _project_pieces(x, g_ref, w_ref, cos_ref, sin_ref, store_u, store_gate, q_ref, k_ref, v_ref):
    o_gate, o_q, o_k, o_v = LRU_WIDTH, 2 * LRU_WIDTH, 2 * LRU_WIDTH + Q_WIDTH, 2 * LRU_WIDTH + Q_WIDTH + KV_WIDTH
    xn = []

    def normed():
        if not xn:
            xn.append(_rms(x(), g_ref[...]).astype(BF16))
        return xn[0]

    def rope_into(ref, z):
        for j in range(z.shape[1] // LANES):
            cols = slice(j * LANES, (j + 1) * LANES)
            ref[:, cols] = _rope(z[:, cols], cos_ref[...], sin_ref[...])

    def store_v(z):
        v_ref[...] = z

    return [lambda: rope_into(q_ref, _dot(normed(), w_ref[:, o_q:o_k])),
            lambda: rope_into(k_ref, _dot(normed(), w_ref[:, o_k:o_v])),
            lambda: store_v(_dot(normed(), w_ref[:, o_v:])),
            lambda: store_u(_dot(normed(), w_ref[:, :o_gate])),
            lambda: store_gate(_dot(normed(), w_ref[:, o_gate:o_q]))]


def _proj_body(x_ref, g_ref, w_ref, cos_ref, sin_ref, u_ref, gate_ref, q_ref, k_ref, v_ref):
    def store_u(z):
        u_ref[...] = z

    def store_gate(z):
        gate_ref[...] = z

    for piece in _project_pieces(lambda: x_ref[...], g_ref, w_ref, cos_ref, sin_ref, store_u, store_gate,
                                 q_ref, k_ref, v_ref):
        piece()


def _proj(x, g, w_in, cos, sin):
    rows = x.shape[0]
    tm = min(ROW_TILE, rows, cos.shape[0])
    pos_blocks = cos.shape[0] // tm

    def row_spec(width):
        return pl.BlockSpec((tm, width), lambda i: (i, 0))

    pos_spec = pl.BlockSpec((tm, LANES), lambda i: (i % pos_blocks, 0))
    widths = (LRU_WIDTH, LRU_WIDTH, Q_WIDTH, KV_WIDTH, KV_WIDTH)
    return pl.pallas_call(
        _proj_body,
        grid=(rows // tm,),
        in_specs=[row_spec(D_MODEL), _const_spec((1, D_MODEL)), _const_spec((D_MODEL, IN_COLS)), pos_spec, pos_spec],
        out_specs=[row_spec(w) for w in widths],
        out_shape=[jax.ShapeDtypeStruct((rows, w), F32) for w in widths],
        compiler_params=_params(1),
    )(x, g, w_in, cos, sin)


def _rope_tables(pos):
    half = HEAD_DIM // 2
    inv = ROPE_THETA ** (-np.arange(half, dtype=np.float64) / half)
    ang = pos.astype(np.float64)[:, None] * inv[None, :]
    cos = np.cos(ang)
    sin = np.sin(ang)
    reps = LANES // HEAD_DIM
    return (jnp.asarray(np.tile(np.concatenate([cos, cos], axis=-1), (1, reps)), dtype=F32),
            jnp.asarray(np.tile(np.concatenate([-sin, sin], axis=-1), (1, reps)), dtype=F32))


def _softplus(x):
    return jnp.maximum(x, 0.0) + jnp.log1p(jnp.exp(-jnp.abs(x)))


def _lru_coeffs(conv, wab, bab, lam):
    w = conv.shape[1]
    gates = _dot(conv.astype(BF16), wab) + bab
    r = jax.nn.sigmoid(gates[:, :w])
    gi = jax.nn.sigmoid(gates[:, w:])
    log_a = -LRU_C * r * _softplus(-lam)
    a = jnp.exp(log_a)
    b = jnp.sqrt(-jnp.tanh(log_a) * (a * a + 1.0)) * (gi * conv)
    return a, b


def _segment_scan(a, b, seg):
    step = 1
    while step < seg:
        a, b = _scan_step(a, b, seg, step)
        step *= 2
    return a, b


def _scan_step(a, b, seg, step):
    pos = lax.broadcasted_iota(jnp.int32, a.shape, 0) % seg
    live = pos >= step
    a_prev = pltpu.roll(a, step, axis=0)
    b_prev = pltpu.roll(b, step, axis=0)
    return jnp.where(live, a * a_prev, a), jnp.where(live, a * b_prev + b, b)


def _lru_hidden(conv, h_in, seg, wab, bab, lam):
    a, b = _lru_coeffs(conv, wab, bab, lam)
    a_cum, h_local = _segment_scan(a, b, seg)
    return a_cum * h_in + h_local


def _lane_chunks(width):
    return [slice(c * LANES, (c + 1) * LANES) for c in range(width // LANES)]


def _ffn_pieces(normed, wg_ref, wu_ref, wd_ref, state):
    cols = [slice(f, min(f + FFN_CHUNK, D_FF)) for f in range(0, D_FF, FFN_CHUNK)]

    def gate_up(f):
        return _dot(normed(), wg_ref[:, cols[f]]), _dot(normed(), wu_ref[:, cols[f]])

    def piece(f):
        def run():
            gate, up = state.pop("gate_up") if "gate_up" in state else gate_up(f)
            if f + 1 < len(cols):
                state["gate_up"] = gate_up(f + 1)
            part = _dot((gate * jax.nn.sigmoid(gate) * up).astype(BF16), wd_ref[cols[f], :])
            state["acc"] = part if "acc" not in state else state["acc"] + part
        return run

    return [piece(f) for f in range(len(cols))]


def _interleave(primary, secondary):
    due = [((i + 1) * len(primary)) // (len(secondary) + 1) for i in range(len(secondary))]
    pending = list(zip(due, secondary))
    for i, piece in enumerate(primary):
        while pending and pending[0][0] <= i:
            pending.pop(0)[1]()
        piece()
    for _, piece in pending:
        piece()


def _ffn_proj_lru_body(x_ref, g1_ref, wg_ref, wu_ref, wd_ref, gmix_ref, win_ref, cos_ref, sin_ref,
                       cw_ref, cb_ref, wab_ref, bab_ref, lam_ref, gout_ref, og_ref, ou_ref, od_ref,
                       x_out_ref, q_ref, k_ref, v_ref, lru_ref, hlast_ref, utail_ref, og16_ref, ou16_ref, od16_ref,
                       ug_ref, ext_ref, h_ref, hs_ref, *, tt, tiles, tiles_per_seq):
    s = pl.program_id(0)

    @pl.when(s < D_FF // FFN_CHUNK)
    def _():
        for src, dst in ((og_ref, og16_ref), (ou_ref, ou16_ref), (od_ref, od16_ref)):
            dst[...] = src[...].astype(BF16)

    refs = (x_ref, g1_ref, wg_ref, wu_ref, wd_ref, gmix_ref, win_ref, cos_ref, sin_ref,
            cw_ref, cb_ref, wab_ref, bab_ref, lam_ref, gout_ref,
            x_out_ref, q_ref, k_ref, v_ref, lru_ref, hlast_ref, utail_ref, ug_ref, ext_ref, h_ref, hs_ref)

    @pl.when(s == 0)
    def _():
        ug_ref[1] = jnp.zeros((2, tt, LRU_WIDTH), F32)

    @pl.when((s == 0) | (lax.rem(s - 1, tiles_per_seq) == 0))
    def _():
        ext_ref[0:SUBLANES, :] = jnp.zeros((SUBLANES, LRU_WIDTH), F32)
        h_ref[...] = jnp.zeros_like(h_ref)

    @pl.when(s < tiles)
    def _():
        _ffn_proj_lru_step(*refs, tt=tt, project=True)

    @pl.when(s == tiles)
    def _():
        _ffn_proj_lru_step(*refs, tt=tt, project=False)


def _ffn_proj_lru_step(x_ref, g1_ref, wg_ref, wu_ref, wd_ref, gmix_ref, win_ref, cos_ref, sin_ref,
                       cw_ref, cb_ref, wab_ref, bab_ref, lam_ref, gout_ref,
                       x_out_ref, q_ref, k_ref, v_ref, lru_ref, hlast_ref, utail_ref,
                       ug_ref, ext_ref, h_ref, hs_ref, *, tt, project):
    pad = SUBLANES
    lead_slot = pl.program_id(0) % 2
    lag_slot = 1 - lead_slot

    state = {"sumsq": jnp.zeros((tt, 1), F32)}
    scan_steps = [1 << i for i in range(tt.bit_length() - 1)]

    def lru_pieces(c, cols):
        def coeffs():
            ext_ref[pad:pad + tt, cols] = ug_ref[lag_slot, 0, :, cols]
            conv = cb_ref[:, cols]
            for j in range(CONV_WIDTH):
                start = pad - (CONV_WIDTH - 1) + j
                conv = conv + ext_ref[start:start + tt, cols] * cw_ref[j:j + 1, cols]
            ext_ref[0:pad, cols] = ext_ref[tt:tt + pad, cols]
            utail_ref[0, :, cols] = ext_ref[0:pad, cols]
            state[c] = _lru_coeffs(conv, wab_ref[c], bab_ref[c], lam_ref[:, cols])

        def scan(steps):
            def run():
                for step in steps:
                    state[c] = _scan_step(*state[c], tt, step)
            return run

        def finish():
            a_cum, h_local = state.pop(c)
            hs_ref[:, cols] = a_cum * h_ref[:, cols] + h_local
            h_ref[:, cols] = hs_ref[tt - 1:tt, cols]
            hlast_ref[0, :, cols] = h_ref[:, cols]
            y = hs_ref[:, cols] * jax.nn.gelu(ug_ref[lag_slot, 1, :, cols])
            hs_ref[:, cols] = y
            state["sumsq"] = state["sumsq"] + jnp.sum(y * y, axis=-1, keepdims=True)

        half = len(scan_steps) // 2
        return [coeffs, scan(scan_steps[:half]), scan(scan_steps[half:]), finish]

    vector_pieces = [p for c, cols in enumerate(_lane_chunks(LRU_WIDTH)) for p in lru_pieces(c, cols)]

    def ffn_in():
        if "xn" not in state:
            state["xn"] = _rms(x_ref[...], g1_ref[...]).astype(BF16)
        return state["xn"]

    def ffn_out():
        if "x" not in state:
            state["x"] = x_ref[...] + 0.5 * state.pop("acc")
            x_out_ref[...] = state["x"]
        return state["x"]

    def store_u(z):
        ug_ref[lead_slot, 0] = z

    def store_gate(z):
        ug_ref[lead_slot, 1] = z

    matmul_pieces = _ffn_pieces(ffn_in, wg_ref, wu_ref, wd_ref, state)
    matmul_pieces += _project_pieces(ffn_out, gmix_ref, win_ref, cos_ref, sin_ref, store_u, store_gate,
                                     q_ref, k_ref, v_ref)

    assert len(vector_pieces) == len(matmul_pieces)
    for vector_piece, matmul_piece in zip(vector_pieces, matmul_pieces):
        vector_piece()
        if project:
            matmul_piece()
    lru_ref[...] = hs_ref[...] * lax.rsqrt(state["sumsq"] * (1.0 / LRU_WIDTH) + EPS) * gout_ref[...]


def _ffn_proj_lru(x, seq, g1, wg, wu, wd, g_mix, w_in, cos, sin, conv_w, conv_b, wab, bab, lam, g_out,
                  other_g, other_u, other_d):
    rows = x.shape[0]
    n = rows // seq
    tt = min(LRU_TILE, seq)
    tiles_per_seq = seq // tt
    tiles = rows // tt
    pos_blocks = cos.shape[0] // tt
    lead = lambda s: jnp.minimum(s, tiles - 1)
    lag = lambda s: jnp.maximum(s - 1, 0)

    def lead_spec(width):
        return pl.BlockSpec((tt, width), lambda s: (lead(s), 0))

    pos_spec = pl.BlockSpec((tt, LANES), lambda s: (lead(s) % pos_blocks, 0))
    seq_spec = lambda r: pl.BlockSpec((1, r, LRU_WIDTH), lambda s: (lag(s) // tiles_per_seq, 0, 0))
    widths = (D_MODEL, Q_WIDTH, KV_WIDTH, KV_WIDTH)
    last_chunk = D_FF // FFN_CHUNK - 1
    assert last_chunk <= tiles
    chunk = lambda s: jnp.minimum(s, last_chunk)
    col_spec = pl.BlockSpec((D_MODEL, FFN_CHUNK), lambda s: (0, chunk(s)))
    row_spec = pl.BlockSpec((FFN_CHUNK, D_MODEL), lambda s: (chunk(s), 0))
    wide, tall = jax.ShapeDtypeStruct((D_MODEL, D_FF), BF16), jax.ShapeDtypeStruct((D_FF, D_MODEL), BF16)
    return pl.pallas_call(
        functools.partial(_ffn_proj_lru_body, tt=tt, tiles=tiles, tiles_per_seq=tiles_per_seq),
        grid=(tiles + 1,),
        in_specs=[lead_spec(D_MODEL), _const_spec((1, D_MODEL)), _const_spec((D_MODEL, D_FF)), _const_spec((D_MODEL, D_FF)),
                  _const_spec((D_FF, D_MODEL)), _const_spec((1, D_MODEL)), _const_spec((D_MODEL, IN_COLS)),
                  pos_spec, pos_spec] + _lru_weight_specs() + [col_spec, col_spec, row_spec],
        out_specs=[lead_spec(w) for w in widths]
        + [pl.BlockSpec((tt, LRU_WIDTH), lambda s: (lag(s), 0)), seq_spec(1), seq_spec(SUBLANES)]
        + [col_spec, col_spec, row_spec],
        out_shape=[jax.ShapeDtypeStruct((rows, w), F32) for w in widths]
        + [jax.ShapeDtypeStruct((rows, LRU_WIDTH), F32), jax.ShapeDtypeStruct((n, 1, LRU_WIDTH), F32),
           jax.ShapeDtypeStruct((n, SUBLANES, LRU_WIDTH), F32), wide, wide, tall],
        scratch_shapes=[pltpu.VMEM((2, 2, tt, LRU_WIDTH), F32), pltpu.VMEM((tt + SUBLANES, LRU_WIDTH), F32),
                        pltpu.VMEM((1, LRU_WIDTH), F32), pltpu.VMEM((tt, LRU_WIDTH), F32)],
        compiler_params=_params(1),
    )(x, g1, wg, wu, wd, g_mix, w_in, cos, sin, conv_w, conv_b, wab, bab, lam, g_out, other_g, other_u, other_d)


def _lru_sample_body(u_ref, gate_ref, cpad_ref, h0_ref, cw_ref, cb_ref, wab_ref, bab_ref, lam_ref, gout_ref,
                     o_ref, hs_ref, *, seg):
    rows = u_ref.shape[0]
    pos = lax.broadcasted_iota(jnp.int32, (rows, LANES), 0) % seg
    sumsq = jnp.zeros((rows, 1), F32)
    for c, cols in enumerate(_lane_chunks(LRU_WIDTH)):
        u = u_ref[:, cols]
        cpad = cpad_ref[:, cols]
        conv = cb_ref[:, cols] + u * cw_ref[CONV_WIDTH - 1:CONV_WIDTH, cols]
        for back in range(1, CONV_WIDTH):
            shifted = jnp.where(pos >= back, pltpu.roll(u, back, axis=0),
                                pltpu.roll(cpad, (back - seg) % rows, axis=0))
            conv = conv + shifted * cw_ref[CONV_WIDTH - 1 - back:CONV_WIDTH - back, cols]
        h = _lru_hidden(conv, h0_ref[:, cols], seg, wab_ref[c], bab_ref[c], lam_ref[:, cols])
        hs_ref[:, cols] = h
        y = h * jax.nn.gelu(gate_ref[:, cols])
        o_ref[:, cols] = y
        sumsq = sumsq + jnp.sum(y * y, axis=-1, keepdims=True)
    o_ref[...] = o_ref[...] * lax.rsqrt(sumsq * (1.0 / LRU_WIDTH) + EPS) * gout_ref[...]


def _lru_weight_specs():
    chunks = LRU_WIDTH // LANES
    return [_const_spec((CONV_WIDTH, LRU_WIDTH)), _const_spec((1, LRU_WIDTH)),
            _const_spec((chunks, LANES, 2 * LANES)), _const_spec((chunks, 1, 2 * LANES)),
            _const_spec((1, LRU_WIDTH)), _const_spec((1, LRU_WIDTH))]


def _lru_gate_chunks(w_a, b_a, w_i, b_i):
    chunks = LRU_WIDTH // LANES
    per = LRU_BLOCKS // chunks
    wa = w_a.reshape(chunks, per, *w_a.shape[1:])
    wi = w_i.reshape(chunks, per, *w_i.shape[1:])
    wab = jnp.stack([jnp.concatenate([_block_diag(wa[c]), _block_diag(wi[c])], axis=1) for c in range(chunks)])
    bab = jnp.concatenate([b_a.reshape(chunks, 1, LANES), b_i.reshape(chunks, 1, LANES)], axis=2)
    return wab.astype(BF16), bab


def _block_diag(w):
    nb, bi, bj = w.shape
    eye = jnp.eye(nb, dtype=w.dtype)
    return jnp.einsum('gij,gh->gihj', w, eye).reshape(nb * bi, nb * bj)


def _bdot_nt(a, b):
    return lax.dot_general(a, b, (((2,), (2,)), ((0,), (0,))), preferred_element_type=F32)


def _bdot(a, b):
    return lax.dot_general(a, b, (((2,), (1,)), ((0,), (0,))), preferred_element_type=F32)


def _swa_pieces(q, score, weigh, sink_ref, mask, g_out, store):
    group = ATTN_HEADS // KV_HEADS
    straight = [h for h in range(ATTN_HEADS) if (h % 2) == (h // group)]
    swapped = [h for h in range(ATTN_HEADS) if (h % 2) != (h // group)]
    scale = HEAD_DIM ** -0.5
    out_half = {}

    def low_lanes(shape):
        return lax.broadcasted_iota(jnp.int32, shape, 2) < HEAD_DIM

    def head_group(heads, swap):
        held = {}

        def scores():
            qv = q()
            n, r, _ = qv.shape
            low = low_lanes((n, r, LANES))
            zero = jnp.zeros((n, r, LANES), F32)
            qs = jnp.concatenate(
                [jnp.where(low if h % 2 == 0 else ~low, qv[:, :, (h // 2) * LANES:(h // 2 + 1) * LANES], zero)
                 for h in heads], axis=1)
            held["s"] = score(qs.astype(BF16), swap) * scale

        def probabilities():
            s = held.pop("s")
            r = s.shape[1] // len(heads)
            visible = mask()[None]
            probs = []
            for i, h in enumerate(heads):
                sh = jnp.where(visible, s[:, i * r:(i + 1) * r, :], -jnp.inf)
                sink = sink_ref[h]
                m = jnp.maximum(jnp.max(sh, axis=-1, keepdims=True), sink)
                e = jnp.exp(sh - m)
                denom = jnp.sum(e, axis=-1, keepdims=True) + jnp.exp(sink - m)
                probs.append(e / denom)
            held["p"] = jnp.concatenate(probs, axis=1).astype(BF16)

        def weighted_values():
            p = held.pop("p")
            r = p.shape[1] // len(heads)
            o = weigh(p, swap)
            for i, h in enumerate(heads):
                out_half[h] = o[:, i * r:(i + 1) * r, :]

        return [scores, probabilities, weighted_values]

    def finish():
        low = low_lanes(out_half[0].shape)
        out = jnp.concatenate([jnp.where(low, out_half[2 * j], out_half[2 * j + 1])
                               for j in range(Q_WIDTH // LANES)], axis=2)
        store(_rms(out, g_out))

    return head_group(straight, False) + head_group(swapped, True) + [finish]


def _band_mask(first_block):
    i = lax.broadcasted_iota(jnp.int32, (WINDOW, 2 * WINDOW), 0)
    j = lax.broadcasted_iota(jnp.int32, (WINDOW, 2 * WINDOW), 1)
    dist = i + WINDOW - j
    return (dist >= 0) & (dist < WINDOW) & (jnp.logical_not(first_block) | (j >= WINDOW))


def _swa_sample_body(sink_ref, q_ref, k_ref, v_ref, bk_ref, bv_ref, gout_ref, o_ref, nk_ref, nv_ref, *, s_len, past):
    def mask():
        qp = past + lax.broadcasted_iota(jnp.int32, (s_len, WINDOW + s_len), 0)
        col = lax.broadcasted_iota(jnp.int32, (s_len, WINDOW + s_len), 1)
        kp = jnp.where(col < WINDOW, past - WINDOW + col, past + col - WINDOW)
        dist = qp - kp
        return (dist >= 0) & (dist < WINDOW) & (kp >= 0)

    def halves(z, axis, swap):
        return pltpu.roll(z, HEAD_DIM, axis=axis) if swap else z

    def score(qs, swap):
        carried = _bdot(qs, halves(bk_ref[...], 1, swap).astype(BF16))
        fresh = _bdot_nt(qs, halves(k_ref[...], 2, swap).astype(BF16))
        return jnp.concatenate([carried, fresh], axis=2)

    def weigh(p, swap):
        return (_bdot_nt(p[:, :, :WINDOW], halves(bv_ref[...], 1, swap).astype(BF16))
                + _bdot(p[:, :, WINDOW:], halves(v_ref[...], 2, swap).astype(BF16)))

    def store(o):
        o_ref[...] = o.reshape(o_ref.shape)

    for piece in _swa_pieces(lambda: q_ref[...], score, weigh, sink_ref, mask, gout_ref[...], store):
        piece()

    def slide(window_ref, fresh_ref, out_ref):
        n = window_ref.shape[0]
        tail = jnp.concatenate([jnp.zeros((n, WINDOW - s_len, KV_WIDTH), F32), fresh_ref[...]], axis=1)
        lane = lax.broadcasted_iota(jnp.int32, (n, KV_WIDTH, WINDOW), 2)
        out_ref[...] = jnp.where(lane >= WINDOW - s_len, jnp.swapaxes(tail, 1, 2),
                                 pltpu.roll(window_ref[...], WINDOW - s_len, axis=2))

    slide(bk_ref, k_ref, nk_ref)
    slide(bv_ref, v_ref, nv_ref)


def _softmax(s):
    e = jnp.exp(s - jnp.max(s, axis=-1, keepdims=True))
    return e / jnp.sum(e, axis=-1, keepdims=True)


def _mix_and_query(x_ref, lru_ref, attn_ref, wout_ref, gx_ref, wcq_ref):
    x = (x_ref[...] + _dot(lru_ref[...].astype(BF16), wout_ref[:LRU_WIDTH, :])
         + _dot(attn_ref[...].astype(BF16), wout_ref[LRU_WIDTH:, :]))
    return x, _dot(_rms(x, gx_ref[...]).astype(BF16), wcq_ref[...])


def _swa_mix_ffn_body(sink_ref, q_ref, k_ref, v_ref, kp_ref, vp_ref, gattn_ref,
                      x_ref, lru_ref, wout_ref, gx_ref, wcq_ref, mk_ref, mv_ref, wco_ref,
                      g2_ref, wg_ref, wu_ref, wd_ref, gf_ref, o_ref, attn_ref,
                      *, tt, tiles, tiles_per_seq, final_norm):
    s = pl.program_id(0)
    refs = (sink_ref, q_ref, k_ref, v_ref, kp_ref, vp_ref, gattn_ref, x_ref, lru_ref, wout_ref, gx_ref, wcq_ref,
            mk_ref, mv_ref, wco_ref, g2_ref, wg_ref, wu_ref, wd_ref, gf_ref, o_ref, attn_ref)
    step = functools.partial(_swa_mix_ffn_step, *refs, tt=tt, tiles_per_seq=tiles_per_seq, final_norm=final_norm)

    @pl.when(s == 0)
    def _():
        step(attend=True, layer=False)

    @pl.when((s > 0) & (s < tiles))
    def _():
        step(attend=True, layer=True)

    @pl.when(s == tiles)
    def _():
        step(attend=False, layer=True)


def _swa_mix_ffn_step(sink_ref, q_ref, k_ref, v_ref, kp_ref, vp_ref, gattn_ref,
                      x_ref, lru_ref, wout_ref, gx_ref, wcq_ref, mk_ref, mv_ref, wco_ref,
                      g2_ref, wg_ref, wu_ref, wd_ref, gf_ref, o_ref, attn_ref,
                      *, tt, tiles_per_seq, final_norm, attend, layer):
    s = pl.program_id(0)
    lead_slot = s % 2
    lag_slot = 1 - lead_slot

    blocks = tt // WINDOW
    first_pos_block = lax.rem(s, tiles_per_seq) * blocks
    attention_pieces = []
    for j in range(blocks):
        rows = slice(j * WINDOW, (j + 1) * WINDOW)

        def band(ref, prev_ref, swap, j=j, rows=rows):
            prev = prev_ref[...] if j == 0 else ref[(j - 1) * WINDOW:j * WINDOW, :]
            both = jnp.concatenate([prev, ref[rows, :]], axis=0)[None]
            return (pltpu.roll(both, HEAD_DIM, axis=2) if swap else both).astype(BF16)

        def score(qs, swap, band=band):
            return _bdot_nt(qs, band(k_ref, kp_ref, swap))

        def weigh(p, swap, band=band):
            return _bdot(p, band(v_ref, vp_ref, swap))

        def store(o, rows=rows):
            attn_ref[lead_slot, rows, :] = o[0]

        attention_pieces += _swa_pieces(lambda rows=rows: q_ref[rows, :][None], score, weigh, sink_ref,
                                        lambda j=j: _band_mask(first_pos_block + j == 0), gattn_ref[...], store)

    state = {}
    scale = X_HEAD_DIM ** -0.5

    def mix():
        state["x"] = (x_ref[...] + _dot(lru_ref[...].astype(BF16), wout_ref[:LRU_WIDTH, :])
                      + _dot(attn_ref[lag_slot].astype(BF16), wout_ref[LRU_WIDTH:, :]))

    def query():
        state["q"] = _dot(_rms(state["x"], gx_ref[...]).astype(BF16), wcq_ref[...])

    def memory_head(h):
        def run():
            cols = slice(h * X_HEAD_DIM, (h + 1) * X_HEAD_DIM)
            sc = _dot_nt(state["q"][:, cols].astype(BF16), mk_ref[0, :, cols].astype(BF16)) * scale
            state["o", h] = _dot(_softmax(sc).astype(BF16), mv_ref[0, :, cols].astype(BF16))
        return run

    def memory_out():
        o = jnp.concatenate([state.pop(("o", h)) for h in range(X_HEADS)], axis=1)
        state["x"] = state["x"] + _dot(o.astype(BF16), wco_ref[...])
        state["xn"] = _rms(state["x"], g2_ref[...]).astype(BF16)

    def finish():
        y = state["x"] + 0.5 * state["acc"]
        o_ref[...] = _rms(y, gf_ref[...]) if final_norm else y

    layer_pieces = ([mix, query] + [memory_head(h) for h in range(X_HEADS)] + [memory_out]
                    + _ffn_pieces(lambda: state["xn"], wg_ref, wu_ref, wd_ref, state) + [finish])
    _interleave(layer_pieces if layer else [], attention_pieces if attend else [])


def _swa_mix_ffn(q, k, v, sink, g_attn, x, lru_out, seq, w_out, g_x, w_cq, mk, mv, w_co, g2, wg, wu, wd, g_final,
                 final_norm):
    rows = x.shape[0]
    tt = min(LRU_TILE, seq)
    tiles_per_seq = seq // tt
    tiles = rows // tt
    blocks = tt // WINDOW
    lead = lambda s: jnp.minimum(s, tiles - 1)
    lag = lambda s: jnp.maximum(s - 1, 0)

    def lead_spec(width):
        return pl.BlockSpec((tt, width), lambda s: (lead(s), 0))

    def lag_spec(width):
        return pl.BlockSpec((tt, width), lambda s: (lag(s), 0))

    prev_spec = pl.BlockSpec((WINDOW, KV_WIDTH), lambda s: (jnp.maximum(lead(s) * blocks - 1, 0), 0))
    mem_spec = pl.BlockSpec((1, N_MEM, D_MODEL), lambda s: (lag(s) // tiles_per_seq, 0, 0))
    w_spec = _const_spec((D_MODEL, D_MODEL))
    return pl.pallas_call(
        functools.partial(_swa_mix_ffn_body, tt=tt, tiles=tiles, tiles_per_seq=tiles_per_seq, final_norm=final_norm),
        grid=(tiles + 1,),
        in_specs=[pl.BlockSpec(memory_space=pltpu.SMEM), lead_spec(Q_WIDTH), lead_spec(KV_WIDTH), lead_spec(KV_WIDTH),
                  prev_spec, prev_spec, _const_spec((1, Q_WIDTH)),
                  lag_spec(D_MODEL), lag_spec(LRU_WIDTH), w_spec, _const_spec((1, D_MODEL)), w_spec,
                  mem_spec, mem_spec, w_spec,
                  _const_spec((1, D_MODEL)), _const_spec((D_MODEL, D_FF)), _const_spec((D_MODEL, D_FF)),
                  _const_spec((D_FF, D_MODEL)), _const_spec((1, D_MODEL))],
        out_specs=lag_spec(D_MODEL),
        out_shape=jax.ShapeDtypeStruct((rows, D_MODEL), F32),
        scratch_shapes=[pltpu.VMEM((2, tt, Q_WIDTH), F32)],
        compiler_params=_params(1),
    )(sink, q, k, v, k, v, g_attn, x, lru_out, w_out, g_x, w_cq, mk, mv, w_co, g2, wg, wu, wd, g_final)


def _sample_mixer_body(sink_ref, u_ref, gate_ref, cpad_ref, h0_ref, q_ref, k_ref, v_ref, bk_ref, bv_ref, x_ref,
                       cw_ref, cb_ref, wab_ref, bab_ref, lam_ref, glru_ref, gattn_ref, wout_ref, gx_ref, wcq_ref,
                       x_out_ref, xq_ref, hs_ref, nk_ref, nv_ref, lru_ref, attn_ref, *, seg, past):
    _lru_sample_body(u_ref, gate_ref, cpad_ref, h0_ref, cw_ref, cb_ref, wab_ref, bab_ref, lam_ref, glru_ref,
                     lru_ref, hs_ref, seg=seg)
    _swa_sample_body(sink_ref, q_ref, k_ref, v_ref, bk_ref, bv_ref, gattn_ref, attn_ref, nk_ref, nv_ref,
                     s_len=seg, past=past)
    x_out_ref[...], xq_ref[...] = _mix_and_query(x_ref, lru_ref, attn_ref, wout_ref, gx_ref, wcq_ref)


def _channel_major(window):
    n = window.shape[0]
    return jnp.transpose(window, (0, 2, 3, 1)).reshape(n, KV_WIDTH, WINDOW)


def _position_major(window):
    n = window.shape[0]
    return jnp.transpose(window.reshape(n, KV_HEADS, HEAD_DIM, WINDOW), (0, 3, 1, 2))


def _sample_mixer(u, gate, conv_pad, h0_rep, q, k, v, buf_k, buf_v, x, sink, lru_w, g_attn, w_out, g_x, w_cq, past):
    n, seg, _ = q.shape
    sb = min(SAMPLE_SEQ_TILE, n)
    rows = sb * seg

    def row_spec(width):
        return pl.BlockSpec((rows, width), lambda i: (i, 0))

    def seq_spec(steps, width):
        return pl.BlockSpec((sb, steps, width), lambda i: (i, 0, 0))

    buf_spec = seq_spec(WINDOW, KV_WIDTH)
    w_spec = _const_spec((D_MODEL, D_MODEL))
    total = n * seg
    return pl.pallas_call(
        functools.partial(_sample_mixer_body, seg=seg, past=past),
        grid=(n // sb,),
        in_specs=[pl.BlockSpec(memory_space=pltpu.SMEM)] + [row_spec(LRU_WIDTH)] * 4
        + [seq_spec(seg, Q_WIDTH), seq_spec(seg, KV_WIDTH), seq_spec(seg, KV_WIDTH), buf_spec, buf_spec,
           row_spec(D_MODEL)] + _lru_weight_specs()
        + [_const_spec((1, Q_WIDTH)), w_spec, _const_spec((1, D_MODEL)), w_spec],
        out_specs=[row_spec(D_MODEL), row_spec(D_MODEL), row_spec(LRU_WIDTH), buf_spec, buf_spec],
        out_shape=[jax.ShapeDtypeStruct((total, D_MODEL), F32), jax.ShapeDtypeStruct((total, D_MODEL), F32),
                   jax.ShapeDtypeStruct((total, LRU_WIDTH), F32),
                   jax.ShapeDtypeStruct((n, WINDOW, KV_WIDTH), F32), jax.ShapeDtypeStruct((n, WINDOW, KV_WIDTH), F32)],
        scratch_shapes=[pltpu.VMEM((rows, LRU_WIDTH), F32), pltpu.VMEM((rows, Q_WIDTH), F32)],
        compiler_params=_params(1),
    )(sink, u, gate, conv_pad, h0_rep, q, k, v, buf_k, buf_v, x, *lru_w, g_attn, w_out, g_x, w_cq)


def _xattn_cache_pieces(q_ref, mk_ref, mv_ref, o_ref):
    sb, steps, _ = q_ref.shape
    blocks = D_MODEL // LANES
    chunks = X_HEAD_DIM // LANES
    width = mk_ref.shape[1]
    block_cls = [(j % chunks) * X_HEADS + j // chunks for j in range(blocks)]
    scale = X_HEAD_DIM ** -0.5
    held = {}

    def lane_class():
        return lax.broadcasted_iota(jnp.int32, (sb, steps, width), 2) % blocks

    def scores():
        qs = jnp.concatenate([q_ref[:, :, j * LANES:(j + 1) * LANES] for j in range(blocks)], axis=1)
        held["s"] = _bdot_nt(qs.astype(BF16), mk_ref[...].astype(BF16)) * scale

    def probabilities():
        s = held.pop("s")
        cls = lane_class()
        part = jnp.zeros((sb, steps, width), F32)
        for j in range(blocks):
            part = part + jnp.where(cls == block_cls[j], s[:, j * steps:(j + 1) * steps, :], 0.0)
        score = part + pltpu.roll(part, width - X_HEADS, axis=2)
        top = jnp.zeros((sb, steps, width), F32)
        for h in range(X_HEADS):
            mine = cls == h
            top = jnp.where(mine, jnp.max(jnp.where(mine, score, -jnp.inf), axis=-1, keepdims=True), top)
        e = jnp.where(cls < X_HEADS, jnp.exp(score - top), 0.0)
        denom = jnp.ones((sb, steps, width), F32)
        for h in range(X_HEADS):
            mine = cls == h
            denom = jnp.where(mine, jnp.sum(jnp.where(mine, e, 0.0), axis=-1, keepdims=True), denom)
        p = e / denom
        p = p + pltpu.roll(p, X_HEADS, axis=2)
        held["p"] = jnp.concatenate([jnp.where(cls == block_cls[j], p, 0.0) for j in range(blocks)],
                                    axis=1).astype(BF16)

    def weighted_values():
        o = _bdot(held.pop("p"), mv_ref[...].astype(BF16))
        for j in range(blocks):
            o_ref[:, :, j * LANES:(j + 1) * LANES] = o[:, j * steps:(j + 1) * steps, :]

    return [scores, probabilities, weighted_values]


def _xattn_cache_body(q_ref, mk_ref, mv_ref, o_ref):
    for piece in _xattn_cache_pieces(q_ref, mk_ref, mv_ref, o_ref):
        piece()


def _interleave_chunks(cache):
    n = cache.shape[0]
    chunks = X_HEAD_DIM // LANES
    c = cache.reshape(n, N_MEM, X_HEADS, chunks, LANES)
    return jnp.transpose(c, (0, 1, 3, 2, 4)).reshape(n, N_MEM * chunks * X_HEADS, LANES)


def _xattn_cache(q, cache_k, cache_v):
    count, s_len, _ = q.shape
    sb = min(XATTN_SEQ_TILE, count)
    q_spec = pl.BlockSpec((sb, s_len, D_MODEL), lambda i: (i, 0, 0))
    rows = cache_k.shape[1]
    steps = count // sb
    hbm_spec = pl.BlockSpec(memory_space=pl.ANY)
    ring = pltpu.VMEM((XATTN_RING, sb, rows, LANES), F32)
    return pl.pallas_call(
        functools.partial(_xattn_ring_body, sb=sb, steps=steps),
        grid=(steps,),
        in_specs=[q_spec, hbm_spec, hbm_spec],
        out_specs=q_spec,
        out_shape=jax.ShapeDtypeStruct((count, s_len, D_MODEL), F32),
        scratch_shapes=[ring, ring, pltpu.SemaphoreType.DMA((2, XATTN_RING))],
        compiler_params=_params(1),
    )(q, cache_k, cache_v)


def _xattn_ring_body(q_ref, k_hbm, v_hbm, o_ref, k_ring, v_ring, sems, *, sb, steps):
    i = pl.program_id(0)

    def fetch(block, slot):
        return [pltpu.make_async_copy(hbm.at[pl.ds(block * sb, sb)], ring.at[slot], sems.at[which, slot])
                for which, (hbm, ring) in enumerate(((k_hbm, k_ring), (v_hbm, v_ring)))]

    @pl.when(i == 0)
    def _():
        for block in range(min(XATTN_RING, steps)):
            for copy in fetch(block, block):
                copy.start()

    slot = lax.rem(i, XATTN_RING)
    for copy in fetch(i, slot):
        copy.wait()
    for piece in _xattn_cache_pieces(q_ref, k_ring.at[slot], v_ring.at[slot], o_ref):
        piece()

    @pl.when(i + XATTN_RING < steps)
    def _():
        for copy in fetch(i + XATTN_RING, slot):
            copy.start()


def _mem_kv_body(mem_ref, g_ref, wk_ref, wv_ref, k_ref, v_ref):
    mm = _rms(mem_ref[...], g_ref[...]).astype(BF16)
    k_ref[...] = _dot(mm, wk_ref[...].astype(BF16))
    v_ref[...] = _dot(mm, wv_ref[...].astype(BF16))


def _mem_kv(mem, g, w_ck, w_cv):
    rows = mem.shape[0]
    tm = min(ROW_TILE, rows)
    row_spec = pl.BlockSpec((tm, D_MODEL), lambda i: (i, 0))
    w_spec = _const_spec((D_MODEL, D_MODEL))
    return pl.pallas_call(
        _mem_kv_body,
        grid=(rows // tm,),
        in_specs=[row_spec, _const_spec((1, D_MODEL)), w_spec, w_spec],
        out_specs=[row_spec, row_spec],
        out_shape=[jax.ShapeDtypeStruct((rows, D_MODEL), F32)] * 2,
        compiler_params=_params(1),
    )(mem, g, w_ck, w_cv)


def kernel(x_prompt, x_sample, mem_prompt, cache_mem_k, cache_mem_v, cache_swa_k, cache_swa_v, state_conv, state_lru_h,
           g_ffn1, w1_gate, w1_up, w1_down, g_mix, w_in, conv_w, conv_b, w_a, b_a, w_i, b_i, lam, sink,
           g_lru_out, g_attn_out, w_out, g_xattn, g_mem, w_cq, w_ck, w_cv, w_co, g_ffn2, w2_gate, w2_up, w2_down,
           g_final):
    nbp, seq, _ = x_prompt.shape
    nbs, dec_seq, _ = x_sample.shape
    depth = g_ffn1.shape[0]
    past = PAST_LEN
    cos_p, sin_p = _rope_tables(np.arange(seq, dtype=np.int32))
    sample_pos_rows = min(ROW_TILE, nbs * dec_seq)
    cos_s, sin_s = _rope_tables(np.tile(past + np.arange(dec_seq, dtype=np.int32), sample_pos_rows // dec_seq))

    xp = x_prompt.reshape(nbp * seq, D_MODEL)
    xs = x_sample.reshape(nbs * dec_seq, D_MODEL)
    g_fin = g_final.reshape(1, D_MODEL)
    row = lambda a: a.reshape(1, -1)
    outs = [[] for _ in range(10)]
    for l in range(depth):
        last = l == depth - 1
        bf = lambda a: a[l].astype(BF16)
        win, wout, wcq, wco = bf(w_in), bf(w_out), bf(w_cq), bf(w_co)
        wab, bab = _lru_gate_chunks(w_a[l], b_a[l], w_i[l], b_i[l])
        lru_w = (conv_w[l], row(conv_b[l]), wab, bab, row(lam[l]), row(g_lru_out[l]))

        xs, w1g, w1u, w1d = _ffn_cast(xs, row(g_ffn1[l]), w1_gate[l], w1_up[l], w1_down[l])
        u_s, gate, q, k, v = _proj(xs, row(g_mix[l]), win, cos_s, sin_s)
        conv_pad = jnp.pad(state_conv[l], ((0, 0), (dec_seq - (CONV_WIDTH - 1), 0), (0, 0)))
        h0_rep = jnp.repeat(state_lru_h[l], dec_seq, axis=0)
        per_seq = lambda a: a.reshape(nbs, dec_seq, a.shape[-1])
        xs, xq, hs, new_k, new_v = _sample_mixer(
            u_s, gate, conv_pad.reshape(nbs * dec_seq, LRU_WIDTH), h0_rep, per_seq(q), per_seq(k), per_seq(v),
            _channel_major(cache_swa_k[l]), _channel_major(cache_swa_v[l]), xs,
            sink[l], lru_w, row(g_attn_out[l]), wout, row(g_xattn[l]), wcq, past)
        xq, cache_k, cache_v = per_seq(xq), _interleave_chunks(cache_mem_k[l]), _interleave_chunks(cache_mem_v[l])

        mk_p, mv_p = _mem_kv(mem_prompt.reshape(nbp * N_MEM, D_MODEL), row(g_mem[l]), w_ck[l], w_cv[l])
        mk_p = mk_p.reshape(nbp, N_MEM, D_MODEL)
        mv_p = mv_p.reshape(nbp, N_MEM, D_MODEL)
        xp, q, k, v, lru_out, h_last, u_tail, w2g, w2u, w2d = _ffn_proj_lru(
            xp, seq, row(g_ffn1[l]), w1g, w1u, w1d, row(g_mix[l]), win, cos_p, sin_p, *lru_w,
            w2_gate[l], w2_up[l], w2_down[l])
        k3 = k.reshape(nbp, seq, KV_WIDTH)
        v3 = v.reshape(nbp, seq, KV_WIDTH)
        xp = _swa_mix_ffn(q, k, v, sink[l], row(g_attn_out[l]), xp, lru_out, seq, wout, row(g_xattn[l]), wcq,
                          mk_p, mv_p, wco, row(g_ffn2[l]), w2g, w2u, w2d, g_fin, last)
        outs[0].append(mk_p.reshape(nbp, N_MEM, X_HEADS, X_HEAD_DIM))
        outs[1].append(mv_p.reshape(nbp, N_MEM, X_HEADS, X_HEAD_DIM))
        outs[2].append(k3[:, -WINDOW:].reshape(nbp, WINDOW, KV_HEADS, HEAD_DIM))
        outs[3].append(v3[:, -WINDOW:].reshape(nbp, WINDOW, KV_HEADS, HEAD_DIM))
        outs[4].append(u_tail[:, -(CONV_WIDTH - 1):])
        outs[5].append(h_last.reshape(nbp, LRU_WIDTH))

        xo = _xattn_cache(xq, cache_k, cache_v)
        xs = _ffn(xs, row(g_ffn2[l]), w2g, w2u, w2d, g_fin, last, xo.reshape(nbs * dec_seq, D_MODEL), wco)
        outs[6].append(_position_major(new_k))
        outs[7].append(_position_major(new_v))
        outs[8].append(u_s.reshape(nbs, dec_seq, LRU_WIDTH)[:, -(CONV_WIDTH - 1):])
        outs[9].append(hs.reshape(nbs, dec_seq, LRU_WIDTH)[:, -1])

    return (xp.reshape(nbp, seq, D_MODEL), xs.reshape(nbs, dec_seq, D_MODEL)) + tuple(jnp.stack(o) for o in outs)
```
